```python
import functools
import jax, jax.numpy as jnp
from jax import lax
import numpy as np

D_MODEL = 4096
BATCH = 2
SEQ = 4096
DEPTH = 1
DEC_BATCH = 32
DEC_SEQ = 1
PAST_LEN = 8192
PAGE_SIZE = 128

RW_HEAD_DIM = 64
RW_HEADS = (D_MODEL // 2) // RW_HEAD_DIM
RW_WIDTH = RW_HEADS * RW_HEAD_DIM
W_LORA = 96
A_LORA = 96
G_LORA = 256
RW_PROJ = 3 * RW_WIDTH + W_LORA + A_LORA + G_LORA
RW_SPLITS = [RW_WIDTH, 2 * RW_WIDTH, 3 * RW_WIDTH, 3 * RW_WIDTH + W_LORA, 3 * RW_WIDTH + W_LORA + A_LORA]
RW_GN_EPS = 64e-5
HEAD_DIM = 128
N_HEADS = (D_MODEL // 2) // HEAD_DIM
N_KV_HEADS = N_HEADS // 4
GROUP = N_HEADS // N_KV_HEADS
ATT_WIDTH = N_HEADS * HEAD_DIM
KV_WIDTH = N_KV_HEADS * HEAD_DIM
ROT_DIM = HEAD_DIM // 4
ROPE_THETA = 500000.0
IDX_HEADS = 16
IDX_DIM = 64
IDX_ROT_DIM = IDX_DIM // 4
TOPK_MAX = 256
Q_BLOCK = 128
MEM_TOKENS = 256
MEM_HEADS = 4
MEM_HEAD_DIM = 128
MEM_WIDTH = MEM_HEADS * MEM_HEAD_DIM
N_GROUPS = 8
EXPERTS_PER_GROUP = 8
N_EXPERTS = N_GROUPS * EXPERTS_PER_GROUP
TOP_K_EXPERTS = 2
D_EXPERT = D_MODEL // 8
MOE_BLOCK_PROMPT = 128
MOE_BLOCK_SAMPLE = 8
LN_EPS = 1e-5
DEEPNORM_ALPHA = (2 * DEPTH) ** 0.25
DEEPNORM_BETA = (8 * DEPTH) ** -0.25
IN_SIZES = [RW_PROJ, ATT_WIDTH, KV_WIDTH, KV_WIDTH, IDX_HEADS * IDX_DIM, IDX_DIM, IDX_HEADS, D_MODEL, D_MODEL]
IN_SPLITS = [int(s) for s in np.cumsum(IN_SIZES)[:-1]]
C_IN = sum(IN_SIZES)

kernel_name = 'hybrid_rwkv7_dsa_hmoe_decode_step'


def layer_norm(x, g, b):
    xf = x.astype(jnp.float32)
    mu = xf.mean(-1, keepdims=True)
    var = jnp.square(xf - mu).mean(-1, keepdims=True)
    return ((xf - mu) * lax.rsqrt(var + LN_EPS) * g + b).astype(x.dtype)


def rope_partial(x, pos, rot_dim):
    half = rot_dim // 2
    inv_freq = ROPE_THETA ** (-jnp.arange(half, dtype=jnp.float32) / half)
    ang = pos.astype(jnp.float32)[:, None] * inv_freq[None, :]
    bshape = (pos.shape[0],) + (1,) * (x.ndim - 3) + (half,)
    cos = jnp.cos(ang).reshape(bshape)
    sin = jnp.sin(ang).reshape(bshape)
    x1 = x[..., :half].astype(jnp.float32)
    x2 = x[..., half:rot_dim].astype(jnp.float32)
    rot = jnp.concatenate([x1 * cos - x2 * sin, x2 * cos + x1 * sin], axis=-1).astype(x.dtype)
    return jnp.concatenate([rot, x[..., rot_dim:]], axis=-1)


def wkv7_scan(S0, r, decay, k, v, a_vec, b_vec):
    xs = tuple(jnp.moveaxis(t, 1, 0) for t in (r, decay, k, v, a_vec, b_vec))

    def step(S, inp):
        r_t, w_t, k_t, v_t, a_t, b_t = inp
        Sa = jnp.einsum('bhvk,bhk->bhv', S, a_t)
        S = S * w_t[:, :, None, :] + Sa[..., None] * b_t[:, :, None, :] + v_t[..., None] * k_t[:, :, None, :]
        return S, jnp.einsum('bhvk,bhk->bhv', S, r_t)

    S, ys = lax.scan(step, S0, xs)
    return S, jnp.moveaxis(ys, 0, 1)


def rwkv7_branch(z_rw, shift_prev, wkv_prev, p):
    B, T, _ = z_rw.shape
    z_prev = jnp.concatenate([shift_prev[:, None, :].astype(z_rw.dtype), z_rw[:, :-1]], axis=1)
    zs = (z_rw + (z_prev - z_rw) * p['rw_mu']).astype(jnp.float32)
    zr, zk, zv, zw, za, zg = jnp.split(zs, RW_SPLITS, axis=-1)
    w_log = -jax.nn.softplus(-(p['rw_w0'] + jnp.tanh(zw) @ p['rw_w2'])) - 0.5
    decay = jnp.exp(-jnp.exp(w_log))
    a = jax.nn.sigmoid(p['rw_a0'] + za @ p['rw_a2'])
    g = jax.nn.sigmoid(zg) @ p['rw_g2']
    heads = lambda t: t.reshape(B, T, RW_HEADS, RW_HEAD_DIM)
    r, k, v, a, decay = heads(zr), heads(zk), heads(zv), heads(a), heads(decay)
    kk = k * p['rw_k_k']
    kk = kk / jnp.maximum(jnp.sqrt(jnp.sum(kk * kk, axis=-1, keepdims=True)), 1e-12)
    k = k * (1.0 + (a - 1.0) * p['rw_k_a'])
    S, y = wkv7_scan(wkv_prev.astype(jnp.float32), r, decay, k, v, -kk, kk * a)
    mu = y.mean(-1, keepdims=True)
    var = jnp.square(y - mu).mean(-1, keepdims=True)
    y = (y - mu) * lax.rsqrt(var + RW_GN_EPS) * p['rw_ln_w'] + p['rw_ln_b']
    y = y + jnp.sum(r * k * p['rw_r_k'], axis=-1, keepdims=True) * v
    y = y.reshape(B, T, RW_WIDTH) * g
    return y.astype(z_rw.dtype), S


def indexer_topk(qi, wi, ki, qpos, kpos, topk):
    dots = jnp.einsum('bqhd,bld->bqhl', qi, ki).astype(jnp.float32) * IDX_DIM ** -0.5
    score = jnp.einsum('bqh,bqhl->bql', wi.astype(jnp.float32), jax.nn.relu(dots))
    score = jnp.where(kpos[None, None, :] <= qpos[None, :, None], score, -jnp.inf)
    _, idx = lax.top_k(score, topk)
    return idx


def sparse_attend(q, k_sel, v_sel, valid):
    B, Q = q.shape[:2]
    qg = q.reshape(B, Q, N_KV_HEADS, GROUP, HEAD_DIM)
    s = jnp.einsum('bqgrd,bqkgd->bqgrk', qg, k_sel).astype(jnp.float32) * HEAD_DIM ** -0.5
    s = jnp.where(valid[:, :, None, None, :], s, -jnp.inf)
    pr = jax.nn.softmax(s, axis=-1).astype(v_sel.dtype)
    o = jnp.einsum('bqgrk,bqkgd->bqgrd', pr, v_sel)
    return o.reshape(B, Q, N_HEADS, HEAD_DIM)


def dsa_prompt(q, k, v, qi, ki, wi, topk):
    B, S = q.shape[:2]
    kpos = jnp.arange(S)
    take_rows = jax.vmap(lambda rows, ii: rows[ii])

    def block(i):
        start = i * Q_BLOCK
        qb = lax.dynamic_slice_in_dim(q, start, Q_BLOCK, axis=1)
        qib = lax.dynamic_slice_in_dim(qi, start, Q_BLOCK, axis=1)
        wib = lax.dynamic_slice_in_dim(wi, start, Q_BLOCK, axis=1)
        qpos = start + jnp.arange(Q_BLOCK)
        idx = indexer_topk(qib, wib, ki, qpos, kpos, topk)
        valid = idx <= qpos[None, :, None]
        return sparse_attend(qb, take_rows(k, idx), take_rows(v, idx), valid)

    out = lax.map(block, jnp.arange(S // Q_BLOCK))
    return out.transpose(1, 0, 2, 3, 4).reshape(B, S, ATT_WIDTH)


def dsa_sample(q, k, v, qi, ki, wi, cache_k, cache_v, cache_ki, page_table, topk):
    DB, DS = q.shape[:2]
    past = page_table.shape[1] * PAGE_SIZE
    ki_past = cache_ki[page_table].reshape(DB, past, IDX_DIM).astype(ki.dtype)
    ki_all = jnp.concatenate([ki_past, ki], axis=1)
    qpos = past + jnp.arange(DS)
    kpos = jnp.arange(past + DS)
    idx = indexer_topk(qi, wi, ki_all, qpos, kpos, topk)
    in_past = idx < past
    pidx = jnp.minimum(idx, past - 1)
    phys = jax.vmap(lambda pt, ii: pt[ii])(page_table, pidx // PAGE_SIZE)
    slot = pidx % PAGE_SIZE
    nidx = jnp.clip(idx - past, 0, DS - 1)
    take_rows = jax.vmap(lambda rows, ii: rows[ii])
    sel = in_past[..., None, None]
    k_sel = jnp.where(sel, cache_k[phys, slot].astype(k.dtype), take_rows(k, nidx))
    v_sel = jnp.where(sel, cache_v[phys, slot].astype(v.dtype), take_rows(v, nidx))
    valid = idx <= qpos[None, :, None]
    return sparse_attend(q, k_sel, v_sel, valid).reshape(DB, DS, ATT_WIDTH)


def memory_attend(q, mk, mv):
    s = jnp.einsum('bthd,bmhd->bhtm', q, mk.astype(q.dtype)).astype(jnp.float32) * MEM_HEAD_DIM ** -0.5
    pr = jax.nn.softmax(s, axis=-1).astype(q.dtype)
    o = jnp.einsum('bhtm,bmhd->bthd', pr, mv.astype(q.dtype))
    return o.reshape(q.shape[0], q.shape[1], MEM_WIDTH)


def hier_route(x2d, w_grp, b_grp, w_exp, b_exp):
    T = x2d.shape[0]
    rows = jnp.arange(T)
    gl = (x2d @ w_grp).astype(jnp.float32) + b_grp
    gp = jax.nn.softmax(gl, axis=-1)
    g_sel = jnp.argmax(gl, axis=-1)
    g_prob = gp[rows, g_sel][:, None]
    el = ((x2d @ w_exp).astype(jnp.float32) + b_exp).reshape(T, N_GROUPS, EXPERTS_PER_GROUP)
    top_l, top_i = lax.top_k(el[rows, g_sel], TOP_K_EXPERTS)
    expert_idx = (g_sel[:, None] * EXPERTS_PER_GROUP + top_i).astype(jnp.int32)
    return expert_idx, g_prob * jax.nn.softmax(top_l, axis=-1)


def moe_apply(x2d, expert_idx, weights, w_gate, w_up, w_down, block_rows):
    T, D = x2d.shape
    K = expert_idx.shape[1]
    A = T * K
    flat_e = expert_idx.reshape(A)
    order = jnp.argsort(flat_e)
    sorted_e = flat_e[order]
    counts = jnp.bincount(flat_e, length=N_EXPERTS)
    padded = (counts + block_rows - 1) // block_rows * block_rows
    pad_end = jnp.cumsum(padded)
    pad_start = pad_end - padded
    start = jnp.cumsum(counts) - counts
    dest = pad_start[sorted_e] + jnp.arange(A) - start[sorted_e]
    n_blocks = -(-A // block_rows) + N_EXPERTS
    n_rows = n_blocks * block_rows
    row_token = jnp.full((n_rows,), T, jnp.int32).at[dest].set((order // K).astype(jnp.int32))
    x_pad = jnp.concatenate([x2d, jnp.zeros((1, D), x2d.dtype)], axis=0)
    x_rows = x_pad[row_token].reshape(n_blocks, block_rows, D)
    block_e = jnp.minimum(jnp.searchsorted(pad_end, jnp.arange(n_blocks) * block_rows, side='right'), N_EXPERTS - 1)

    def expert_block(args):
        xb, e = args
        h = jax.nn.silu(xb @ w_gate[e]) * (xb @ w_up[e])
        return h @ w_down[e]

    y_rows = lax.map(expert_block, (x_rows, block_e)).reshape(n_rows, D)
    y_assign = jnp.zeros((A, D), y_rows.dtype).at[order].set(y_rows[dest])
    y = jnp.sum(y_assign.reshape(T, K, D).astype(jnp.float32) * weights[..., None], axis=1)
    return y.astype(x2d.dtype)


def mixer_sublayer(x, pos, shift_prev, wkv_prev, attend, p):
    B, T, _ = x.shape
    z = x @ p['w_in']
    z_rw, z_q, z_k, z_v, z_qi, z_ki, z_wi, z_ga, z_gb = jnp.split(z, IN_SPLITS, axis=-1)
    rw_out, wkv_new = rwkv7_branch(z_rw, shift_prev, wkv_prev, p)
    q = rope_partial(z_q.reshape(B, T, N_HEADS, HEAD_DIM), pos, ROT_DIM)
    k = rope_partial(z_k.reshape(B, T, N_KV_HEADS, HEAD_DIM), pos, ROT_DIM)
    v = z_v.reshape(B, T, N_KV_HEADS, HEAD_DIM)
    qi = rope_partial(z_qi.reshape(B, T, IDX_HEADS, IDX_DIM), pos, IDX_ROT_DIM)
    ki = rope_partial(layer_norm(z_ki, p['idx_ln_w'], p['idx_ln_b']), pos, IDX_ROT_DIM)
    wi = z_wi * IDX_HEADS ** -0.5
    att_out = attend(q, k, v, qi, ki, wi)
    merged = jax.nn.sigmoid(z_ga) * (rw_out @ p['w_branch_a']) + jax.nn.sigmoid(z_gb) * (att_out @ p['w_branch_b'])
    return merged @ p['w_out'], (k, v, ki, wkv_new, z_rw[:, -1])


def decoder_layer(x, pos, shift_prev, wkv_prev, attend, mem_k, mem_v, moe_rows, p):
    B, T, D = x.shape
    y, new_state = mixer_sublayer(x, pos, shift_prev, wkv_prev, attend, p)
    x = layer_norm(DEEPNORM_ALPHA * x + y, p['ln1_w'], p['ln1_b'])
    mq = (x @ p['w_mem_q']).reshape(B, T, MEM_HEADS, MEM_HEAD_DIM)
    x = layer_norm(DEEPNORM_ALPHA * x + memory_attend(mq, mem_k, mem_v) @ p['w_mem_o'], p['ln2_w'], p['ln2_b'])
    x2 = x.reshape(B * T, D)
    e_idx, e_w = hier_route(x2, p['w_router_grp'], p['b_router_grp'], p['w_router_exp'], p['b_router_exp'])
    ff = moe_apply(x2, e_idx, e_w, p['w_exp_gate'], p['w_exp_up'], p['w_exp_down'], moe_rows).reshape(B, T, D)
    x = layer_norm(DEEPNORM_ALPHA * x + ff, p['ln3_w'], p['ln3_b'])
    return x, new_state


def setup_inputs(seed: int = 0) -> dict:
    key = jax.random.key(seed)
    keys = iter(jax.random.split(key, 64))
    f32 = jnp.float32

    def nrm(shape, scale=1.0):
        return jax.random.normal(next(keys), shape, f32) * scale

    def unif(shape, lo, hi):
        return jax.random.uniform(next(keys), shape, f32, lo, hi)

    n_pages = PAST_LEN // PAGE_SIZE
    n_pool = (DEC_BATCH * n_pages * 5) // 4
    L = DEPTH
    beta = DEEPNORM_BETA
    x_prompt = nrm((BATCH, SEQ, D_MODEL))
    x_sample = nrm((DEC_BATCH, DEC_SEQ, D_MODEL))
    mem_prompt = nrm((BATCH, MEM_TOKENS, D_MODEL))
    cache_k = nrm((L, n_pool, PAGE_SIZE, N_KV_HEADS, HEAD_DIM))
    cache_v = nrm((L, n_pool, PAGE_SIZE, N_KV_HEADS, HEAD_DIM))
    cache_idx_k = nrm((L, n_pool, PAGE_SIZE, IDX_DIM))
    perm = jax.random.permutation(next(keys), n_pool)
    page_table = perm[: DEC_BATCH * n_pages].reshape(DEC_BATCH, n_pages).astype(jnp.int32)
    return {
        'x_prompt': x_prompt,
        'x_sample': x_sample,
        'mem_prompt': mem_prompt,
        'cache_k': cache_k,
        'cache_v': cache_v,
        'cache_idx_k': cache_idx_k,
        'page_table': page_table,
        'state_wkv': nrm((L, DEC_BATCH, RW_HEADS, RW_HEAD_DIM, RW_HEAD_DIM), 0.5),
        'state_shift': nrm((L, DEC_BATCH, RW_PROJ)),
        'cache_mem_k': nrm((L, DEC_BATCH, MEM_TOKENS, MEM_HEADS, MEM_HEAD_DIM)),
        'cache_mem_v': nrm((L, DEC_BATCH, MEM_TOKENS, MEM_HEADS, MEM_HEAD_DIM)),
        'w_in': nrm((L, D_MODEL, C_IN), D_MODEL ** -0.5),
        'rw_mu': unif((L, RW_PROJ), 0.0, 1.0),
        'rw_w0': unif((L, RW_WIDTH), -6.0, -1.0),
        'rw_w2': nrm((L, W_LORA, RW_WIDTH), 0.3 * W_LORA ** -0.5),
        'rw_a0': nrm((L, RW_WIDTH), 0.1),
        'rw_a2': nrm((L, A_LORA, RW_WIDTH), A_LORA ** -0.5),
        'rw_g2': nrm((L, G_LORA, RW_WIDTH), G_LORA ** -0.5),
        'rw_k_k': 0.85 + nrm((L, RW_HEADS, RW_HEAD_DIM), 0.05),
        'rw_k_a': 1.0 + nrm((L, RW_HEADS, RW_HEAD_DIM), 0.05),
        'rw_r_k': nrm((L, RW_HEADS, RW_HEAD_DIM), 0.1),
        'rw_ln_w': 1.0 + nrm((L, RW_HEADS, RW_HEAD_DIM), 0.05),
        'rw_ln_b': nrm((L, RW_HEADS, RW_HEAD_DIM), 0.01),
        'idx_ln_w': 1.0 + nrm((L, IDX_DIM), 0.05),
        'idx_ln_b': nrm((L, IDX_DIM), 0.01),
        'w_branch_a': nrm((L, RW_WIDTH, D_MODEL), beta * RW_WIDTH ** -0.5),
        'w_branch_b': nrm((L, ATT_WIDTH, D_MODEL), beta * ATT_WIDTH ** -0.5),
        'w_out': nrm((L, D_MODEL, D_MODEL), beta * D_MODEL ** -0.5),
        'ln1_w': 1.0 + nrm((L, D_MODEL), 0.05),
        'ln1_b': nrm((L, D_MODEL), 0.01),
        'w_mem_q': nrm((L, D_MODEL, MEM_WIDTH), D_MODEL ** -0.5),
        'w_mem_k': nrm((L, D_MODEL, MEM_WIDTH), D_MODEL ** -0.5),
        'w_mem_v': nrm((L, D_MODEL, MEM_WIDTH), D_MODEL ** -0.5),
        'w_mem_o': nrm((L, MEM_WIDTH, D_MODEL), beta * MEM_WIDTH ** -0.5),
        'ln2_w': 1.0 + nrm((L, D_MODEL), 0.05),
        'ln2_b': nrm((L, D_MODEL), 0.01),
        'w_router_grp': nrm((L, D_MODEL, N_GROUPS), D_MODEL ** -0.5),
        'b_router_grp': nrm((L, N_GROUPS), 0.01),
        'w_router_exp': nrm((L, D_MODEL, N_EXPERTS), D_MODEL ** -0.5),
        'b_router_exp': nrm((L, N_EXPERTS), 0.01),
        'w_exp_gate': nrm((L, N_EXPERTS, D_MODEL, D_EXPERT), D_MODEL ** -0.5),
        'w_exp_up': nrm((L, N_EXPERTS, D_MODEL, D_EXPERT), D_MODEL ** -0.5),
        'w_exp_down': nrm((L, N_EXPERTS, D_EXPERT, D_MODEL), beta * D_EXPERT ** -0.5),
        'ln3_w': 1.0 + nrm((L, D_MODEL), 0.05),
        'ln3_b': nrm((L, D_MODEL), 0.01),
    }


def reference(x_prompt, x_sample, mem_prompt, cache_k, cache_v, cache_idx_k, page_table,
              state_wkv, state_shift, cache_mem_k, cache_mem_v,
              w_in, rw_mu, rw_w0, rw_w2, rw_a0, rw_a2, rw_g2, rw_k_k, rw_k_a, rw_r_k,
              rw_ln_w, rw_ln_b, idx_ln_w, idx_ln_b, w_branch_a, w_branch_b, w_out, ln1_w, ln1_b,
              w_mem_q, w_mem_k, w_mem_v, w_mem_o, ln2_w, ln2_b,
              w_router_grp, b_router_grp, w_router_exp, b_router_exp,
              w_exp_gate, w_exp_up, w_exp_down, ln3_w, ln3_b):
    B, S, _ = x_prompt.shape
    DB, DS, _ = x_sample.shape
    past = page_table.shape[1] * PAGE_SIZE
    topk_prompt = min(TOPK_MAX, S // 4)
    topk_sample = min(TOPK_MAX, (past + DS) // 4)
    pos_prompt = jnp.arange(S, dtype=jnp.int32)
    pos_sample = past + jnp.arange(DS, dtype=jnp.int32)
    xp, xs = x_prompt, x_sample
    new_p, new_s = [], []
    for l in range(DEPTH):
        p = {
            'w_in': w_in[l], 'rw_mu': rw_mu[l], 'rw_w0': rw_w0[l], 'rw_w2': rw_w2[l],
            'rw_a0': rw_a0[l], 'rw_a2': rw_a2[l], 'rw_g2': rw_g2[l], 'rw_k_k': rw_k_k[l],
            'rw_k_a': rw_k_a[l], 'rw_r_k': rw_r_k[l], 'rw_ln_w': rw_ln_w[l], 'rw_ln_b': rw_ln_b[l],
            'idx_ln_w': idx_ln_w[l], 'idx_ln_b': idx_ln_b[l], 'w_branch_a': w_branch_a[l],
            'w_branch_b': w_branch_b[l], 'w_out': w_out[l], 'ln1_w': ln1_w[l], 'ln1_b': ln1_b[l],
            'w_mem_q': w_mem_q[l], 'w_mem_o': w_mem_o[l], 'ln2_w': ln2_w[l], 'ln2_b': ln2_b[l],
            'w_router_grp': w_router_grp[l], 'b_router_grp': b_router_grp[l],
            'w_router_exp': w_router_exp[l], 'b_router_exp': b_router_exp[l],
            'w_exp_gate': w_exp_gate[l], 'w_exp_up': w_exp_up[l], 'w_exp_down': w_exp_down[l],
            'ln3_w': ln3_w[l], 'ln3_b': ln3_b[l],
        }
        mem_k = (mem_prompt @ w_mem_k[l]).reshape(B, MEM_TOKENS, MEM_HEADS, MEM_HEAD_DIM)
        mem_v = (mem_prompt @ w_mem_v[l]).reshape(B, MEM_TOKENS, MEM_HEADS, MEM_HEAD_DIM)
        attend_p = functools.partial(dsa_prompt, topk=topk_prompt)
        attend_s = functools.partial(dsa_sample, cache_k=cache_k[l], cache_v=cache_v[l],
                                     cache_ki=cache_idx_k[l], page_table=page_table, topk=topk_sample)
        xp, st_p = decoder_layer(xp, pos_prompt, jnp.zeros((B, RW_PROJ), xp.dtype),
                                 jnp.zeros((B, RW_HEADS, RW_HEAD_DIM, RW_HEAD_DIM), jnp.float32),
                                 attend_p, mem_k, mem_v, MOE_BLOCK_PROMPT, p)
        xs, st_s = decoder_layer(xs, pos_sample, state_shift[l], state_wkv[l], attend_s,
                                 cache_mem_k[l], cache_mem_v[l], MOE_BLOCK_SAMPLE, p)
        new_p.append(st_p + (mem_k, mem_v))
        new_s.append(st_s)
    new_k_prompt = jnp.stack([st[0] for st in new_p])
    new_v_prompt = jnp.stack([st[1] for st in new_p])
    new_idx_k_prompt = jnp.stack([st[2] for st in new_p])
    new_wkv_prompt = jnp.stack([st[3] for st in new_p])
    new_shift_prompt = jnp.stack([st[4] for st in new_p])
    mem_k_prompt = jnp.stack([st[5] for st in new_p])
    mem_v_prompt = jnp.stack([st[6] for st in new_p])
    new_k_sample = jnp.stack([st[0] for st in new_s])
    new_v_sample = jnp.stack([st[1] for st in new_s])
    new_idx_k_sample = jnp.stack([st[2] for st in new_s])
    new_wkv_sample = jnp.stack([st[3] for st in new_s])
    new_shift_sample = jnp.stack([st[4] for st in new_s])
    return (xp, xs, new_k_prompt, new_v_prompt, new_idx_k_prompt, new_wkv_prompt, new_shift_prompt,
            mem_k_prompt, mem_v_prompt, new_k_sample, new_v_sample, new_idx_k_sample,
            new_wkv_sample, new_shift_sample)
```

```python
import functools
import math

import jax
import jax.numpy as jnp
import numpy as np
from jax import lax
from jax.experimental import pallas as pl
from jax.experimental.pallas import tpu as pltpu

F32 = jnp.float32
BF16 = jnp.bfloat16
I32 = jnp.int32
HIGHEST = lax.Precision.HIGHEST

D_MODEL = 4096
RW_HEAD_DIM = 64
RW_HEADS = 32
RW_WIDTH = 2048
W_LORA = 96
A_LORA = 96
G_LORA = 256
RW_PROJ = 3 * RW_WIDTH + W_LORA + A_LORA + G_LORA
RW_GN_EPS = 64e-5
HEAD_DIM = 128
N_HEADS = 16
N_KV_HEADS = 4
GROUP = 4
ATT_WIDTH = 2048
KV_WIDTH = 512
ROT_DIM = 32
ROPE_THETA = 500000.0
IDX_HEADS = 16
IDX_DIM = 64
IDX_ROT_DIM = 16
TOPK_MAX = 256
PAGE_SIZE = 128
MEM_HEADS = 4
MEM_HEAD_DIM = 128
MEM_WIDTH = 512
N_GROUPS = 8
EXPERTS_PER_GROUP = 8
N_EXPERTS = 64
D_EXPERT = 512
LN_EPS = 1e-5
DEEPNORM_ALPHA = 2.0 ** 0.25
EXP_M05 = math.exp(-0.5)

LANES = 128
SUBLANES = 8
VMEM_LIMIT = 56 * 1024 * 1024

C_R, C_K, C_V, C_Q, C_GA, C_GB = 0, 2048, 4096, 6144, 8192, 12288
C_KA, C_VA, C_IQ, C_LORA, C_IKW = 16384, 16896, 17408, 18432, 18944
C_TOTAL = 19456

INT_MIN = -(2 ** 31)


def _round_up(n, m):
    return -(-n // m) * m


def _cparams(sem):
    return pltpu.CompilerParams(dimension_semantics=sem, vmem_limit_bytes=VMEM_LIMIT)


def _dot(a, b, precision=None):
    return jnp.dot(a, b, preferred_element_type=F32, precision=precision)


def _dot_nt(a, b, precision=None):
    return lax.dot_general(a, b, (((1,), (1,)), ((), ())), preferred_element_type=F32, precision=precision)


def _sigmoid(x):
    return 1.0 / (1.0 + jnp.exp(-x))


def _mm_body(x_ref, w_ref, o_ref, acc_ref):
    k = pl.program_id(2)

    @pl.when(k == 0)
    def _():
        acc_ref[...] = jnp.zeros_like(acc_ref)

    acc_ref[...] += _dot(x_ref[...], w_ref[...])

    @pl.when(k == pl.num_programs(2) - 1)
    def _():
        o_ref[...] = acc_ref[...].astype(o_ref.dtype)


def _mm(x, w, tm, tn, tk, out_dtype=F32, name="mm"):
    m, kd = x.shape
    n = w.shape[1]
    return pl.pallas_call(
        _mm_body,
        grid=(m // tm, n // tn, kd // tk),
        in_specs=[pl.BlockSpec((tm, tk), lambda i, j, k: (i, k)),
                  pl.BlockSpec((tk, tn), lambda i, j, k: (k, j))],
        out_specs=pl.BlockSpec((tm, tn), lambda i, j, k: (i, j)),
        out_shape=jax.ShapeDtypeStruct((m, n), out_dtype),
        scratch_shapes=[pltpu.VMEM((tm, tn), F32)],
        compiler_params=_cparams(("parallel", "parallel", "arbitrary")),
        name=name,
    )(x, w)


def _layer_norm_rows(x, g, b):
    mu = jnp.mean(x, axis=-1, keepdims=True)
    d = x - mu
    var = jnp.mean(d * d, axis=-1, keepdims=True)
    return d * lax.rsqrt(var + LN_EPS) * g + b


def _mm_ln_body(x_ref, w_ref, res_ref, g_ref, b_ref, o_ref, ob_ref, acc_ref):
    k = pl.program_id(1)

    @pl.when(k == 0)
    def _():
        acc_ref[...] = jnp.zeros_like(acc_ref)

    acc_ref[...] += _dot(x_ref[...], w_ref[...])

    @pl.when(k == pl.num_programs(1) - 1)
    def _():
        y = _layer_norm_rows(DEEPNORM_ALPHA * res_ref[...] + acc_ref[...], g_ref[...], b_ref[...])
        o_ref[...] = y
        ob_ref[...] = y.astype(BF16)


def _mm_ln(x, w, res, g, b, tm, tk, name="mm_ln"):
    m, kd = x.shape
    n = w.shape[1]
    return pl.pallas_call(
        _mm_ln_body,
        grid=(m // tm, kd // tk),
        in_specs=[pl.BlockSpec((tm, tk), lambda i, k: (i, k)),
                  pl.BlockSpec((tk, n), lambda i, k: (k, 0)),
                  pl.BlockSpec((tm, n), lambda i, k: (i, 0)),
                  pl.BlockSpec((1, n), lambda i, k: (0, 0)),
                  pl.BlockSpec((1, n), lambda i, k: (0, 0))],
        out_specs=[pl.BlockSpec((tm, n), lambda i, k: (i, 0)),
                   pl.BlockSpec((tm, n), lambda i, k: (i, 0))],
        out_shape=[jax.ShapeDtypeStruct((m, n), F32), jax.ShapeDtypeStruct((m, n), BF16)],
        scratch_shapes=[pltpu.VMEM((tm, n), F32)],
        compiler_params=_cparams(("parallel", "arbitrary")),
        name=name,
    )(x, w, res, g, b)


def _branch_merge_body(rw_ref, at_ref, wa_ref, wb_ref, ga_ref, gb_ref, o_ref):
    a = _dot(rw_ref[...], wa_ref[...])
    b = _dot(at_ref[...], wb_ref[...])
    o_ref[...] = (_sigmoid(ga_ref[...]) * a + _sigmoid(gb_ref[...]) * b).astype(o_ref.dtype)


def _branch_merge(rw, att, wa, wb, z, tm, tn):
    m = rw.shape[0]
    n = wa.shape[1]
    ga0, gb0 = C_GA // tn, C_GB // tn
    return pl.pallas_call(
        _branch_merge_body,
        grid=(m // tm, n // tn),
        in_specs=[pl.BlockSpec((tm, RW_WIDTH), lambda i, j: (i, 0)),
                  pl.BlockSpec((tm, ATT_WIDTH), lambda i, j: (i, 0)),
                  pl.BlockSpec((RW_WIDTH, tn), lambda i, j: (0, j)),
                  pl.BlockSpec((ATT_WIDTH, tn), lambda i, j: (0, j)),
                  pl.BlockSpec((tm, tn), lambda i, j: (i, ga0 + j)),
                  pl.BlockSpec((tm, tn), lambda i, j: (i, gb0 + j))],
        out_specs=pl.BlockSpec((tm, tn), lambda i, j: (i, j)),
        out_shape=jax.ShapeDtypeStruct((m, n), BF16),
        compiler_params=_cparams(("parallel", "parallel")),
        name="branch_merge",
    )(rw, att, wa, wb, z, z)


def _seg_sum(x, ind, ind_t):
    return _dot(_dot(x, ind, HIGHEST), ind_t, HIGHEST)


def _rwkv_tokens(zr, zk, zv, zl, pr, pk, pv, plo, prm, mu_l, w2, a2, g2, ind, ind_t):
    r = zr + (pr - zr) * prm[0:1]
    kx = zk + (pk - zk) * prm[1:2]
    v = zv + (pv - zv) * prm[2:3]
    zsl = zl + (plo - zl) * mu_l
    tw = jnp.tanh(zsl[:, 0:128]).astype(BF16)
    xw = prm[3:4] + _dot(tw, w2)
    lw = -EXP_M05 * _sigmoid(xw)
    a = _sigmoid(prm[4:5] + _dot(zsl[:, 128:256].astype(BF16), a2))
    g = _dot(_sigmoid(zsl[:, 256:512]).astype(BF16), g2)
    kk = kx * prm[5:6]
    n2 = _seg_sum(kk * kk, ind, ind_t)
    kkn = kk / jnp.maximum(jnp.sqrt(n2), 1e-12)
    kmod = kx * (1.0 + (a - 1.0) * prm[6:7])
    return r, lw, kmod, v, kkn, a, g


def _rwkv_post(y, r, kmod, v, g, prm, ind, ind_t):
    inv_n = 1.0 / RW_HEAD_DIM
    mean = _seg_sum(y, ind, ind_t) * inv_n
    d = y - mean
    var = _seg_sum(d * d, ind, ind_t) * inv_n
    yn = d * lax.rsqrt(var + RW_GN_EPS) * prm[8:9] + prm[9:10]
    bonus = _seg_sum(r * kmod * prm[7:8], ind, ind_t) * v
    return (yn + bonus) * g


WKV_C = 64
WKV_HQ = 4
WKV_W = WKV_HQ * RW_HEAD_DIM


def _wkv_chunk_body(zr_ref, zk_ref, zv_ref, zl_ref, prm_ref, mul_ref, w2_ref, a2_ref, g2_ref,
                    ind_ref, indt_ref, sel_ref, o_ref, so_ref, s_ref, cr_ref, ck_ref, cv_ref, cl_ref):
    c = pl.program_id(2)
    C = WKV_C

    @pl.when(c == 0)
    def _():
        s_ref[...] = jnp.zeros_like(s_ref)
        cr_ref[...] = jnp.zeros_like(cr_ref)
        ck_ref[...] = jnp.zeros_like(ck_ref)
        cv_ref[...] = jnp.zeros_like(cv_ref)
        cl_ref[...] = jnp.zeros_like(cl_ref)

    rows = lax.broadcasted_iota(I32, (C, 1), 0)

    def shifted(z, carry_ref):
        prev = jnp.where(rows == 0, carry_ref[0:1, :], pltpu.roll(z, 1, 0))
        carry_ref[0:1, :] = z[C - 1:C, :]
        return prev

    zr, zk, zv, zl = zr_ref[...], zk_ref[...], zv_ref[...], zl_ref[...]
    pr, pk, pv, plo = shifted(zr, cr_ref), shifted(zk, ck_ref), shifted(zv, cv_ref), shifted(zl, cl_ref)
    prm = prm_ref[...]
    ind, ind_t = ind_ref[...], indt_ref[...]
    r, lw, kmod, v, kkn, a, g = _rwkv_tokens(zr, zk, zv, zl, pr, pk, pv, plo, prm, mul_ref[0:1, :],
                                            w2_ref[...], a2_ref[...], g2_ref[...], ind, ind_t)
    al = -kkn
    be = kkn * a

    ti = lax.broadcasted_iota(I32, (C, C), 0)
    tj = lax.broadcasted_iota(I32, (C, C), 1)
    tri_incl = tj <= ti
    tri_strict = tj < ti
    cum = _dot(tri_incl.astype(F32), lw, HIGHEST)
    cum_l = cum[C - 1:C, :]
    p_inv = jnp.exp(-cum)
    p_rel = jnp.exp(cum_l - cum)
    ab = al * jnp.exp(cum - lw)
    rb = r * jnp.exp(cum)
    bt = (be * p_inv).astype(BF16)
    kt = (kmod * p_inv).astype(BF16)
    bh = be * p_rel
    kh = kmod * p_rel

    s0 = s_ref[...]
    ar = jnp.concatenate([ab, rb], axis=0)
    gs = _dot_nt(ar.astype(BF16), s0.astype(BF16))
    g_a, g_r = gs[0:C], gs[C:2 * C]

    lane_head = lax.broadcasted_iota(I32, (1, WKV_W), 1) // RW_HEAD_DIM
    eye = (ti == tj).astype(F32)
    u = jnp.zeros((C, WKV_W), F32)
    y = g_r
    for j in range(WKV_HQ):
        hm = lane_head == j
        arj = jnp.where(hm, ar, 0.0).astype(BF16)
        abj = _dot_nt(arj, bt)
        akj = _dot_nt(arj, kt)
        a_ab = jnp.where(tri_strict, abj[0:C], 0.0)
        a_rb = jnp.where(tri_incl, abj[C:2 * C], 0.0)
        a_ak = jnp.where(tri_strict, akj[0:C], 0.0)
        a_rk = jnp.where(tri_incl, akj[C:2 * C], 0.0)
        x = a_ab
        tm = eye + x
        for _ in range(5):
            xb = x.astype(BF16)
            x = _dot(xb, xb)
            tm = tm + _dot(tm.astype(BF16), x.astype(BF16))
        vj = jnp.where(hm, v, 0.0).astype(BF16)
        wj = jnp.where(hm, g_a, 0.0) + _dot(a_ak.astype(BF16), vj)
        uj = _dot(tm.astype(BF16), wj.astype(BF16))
        y = y + _dot(a_rb.astype(BF16), uj.astype(BF16)) + _dot(a_rk.astype(BF16), vj)
        u = u + uj

    uv_t = jnp.concatenate([u, v], axis=0).T.astype(BF16)
    bk = jnp.concatenate([bh, kh], axis=0).astype(BF16)
    upd = _dot(uv_t, bk)
    hv = lax.broadcasted_iota(I32, (WKV_W, 1), 0) // RW_HEAD_DIM
    s_new = s0 * jnp.exp(cum_l) + jnp.where(hv == lane_head, upd, 0.0)
    s_ref[...] = s_new

    o_ref[...] = _rwkv_post(y, r, kmod, v, g, prm, ind, ind_t).astype(o_ref.dtype)

    @pl.when(c == pl.num_programs(2) - 1)
    def _():
        for j in range(WKV_HQ):
            rows_j = s_new[j * RW_HEAD_DIM:(j + 1) * RW_HEAD_DIM, :]
            so_ref[0, j] = _dot(rows_j, sel_ref[j], HIGHEST)


def _wkv_prompt(z, prm, mu_l, w2, a2, g2, ind, ind_t, sel, batch, seq):
    nc = seq // WKV_C
    nq = RW_HEADS // WKV_HQ
    W = WKV_W
    row = lambda b, q, c: b * nc + c
    return pl.pallas_call(
        _wkv_chunk_body,
        grid=(batch, nq, nc),
        in_specs=[pl.BlockSpec((WKV_C, W), lambda b, q, c: (row(b, q, c), C_R // W + q)),
                  pl.BlockSpec((WKV_C, W), lambda b, q, c: (row(b, q, c), C_K // W + q)),
                  pl.BlockSpec((WKV_C, W), lambda b, q, c: (row(b, q, c), C_V // W + q)),
                  pl.BlockSpec((WKV_C, 512), lambda b, q, c: (row(b, q, c), C_LORA // 512)),
                  pl.BlockSpec((16, W), lambda b, q, c: (0, q)),
                  pl.BlockSpec((8, 512), lambda b, q, c: (0, 0)),
                  pl.BlockSpec((128, W), lambda b, q, c: (0, q)),
                  pl.BlockSpec((128, W), lambda b, q, c: (0, q)),
                  pl.BlockSpec((256, W), lambda b, q, c: (0, q)),
                  pl.BlockSpec((W, 128), lambda b, q, c: (0, 0)),
                  pl.BlockSpec((128, W), lambda b, q, c: (0, 0)),
                  pl.BlockSpec((WKV_HQ, W, RW_HEAD_DIM), lambda b, q, c: (0, 0, 0))],
        out_specs=[pl.BlockSpec((WKV_C, W), lambda b, q, c: (row(b, q, c), q)),
                   pl.BlockSpec((1, WKV_HQ, RW_HEAD_DIM, RW_HEAD_DIM), lambda b, q, c: (b, q, 0, 0))],
        out_shape=[jax.ShapeDtypeStruct((batch * seq, RW_WIDTH), BF16),
                   jax.ShapeDtypeStruct((batch, RW_HEADS, RW_HEAD_DIM, RW_HEAD_DIM), F32)],
        scratch_shapes=[pltpu.VMEM((W, W), F32), pltpu.VMEM((8, W), F32), pltpu.VMEM((8, W), F32),
                        pltpu.VMEM((8, W), F32), pltpu.VMEM((8, 512), F32)],
        compiler_params=_cparams(("parallel", "parallel", "arbitrary")),
        name="wkv_prompt",
    )(z, z, z, z, prm, mu_l, w2, a2, g2, ind, ind_t, sel)


def _wkv_tok_body(zr_ref, zk_ref, zv_ref, zl_ref, pr_ref, pk_ref, pv_ref, pl_ref, prm_ref, mul_ref, w2_ref, a2_ref,
                  g2_ref, ind_ref, indt_ref, r_ref, w_ref, al_ref, be_ref, km_ref, v_ref, g_ref, bo_ref):
    prm = prm_ref[...]
    ind, ind_t = ind_ref[...], indt_ref[...]
    r, lw, kmod, v, kkn, a, g = _rwkv_tokens(zr_ref[...], zk_ref[...], zv_ref[...], zl_ref[...], pr_ref[...],
                                            pk_ref[...], pv_ref[...], pl_ref[...], prm, mul_ref[0:1, :],
                                            w2_ref[...], a2_ref[...], g2_ref[...], ind, ind_t)
    r_ref[...] = r
    w_ref[...] = jnp.exp(lw)
    al_ref[...] = -kkn
    be_ref[...] = kkn * a
    km_ref[...] = kmod
    v_ref[...] = v
    g_ref[...] = g
    bo_ref[...] = _seg_sum(r * kmod * prm[7:8], ind, ind_t)


def _wkv_tokens_sample(z, prev_r, prev_k, prev_v, prev_l, prm, mu_l, w2, a2, g2, ind, ind_t, row0, n):
    rb = row0 // n
    full = lambda a: pl.BlockSpec(a.shape, lambda i: (0,) * a.ndim)
    zspec = lambda w, c0: pl.BlockSpec((n, w), lambda i: (rb, c0 // w))
    return pl.pallas_call(
        _wkv_tok_body,
        grid=(1,),
        in_specs=[zspec(RW_WIDTH, C_R), zspec(RW_WIDTH, C_K), zspec(RW_WIDTH, C_V), zspec(512, C_LORA),
                  full(prev_r), full(prev_k), full(prev_v), full(prev_l), full(prm), full(mu_l), full(w2), full(a2),
                  full(g2), full(ind), full(ind_t)],
        out_specs=[pl.BlockSpec((n, RW_WIDTH), lambda i: (0, 0))] * 8,
        out_shape=[jax.ShapeDtypeStruct((n, RW_WIDTH), F32)] * 8,
        compiler_params=_cparams(("arbitrary",)),
        name="wkv_tokens_sample",
    )(z, z, z, z, prev_r, prev_k, prev_v, prev_l, prm, mu_l, w2, a2, g2, ind, ind_t)


def _wkv_step_body(s_ref, w_ref, al_ref, be_ref, km_ref, r_ref, v_ref, g_ref, bo_ref, lnw_ref, lnb_ref, o_ref, so_ref):
    s = s_ref[...]
    vcol = v_ref[...]
    sa = jnp.sum(s * al_ref[...], axis=-1, keepdims=True)
    s2 = s * w_ref[...] + sa * be_ref[...] + vcol * km_ref[...]
    so_ref[...] = s2
    y = jnp.sum(s2 * r_ref[...], axis=-1, keepdims=True)
    mean = jnp.mean(y, axis=2, keepdims=True)
    d = y - mean
    var = jnp.mean(d * d, axis=2, keepdims=True)
    yn = d * lax.rsqrt(var + RW_GN_EPS) * lnw_ref[...] + lnb_ref[...]
    o_ref[...] = (yn + bo_ref[...] * vcol) * g_ref[...]


def _wkv_step(state, w, al, be, km, r, v, g, bo, lnw, lnb):
    n, h = state.shape[0], state.shape[1]
    hq = 2
    rowspec = pl.BlockSpec((n, hq, 1, RW_HEAD_DIM), lambda q: (0, q, 0, 0))
    colspec = pl.BlockSpec((n, hq, RW_HEAD_DIM, 1), lambda q: (0, q, 0, 0))
    pcol = pl.BlockSpec((1, hq, RW_HEAD_DIM, 1), lambda q: (0, q, 0, 0))
    sspec = pl.BlockSpec((n, hq, RW_HEAD_DIM, RW_HEAD_DIM), lambda q: (0, q, 0, 0))
    return pl.pallas_call(
        _wkv_step_body,
        grid=(h // hq,),
        in_specs=[sspec, rowspec, rowspec, rowspec, rowspec, rowspec, colspec, colspec, colspec, pcol, pcol],
        out_specs=[colspec, sspec],
        out_shape=[jax.ShapeDtypeStruct((n, h, RW_HEAD_DIM, 1), F32), jax.ShapeDtypeStruct(state.shape, F32)],
        compiler_params=_cparams(("parallel",)),
        name="wkv_step",
    )(state, w, al, be, km, r, v, g, bo, lnw, lnb)


def _rope_tables(pos, rot_dim, period):
    half = rot_dim // 2
    t = pos.shape[0]
    inv_freq = ROPE_THETA ** (-jnp.arange(half, dtype=F32) / half)
    ang = pos.astype(F32)[:, None] * inv_freq[None, :]
    cos, sin = jnp.cos(ang), jnp.sin(ang)
    zh = jnp.zeros((t, half), F32)
    rest = period - rot_dim
    c = jnp.concatenate([cos, cos, jnp.ones((t, rest), F32)], axis=1)
    s1 = jnp.concatenate([-sin, zh, jnp.zeros((t, rest), F32)], axis=1)
    s2 = jnp.concatenate([zh, sin, jnp.zeros((t, rest), F32)], axis=1)
    rep = LANES // period
    return jnp.stack([jnp.tile(a, (1, rep)) for a in (c, s1, s2)], axis=0)


def _rope(x, tab, half):
    n = x.shape[1]
    rep = n // LANES
    c, s1, s2 = [jnp.tile(tab[i], (1, rep)) if rep > 1 else tab[i] for i in range(3)]
    return x * c + pltpu.roll(x, n - half, 1) * s1 + pltpu.roll(x, half, 1) * s2


def _prep_body(q_ref, ka_ref, va_ref, iq_ref, ikw_ref, ta_ref, ti_ref, lnw_ref, lnb_ref,
               qo_ref, ko_ref, kb_ref, vb_ref, qio_ref, kio_ref, kid_ref):
    ta = ta_ref[...]
    ti = ti_ref[...]
    qo_ref[...] = _rope(q_ref[...], ta, ROT_DIM // 2).astype(BF16)
    k = _rope(ka_ref[...], ta, ROT_DIM // 2)
    ko_ref[...] = k
    kb_ref[...] = k.astype(BF16)
    vb_ref[...] = va_ref[...].astype(BF16)
    qio_ref[...] = _rope(iq_ref[...], ti, IDX_ROT_DIM // 2).astype(BF16)
    x = ikw_ref[...]
    lane = lax.broadcasted_iota(I32, x.shape, 1)
    is_k = lane < IDX_DIM
    mu = jnp.sum(jnp.where(is_k, x, 0.0), axis=-1, keepdims=True) * (1.0 / IDX_DIM)
    d = jnp.where(is_k, x - mu, 0.0)
    var = jnp.sum(d * d, axis=-1, keepdims=True) * (1.0 / IDX_DIM)
    kn = d * lax.rsqrt(var + LN_EPS) * lnw_ref[...] + lnb_ref[...]
    kr = _rope(kn, ti, IDX_ROT_DIM // 2)
    kr = jnp.where(is_k, kr, 0.0)
    kio_ref[...] = jnp.where(is_k, kr, x * (IDX_HEADS ** -0.5))
    kid_ref[...] = (kr + pltpu.roll(kr, IDX_DIM, 1)).astype(BF16)


def _prep(z, tab_a, tab_i, ln_w, ln_b, tm):
    m = z.shape[0]
    row = lambda w, c0: pl.BlockSpec((tm, w), lambda i: (i, c0 // w))
    outs = [(ATT_WIDTH, BF16), (KV_WIDTH, F32), (KV_WIDTH, BF16), (KV_WIDTH, BF16),
            (IDX_HEADS * IDX_DIM, BF16), (LANES, F32), (LANES, BF16)]
    return pl.pallas_call(
        _prep_body,
        grid=(m // tm,),
        in_specs=[row(ATT_WIDTH, C_Q), row(KV_WIDTH, C_KA), row(KV_WIDTH, C_VA), row(IDX_HEADS * IDX_DIM, C_IQ),
                  row(LANES, C_IKW),
                  pl.BlockSpec((3, tm, LANES), lambda i: (0, i, 0)),
                  pl.BlockSpec((3, tm, LANES), lambda i: (0, i, 0)),
                  pl.BlockSpec((1, LANES), lambda i: (0, 0)),
                  pl.BlockSpec((1, LANES), lambda i: (0, 0))],
        out_specs=[pl.BlockSpec((tm, w), lambda i: (i, 0)) for w, _ in outs],
        out_shape=[jax.ShapeDtypeStruct((m, w), dt) for w, dt in outs],
        compiler_params=_cparams(("parallel",)),
        name="prep",
    )(z, z, z, z, z, tab_a, tab_i, ln_w, ln_b)


DSA_QB = 128
DSA_TK = 256


def _float_key(s):
    b = pltpu.bitcast(s, I32)
    return b ^ ((b >> 31) & 0x7FFFFFFF)


def _kth_threshold(count_ge, topk):
    def step(b, thr):
        cand = thr + jnp.left_shift(jnp.int32(1), 31 - b)
        return jnp.where(count_ge(cand) >= topk, cand, thr)
    return lax.fori_loop(0, 32, step, jnp.full((DSA_QB, 1), INT_MIN, I32))


def _dsa_prompt_body(q_ref, kb_ref, vb_ref, qi_ref, kid_ref, kiw_ref, o_ref, keys_ref, jcut_ref, *, topk, seq):
    i = pl.program_id(1)
    nt = (i * DSA_QB + DSA_QB + DSA_TK - 1) // DSA_TK
    qpos = i * DSA_QB + lax.broadcasted_iota(I32, (DSA_QB, 1), 0)
    col0 = lax.broadcasted_iota(I32, (1, DSA_TK), 1)
    lane = lax.broadcasted_iota(I32, (1, LANES), 1)
    kiw = kiw_ref[...]

    def score_tile(t, carry):
        kd = kid_ref[pl.ds(t * DSA_TK, DSA_TK), :]
        s = jnp.zeros((DSA_QB, DSA_TK), F32)
        for h in range(IDX_HEADS):
            qt = qi_ref[:, (h // 2) * LANES:(h // 2 + 1) * LANES]
            qh = jnp.where((lane // IDX_DIM) == (h % 2), qt, jnp.zeros_like(qt))
            d = _dot_nt(qh, kd) * (IDX_DIM ** -0.5)
            s = s + kiw[:, IDX_DIM + h:IDX_DIM + h + 1] * jnp.maximum(d, 0.0)
        valid = (t * DSA_TK + col0) <= qpos
        keys_ref[:, pl.ds(t * DSA_TK, DSA_TK)] = jnp.where(valid, _float_key(s), INT_MIN)
        return carry

    lax.fori_loop(0, nt, score_tile, 0)

    def count(pred):
        def body(t, acc):
            kt = keys_ref[:, pl.ds(t * DSA_TK, DSA_TK)]
            return acc + pred(kt, t * DSA_TK + col0).astype(I32)
        acc = lax.fori_loop(0, nt, body, jnp.zeros((DSA_QB, DSA_TK), I32))
        return jnp.sum(acc, axis=-1, keepdims=True)

    thr = _kth_threshold(lambda c: count(lambda kt, col: kt >= c), topk)
    n_gt = count(lambda kt, col: kt > thr)
    n_eq = count(lambda kt, col: (kt == thr) & (col <= qpos))
    need = topk - n_gt
    jcut_ref[...] = jnp.full(jcut_ref.shape, seq, I32)
    excess = (n_eq > need) & (thr > INT_MIN)

    @pl.when(jnp.max(excess.astype(I32)) > 0)
    def _():
        def step(b, jm):
            cand = jm + jnp.left_shift(jnp.int32(1), 30 - b)
            c = count(lambda kt, col: (kt == thr) & (col <= qpos) & (col < cand))
            return jnp.where(c < need, cand, jm)
        jm = lax.fori_loop(0, 31, step, jnp.zeros((DSA_QB, 1), I32))
        jcut_ref[...] = jnp.broadcast_to(jnp.where(excess, jm, seq), jcut_ref.shape)

    jcut = jcut_ref[:, 0:1]
    scale = HEAD_DIM ** -0.5

    for h in range(N_HEADS):
        g = h // GROUP
        qh = q_ref[:, h * HEAD_DIM:(h + 1) * HEAD_DIM]

        def att_tile(t, carry):
            m, l, acc = carry
            kt = keys_ref[:, pl.ds(t * DSA_TK, DSA_TK)]
            col = t * DSA_TK + col0
            sel = ((kt > thr) | ((kt == thr) & (col <= jcut))) & (col <= qpos)
            k_t = kb_ref[pl.ds(t * DSA_TK, DSA_TK), g * HEAD_DIM:(g + 1) * HEAD_DIM]
            v_t = vb_ref[pl.ds(t * DSA_TK, DSA_TK), g * HEAD_DIM:(g + 1) * HEAD_DIM]
            s = jnp.where(sel, _dot_nt(qh, k_t) * scale, -1e30)
            m_new = jnp.maximum(m, jnp.max(s, axis=-1, keepdims=True))
            p = jnp.where(sel, jnp.exp(s - m_new), 0.0)
            corr = jnp.exp(m - m_new)
            l = l * corr + jnp.sum(p, axis=-1, keepdims=True)
            acc = acc * corr + _dot(p.astype(BF16), v_t)
            return m_new, l, acc

        m0 = jnp.full((DSA_QB, 1), -1e30, F32)
        l0 = jnp.zeros((DSA_QB, 1), F32)
        a0 = jnp.zeros((DSA_QB, HEAD_DIM), F32)
        m, l, acc = lax.fori_loop(0, nt, att_tile, (m0, l0, a0))
        o_ref[:, h * HEAD_DIM:(h + 1) * HEAD_DIM] = (acc / l).astype(o_ref.dtype)


def _dsa_prompt(qb, kb, vb, qib, kid, kiw, batch, seq, topk):
    nb = seq // DSA_QB
    body = functools.partial(_dsa_prompt_body, topk=topk, seq=seq)
    return pl.pallas_call(
        body,
        grid=(batch, nb),
        in_specs=[pl.BlockSpec((DSA_QB, ATT_WIDTH), lambda b, i: (b * nb + i, 0)),
                  pl.BlockSpec((seq, KV_WIDTH), lambda b, i: (b, 0)),
                  pl.BlockSpec((seq, KV_WIDTH), lambda b, i: (b, 0)),
                  pl.BlockSpec((DSA_QB, IDX_HEADS * IDX_DIM), lambda b, i: (b * nb + i, 0)),
                  pl.BlockSpec((seq, LANES), lambda b, i: (b, 0)),
                  pl.BlockSpec((DSA_QB, LANES), lambda b, i: (b * nb + i, 0))],
        out_specs=pl.BlockSpec((DSA_QB, ATT_WIDTH), lambda b, i: (b * nb + i, 0)),
        out_shape=jax.ShapeDtypeStruct((batch * seq, ATT_WIDTH), BF16),
        scratch_shapes=[pltpu.VMEM((DSA_QB, seq), I32), pltpu.VMEM((DSA_QB, LANES), I32)],
        compiler_params=_cparams(("parallel", "arbitrary")),
        name="dsa_prompt",
    )(qb, kb, vb, qib, kid, kiw)


def _idx_pages_body(pt_ref, qi_ref, wi_ref, ck_ref, o_ref):
    p = pl.program_id(1)
    kp = ck_ref[0].astype(BF16)
    d = _dot_nt(qi_ref[0], kp) * (IDX_DIM ** -0.5)
    o_ref[0, pl.ds(p, 1), :] = jnp.sum(wi_ref[0] * jnp.maximum(d, 0.0), axis=0, keepdims=True)


def _idx_pages(page_table, qi, wi, cache_ki):
    n, npg = page_table.shape
    grid_spec = pltpu.PrefetchScalarGridSpec(
        num_scalar_prefetch=1,
        grid=(n, npg),
        in_specs=[pl.BlockSpec((1, IDX_HEADS, IDX_DIM), lambda s, p, pt: (s, 0, 0)),
                  pl.BlockSpec((1, IDX_HEADS, 1), lambda s, p, pt: (s, 0, 0)),
                  pl.BlockSpec((1, PAGE_SIZE, IDX_DIM), lambda s, p, pt: (pt[s, p], 0, 0))],
        out_specs=pl.BlockSpec((1, npg, PAGE_SIZE), lambda s, p, pt: (s, 0, 0)),
    )
    return pl.pallas_call(
        _idx_pages_body,
        grid_spec=grid_spec,
        out_shape=jax.ShapeDtypeStruct((n, npg, PAGE_SIZE), F32),
        compiler_params=_cparams(("parallel", "arbitrary")),
        name="idx_pages",
    )(page_table, qi, wi, cache_ki)


def _sel_sample_body(sc_ref, qi_ref, wi_ref, ks_ref, m_ref, ms_ref, *, topk):
    npg = sc_ref.shape[1]
    keys = _float_key(sc_ref[0])
    d = jnp.sum(qi_ref[0].astype(F32) * ks_ref[0].astype(F32), axis=-1, keepdims=True) * (IDX_DIM ** -0.5)
    s_self = jnp.sum(wi_ref[0] * jnp.maximum(d, 0.0), axis=0, keepdims=True)
    k_self = _float_key(s_self)
    pos = lax.broadcasted_iota(I32, keys.shape, 0) * PAGE_SIZE + lax.broadcasted_iota(I32, keys.shape, 1)

    def total(x):
        return jnp.sum(jnp.sum(x.astype(I32), axis=1, keepdims=True), axis=0, keepdims=True)

    def step(b, thr):
        cand = thr + jnp.left_shift(jnp.int32(1), 31 - b)
        c = total(keys >= cand) + (k_self >= cand).astype(I32)
        return jnp.where(c >= topk, cand, thr)

    thr = lax.fori_loop(0, 32, step, jnp.full((1, 1), INT_MIN, I32))
    need = topk - total(keys > thr) - (k_self > thr).astype(I32)
    eq = keys == thr

    def jstep(b, jm):
        cand = jm + jnp.left_shift(jnp.int32(1), 30 - b)
        return jnp.where(total(eq & (pos < cand)) < need, cand, jm)

    jm = lax.fori_loop(0, 31, jstep, jnp.zeros((1, 1), I32))
    n_eq_kept = total(eq & (pos <= jm))
    m_ref[0] = ((keys > thr) | (eq & (pos <= jm))).astype(F32)
    self_sel = (k_self > thr) | ((k_self == thr) & (n_eq_kept < need))
    ms_ref[0] = jnp.broadcast_to(self_sel.astype(F32), (1, LANES))


def _sel_sample(scores, qi, wi, kself, topk):
    n, npg, _ = scores.shape
    return pl.pallas_call(
        functools.partial(_sel_sample_body, topk=topk),
        grid=(n,),
        in_specs=[pl.BlockSpec((1, npg, PAGE_SIZE), lambda s: (s, 0, 0)),
                  pl.BlockSpec((1, IDX_HEADS, IDX_DIM), lambda s: (s, 0, 0)),
                  pl.BlockSpec((1, IDX_HEADS, 1), lambda s: (s, 0, 0)),
                  pl.BlockSpec((1, 1, IDX_DIM), lambda s: (s, 0, 0))],
        out_specs=[pl.BlockSpec((1, npg, PAGE_SIZE), lambda s: (s, 0, 0)),
                   pl.BlockSpec((1, 1, LANES), lambda s: (s, 0, 0))],
        out_shape=[jax.ShapeDtypeStruct((n, npg, PAGE_SIZE), F32), jax.ShapeDtypeStruct((n, 1, LANES), F32)],
        compiler_params=_cparams(("parallel",)),
        name="sel_sample",
    )(scores, qi, wi, kself)


def _att_pages_body(pt_ref, q_ref, m_ref, ms_ref, ks_ref, vs_ref, ck_ref, cv_ref, o_ref, mx_ref, l_ref, acc_ref):
    p = pl.program_id(1)
    scale = HEAD_DIM ** -0.5
    q = q_ref[0]
    row_g = lax.broadcasted_iota(I32, (N_HEADS, 1), 0) // GROUP

    @pl.when(p == 0)
    def _():
        mx_ref[...] = jnp.full(mx_ref.shape, -1e30, F32)
        l_ref[...] = jnp.zeros_like(l_ref)
        acc_ref[...] = jnp.zeros_like(acc_ref)

    s = jnp.zeros((N_HEADS, PAGE_SIZE), F32)
    for g in range(N_KV_HEADS):
        kg = ck_ref[0, :, g, :].astype(BF16)
        s = jnp.where(row_g == g, _dot_nt(q, kg), s)
    sel = m_ref[0, pl.ds(p, 1), :] > 0.5
    s = jnp.where(sel, s * scale, -1e30)
    m_old = mx_ref[:, 0:1]
    m_new = jnp.maximum(m_old, jnp.max(s, axis=-1, keepdims=True))
    pr = jnp.where(sel, jnp.exp(s - m_new), 0.0)
    corr = jnp.exp(m_old - m_new)
    l_new = l_ref[:, 0:1] * corr + jnp.sum(pr, axis=-1, keepdims=True)
    pv = jnp.zeros((N_HEADS, HEAD_DIM), F32)
    prb = pr.astype(BF16)
    for g in range(N_KV_HEADS):
        vg = cv_ref[0, :, g, :].astype(BF16)
        pv = jnp.where(row_g == g, _dot(prb, vg), pv)
    acc = acc_ref[...] * corr + pv
    mx_ref[...] = jnp.broadcast_to(m_new, mx_ref.shape)
    l_ref[...] = jnp.broadcast_to(l_new, l_ref.shape)
    acc_ref[...] = acc

    @pl.when(p == pl.num_programs(1) - 1)
    def _():
        ssel = ms_ref[0][:, 0:1] > 0.5
        s1 = jnp.sum(q.astype(F32) * ks_ref[0].astype(F32), axis=-1, keepdims=True) * scale
        s1 = jnp.where(ssel, s1, -1e30)
        m2 = jnp.maximum(m_new, s1)
        p1 = jnp.where(ssel, jnp.exp(s1 - m2), 0.0)
        c2 = jnp.exp(m_new - m2)
        l2 = l_new * c2 + p1
        a2 = acc * c2 + p1.astype(BF16).astype(F32) * vs_ref[0].astype(F32)
        o_ref[0] = (a2 / l2).astype(o_ref.dtype)


def _att_pages(page_table, q, mask, mself, kself, vself, cache_k, cache_v):
    n, npg = page_table.shape
    seqspec = lambda r, c: pl.BlockSpec((1, r, c), lambda s, p, pt: (s, 0, 0))
    pagespec = pl.BlockSpec((1, PAGE_SIZE, N_KV_HEADS, HEAD_DIM), lambda s, p, pt: (pt[s, p], 0, 0, 0))
    grid_spec = pltpu.PrefetchScalarGridSpec(
        num_scalar_prefetch=1,
        grid=(n, npg),
        in_specs=[seqspec(N_HEADS, HEAD_DIM), seqspec(npg, PAGE_SIZE), seqspec(1, LANES),
                  seqspec(N_HEADS, HEAD_DIM), seqspec(N_HEADS, HEAD_DIM), pagespec, pagespec],
        out_specs=seqspec(N_HEADS, HEAD_DIM),
        scratch_shapes=[pltpu.VMEM((N_HEADS, LANES), F32), pltpu.VMEM((N_HEADS, LANES), F32),
                        pltpu.VMEM((N_HEADS, HEAD_DIM), F32)],
    )
    return pl.pallas_call(
        _att_pages_body,
        grid_spec=grid_spec,
        out_shape=jax.ShapeDtypeStruct((n, N_HEADS, HEAD_DIM), BF16),
        compiler_params=_cparams(("parallel", "arbitrary")),
        name="att_pages",
    )(page_table, q, mask, mself, kself, vself, cache_k, cache_v)


def _mem_att_prompt_body(q_ref, k_ref, v_ref, o_ref):
    scale = MEM_HEAD_DIM ** -0.5
    for h in range(MEM_HEADS):
        sl = slice(h * MEM_HEAD_DIM, (h + 1) * MEM_HEAD_DIM)
        s = _dot_nt(q_ref[:, sl], k_ref[:, sl]) * scale
        m = jnp.max(s, axis=-1, keepdims=True)
        e = jnp.exp(s - m)
        pr = e / jnp.sum(e, axis=-1, keepdims=True)
        o_ref[:, sl] = _dot(pr.astype(BF16), v_ref[:, sl]).astype(o_ref.dtype)


def _mem_att_prompt(mq, mk, mv, batch, seq, tq):
    m = mk.shape[0] // batch
    nb = seq // tq
    return pl.pallas_call(
        _mem_att_prompt_body,
        grid=(batch * nb,),
        in_specs=[pl.BlockSpec((tq, MEM_WIDTH), lambda i: (i, 0)),
                  pl.BlockSpec((m, MEM_WIDTH), lambda i: (i // nb, 0)),
                  pl.BlockSpec((m, MEM_WIDTH), lambda i: (i // nb, 0))],
        out_specs=pl.BlockSpec((tq, MEM_WIDTH), lambda i: (i, 0)),
        out_shape=jax.ShapeDtypeStruct((batch * seq, MEM_WIDTH), BF16),
        compiler_params=_cparams(("parallel",)),
        name="mem_att_prompt",
    )(mq, mk, mv)


def _mem_att_sample_body(q_ref, k_ref, v_ref, o_ref):
    scale = MEM_HEAD_DIM ** -0.5
    q = q_ref[0].astype(F32)
    prod = k_ref[0] * q
    for h in range(MEM_HEADS):
        sl = slice(h * MEM_HEAD_DIM, (h + 1) * MEM_HEAD_DIM)
        s = jnp.sum(prod[:, sl], axis=-1, keepdims=True) * scale
        m = jnp.max(s, axis=0, keepdims=True)
        e = jnp.exp(s - m)
        pr = e / jnp.sum(e, axis=0, keepdims=True)
        o_ref[0, :, sl] = jnp.sum(pr * v_ref[0][:, sl], axis=0, keepdims=True).astype(o_ref.dtype)


def _mem_att_sample(mq, mk, mv):
    n, m, w = mk.shape
    return pl.pallas_call(
        _mem_att_sample_body,
        grid=(n,),
        in_specs=[pl.BlockSpec((1, 1, w), lambda s: (s, 0, 0)),
                  pl.BlockSpec((1, m, w), lambda s: (s, 0, 0)),
                  pl.BlockSpec((1, m, w), lambda s: (s, 0, 0))],
        out_specs=pl.BlockSpec((1, 1, w), lambda s: (s, 0, 0)),
        out_shape=jax.ShapeDtypeStruct((n, 1, w), BF16),
        compiler_params=_cparams(("parallel",)),
        name="mem_att_sample",
    )(mq, mk, mv)


def _router_body(x_ref, w_ref, b_ref, ei_ref, ew_ref, acc_ref):
    k = pl.program_id(1)

    @pl.when(k == 0)
    def _():
        acc_ref[...] = jnp.zeros_like(acc_ref)

    acc_ref[...] += _dot(x_ref[...], w_ref[...], HIGHEST)

    @pl.when(k == pl.num_programs(1) - 1)
    def _():
        lg = acc_ref[...] + b_ref[...]
        lane = lax.broadcasted_iota(I32, lg.shape, 1)
        neg = jnp.float32(-jnp.inf)
        is_g = lane < N_GROUPS
        glm = jnp.where(is_g, lg, neg)
        gmax = jnp.max(glm, axis=-1, keepdims=True)
        g_sel = jnp.min(jnp.where(glm == gmax, lane, LANES), axis=-1, keepdims=True)
        g_prob = 1.0 / jnp.sum(jnp.where(is_g, jnp.exp(lg - gmax), 0.0), axis=-1, keepdims=True)
        e_id = lane - N_GROUPS
        in_grp = (e_id >= 0) & (e_id < N_EXPERTS) & ((e_id // EXPERTS_PER_GROUP) == g_sel)
        el = jnp.where(in_grp, lg, neg)
        m1 = jnp.max(el, axis=-1, keepdims=True)
        i1 = jnp.min(jnp.where(in_grp & (el == m1), lane, LANES), axis=-1, keepdims=True)
        rest = in_grp & (lane != i1)
        el2 = jnp.where(rest, lg, neg)
        m2 = jnp.max(el2, axis=-1, keepdims=True)
        i2 = jnp.min(jnp.where(rest & (el2 == m2), lane, LANES), axis=-1, keepdims=True)
        t = jnp.exp(m2 - m1)
        w1 = g_prob / (1.0 + t)
        w2 = g_prob * t / (1.0 + t)
        ei_ref[...] = jnp.where(lane == 0, i1 - N_GROUPS, jnp.where(lane == 1, i2 - N_GROUPS, 0))
        ew_ref[...] = jnp.where(lane == 0, w1, jnp.where(lane == 1, w2, 0.0))


def _router(x, w, b, tm, tk):
    m, kd = x.shape
    return pl.pallas_call(
        _router_body,
        grid=(m // tm, kd // tk),
        in_specs=[pl.BlockSpec((tm, tk), lambda i, k: (i, k)),
                  pl.BlockSpec((tk, LANES), lambda i, k: (k, 0)),
                  pl.BlockSpec((1, LANES), lambda i, k: (0, 0))],
        out_specs=[pl.BlockSpec((tm, LANES), lambda i, k: (i, 0)), pl.BlockSpec((tm, LANES), lambda i, k: (i, 0))],
        out_shape=[jax.ShapeDtypeStruct((m, LANES), I32), jax.ShapeDtypeStruct((m, LANES), F32)],
        scratch_shapes=[pltpu.VMEM((tm, LANES), F32)],
        compiler_params=_cparams(("parallel", "arbitrary")),
        name="router",
    )(x, w, b)


MOE_BR = 128


def _gather_body(idx_ref, nblk_ref, src_ref, o_ref, sem):
    i = pl.program_id(0)
    g = o_ref.shape[0]

    def row_copy(src_row, dst_row):
        return pltpu.make_async_copy(src_ref.at[pl.ds(src_row, 1)], o_ref.at[pl.ds(dst_row, 1)], sem)

    @pl.when(i < nblk_ref[0])
    def _():
        def issue(r, c):
            row_copy(idx_ref[i * g + r], r).start()
            return c
        lax.fori_loop(0, g, issue, 0)

        def drain(r, c):
            row_copy(0, r).wait()
            return c
        lax.fori_loop(0, g, drain, 0)

    @pl.when(i >= nblk_ref[0])
    def _():
        o_ref[...] = jnp.zeros_like(o_ref)


def _gather_rows(src, idx, nblk, g):
    m = idx.shape[0]
    d = src.shape[1]
    grid_spec = pltpu.PrefetchScalarGridSpec(
        num_scalar_prefetch=2,
        grid=(m // g,),
        in_specs=[pl.BlockSpec(memory_space=pl.ANY)],
        out_specs=pl.BlockSpec((g, d), lambda i, idx, nb: (i, 0)),
        scratch_shapes=[pltpu.SemaphoreType.DMA(())],
    )
    return pl.pallas_call(
        _gather_body,
        grid_spec=grid_spec,
        out_shape=jax.ShapeDtypeStruct((m, d), src.dtype),
        compiler_params=_cparams(("arbitrary",)),
        name="gather_rows",
    )(idx, nblk, src)


def _expert_up_body(be_ref, nblk_ref, x_ref, wg_ref, wu_ref, h_ref, wgb_ref, wub_ref):
    i = pl.program_id(1)
    changed = jnp.logical_or(i == 0, be_ref[i] != be_ref[jnp.maximum(i - 1, 0)])

    @pl.when(jnp.logical_and(i < nblk_ref[0], changed))
    def _():
        wgb_ref[...] = wg_ref[0].astype(BF16)
        wub_ref[...] = wu_ref[0].astype(BF16)

    @pl.when(i < nblk_ref[0])
    def _():
        x = x_ref[...].astype(BF16)
        a = _dot(x, wgb_ref[...])
        u = _dot(x, wub_ref[...])
        h_ref[...] = (a * _sigmoid(a) * u).astype(h_ref.dtype)

    @pl.when(i >= nblk_ref[0])
    def _():
        h_ref[...] = jnp.zeros_like(h_ref)


def _expert_up(block_e, nblk, xs, w_gate, w_up, th):
    nr, d = xs.shape
    nb = nr // MOE_BR
    nh = D_EXPERT // th
    blk = lambda i, nbk: jnp.minimum(i, nbk[0] - 1)
    grid_spec = pltpu.PrefetchScalarGridSpec(
        num_scalar_prefetch=2,
        grid=(nh, nb),
        in_specs=[pl.BlockSpec((MOE_BR, d), lambda j, i, be, nbk: (blk(i, nbk), 0)),
                  pl.BlockSpec((1, d, th), lambda j, i, be, nbk: (be[blk(i, nbk)], 0, j)),
                  pl.BlockSpec((1, d, th), lambda j, i, be, nbk: (be[blk(i, nbk)], 0, j))],
        out_specs=pl.BlockSpec((MOE_BR, th), lambda j, i, be, nbk: (i, j)),
        scratch_shapes=[pltpu.VMEM((d, th), BF16), pltpu.VMEM((d, th), BF16)],
    )
    return pl.pallas_call(
        _expert_up_body,
        grid_spec=grid_spec,
        out_shape=jax.ShapeDtypeStruct((nr, D_EXPERT), BF16),
        compiler_params=_cparams(("arbitrary", "arbitrary")),
        name="expert_up",
    )(block_e, nblk, xs, w_gate, w_up)


def _expert_down_body(be_ref, nblk_ref, h_ref, wd_ref, y_ref, wdb_ref):
    i = pl.program_id(0)
    changed = jnp.logical_or(i == 0, be_ref[i] != be_ref[jnp.maximum(i - 1, 0)])

    @pl.when(jnp.logical_and(i < nblk_ref[0], changed))
    def _():
        wdb_ref[...] = wd_ref[0].astype(BF16)

    @pl.when(i < nblk_ref[0])
    def _():
        y_ref[...] = _dot(h_ref[...], wdb_ref[...])

    @pl.when(i >= nblk_ref[0])
    def _():
        y_ref[...] = jnp.zeros_like(y_ref)


def _expert_down(block_e, nblk, h, w_down):
    nr = h.shape[0]
    d = w_down.shape[2]
    blk = lambda i, nbk: jnp.minimum(i, nbk[0] - 1)
    grid_spec = pltpu.PrefetchScalarGridSpec(
        num_scalar_prefetch=2,
        grid=(nr // MOE_BR,),
        in_specs=[pl.BlockSpec((MOE_BR, D_EXPERT), lambda i, be, nbk: (blk(i, nbk), 0)),
                  pl.BlockSpec((1, D_EXPERT, d), lambda i, be, nbk: (be[blk(i, nbk)], 0, 0))],
        out_specs=pl.BlockSpec((MOE_BR, d), lambda i, be, nbk: (i, 0)),
        scratch_shapes=[pltpu.VMEM((D_EXPERT, d), BF16)],
    )
    return pl.pallas_call(
        _expert_down_body,
        grid_spec=grid_spec,
        out_shape=jax.ShapeDtypeStruct((nr, d), F32),
        compiler_params=_cparams(("arbitrary",)),
        name="expert_down",
    )(block_e, nblk, h, w_down)


def _combine_ln_body(x_ref, y0_ref, y1_ref, ew_ref, g_ref, b_ref, o_ref):
    ew = ew_ref[...]
    ff = y0_ref[...] * ew[:, 0:1] + y1_ref[...] * ew[:, 1:2]
    o_ref[...] = _layer_norm_rows(DEEPNORM_ALPHA * x_ref[...] + ff, g_ref[...], b_ref[...])


def _combine_ln(x, yg, ew, g, b, tm):
    m, d = x.shape
    nb = m // tm
    return pl.pallas_call(
        _combine_ln_body,
        grid=(nb,),
        in_specs=[pl.BlockSpec((tm, d), lambda i: (i, 0)),
                  pl.BlockSpec((tm, d), lambda i: (i, 0)),
                  pl.BlockSpec((tm, d), lambda i: (nb + i, 0)),
                  pl.BlockSpec((tm, LANES), lambda i: (i, 0)),
                  pl.BlockSpec((1, d), lambda i: (0, 0)),
                  pl.BlockSpec((1, d), lambda i: (0, 0))],
        out_specs=pl.BlockSpec((tm, d), lambda i: (i, 0)),
        out_shape=jax.ShapeDtypeStruct((m, d), F32),
        compiler_params=_cparams(("parallel",)),
        name="combine_ln",
    )(x, yg, yg, ew, g, b)


def _pad_cols(x, n):
    return jnp.pad(x, ((0, 0), (0, n - x.shape[1])))


def _pack_w_in(w):
    o = np.cumsum([0, RW_PROJ, ATT_WIDTH, KV_WIDTH, KV_WIDTH, IDX_HEADS * IDX_DIM, IDX_DIM, IDX_HEADS, D_MODEL, D_MODEL])
    rw, q, ka, va, iq, ik, iw, ga, gb = [w[:, int(o[i]):int(o[i + 1])] for i in range(9)]
    lora = jnp.concatenate([_pad_cols(rw[:, 6144:6240], 128), _pad_cols(rw[:, 6240:6336], 128), rw[:, 6336:6592]], axis=1)
    ikw = _pad_cols(jnp.concatenate([ik, iw], axis=1), 128)
    cols = jnp.concatenate([rw[:, 0:6144], q, ga, gb, ka, va, iq, lora, ikw], axis=1)
    return _pad_cols(cols, C_TOTAL).astype(BF16)


def _lora_cols(x):
    return jnp.concatenate([_pad_cols(x[:, 6144:6240], 128), _pad_cols(x[:, 6240:6336], 128), x[:, 6336:6592]], axis=1)


def _pack_rwkv(rw_mu, rw_w0, rw_w2, rw_a0, rw_a2, rw_g2, rw_k_k, rw_k_a, rw_r_k, rw_ln_w, rw_ln_b):
    flat = lambda t: t.reshape(1, RW_WIDTH)
    mu = rw_mu.reshape(1, RW_PROJ)
    rows = [mu[:, 0:2048], mu[:, 2048:4096], mu[:, 4096:6144], flat(rw_w0), flat(rw_a0), flat(rw_k_k),
            flat(rw_k_a), flat(rw_r_k), flat(rw_ln_w), flat(rw_ln_b)]
    prm = jnp.pad(jnp.concatenate(rows, axis=0), ((0, 6), (0, 0)))
    mu_l = jnp.pad(_lora_cols(mu), ((0, 7), (0, 0)))
    w2 = jnp.pad(rw_w2, ((0, 128 - W_LORA), (0, 0))).astype(BF16)
    a2 = jnp.pad(rw_a2, ((0, 128 - A_LORA), (0, 0))).astype(BF16)
    g2 = rw_g2.astype(BF16)
    return prm, mu_l, w2, a2, g2


def _head_indicators(width):
    lane = np.arange(width)[:, None] // RW_HEAD_DIM
    ind = (lane == np.arange(128)[None, :]).astype(np.float32)
    return jnp.asarray(ind), jnp.asarray(ind.T)


def _head_selectors():
    sel = np.zeros((WKV_HQ, WKV_W, RW_HEAD_DIM), np.float32)
    for j in range(WKV_HQ):
        sel[j, j * RW_HEAD_DIM + np.arange(RW_HEAD_DIM), np.arange(RW_HEAD_DIM)] = 1.0
    return jnp.asarray(sel)


def kernel(x_prompt, x_sample, mem_prompt, cache_k, cache_v, cache_idx_k, page_table, state_wkv, state_shift, cache_mem_k, cache_mem_v, w_in, rw_mu, rw_w0, rw_w2, rw_a0, rw_a2, rw_g2, rw_k_k, rw_k_a, rw_r_k, rw_ln_w, rw_ln_b, idx_ln_w, idx_ln_b, w_branch_a, w_branch_b, w_out, ln1_w, ln1_b, w_mem_q, w_mem_k, w_mem_v, w_mem_o, ln2_w, ln2_b, w_router_grp, b_router_grp, w_router_exp, b_router_exp, w_exp_gate, w_exp_up, w_exp_down, ln3_w, ln3_b):
    B, S, D = x_prompt.shape
    DB, DS, _ = x_sample.shape
    assert DS == 1 and cache_k.shape[0] == 1
    TP = B * S
    T = TP + DB
    MP = _round_up(T, 640)
    past = page_table.shape[1] * PAGE_SIZE
    n_mem = mem_prompt.shape[1]
    row1 = lambda a: a.reshape(1, -1)

    def pad_rows(a):
        return jnp.concatenate([a, jnp.zeros((MP - a.shape[0],) + a.shape[1:], a.dtype)], axis=0)

    x_all = pad_rows(jnp.concatenate([x_prompt.reshape(TP, D), x_sample.reshape(DB, D)], axis=0))
    z = _mm(x_all.astype(BF16), _pack_w_in(w_in[0]), 640, 1024, 1024, name="in_proj")

    prm, mu_l, w2, a2, g2 = _pack_rwkv(rw_mu[0], rw_w0[0], rw_w2[0], rw_a0[0], rw_a2[0], rw_g2[0], rw_k_k[0],
                                       rw_k_a[0], rw_r_k[0], rw_ln_w[0], rw_ln_b[0])
    ind_q, indt_q = _head_indicators(WKV_W)
    rw_p, wkv_p = _wkv_prompt(z, prm, mu_l, w2, a2, g2, ind_q, indt_q, _head_selectors(), B, S)
    ss = state_shift[0]
    ind_f, indt_f = _head_indicators(RW_WIDTH)
    tok = _wkv_tokens_sample(z, ss[:, 0:2048], ss[:, 2048:4096], ss[:, 4096:6144], _lora_cols(ss), prm, mu_l, w2, a2,
                             g2, ind_f, indt_f, TP, DB)
    t_r, t_w, t_al, t_be, t_km, t_v, t_g, t_bo = tok
    rowv = lambda a: a.reshape(DB, RW_HEADS, 1, RW_HEAD_DIM)
    colv = lambda a: a.reshape(DB, RW_HEADS, RW_HEAD_DIM, 1)
    y_col, wkv_s = _wkv_step(state_wkv[0], rowv(t_w), rowv(t_al), rowv(t_be), rowv(t_km), rowv(t_r), colv(t_v),
                             colv(t_g), colv(t_bo), rw_ln_w[0].reshape(1, RW_HEADS, RW_HEAD_DIM, 1),
                             rw_ln_b[0].reshape(1, RW_HEADS, RW_HEAD_DIM, 1))
    rw_all = pad_rows(jnp.concatenate([rw_p, y_col.reshape(DB, RW_WIDTH).astype(BF16)], axis=0))

    pos = jnp.concatenate([jnp.tile(jnp.arange(S, dtype=I32), B), jnp.full((MP - TP,), past, I32)])
    tab_a = _rope_tables(pos, ROT_DIM, HEAD_DIM)
    tab_i = _rope_tables(pos, IDX_ROT_DIM, IDX_DIM)
    qb, k_rot, kb, vb, qib, kiw, kid = _prep(z, tab_a, tab_i, _pad_cols(row1(idx_ln_w[0]), LANES),
                                             _pad_cols(row1(idx_ln_b[0]), LANES), 128)
    att_p = _dsa_prompt(qb, kb, vb, qib, kid, kiw, B, S, min(TOPK_MAX, S // 4))
    qi_s = qib[TP:T].reshape(DB, IDX_HEADS, IDX_DIM)
    wi_s = kiw[TP:T, IDX_DIM:IDX_DIM + IDX_HEADS].reshape(DB, IDX_HEADS, 1)
    scores = _idx_pages(page_table, qi_s, wi_s, cache_idx_k[0])
    mask, mself = _sel_sample(scores, qi_s, wi_s, kid[TP:T, 0:IDX_DIM].reshape(DB, 1, IDX_DIM),
                              min(TOPK_MAX, (past + DS) // 4))
    expand = lambda a: jnp.repeat(a[TP:T].reshape(DB, N_KV_HEADS, HEAD_DIM), GROUP, axis=1)
    att_s = _att_pages(page_table, qb[TP:T].reshape(DB, N_HEADS, HEAD_DIM), mask, mself, expand(kb), expand(vb),
                       cache_k[0], cache_v[0])
    att_all = pad_rows(jnp.concatenate([att_p, att_s.reshape(DB, ATT_WIDTH)], axis=0))

    merged = _branch_merge(rw_all, att_all, w_branch_a[0].astype(BF16), w_branch_b[0].astype(BF16), z, 640, 1024)
    x1, x1b = _mm_ln(merged, w_out[0].astype(BF16), x_all, row1(ln1_w[0]), row1(ln1_b[0]), 320, 512, name="out_ln1")

    mq = _mm(x1b, w_mem_q[0].astype(BF16), 640, MEM_WIDTH, 1024, out_dtype=BF16, name="mem_q")
    mem2d = mem_prompt.reshape(B * n_mem, D).astype(BF16)
    mem_k = _mm(mem2d, w_mem_k[0].astype(BF16), B * n_mem, MEM_WIDTH, 1024, name="mem_k")
    mem_v = _mm(mem2d, w_mem_v[0].astype(BF16), B * n_mem, MEM_WIDTH, 1024, name="mem_v")
    ma_p = _mem_att_prompt(mq, mem_k.astype(BF16), mem_v.astype(BF16), B, S, 512)
    ma_s = _mem_att_sample(mq[TP:T].reshape(DB, 1, MEM_WIDTH), cache_mem_k[0].reshape(DB, n_mem, MEM_WIDTH),
                           cache_mem_v[0].reshape(DB, n_mem, MEM_WIDTH))
    ma_all = pad_rows(jnp.concatenate([ma_p, ma_s.reshape(DB, MEM_WIDTH)], axis=0))
    x2, _ = _mm_ln(ma_all, w_mem_o[0].astype(BF16), x1, row1(ln2_w[0]), row1(ln2_b[0]), 320, 512, name="mem_o_ln2")

    w_r = _pad_cols(jnp.concatenate([w_router_grp[0], w_router_exp[0]], axis=1), LANES)
    b_r = _pad_cols(row1(jnp.concatenate([b_router_grp[0], b_router_exp[0]])), LANES)
    e_idx, e_w = _router(x2, w_r, b_r, 640, 1024)
    n_assign = 2 * T
    flat_e = e_idx[:T, 0:2].reshape(n_assign)
    order = jnp.argsort(flat_e).astype(I32)
    sorted_e = flat_e[order]
    counts = jnp.bincount(flat_e, length=N_EXPERTS).astype(I32)
    padded = (counts + MOE_BR - 1) // MOE_BR * MOE_BR
    pad_end = jnp.cumsum(padded)
    start = jnp.cumsum(counts) - counts
    dest = ((pad_end - padded)[sorted_e] + jnp.arange(n_assign, dtype=I32) - start[sorted_e]).astype(I32)
    n_blocks = -(-n_assign // MOE_BR) + N_EXPERTS
    row_token = jnp.zeros((n_blocks * MOE_BR,), I32).at[dest].set(order // 2)
    block_e = jnp.minimum(jnp.searchsorted(pad_end, jnp.arange(n_blocks, dtype=I32) * MOE_BR, side='right'),
                          N_EXPERTS - 1).astype(I32)
    n_used = (pad_end[-1] // MOE_BR).astype(I32).reshape(1)
    slot = jnp.zeros((n_assign,), I32).at[order].set(dest).reshape(T, 2)
    xs = _gather_rows(x2, row_token, n_used, MOE_BR)
    hid = _expert_up(block_e, n_used, xs, w_exp_gate[0], w_exp_up[0], 256)
    y_rows = _expert_down(block_e, n_used, hid, w_exp_down[0])
    slot_pad = jnp.concatenate([jnp.pad(slot[:, 0], (0, MP - T)), jnp.pad(slot[:, 1], (0, MP - T))])
    y_tok = _gather_rows(y_rows, slot_pad, jnp.full((1,), 2 * MP // MOE_BR, I32), MOE_BR)
    x3 = _combine_ln(x2, y_tok, e_w, row1(ln3_w[0]), row1(ln3_b[0]), 128)

    kv5 = lambda a, n, s: a.reshape(1, n, s, N_KV_HEADS, HEAD_DIM)
    va = z[:, C_VA:C_VA + KV_WIDTH]
    ki = kiw[:, 0:IDX_DIM]
    zl = jnp.concatenate([z[S - 1:TP:S], z[TP:T]], axis=0)
    shift_cols = jnp.concatenate([zl[:, 0:6144], zl[:, C_LORA:C_LORA + W_LORA],
                                  zl[:, C_LORA + 128:C_LORA + 128 + A_LORA], zl[:, C_LORA + 256:C_LORA + 512]], axis=1)
    mem5 = lambda a: a.reshape(1, B, n_mem, MEM_HEADS, MEM_HEAD_DIM)
    return (x3[:TP].reshape(B, S, D), x3[TP:T].reshape(DB, DS, D),
            kv5(k_rot[:TP], B, S), kv5(va[:TP], B, S), ki[:TP].reshape(1, B, S, IDX_DIM),
            wkv_p[None], shift_cols[:B][None], mem5(mem_k), mem5(mem_v),
            kv5(k_rot[TP:T], DB, DS), kv5(va[TP:T], DB, DS), ki[TP:T].reshape(1, DB, DS, IDX_DIM),
            wkv_s[None], shift_cols[B:][None])
```

```python
import functools
import math

import jax
import jax.numpy as jnp
import numpy as np
from jax import lax
from jax.experimental import pallas as pl
from jax.experimental.pallas import tpu as pltpu

F32 = jnp.float32
BF16 = jnp.bfloat16
I32 = jnp.int32
HIGHEST = lax.Precision.HIGHEST

D_MODEL = 4096
RW_HEAD_DIM = 64
RW_HEADS = 32
RW_WIDTH = 2048
W_LORA = 96
A_LORA = 96
G_LORA = 256
RW_PROJ = 3 * RW_WIDTH + W_LORA + A_LORA + G_LORA
RW_GN_EPS = 64e-5
HEAD_DIM = 128
N_HEADS = 16
N_KV_HEADS = 4
GROUP = 4
ATT_WIDTH = 2048
KV_WIDTH = 512
ROT_DIM = 32
ROPE_THETA = 500000.0
IDX_HEADS = 16
IDX_DIM = 64
IDX_ROT_DIM = 16
TOPK_MAX = 256
PAGE_SIZE = 128
MEM_HEADS = 4
MEM_HEAD_DIM = 128
MEM_WIDTH = 512
N_GROUPS = 8
EXPERTS_PER_GROUP = 8
N_EXPERTS = 64
D_EXPERT = 512
LN_EPS = 1e-5
DEEPNORM_ALPHA = 2.0 ** 0.25
EXP_M05 = math.exp(-0.5)

LANES = 128
SUBLANES = 8
VMEM_LIMIT = 56 * 1024 * 1024

C_R, C_K, C_V, C_Q, C_GA, C_GB = 0, 2048, 4096, 6144, 8192, 12288
C_KA, C_VA, C_IQ, C_LORA, C_IKW = 16384, 16896, 17408, 18432, 18944
C_TOTAL = 19456

INT_MIN = -(2 ** 31)


def _round_up(n, m):
    return -(-n // m) * m


def _cparams(sem):
    return pltpu.CompilerParams(dimension_semantics=sem, vmem_limit_bytes=VMEM_LIMIT)


def _dot(a, b, precision=None):
    return jnp.dot(a, b, preferred_element_type=F32, precision=precision)


def _dot_nt(a, b, precision=None):
    return lax.dot_general(a, b, (((1,), (1,)), ((), ())), preferred_element_type=F32, precision=precision)


def _sigmoid(x):
    return 1.0 / (1.0 + jnp.exp(-x))


def _mm_body(x_ref, w_ref, o_ref, acc_ref):
    k = pl.program_id(2)

    @pl.when(k == 0)
    def _():
        acc_ref[...] = jnp.zeros_like(acc_ref)

    acc_ref[...] += _dot(x_ref[...], w_ref[...])

    @pl.when(k == pl.num_programs(2) - 1)
    def _():
        o_ref[...] = acc_ref[...].astype(o_ref.dtype)


def _mm_fullk_body(x_ref, w_ref, o_ref):
    o_ref[...] = _dot(x_ref[...], w_ref[...]).astype(o_ref.dtype)


def _mm(x, w, tm, tn, tk, out_dtype=F32, name="mm"):
    m, kd = x.shape
    n = w.shape[1]
    if tk == kd:
        return pl.pallas_call(
            _mm_fullk_body,
            grid=(m // tm, n // tn),
            in_specs=[pl.BlockSpec((tm, kd), lambda i, j: (i, 0)),
                      pl.BlockSpec((kd, tn), lambda i, j: (0, j))],
            out_specs=pl.BlockSpec((tm, tn), lambda i, j: (i, j)),
            out_shape=jax.ShapeDtypeStruct((m, n), out_dtype),
            compiler_params=_cparams(("parallel", "parallel")),
            name=name,
        )(x, w)
    return pl.pallas_call(
        _mm_body,
        grid=(m // tm, n // tn, kd // tk),
        in_specs=[pl.BlockSpec((tm, tk), lambda i, j, k: (i, k)),
                  pl.BlockSpec((tk, tn), lambda i, j, k: (k, j))],
        out_specs=pl.BlockSpec((tm, tn), lambda i, j, k: (i, j)),
        out_shape=jax.ShapeDtypeStruct((m, n), out_dtype),
        scratch_shapes=[pltpu.VMEM((tm, tn), F32)],
        compiler_params=_cparams(("parallel", "parallel", "arbitrary")),
        name=name,
    )(x, w)


def _layer_norm_rows(x, g, b):
    mu = jnp.mean(x, axis=-1, keepdims=True)
    d = x - mu
    var = jnp.mean(d * d, axis=-1, keepdims=True)
    return d * lax.rsqrt(var + LN_EPS) * g + b


def _mm_ln_body(x_ref, w_ref, res_ref, g_ref, b_ref, o_ref, ob_ref, y_ref, *, tn):
    j = pl.program_id(1)
    y_ref[:, pl.ds(pl.multiple_of(j * tn, tn), tn)] = _dot(x_ref[...], w_ref[...])

    @pl.when(j == pl.num_programs(1) - 1)
    def _():
        y = _layer_norm_rows(DEEPNORM_ALPHA * res_ref[...] + y_ref[...], g_ref[...], b_ref[...])
        o_ref[...] = y
        ob_ref[...] = y.astype(BF16)


def _mm_ln(x, w, res, g, b, tm, tn, name="mm_ln"):
    m, kd = x.shape
    n = w.shape[1]
    return pl.pallas_call(
        functools.partial(_mm_ln_body, tn=tn),
        grid=(m // tm, n // tn),
        in_specs=[pl.BlockSpec((tm, kd), lambda i, j: (i, 0)),
                  pl.BlockSpec((kd, tn), lambda i, j: (0, j)),
                  pl.BlockSpec((tm, n), lambda i, j: (i, 0)),
                  pl.BlockSpec((1, n), lambda i, j: (0, 0)),
                  pl.BlockSpec((1, n), lambda i, j: (0, 0))],
        out_specs=[pl.BlockSpec((tm, n), lambda i, j: (i, 0)),
                   pl.BlockSpec((tm, n), lambda i, j: (i, 0))],
        out_shape=[jax.ShapeDtypeStruct((m, n), F32), jax.ShapeDtypeStruct((m, n), BF16)],
        scratch_shapes=[pltpu.VMEM((tm, n), F32)],
        compiler_params=_cparams(("parallel", "arbitrary")),
        name=name,
    )(x, w, res, g, b)


def _branch_merge_body(rw_ref, at_ref, wa_ref, wb_ref, ga_ref, gb_ref, o_ref):
    a = _dot(rw_ref[...], wa_ref[...])
    b = _dot(at_ref[...], wb_ref[...])
    o_ref[...] = (_sigmoid(ga_ref[...]) * a + _sigmoid(gb_ref[...]) * b).astype(o_ref.dtype)


def _branch_merge(rw, att, wa, wb, z, tm, tn):
    m = rw.shape[0]
    n = wa.shape[1]
    ga0, gb0 = C_GA // tn, C_GB // tn
    return pl.pallas_call(
        _branch_merge_body,
        grid=(m // tm, n // tn),
        in_specs=[pl.BlockSpec((tm, RW_WIDTH), lambda i, j: (i, 0)),
                  pl.BlockSpec((tm, ATT_WIDTH), lambda i, j: (i, 0)),
                  pl.BlockSpec((RW_WIDTH, tn), lambda i, j: (0, j)),
                  pl.BlockSpec((ATT_WIDTH, tn), lambda i, j: (0, j)),
                  pl.BlockSpec((tm, tn), lambda i, j: (i, ga0 + j)),
                  pl.BlockSpec((tm, tn), lambda i, j: (i, gb0 + j))],
        out_specs=pl.BlockSpec((tm, tn), lambda i, j: (i, j)),
        out_shape=jax.ShapeDtypeStruct((m, n), BF16),
        compiler_params=_cparams(("parallel", "parallel")),
        name="branch_merge",
    )(rw, att, wa, wb, z, z)


def _seg_sum(x, ind, ind_t):
    return _dot(_dot(x, ind, HIGHEST), ind_t, HIGHEST)


def _split_bf16(x, parts):
    out = []
    for _ in range(parts):
        t = x.astype(BF16)
        out.append(t)
        x = x - t.astype(F32)
    return out


def _seg_sum_quads(x, bd):
    outs = []
    for q in range(x.shape[1] // WKV_W):
        hi, lo = _split_bf16(x[:, q * WKV_W:(q + 1) * WKV_W], 2)
        outs.append(_dot(hi, bd) + _dot(lo, bd))
    return jnp.concatenate(outs, axis=1) if len(outs) > 1 else outs[0]


def _rwkv_tokens(zr, zk, zv, zl, pr, pk, pv, plo, prm, mu_l, w2, a2, g2, seg):
    r = zr + (pr - zr) * prm[0:1]
    kx = zk + (pk - zk) * prm[1:2]
    v = zv + (pv - zv) * prm[2:3]
    zsl = zl + (plo - zl) * mu_l
    tw = jnp.tanh(zsl[:, 0:128]).astype(BF16)
    xw = prm[3:4] + _dot(tw, w2)
    lw = -EXP_M05 * _sigmoid(xw)
    a = _sigmoid(prm[4:5] + _dot(zsl[:, 128:256].astype(BF16), a2))
    g = _dot(_sigmoid(zsl[:, 256:512]).astype(BF16), g2)
    kk = kx * prm[5:6]
    n2 = seg(kk * kk)
    kkn = kk / jnp.maximum(jnp.sqrt(n2), 1e-12)
    kmod = kx * (1.0 + (a - 1.0) * prm[6:7])
    return r, lw, kmod, v, kkn, a, g


def _rwkv_post(y, r, kmod, v, g, prm, seg):
    inv_n = 1.0 / RW_HEAD_DIM
    mean = seg(y) * inv_n
    d = y - mean
    var = seg(d * d) * inv_n
    yn = d * lax.rsqrt(var + RW_GN_EPS) * prm[8:9] + prm[9:10]
    bonus = seg(r * kmod * prm[7:8]) * v
    return (yn + bonus) * g


WKV_C = 64
WKV_HQ = 4
WKV_W = WKV_HQ * RW_HEAD_DIM
WKV_QPS = 8


def _wkv_chunk_body(zr_ref, zk_ref, zv_ref, zl_ref, prm_ref, mul_ref, w2_ref, a2_ref, g2_ref,
                    sel_ref, o_ref, so_ref, s_ref, cr_ref, ck_ref, cv_ref, cl_ref):
    c = pl.program_id(2)
    C = WKV_C
    W = WKV_W

    @pl.when(c == 0)
    def _():
        s_ref[...] = jnp.zeros_like(s_ref)
        cr_ref[...] = jnp.zeros_like(cr_ref)
        ck_ref[...] = jnp.zeros_like(ck_ref)
        cv_ref[...] = jnp.zeros_like(cv_ref)
        cl_ref[...] = jnp.zeros_like(cl_ref)

    rows = lax.broadcasted_iota(I32, (C, 1), 0)

    def shifted(z, carry_ref):
        prev = jnp.where(rows == 0, carry_ref[0:1, :], pltpu.roll(z, 1, 0))
        carry_ref[0:1, :] = z[C - 1:C, :]
        return prev

    zr, zk, zv, zl = zr_ref[...], zk_ref[...], zv_ref[...], zl_ref[...]
    pr, pk, pv, plo = shifted(zr, cr_ref), shifted(zk, ck_ref), shifted(zv, cv_ref), shifted(zl, cl_ref)
    prm = prm_ref[...]
    lane_head = lax.broadcasted_iota(I32, (1, W), 1) // RW_HEAD_DIM
    hv = lax.broadcasted_iota(I32, (W, 1), 0) // RW_HEAD_DIM
    bd = (hv == lane_head).astype(BF16)
    seg = lambda x: _seg_sum_quads(x, bd)
    r, lw, kmod, v, kkn, a, g = _rwkv_tokens(zr, zk, zv, zl, pr, pk, pv, plo, prm, mul_ref[0:1, :],
                                            w2_ref[...], a2_ref[...], g2_ref[...], seg)
    al = -kkn
    be = kkn * a

    ti = lax.broadcasted_iota(I32, (C, C), 0)
    tj = lax.broadcasted_iota(I32, (C, C), 1)
    tri = (tj <= ti).astype(BF16)
    cum = sum(_dot(tri, part) for part in _split_bf16(lw, 3))
    cum_l = cum[C - 1:C, :]
    p_inv = jnp.exp(-cum)
    p_rel = jnp.exp(cum_l - cum)
    ab = al * jnp.exp(cum - lw)
    rb = r * jnp.exp(cum)
    bt = (be * p_inv).astype(BF16)
    kt = (kmod * p_inv).astype(BF16)
    bk = jnp.concatenate([be * p_rel, kmod * p_rel], axis=0).astype(BF16)
    ar = jnp.concatenate([ab, rb], axis=0)
    pc = jnp.exp(cum_l)

    n4 = WKV_HQ * C
    bi = lax.broadcasted_iota(I32, (n4, n4), 0)
    bj = lax.broadcasted_iota(I32, (n4, n4), 1)
    same = (bi // C) == (bj // C)
    tri_s4 = same & ((bj % C) < (bi % C))
    tri_i4 = same & ((bj % C) <= (bi % C))
    eye4 = (bi == bj).astype(F32)
    masks = [lane_head == j for j in range(WKV_HQ)]

    def stack(x):
        return jnp.concatenate([jnp.where(m, x, jnp.zeros_like(x)) for m in masks], axis=0)

    def block_sum(x):
        return sum(x[j * C:(j + 1) * C] for j in range(WKV_HQ))

    def bdot(a, b):
        return _dot(a.astype(BF16), b.astype(BF16))

    qs = []
    for q in range(WKV_QPS):
        sl = slice(q * W, (q + 1) * W)
        v_q = v[:, sl]
        lhs = jnp.concatenate([stack(ab[:, sl]), stack(rb[:, sl])], axis=0).astype(BF16)
        abr = _dot_nt(lhs, stack(bt[:, sl]))
        akr = _dot_nt(lhs, stack(kt[:, sl]))
        qs.append(dict(sl=sl, v=v_q, v_s=stack(v_q), s0=s_ref[q],
                       x=jnp.where(tri_s4, abr[0:n4], 0.0), a_rb=jnp.where(tri_i4, abr[n4:2 * n4], 0.0),
                       a_ak=jnp.where(tri_s4, akr[0:n4], 0.0), a_rk=jnp.where(tri_i4, akr[n4:2 * n4], 0.0)))
    for d in qs:
        d['pw'] = [d['x']]
    for _ in range(5):
        for d in qs:
            d['pw'].append(bdot(d['pw'][-1], d['pw'][-1]))
    for d in qs:
        pw = d['pw']
        pr_ = [eye4 + pw[2 * i] + pw[2 * i + 1] + bdot(pw[2 * i], pw[2 * i + 1]) for i in range(3)]
        d['t'] = bdot(bdot(pr_[0], pr_[1]), pr_[2])
    for d in qs:
        gs = _dot_nt(ar[:, d['sl']].astype(BF16), d['s0'].astype(BF16))
        d['g_r'] = gs[C:2 * C]
        d['w_s'] = stack(gs[0:C]) + bdot(d['a_ak'], d['v_s'])
    for d in qs:
        d['u_s'] = bdot(d['t'], d['w_s'])
    ys = []
    for q, d in enumerate(qs):
        yv = bdot(jnp.concatenate([d['a_rb'], d['a_rk']], axis=1), jnp.concatenate([d['u_s'], d['v_s']], axis=0))
        ys.append(d['g_r'] + block_sum(yv))
        uv_t = jnp.concatenate([block_sum(d['u_s']), d['v']], axis=0).T.astype(BF16)
        upd = _dot(uv_t, bk[:, d['sl']])
        s_ref[q] = d['s0'] * pc[:, d['sl']] + jnp.where(hv == lane_head, upd, 0.0)

    y = jnp.concatenate(ys, axis=1) if WKV_QPS > 1 else ys[0]
    o_ref[...] = _rwkv_post(y, r, kmod, v, g, prm, seg).astype(o_ref.dtype)

    @pl.when(c == pl.num_programs(2) - 1)
    def _():
        for q in range(WKV_QPS):
            for j in range(WKV_HQ):
                rows_j = s_ref[q, j * RW_HEAD_DIM:(j + 1) * RW_HEAD_DIM, :]
                so_ref[0, q * WKV_HQ + j] = _dot(rows_j, sel_ref[j], HIGHEST)


def _wkv_prompt(z, prm, mu_l, w2, a2, g2, sel, batch, seq):
    nc = seq // WKV_C
    WS = WKV_W * WKV_QPS
    nq = RW_WIDTH // WS
    row = lambda b, q, c: b * nc + c
    return pl.pallas_call(
        _wkv_chunk_body,
        grid=(batch, nq, nc),
        in_specs=[pl.BlockSpec((WKV_C, WS), lambda b, q, c: (row(b, q, c), C_R // WS + q)),
                  pl.BlockSpec((WKV_C, WS), lambda b, q, c: (row(b, q, c), C_K // WS + q)),
                  pl.BlockSpec((WKV_C, WS), lambda b, q, c: (row(b, q, c), C_V // WS + q)),
                  pl.BlockSpec((WKV_C, 512), lambda b, q, c: (row(b, q, c), C_LORA // 512)),
                  pl.BlockSpec((16, WS), lambda b, q, c: (0, q)),
                  pl.BlockSpec((8, 512), lambda b, q, c: (0, 0)),
                  pl.BlockSpec((128, WS), lambda b, q, c: (0, q)),
                  pl.BlockSpec((128, WS), lambda b, q, c: (0, q)),
                  pl.BlockSpec((256, WS), lambda b, q, c: (0, q)),
                  pl.BlockSpec((WKV_HQ, WKV_W, RW_HEAD_DIM), lambda b, q, c: (0, 0, 0))],
        out_specs=[pl.BlockSpec((WKV_C, WS), lambda b, q, c: (row(b, q, c), q)),
                   pl.BlockSpec((1, WKV_HQ * WKV_QPS, RW_HEAD_DIM, RW_HEAD_DIM), lambda b, q, c: (b, q, 0, 0))],
        out_shape=[jax.ShapeDtypeStruct((batch * seq, RW_WIDTH), BF16),
                   jax.ShapeDtypeStruct((batch, RW_HEADS, RW_HEAD_DIM, RW_HEAD_DIM), F32)],
        scratch_shapes=[pltpu.VMEM((WKV_QPS, WKV_W, WKV_W), F32), pltpu.VMEM((8, WS), F32), pltpu.VMEM((8, WS), F32),
                        pltpu.VMEM((8, WS), F32), pltpu.VMEM((8, 512), F32)],
        compiler_params=_cparams(("parallel", "parallel", "arbitrary")),
        name="wkv_prompt",
    )(z, z, z, z, prm, mu_l, w2, a2, g2, sel)


def _wkv_tok_body(zr_ref, zk_ref, zv_ref, zl_ref, pr_ref, pk_ref, pv_ref, pl_ref, prm_ref, mul_ref, w2_ref, a2_ref,
                  g2_ref, ind_ref, indt_ref, r_ref, w_ref, al_ref, be_ref, km_ref, v_ref, g_ref, bo_ref):
    prm = prm_ref[...]
    ind, ind_t = ind_ref[...], indt_ref[...]
    seg = lambda x: _seg_sum(x, ind, ind_t)
    r, lw, kmod, v, kkn, a, g = _rwkv_tokens(zr_ref[...], zk_ref[...], zv_ref[...], zl_ref[...], pr_ref[...],
                                            pk_ref[...], pv_ref[...], pl_ref[...], prm, mul_ref[0:1, :],
                                            w2_ref[...], a2_ref[...], g2_ref[...], seg)
    r_ref[...] = r
    w_ref[...] = jnp.exp(lw)
    al_ref[...] = -kkn
    be_ref[...] = kkn * a
    km_ref[...] = kmod
    v_ref[...] = v
    g_ref[...] = g
    bo_ref[...] = seg(r * kmod * prm[7:8])


def _wkv_tokens_sample(z, prev_r, prev_k, prev_v, prev_l, prm, mu_l, w2, a2, g2, ind, ind_t, row0, n):
    rb = row0 // n
    full = lambda a: pl.BlockSpec(a.shape, lambda i: (0,) * a.ndim)
    zspec = lambda w, c0: pl.BlockSpec((n, w), lambda i: (rb, c0 // w))
    return pl.pallas_call(
        _wkv_tok_body,
        grid=(1,),
        in_specs=[zspec(RW_WIDTH, C_R), zspec(RW_WIDTH, C_K), zspec(RW_WIDTH, C_V), zspec(512, C_LORA),
                  full(prev_r), full(prev_k), full(prev_v), full(prev_l), full(prm), full(mu_l), full(w2), full(a2),
                  full(g2), full(ind), full(ind_t)],
        out_specs=[pl.BlockSpec((n, RW_WIDTH), lambda i: (0, 0))] * 8,
        out_shape=[jax.ShapeDtypeStruct((n, RW_WIDTH), F32)] * 8,
        compiler_params=_cparams(("arbitrary",)),
        name="wkv_tokens_sample",
    )(z, z, z, z, prev_r, prev_k, prev_v, prev_l, prm, mu_l, w2, a2, g2, ind, ind_t)


def _wkv_step_body(s_ref, w_ref, al_ref, be_ref, km_ref, r_ref, v_ref, g_ref, bo_ref, lnw_ref, lnb_ref, o_ref, so_ref):
    s = s_ref[...]
    vcol = v_ref[...]
    sa = jnp.sum(s * al_ref[...], axis=-1, keepdims=True)
    s2 = s * w_ref[...] + sa * be_ref[...] + vcol * km_ref[...]
    so_ref[...] = s2
    y = jnp.sum(s2 * r_ref[...], axis=-1, keepdims=True)
    mean = jnp.mean(y, axis=2, keepdims=True)
    d = y - mean
    var = jnp.mean(d * d, axis=2, keepdims=True)
    yn = d * lax.rsqrt(var + RW_GN_EPS) * lnw_ref[...] + lnb_ref[...]
    o_ref[...] = (yn + bo_ref[...] * vcol) * g_ref[...]


def _wkv_step(state, w, al, be, km, r, v, g, bo, lnw, lnb):
    n, h = state.shape[0], state.shape[1]
    hq = 2
    rowspec = pl.BlockSpec((n, hq, 1, RW_HEAD_DIM), lambda q: (0, q, 0, 0))
    colspec = pl.BlockSpec((n, hq, RW_HEAD_DIM, 1), lambda q: (0, q, 0, 0))
    pcol = pl.BlockSpec((1, hq, RW_HEAD_DIM, 1), lambda q: (0, q, 0, 0))
    sspec = pl.BlockSpec((n, hq, RW_HEAD_DIM, RW_HEAD_DIM), lambda q: (0, q, 0, 0))
    return pl.pallas_call(
        _wkv_step_body,
        grid=(h // hq,),
        in_specs=[sspec, rowspec, rowspec, rowspec, rowspec, rowspec, colspec, colspec, colspec, pcol, pcol],
        out_specs=[colspec, sspec],
        out_shape=[jax.ShapeDtypeStruct((n, h, RW_HEAD_DIM, 1), F32), jax.ShapeDtypeStruct(state.shape, F32)],
        compiler_params=_cparams(("parallel",)),
        name="wkv_step",
    )(state, w, al, be, km, r, v, g, bo, lnw, lnb)


def _rope_tables(pos, rot_dim, period):
    half = rot_dim // 2
    t = pos.shape[0]
    inv_freq = ROPE_THETA ** (-jnp.arange(half, dtype=F32) / half)
    ang = pos.astype(F32)[:, None] * inv_freq[None, :]
    cos, sin = jnp.cos(ang), jnp.sin(ang)
    zh = jnp.zeros((t, half), F32)
    rest = period - rot_dim
    c = jnp.concatenate([cos, cos, jnp.ones((t, rest), F32)], axis=1)
    s1 = jnp.concatenate([-sin, zh, jnp.zeros((t, rest), F32)], axis=1)
    s2 = jnp.concatenate([zh, sin, jnp.zeros((t, rest), F32)], axis=1)
    rep = LANES // period
    return jnp.stack([jnp.tile(a, (1, rep)) for a in (c, s1, s2)], axis=0)


def _rope(x, tab, half):
    n = x.shape[1]
    rep = n // LANES
    c, s1, s2 = [jnp.tile(tab[i], (1, rep)) if rep > 1 else tab[i] for i in range(3)]
    return x * c + pltpu.roll(x, n - half, 1) * s1 + pltpu.roll(x, half, 1) * s2


def _prep_body(q_ref, ka_ref, va_ref, iq_ref, ikw_ref, ta_ref, ti_ref, lnw_ref, lnb_ref,
               qo_ref, ko_ref, kb_ref, vb_ref, qio_ref, kio_ref, kid_ref):
    ta = ta_ref[...]
    ti = ti_ref[...]
    qo_ref[...] = _rope(q_ref[...], ta, ROT_DIM // 2).astype(BF16)
    k = _rope(ka_ref[...], ta, ROT_DIM // 2)
    ko_ref[...] = k
    kb_ref[...] = k.astype(BF16)
    vb_ref[...] = va_ref[...].astype(BF16)
    qio_ref[...] = _rope(iq_ref[...], ti, IDX_ROT_DIM // 2).astype(BF16)
    x = ikw_ref[...]
    lane = lax.broadcasted_iota(I32, x.shape, 1)
    is_k = lane < IDX_DIM
    mu = jnp.sum(jnp.where(is_k, x, 0.0), axis=-1, keepdims=True) * (1.0 / IDX_DIM)
    d = jnp.where(is_k, x - mu, 0.0)
    var = jnp.sum(d * d, axis=-1, keepdims=True) * (1.0 / IDX_DIM)
    kn = d * lax.rsqrt(var + LN_EPS) * lnw_ref[...] + lnb_ref[...]
    kr = _rope(kn, ti, IDX_ROT_DIM // 2)
    kr = jnp.where(is_k, kr, 0.0)
    kio_ref[...] = jnp.where(is_k, kr, x * (IDX_HEADS ** -0.5))
    kid_ref[...] = (kr + pltpu.roll(kr, IDX_DIM, 1)).astype(BF16)


def _prep(z, tab_a, tab_i, ln_w, ln_b, tm):
    m = z.shape[0]
    row = lambda w, c0: pl.BlockSpec((tm, w), lambda i: (i, c0 // w))
    outs = [(ATT_WIDTH, BF16), (KV_WIDTH, F32), (KV_WIDTH, BF16), (KV_WIDTH, BF16),
            (IDX_HEADS * IDX_DIM, BF16), (LANES, F32), (LANES, BF16)]
    return pl.pallas_call(
        _prep_body,
        grid=(m // tm,),
        in_specs=[row(ATT_WIDTH, C_Q), row(KV_WIDTH, C_KA), row(KV_WIDTH, C_VA), row(IDX_HEADS * IDX_DIM, C_IQ),
                  row(LANES, C_IKW),
                  pl.BlockSpec((3, tm, LANES), lambda i: (0, i, 0)),
                  pl.BlockSpec((3, tm, LANES), lambda i: (0, i, 0)),
                  pl.BlockSpec((1, LANES), lambda i: (0, 0)),
                  pl.BlockSpec((1, LANES), lambda i: (0, 0))],
        out_specs=[pl.BlockSpec((tm, w), lambda i: (i, 0)) for w, _ in outs],
        out_shape=[jax.ShapeDtypeStruct((m, w), dt) for w, dt in outs],
        compiler_params=_cparams(("parallel",)),
        name="prep",
    )(z, z, z, z, z, tab_a, tab_i, ln_w, ln_b)


DSA_QB = 128
DSA_TK = 512
DSA_ATK = 256


def _float_key(s):
    b = pltpu.bitcast(s, I32)
    return b ^ ((b >> 31) & 0x7FFFFFFF)


def _kth_threshold(count_ge, topk):
    def step(b, thr):
        cand = thr + jnp.left_shift(jnp.int32(1), 31 - b)
        return jnp.where(count_ge(cand) >= topk, cand, thr)
    return lax.fori_loop(0, 32, step, jnp.full((DSA_QB, 1), INT_MIN, I32))


def _dsa_prompt_body(q_ref, kb_ref, vb_ref, qi_ref, kid_ref, kiw_ref, o_ref, keys_ref, jcut_ref, m_ref, l_ref,
                     acc_ref, qs_ref, *, topk, seq):
    i = pl.program_id(1)
    nt = (i * DSA_QB + DSA_QB + DSA_TK - 1) // DSA_TK
    qpos = i * DSA_QB + lax.broadcasted_iota(I32, (DSA_QB, 1), 0)
    col0 = lax.broadcasted_iota(I32, (1, DSA_TK), 1)
    lane = lax.broadcasted_iota(I32, (1, LANES), 1)
    kiw = kiw_ref[...]

    def score_tile(t, carry):
        kd = kid_ref[pl.ds(t * DSA_TK, DSA_TK), :]
        s = jnp.zeros((DSA_QB, DSA_TK), F32)
        for h in range(IDX_HEADS):
            qt = qi_ref[:, (h // 2) * LANES:(h // 2 + 1) * LANES]
            qh = jnp.where((lane // IDX_DIM) == (h % 2), qt, jnp.zeros_like(qt))
            d = _dot_nt(qh, kd) * (IDX_DIM ** -0.5)
            s = s + kiw[:, IDX_DIM + h:IDX_DIM + h + 1] * jnp.maximum(d, 0.0)
        valid = (t * DSA_TK + col0) <= qpos
        keys_ref[:, pl.ds(t * DSA_TK, DSA_TK)] = jnp.where(valid, _float_key(s), INT_MIN)
        return carry

    lax.fori_loop(0, nt, score_tile, 0)

    def count(pred):
        def body(t, acc):
            kt = keys_ref[:, pl.ds(t * DSA_TK, DSA_TK)]
            hit = pred(kt, t * DSA_TK + col0).astype(I32)
            for c in range(DSA_TK // LANES):
                acc = acc + hit[:, c * LANES:(c + 1) * LANES]
            return acc
        acc = lax.fori_loop(0, nt, body, jnp.zeros((DSA_QB, LANES), I32))
        return jnp.sum(acc, axis=-1, keepdims=True)

    thr = _kth_threshold(lambda c: count(lambda kt, col: kt >= c), topk)
    n_gt = count(lambda kt, col: kt > thr)
    n_eq = count(lambda kt, col: (kt == thr) & (col <= qpos))
    need = topk - n_gt
    jcut_ref[...] = jnp.full(jcut_ref.shape, seq, I32)
    excess = (n_eq > need) & (thr > INT_MIN)

    @pl.when(jnp.max(excess.astype(I32)) > 0)
    def _():
        def step(b, jm):
            cand = jm + jnp.left_shift(jnp.int32(1), 30 - b)
            c = count(lambda kt, col: (kt == thr) & (col <= qpos) & (col < cand))
            return jnp.where(c < need, cand, jm)
        jm = lax.fori_loop(0, 31, step, jnp.zeros((DSA_QB, 1), I32))
        jcut_ref[...] = jnp.broadcast_to(jnp.where(excess, jm, seq), jcut_ref.shape)

    jcut = jcut_ref[:, 0:1]
    scale = HEAD_DIM ** -0.5

    m_ref[...] = jnp.full(m_ref.shape, -1e29, F32)
    l_ref[...] = jnp.zeros_like(l_ref)
    acc_ref[...] = jnp.zeros_like(acc_ref)
    for h in range(N_HEADS):
        qs_ref[h // GROUP, (h % GROUP) * DSA_QB:(h % GROUP + 1) * DSA_QB, :] = q_ref[:, h * HEAD_DIM:(h + 1) * HEAD_DIM]
    rep = DSA_ATK // LANES
    nta = (i * DSA_QB + DSA_QB + DSA_ATK - 1) // DSA_ATK
    cola0 = lax.broadcasted_iota(I32, (1, DSA_ATK), 1)

    def att_tile(t, carry):
        kt = keys_ref[:, pl.ds(t * DSA_ATK, DSA_ATK)]
        col = t * DSA_ATK + cola0
        sel = ((kt > thr) | ((kt == thr) & (col <= jcut))) & (col <= qpos)
        sel4 = jnp.concatenate([sel] * GROUP, axis=0)
        for g in range(N_KV_HEADS):
            k_t = kb_ref[pl.ds(t * DSA_ATK, DSA_ATK), g * HEAD_DIM:(g + 1) * HEAD_DIM]
            v_t = vb_ref[pl.ds(t * DSA_ATK, DSA_ATK), g * HEAD_DIM:(g + 1) * HEAD_DIM]
            s = jnp.where(sel4, _dot_nt(qs_ref[g], k_t) * scale, -1e30)
            m_old = m_ref[g]
            m_new = jnp.maximum(m_old, jnp.max(s, axis=-1, keepdims=True))
            p = jnp.exp(s - jnp.tile(m_new, (1, rep)))
            corr = jnp.exp(m_old - m_new)
            m_ref[g] = m_new
            l_ref[g] = l_ref[g] * corr + jnp.sum(p, axis=-1, keepdims=True)
            acc_ref[g] = acc_ref[g] * corr + _dot(p.astype(BF16), v_t)
        return carry

    lax.fori_loop(0, nta, att_tile, 0)
    for h in range(N_HEADS):
        rows = slice((h % GROUP) * DSA_QB, (h % GROUP + 1) * DSA_QB)
        o_ref[:, h * HEAD_DIM:(h + 1) * HEAD_DIM] = (acc_ref[h // GROUP, rows, :] / l_ref[h // GROUP, rows, :]).astype(o_ref.dtype)


def _dsa_prompt(qb, kb, vb, qib, kid, kiw, batch, seq, topk):
    nb = seq // DSA_QB
    body = functools.partial(_dsa_prompt_body, topk=topk, seq=seq)
    return pl.pallas_call(
        body,
        grid=(batch, nb),
        in_specs=[pl.BlockSpec((DSA_QB, ATT_WIDTH), lambda b, i: (b * nb + i, 0)),
                  pl.BlockSpec((seq, KV_WIDTH), lambda b, i: (b, 0)),
                  pl.BlockSpec((seq, KV_WIDTH), lambda b, i: (b, 0)),
                  pl.BlockSpec((DSA_QB, IDX_HEADS * IDX_DIM), lambda b, i: (b * nb + i, 0)),
                  pl.BlockSpec((seq, LANES), lambda b, i: (b, 0)),
                  pl.BlockSpec((DSA_QB, LANES), lambda b, i: (b * nb + i, 0))],
        out_specs=pl.BlockSpec((DSA_QB, ATT_WIDTH), lambda b, i: (b * nb + i, 0)),
        out_shape=jax.ShapeDtypeStruct((batch * seq, ATT_WIDTH), BF16),
        scratch_shapes=[pltpu.VMEM((DSA_QB, seq), I32), pltpu.VMEM((DSA_QB, LANES), I32),
                        pltpu.VMEM((N_KV_HEADS, GROUP * DSA_QB, LANES), F32),
                        pltpu.VMEM((N_KV_HEADS, GROUP * DSA_QB, LANES), F32),
                        pltpu.VMEM((N_KV_HEADS, GROUP * DSA_QB, HEAD_DIM), F32),
                        pltpu.VMEM((N_KV_HEADS, GROUP * DSA_QB, HEAD_DIM), BF16)],
        compiler_params=_cparams(("parallel", "arbitrary")),
        name="dsa_prompt",
    )(qb, kb, vb, qib, kid, kiw)


def _sel_sample_body(pt_ref, qi_ref, wi_ref, ks_ref, ck_hbm, m_ref, ms_ref, kbuf, sem, *, topk, npg):
    s = pl.program_id(0)
    slot = s % 2

    def page_copy(seq, p, sl):
        return pltpu.make_async_copy(ck_hbm.at[pt_ref[seq, p]], kbuf.at[sl, p], sem.at[sl])

    @pl.when(s == 0)
    def _():
        for p in range(npg):
            page_copy(0, p, 0).start()

    @pl.when(s + 1 < pl.num_programs(0))
    def _():
        for p in range(npg):
            page_copy(s + 1, p, 1 - slot).start()

    for p in range(npg):
        page_copy(s, p, slot).wait()

    kp = kbuf[slot].reshape(npg * PAGE_SIZE, IDX_DIM).astype(BF16)
    dp = _dot_nt(qi_ref[0], kp) * (IDX_DIM ** -0.5)
    keys = _float_key(jnp.sum(wi_ref[0] * jnp.maximum(dp, 0.0), axis=0, keepdims=True))
    d = jnp.sum(qi_ref[0].astype(F32) * ks_ref[0].astype(F32), axis=-1, keepdims=True) * (IDX_DIM ** -0.5)
    s_self = jnp.sum(wi_ref[0] * jnp.maximum(d, 0.0), axis=0, keepdims=True)
    k_self = _float_key(s_self)
    pos = lax.broadcasted_iota(I32, keys.shape, 1)

    def total(x):
        return jnp.sum(x.astype(I32), axis=1, keepdims=True)

    def step(b, thr):
        cand = thr + jnp.left_shift(jnp.int32(1), 31 - b)
        c = total(keys >= cand) + (k_self >= cand).astype(I32)
        return jnp.where(c >= topk, cand, thr)

    thr = lax.fori_loop(0, 32, step, jnp.full((1, 1), INT_MIN, I32))
    need = topk - total(keys > thr) - (k_self > thr).astype(I32)
    eq = keys == thr

    def jstep(b, jm):
        cand = jm + jnp.left_shift(jnp.int32(1), 30 - b)
        return jnp.where(total(eq & (pos < cand)) < need, cand, jm)

    jm = lax.fori_loop(0, 31, jstep, jnp.zeros((1, 1), I32))
    n_eq_kept = total(eq & (pos <= jm))
    m_ref[0] = ((keys > thr) | (eq & (pos <= jm))).astype(F32)
    self_sel = (k_self > thr) | ((k_self == thr) & (n_eq_kept < need))
    ms_ref[0] = jnp.broadcast_to(self_sel.astype(F32), (1, LANES))


def _sel_sample(page_table, qi, wi, kself, cache_ki, topk):
    n, npg = page_table.shape
    past = npg * PAGE_SIZE
    grid_spec = pltpu.PrefetchScalarGridSpec(
        num_scalar_prefetch=1,
        grid=(n,),
        in_specs=[pl.BlockSpec((1, IDX_HEADS, IDX_DIM), lambda s, pt: (s, 0, 0)),
                  pl.BlockSpec((1, IDX_HEADS, 1), lambda s, pt: (s, 0, 0)),
                  pl.BlockSpec((1, 1, IDX_DIM), lambda s, pt: (s, 0, 0)),
                  pl.BlockSpec(memory_space=pl.ANY)],
        out_specs=[pl.BlockSpec((1, 1, past), lambda s, pt: (s, 0, 0)),
                   pl.BlockSpec((1, 1, LANES), lambda s, pt: (s, 0, 0))],
        scratch_shapes=[pltpu.VMEM((2, npg, PAGE_SIZE, IDX_DIM), F32), pltpu.SemaphoreType.DMA((2,))],
    )
    return pl.pallas_call(
        functools.partial(_sel_sample_body, topk=topk, npg=npg),
        grid_spec=grid_spec,
        out_shape=[jax.ShapeDtypeStruct((n, 1, past), F32), jax.ShapeDtypeStruct((n, 1, LANES), F32)],
        compiler_params=_cparams(("arbitrary",)),
        name="sel_sample",
    )(page_table, qi, wi, kself, cache_ki)


ATT_CH = 4


def _att_pages_body(pt_ref, q_ref, m_ref, ms_ref, ks_ref, vs_ref, ck_hbm, cv_hbm, o_ref, kbuf, vbuf, sem, *, npg):
    s = pl.program_id(0)
    nch = npg // ATT_CH
    scale = HEAD_DIM ** -0.5

    def copies(seq, c, sl):
        out = []
        for i in range(ATT_CH):
            pg = pt_ref[seq, c * ATT_CH + i]
            out.append(pltpu.make_async_copy(ck_hbm.at[pg], kbuf.at[sl, i], sem.at[0, sl]))
            out.append(pltpu.make_async_copy(cv_hbm.at[pg], vbuf.at[sl, i], sem.at[1, sl]))
        return out

    @pl.when(s == 0)
    def _():
        for cp in copies(0, 0, 0):
            cp.start()

    q = q_ref[0]
    row_g = lax.broadcasted_iota(I32, (N_HEADS, 1), 0) // GROUP
    lane_g = lax.broadcasted_iota(I32, (1, KV_WIDTH), 1) // HEAD_DIM
    q_bd = jnp.where(row_g == lane_g, jnp.tile(q, (1, N_KV_HEADS)), jnp.zeros((N_HEADS, KV_WIDTH), BF16))

    def page_2d(buf, sl, i):
        return jnp.concatenate([buf[sl, i, :, g, :] for g in range(N_KV_HEADS)], axis=1).astype(BF16)

    def chunk(c, carry):
        m, l, acc = carry
        sl = c % 2

        @pl.when(c + 1 < nch)
        def _():
            for cp in copies(s, c + 1, 1 - sl):
                cp.start()

        @pl.when(jnp.logical_and(c + 1 == nch, s + 1 < pl.num_programs(0)))
        def _():
            for cp in copies(s + 1, 0, 0):
                cp.start()

        for cp in copies(s, c, sl):
            cp.wait()
        for i in range(ATT_CH):
            kp = page_2d(kbuf, sl, i)
            vp = page_2d(vbuf, sl, i)
            pos0 = pl.multiple_of((c * ATT_CH + i) * PAGE_SIZE, PAGE_SIZE)
            sel = m_ref[0, :, pl.ds(pos0, PAGE_SIZE)] > 0.5
            sc = jnp.where(sel, _dot_nt(q_bd, kp) * scale, -1e30)
            m_new = jnp.maximum(m, jnp.max(sc, axis=-1, keepdims=True))
            pr = jnp.exp(sc - m_new)
            corr = jnp.exp(m - m_new)
            l = l * corr + jnp.sum(pr, axis=-1, keepdims=True)
            pv = _dot(pr.astype(BF16), vp)
            own = jnp.zeros((N_HEADS, HEAD_DIM), F32)
            for g in range(N_KV_HEADS):
                own = jnp.where(row_g == g, pv[:, g * HEAD_DIM:(g + 1) * HEAD_DIM], own)
            acc = acc * corr + own
            m = m_new
        return m, l, acc

    init = (jnp.full((N_HEADS, LANES), -1e29, F32), jnp.zeros((N_HEADS, LANES), F32),
            jnp.zeros((N_HEADS, HEAD_DIM), F32))
    m, l, acc = lax.fori_loop(0, nch, chunk, init)

    ssel = ms_ref[0][:, 0:1] > 0.5
    s1 = jnp.sum(q.astype(F32) * ks_ref[0].astype(F32), axis=-1, keepdims=True) * scale
    s1 = jnp.where(ssel, s1, -1e30)
    m2 = jnp.maximum(m, s1)
    p1 = jnp.exp(s1 - m2)
    c2 = jnp.exp(m - m2)
    l2 = l * c2 + p1
    a2 = acc * c2 + p1.astype(BF16).astype(F32) * vs_ref[0].astype(F32)
    o_ref[0] = (a2 / l2).astype(o_ref.dtype)


def _att_pages(page_table, q, mask, mself, kself, vself, cache_k, cache_v):
    n, npg = page_table.shape
    assert npg % (2 * ATT_CH) == 0
    seqspec = lambda r, c: pl.BlockSpec((1, r, c), lambda s, pt: (s, 0, 0))
    anyspec = pl.BlockSpec(memory_space=pl.ANY)
    pages = (2, ATT_CH, PAGE_SIZE, N_KV_HEADS, HEAD_DIM)
    grid_spec = pltpu.PrefetchScalarGridSpec(
        num_scalar_prefetch=1,
        grid=(n,),
        in_specs=[seqspec(N_HEADS, HEAD_DIM), seqspec(1, npg * PAGE_SIZE), seqspec(1, LANES),
                  seqspec(N_HEADS, HEAD_DIM), seqspec(N_HEADS, HEAD_DIM), anyspec, anyspec],
        out_specs=seqspec(N_HEADS, HEAD_DIM),
        scratch_shapes=[pltpu.VMEM(pages, F32), pltpu.VMEM(pages, F32), pltpu.SemaphoreType.DMA((2, 2))],
    )
    return pl.pallas_call(
        functools.partial(_att_pages_body, npg=npg),
        grid_spec=grid_spec,
        out_shape=jax.ShapeDtypeStruct((n, N_HEADS, HEAD_DIM), BF16),
        compiler_params=_cparams(("arbitrary",)),
        name="att_pages",
    )(page_table, q, mask, mself, kself, vself, cache_k, cache_v)


def _mem_att_prompt_body(q_ref, k_ref, v_ref, o_ref):
    scale = MEM_HEAD_DIM ** -0.5
    for h in range(MEM_HEADS):
        sl = slice(h * MEM_HEAD_DIM, (h + 1) * MEM_HEAD_DIM)
        s = _dot_nt(q_ref[:, sl], k_ref[:, sl]) * scale
        m = jnp.max(s, axis=-1, keepdims=True)
        e = jnp.exp(s - m)
        pr = e / jnp.sum(e, axis=-1, keepdims=True)
        o_ref[:, sl] = _dot(pr.astype(BF16), v_ref[:, sl]).astype(o_ref.dtype)


def _mem_att_prompt(mq, mk, mv, batch, seq, tq):
    m = mk.shape[0] // batch
    nb = seq // tq
    return pl.pallas_call(
        _mem_att_prompt_body,
        grid=(batch * nb,),
        in_specs=[pl.BlockSpec((tq, MEM_WIDTH), lambda i: (i, 0)),
                  pl.BlockSpec((m, MEM_WIDTH), lambda i: (i // nb, 0)),
                  pl.BlockSpec((m, MEM_WIDTH), lambda i: (i // nb, 0))],
        out_specs=pl.BlockSpec((tq, MEM_WIDTH), lambda i: (i, 0)),
        out_shape=jax.ShapeDtypeStruct((batch * seq, MEM_WIDTH), BF16),
        compiler_params=_cparams(("parallel",)),
        name="mem_att_prompt",
    )(mq, mk, mv)


def _mem_att_sample_body(q_ref, k_ref, v_ref, o_ref):
    scale = MEM_HEAD_DIM ** -0.5
    q = q_ref[0].astype(F32)
    prod = k_ref[0] * q
    for h in range(MEM_HEADS):
        sl = slice(h * MEM_HEAD_DIM, (h + 1) * MEM_HEAD_DIM)
        s = jnp.sum(prod[:, sl], axis=-1, keepdims=True) * scale
        m = jnp.max(s, axis=0, keepdims=True)
        e = jnp.exp(s - m)
        pr = e / jnp.sum(e, axis=0, keepdims=True)
        o_ref[0, :, sl] = jnp.sum(pr * v_ref[0][:, sl], axis=0, keepdims=True).astype(o_ref.dtype)


def _mem_att_sample(mq, mk, mv):
    n, m, w = mk.shape
    return pl.pallas_call(
        _mem_att_sample_body,
        grid=(n,),
        in_specs=[pl.BlockSpec((1, 1, w), lambda s: (s, 0, 0)),
                  pl.BlockSpec((1, m, w), lambda s: (s, 0, 0)),
                  pl.BlockSpec((1, m, w), lambda s: (s, 0, 0))],
        out_specs=pl.BlockSpec((1, 1, w), lambda s: (s, 0, 0)),
        out_shape=jax.ShapeDtypeStruct((n, 1, w), BF16),
        compiler_params=_cparams(("parallel",)),
        name="mem_att_sample",
    )(mq, mk, mv)


def _router_body(x_ref, w_ref, b_ref, ei_ref, ew_ref, acc_ref):
    k = pl.program_id(1)

    @pl.when(k == 0)
    def _():
        acc_ref[...] = jnp.zeros_like(acc_ref)

    acc_ref[...] += _dot(x_ref[...], w_ref[...], HIGHEST)

    @pl.when(k == pl.num_programs(1) - 1)
    def _():
        lg = acc_ref[...] + b_ref[...]
        lane = lax.broadcasted_iota(I32, lg.shape, 1)
        neg = jnp.float32(-jnp.inf)
        is_g = lane < N_GROUPS
        glm = jnp.where(is_g, lg, neg)
        gmax = jnp.max(glm, axis=-1, keepdims=True)
        g_sel = jnp.min(jnp.where(glm == gmax, lane, LANES), axis=-1, keepdims=True)
        g_prob = 1.0 / jnp.sum(jnp.where(is_g, jnp.exp(lg - gmax), 0.0), axis=-1, keepdims=True)
        e_id = lane - N_GROUPS
        in_grp = (e_id >= 0) & (e_id < N_EXPERTS) & ((e_id // EXPERTS_PER_GROUP) == g_sel)
        el = jnp.where(in_grp, lg, neg)
        m1 = jnp.max(el, axis=-1, keepdims=True)
        i1 = jnp.min(jnp.where(in_grp & (el == m1), lane, LANES), axis=-1, keepdims=True)
        rest = in_grp & (lane != i1)
        el2 = jnp.where(rest, lg, neg)
        m2 = jnp.max(el2, axis=-1, keepdims=True)
        i2 = jnp.min(jnp.where(rest & (el2 == m2), lane, LANES), axis=-1, keepdims=True)
        t = jnp.exp(m2 - m1)
        w1 = g_prob / (1.0 + t)
        w2 = g_prob * t / (1.0 + t)
        ei_ref[...] = jnp.where(lane == 0, i1 - N_GROUPS, jnp.where(lane == 1, i2 - N_GROUPS, 0))
        ew_ref[...] = jnp.where(lane == 0, w1, jnp.where(lane == 1, w2, 0.0))


def _router(x, w, b, tm, tk):
    m, kd = x.shape
    return pl.pallas_call(
        _router_body,
        grid=(m // tm, kd // tk),
        in_specs=[pl.BlockSpec((tm, tk), lambda i, k: (i, k)),
                  pl.BlockSpec((tk, LANES), lambda i, k: (k, 0)),
                  pl.BlockSpec((1, LANES), lambda i, k: (0, 0))],
        out_specs=[pl.BlockSpec((tm, LANES), lambda i, k: (i, 0)), pl.BlockSpec((tm, LANES), lambda i, k: (i, 0))],
        out_shape=[jax.ShapeDtypeStruct((m, LANES), I32), jax.ShapeDtypeStruct((m, LANES), F32)],
        scratch_shapes=[pltpu.VMEM((tm, LANES), F32)],
        compiler_params=_cparams(("parallel", "arbitrary")),
        name="router",
    )(x, w, b)


MOE_BR = 128


def _gather_body(idx_ref, nblk_ref, src_ref, o_ref, sem):
    i = pl.program_id(0)
    g = o_ref.shape[0]

    def row_copy(src_row, dst_row):
        return pltpu.make_async_copy(src_ref.at[pl.ds(src_row, 1)], o_ref.at[pl.ds(dst_row, 1)], sem)

    @pl.when(i < nblk_ref[0])
    def _():
        def issue(r, c):
            row_copy(idx_ref[i * g + r], r).start()
            return c
        lax.fori_loop(0, g, issue, 0, unroll=8)
        pltpu.make_async_copy(src_ref.at[pl.ds(0, g)], o_ref, sem).wait()

    @pl.when(i >= nblk_ref[0])
    def _():
        o_ref[...] = jnp.zeros_like(o_ref)


def _gather_rows(src, idx, nblk, g):
    m = idx.shape[0]
    d = src.shape[1]
    grid_spec = pltpu.PrefetchScalarGridSpec(
        num_scalar_prefetch=2,
        grid=(m // g,),
        in_specs=[pl.BlockSpec(memory_space=pl.ANY)],
        out_specs=pl.BlockSpec((g, d), lambda i, idx, nb: (i, 0)),
        scratch_shapes=[pltpu.SemaphoreType.DMA(())],
    )
    return pl.pallas_call(
        _gather_body,
        grid_spec=grid_spec,
        out_shape=jax.ShapeDtypeStruct((m, d), src.dtype),
        compiler_params=_cparams(("arbitrary",)),
        name="gather_rows",
    )(idx, nblk, src)


def _expert_up_body(be_ref, nblk_ref, x_ref, wg_ref, wu_ref, h_ref, wgb_ref, wub_ref):
    i = pl.program_id(1)
    changed = jnp.logical_or(i == 0, be_ref[i] != be_ref[jnp.maximum(i - 1, 0)])

    @pl.when(jnp.logical_and(i < nblk_ref[0], changed))
    def _():
        wgb_ref[...] = wg_ref[0].astype(BF16)
        wub_ref[...] = wu_ref[0].astype(BF16)

    @pl.when(i < nblk_ref[0])
    def _():
        x = x_ref[...].astype(BF16)
        a = _dot(x, wgb_ref[...])
        u = _dot(x, wub_ref[...])
        h_ref[...] = (a * _sigmoid(a) * u).astype(h_ref.dtype)

    @pl.when(i >= nblk_ref[0])
    def _():
        h_ref[...] = jnp.zeros_like(h_ref)


def _expert_up(block_e, nblk, xs, w_gate, w_up, th):
    nr, d = xs.shape
    nb = nr // MOE_BR
    nh = D_EXPERT // th
    blk = lambda i, nbk: jnp.minimum(i, nbk[0] - 1)
    grid_spec = pltpu.PrefetchScalarGridSpec(
        num_scalar_prefetch=2,
        grid=(nh, nb),
        in_specs=[pl.BlockSpec((MOE_BR, d), lambda j, i, be, nbk: (blk(i, nbk), 0)),
                  pl.BlockSpec((1, d, th), lambda j, i, be, nbk: (be[blk(i, nbk)], 0, j)),
                  pl.BlockSpec((1, d, th), lambda j, i, be, nbk: (be[blk(i, nbk)], 0, j))],
        out_specs=pl.BlockSpec((MOE_BR, th), lambda j, i, be, nbk: (i, j)),
        scratch_shapes=[pltpu.VMEM((d, th), BF16), pltpu.VMEM((d, th), BF16)],
    )
    return pl.pallas_call(
        _expert_up_body,
        grid_spec=grid_spec,
        out_shape=jax.ShapeDtypeStruct((nr, D_EXPERT), BF16),
        compiler_params=_cparams(("arbitrary", "arbitrary")),
        name="expert_up",
    )(block_e, nblk, xs, w_gate, w_up)


def _expert_down_body(be_ref, nblk_ref, h_ref, wd_ref, y_ref, wdb_ref):
    i = pl.program_id(0)
    changed = jnp.logical_or(i == 0, be_ref[i] != be_ref[jnp.maximum(i - 1, 0)])

    @pl.when(jnp.logical_and(i < nblk_ref[0], changed))
    def _():
        wdb_ref[...] = wd_ref[0].astype(BF16)

    @pl.when(i < nblk_ref[0])
    def _():
        y_ref[...] = _dot(h_ref[...], wdb_ref[...])

    @pl.when(i >= nblk_ref[0])
    def _():
        y_ref[...] = jnp.zeros_like(y_ref)


def _expert_down(block_e, nblk, h, w_down):
    nr = h.shape[0]
    d = w_down.shape[2]
    blk = lambda i, nbk: jnp.minimum(i, nbk[0] - 1)
    grid_spec = pltpu.PrefetchScalarGridSpec(
        num_scalar_prefetch=2,
        grid=(nr // MOE_BR,),
        in_specs=[pl.BlockSpec((MOE_BR, D_EXPERT), lambda i, be, nbk: (blk(i, nbk), 0)),
                  pl.BlockSpec((1, D_EXPERT, d), lambda i, be, nbk: (be[blk(i, nbk)], 0, 0))],
        out_specs=pl.BlockSpec((MOE_BR, d), lambda i, be, nbk: (i, 0)),
        scratch_shapes=[pltpu.VMEM((D_EXPERT, d), BF16)],
    )
    return pl.pallas_call(
        _expert_down_body,
        grid_spec=grid_spec,
        out_shape=jax.ShapeDtypeStruct((nr, d), F32),
        compiler_params=_cparams(("arbitrary",)),
        name="expert_down",
    )(block_e, nblk, h, w_down)


def _combine_ln_body(x_ref, y0_ref, y1_ref, ew_ref, g_ref, b_ref, o_ref):
    ew = ew_ref[...]
    ff = y0_ref[...] * ew[:, 0:1] + y1_ref[...] * ew[:, 1:2]
    o_ref[...] = _layer_norm_rows(DEEPNORM_ALPHA * x_ref[...] + ff, g_ref[...], b_ref[...])


def _combine_ln(x, yg, ew, g, b, tm):
    m, d = x.shape
    nb = m // tm
    return pl.pallas_call(
        _combine_ln_body,
        grid=(nb,),
        in_specs=[pl.BlockSpec((tm, d), lambda i: (i, 0)),
                  pl.BlockSpec((tm, d), lambda i: (i, 0)),
                  pl.BlockSpec((tm, d), lambda i: (nb + i, 0)),
                  pl.BlockSpec((tm, LANES), lambda i: (i, 0)),
                  pl.BlockSpec((1, d), lambda i: (0, 0)),
                  pl.BlockSpec((1, d), lambda i: (0, 0))],
        out_specs=pl.BlockSpec((tm, d), lambda i: (i, 0)),
        out_shape=jax.ShapeDtypeStruct((m, d), F32),
        compiler_params=_cparams(("parallel",)),
        name="combine_ln",
    )(x, yg, yg, ew, g, b)


def _pad_cols(x, n):
    return jnp.pad(x, ((0, 0), (0, n - x.shape[1])))


def _pack_w_in(w):
    o = np.cumsum([0, RW_PROJ, ATT_WIDTH, KV_WIDTH, KV_WIDTH, IDX_HEADS * IDX_DIM, IDX_DIM, IDX_HEADS, D_MODEL, D_MODEL])
    rw, q, ka, va, iq, ik, iw, ga, gb = [w[:, int(o[i]):int(o[i + 1])] for i in range(9)]
    lora = jnp.concatenate([_pad_cols(rw[:, 6144:6240], 128), _pad_cols(rw[:, 6240:6336], 128), rw[:, 6336:6592]], axis=1)
    ikw = _pad_cols(jnp.concatenate([ik, iw], axis=1), 128)
    cols = jnp.concatenate([rw[:, 0:6144], q, ga, gb, ka, va, iq, lora, ikw], axis=1)
    return _pad_cols(cols, C_TOTAL).astype(BF16)


def _lora_cols(x):
    return jnp.concatenate([_pad_cols(x[:, 6144:6240], 128), _pad_cols(x[:, 6240:6336], 128), x[:, 6336:6592]], axis=1)


def _pack_rwkv(rw_mu, rw_w0, rw_w2, rw_a0, rw_a2, rw_g2, rw_k_k, rw_k_a, rw_r_k, rw_ln_w, rw_ln_b):
    flat = lambda t: t.reshape(1, RW_WIDTH)
    mu = rw_mu.reshape(1, RW_PROJ)
    rows = [mu[:, 0:2048], mu[:, 2048:4096], mu[:, 4096:6144], flat(rw_w0), flat(rw_a0), flat(rw_k_k),
            flat(rw_k_a), flat(rw_r_k), flat(rw_ln_w), flat(rw_ln_b)]
    prm = jnp.pad(jnp.concatenate(rows, axis=0), ((0, 6), (0, 0)))
    mu_l = jnp.pad(_lora_cols(mu), ((0, 7), (0, 0)))
    w2 = jnp.pad(rw_w2, ((0, 128 - W_LORA), (0, 0))).astype(BF16)
    a2 = jnp.pad(rw_a2, ((0, 128 - A_LORA), (0, 0))).astype(BF16)
    g2 = rw_g2.astype(BF16)
    return prm, mu_l, w2, a2, g2


def _head_indicators(width):
    lane = np.arange(width)[:, None] // RW_HEAD_DIM
    ind = (lane == np.arange(128)[None, :]).astype(np.float32)
    return jnp.asarray(ind), jnp.asarray(ind.T)


def _head_selectors():
    sel = np.zeros((WKV_HQ, WKV_W, RW_HEAD_DIM), np.float32)
    for j in range(WKV_HQ):
        sel[j, j * RW_HEAD_DIM + np.arange(RW_HEAD_DIM), np.arange(RW_HEAD_DIM)] = 1.0
    return jnp.asarray(sel)


def kernel(x_prompt, x_sample, mem_prompt, cache_k, cache_v, cache_idx_k, page_table, state_wkv, state_shift, cache_mem_k, cache_mem_v, w_in, rw_mu, rw_w0, rw_w2, rw_a0, rw_a2, rw_g2, rw_k_k, rw_k_a, rw_r_k, rw_ln_w, rw_ln_b, idx_ln_w, idx_ln_b, w_branch_a, w_branch_b, w_out, ln1_w, ln1_b, w_mem_q, w_mem_k, w_mem_v, w_mem_o, ln2_w, ln2_b, w_router_grp, b_router_grp, w_router_exp, b_router_exp, w_exp_gate, w_exp_up, w_exp_down, ln3_w, ln3_b):
    B, S, D = x_prompt.shape
    DB, DS, _ = x_sample.shape
    assert DS == 1 and cache_k.shape[0] == 1
    TP = B * S
    T = TP + DB
    MP = _round_up(T, 640)
    past = page_table.shape[1] * PAGE_SIZE
    n_mem = mem_prompt.shape[1]
    row1 = lambda a: a.reshape(1, -1)

    def pad_rows(a):
        return jnp.concatenate([a, jnp.zeros((MP - a.shape[0],) + a.shape[1:], a.dtype)], axis=0)

    x_all = pad_rows(jnp.concatenate([x_prompt.reshape(TP, D), x_sample.reshape(DB, D)], axis=0))
    z = _mm(x_all.astype(BF16), _pack_w_in(w_in[0]), 640, 1024, D, name="in_proj")

    prm, mu_l, w2, a2, g2 = _pack_rwkv(rw_mu[0], rw_w0[0], rw_w2[0], rw_a0[0], rw_a2[0], rw_g2[0], rw_k_k[0],
                                       rw_k_a[0], rw_r_k[0], rw_ln_w[0], rw_ln_b[0])
    rw_p, wkv_p = _wkv_prompt(z, prm, mu_l, w2, a2, g2, _head_selectors(), B, S)
    ss = state_shift[0]
    ind_f, indt_f = _head_indicators(RW_WIDTH)
    tok = _wkv_tokens_sample(z, ss[:, 0:2048], ss[:, 2048:4096], ss[:, 4096:6144], _lora_cols(ss), prm, mu_l, w2, a2,
                             g2, ind_f, indt_f, TP, DB)
    t_r, t_w, t_al, t_be, t_km, t_v, t_g, t_bo = tok
    rowv = lambda a: a.reshape(DB, RW_HEADS, 1, RW_HEAD_DIM)
    colv = lambda a: a.reshape(DB, RW_HEADS, RW_HEAD_DIM, 1)
    y_col, wkv_s = _wkv_step(state_wkv[0], rowv(t_w), rowv(t_al), rowv(t_be), rowv(t_km), rowv(t_r), colv(t_v),
                             colv(t_g), colv(t_bo), rw_ln_w[0].reshape(1, RW_HEADS, RW_HEAD_DIM, 1),
                             rw_ln_b[0].reshape(1, RW_HEADS, RW_HEAD_DIM, 1))
    rw_all = pad_rows(jnp.concatenate([rw_p, y_col.reshape(DB, RW_WIDTH).astype(BF16)], axis=0))

    pos = jnp.concatenate([jnp.tile(jnp.arange(S, dtype=I32), B), jnp.full((MP - TP,), past, I32)])
    tab_a = _rope_tables(pos, ROT_DIM, HEAD_DIM)
    tab_i = _rope_tables(pos, IDX_ROT_DIM, IDX_DIM)
    qb, k_rot, kb, vb, qib, kiw, kid = _prep(z, tab_a, tab_i, _pad_cols(row1(idx_ln_w[0]), LANES),
                                             _pad_cols(row1(idx_ln_b[0]), LANES), 128)
    att_p = _dsa_prompt(qb, kb, vb, qib, kid, kiw, B, S, min(TOPK_MAX, S // 4))
    qi_s = qib[TP:T].reshape(DB, IDX_HEADS, IDX_DIM)
    wi_s = kiw[TP:T, IDX_DIM:IDX_DIM + IDX_HEADS].reshape(DB, IDX_HEADS, 1)
    mask, mself = _sel_sample(page_table, qi_s, wi_s, kid[TP:T, 0:IDX_DIM].reshape(DB, 1, IDX_DIM), cache_idx_k[0],
                              min(TOPK_MAX, (past + DS) // 4))
    expand = lambda a: jnp.repeat(a[TP:T].reshape(DB, N_KV_HEADS, HEAD_DIM), GROUP, axis=1)
    att_s = _att_pages(page_table, qb[TP:T].reshape(DB, N_HEADS, HEAD_DIM), mask, mself, expand(kb), expand(vb),
                       cache_k[0], cache_v[0])
    att_all = pad_rows(jnp.concatenate([att_p, att_s.reshape(DB, ATT_WIDTH)], axis=0))

    merged = _branch_merge(rw_all, att_all, w_branch_a[0].astype(BF16), w_branch_b[0].astype(BF16), z, 640, 1024)
    x1, x1b = _mm_ln(merged, w_out[0].astype(BF16), x_all, row1(ln1_w[0]), row1(ln1_b[0]), 320, 512, name="out_ln1")

    mq = _mm(x1b, w_mem_q[0].astype(BF16), 640, MEM_WIDTH, D, out_dtype=BF16, name="mem_q")
    mem2d = mem_prompt.reshape(B * n_mem, D).astype(BF16)
    mem_k = _mm(mem2d, w_mem_k[0].astype(BF16), B * n_mem, MEM_WIDTH, D, name="mem_k")
    mem_v = _mm(mem2d, w_mem_v[0].astype(BF16), B * n_mem, MEM_WIDTH, D, name="mem_v")
    ma_p = _mem_att_prompt(mq, mem_k.astype(BF16), mem_v.astype(BF16), B, S, 512)
    ma_s = _mem_att_sample(mq[TP:T].reshape(DB, 1, MEM_WIDTH), cache_mem_k[0].reshape(DB, n_mem, MEM_WIDTH),
                           cache_mem_v[0].reshape(DB, n_mem, MEM_WIDTH))
    ma_all = pad_rows(jnp.concatenate([ma_p, ma_s.reshape(DB, MEM_WIDTH)], axis=0))
    x2, _ = _mm_ln(ma_all, w_mem_o[0].astype(BF16), x1, row1(ln2_w[0]), row1(ln2_b[0]), 320, 512, name="mem_o_ln2")

    w_r = _pad_cols(jnp.concatenate([w_router_grp[0], w_router_exp[0]], axis=1), LANES)
    b_r = _pad_cols(row1(jnp.concatenate([b_router_grp[0], b_router_exp[0]])), LANES)
    e_idx, e_w = _router(x2, w_r, b_r, 640, 1024)
    n_assign = 2 * T
    flat_e = e_idx[:T, 0:2].reshape(n_assign)
    order = jnp.argsort(flat_e).astype(I32)
    sorted_e = flat_e[order]
    counts = jnp.bincount(flat_e, length=N_EXPERTS).astype(I32)
    padded = (counts + MOE_BR - 1) // MOE_BR * MOE_BR
    pad_end = jnp.cumsum(padded)
    start = jnp.cumsum(counts) - counts
    dest = ((pad_end - padded)[sorted_e] + jnp.arange(n_assign, dtype=I32) - start[sorted_e]).astype(I32)
    n_blocks = -(-n_assign // MOE_BR) + N_EXPERTS
    row_token = jnp.zeros((n_blocks * MOE_BR,), I32).at[dest].set(order // 2)
    block_e = jnp.minimum(jnp.searchsorted(pad_end, jnp.arange(n_blocks, dtype=I32) * MOE_BR, side='right'),
                          N_EXPERTS - 1).astype(I32)
    n_used = (pad_end[-1] // MOE_BR).astype(I32).reshape(1)
    slot = jnp.zeros((n_assign,), I32).at[order].set(dest).reshape(T, 2)
    xs = _gather_rows(x2, row_token, n_used, MOE_BR)
    hid = _expert_up(block_e, n_used, xs, w_exp_gate[0], w_exp_up[0], 256)
    y_rows = _expert_down(block_e, n_used, hid, w_exp_down[0])
    slot_pad = jnp.concatenate([jnp.pad(slot[:, 0], (0, MP - T)), jnp.pad(slot[:, 1], (0, MP - T))])
    y_tok = _gather_rows(y_rows, slot_pad, jnp.full((1,), 2 * MP // MOE_BR, I32), MOE_BR)
    x3 = _combine_ln(x2, y_tok, e_w, row1(ln3_w[0]), row1(ln3_b[0]), 128)

    kv5 = lambda a, n, s: a.reshape(1, n, s, N_KV_HEADS, HEAD_DIM)
    va = z[:, C_VA:C_VA + KV_WIDTH]
    ki = kiw[:, 0:IDX_DIM]
    zl = jnp.concatenate([z[S - 1:TP:S], z[TP:T]], axis=0)
    shift_cols = jnp.concatenate([zl[:, 0:6144], zl[:, C_LORA:C_LORA + W_LORA],
                                  zl[:, C_LORA + 128:C_LORA + 128 + A_LORA], zl[:, C_LORA + 256:C_LORA + 512]], axis=1)
    mem5 = lambda a: a.reshape(1, B, n_mem, MEM_HEADS, MEM_HEAD_DIM)
    return (x3[:TP].reshape(B, S, D), x3[TP:T].reshape(DB, DS, D),
            kv5(k_rot[:TP], B, S), kv5(va[:TP], B, S), ki[:TP].reshape(1, B, S, IDX_DIM),
            wkv_p[None], shift_cols[:B][None], mem5(mem_k), mem5(mem_v),
            kv5(k_rot[TP:T], DB, DS), kv5(va[TP:T], DB, DS), ki[TP:T].reshape(1, DB, DS, IDX_DIM),
            wkv_s[None], shift_cols[B:][None])
```

```python
import functools
import math

import jax
import jax.numpy as jnp
import numpy as np
from jax import lax
from jax.experimental import pallas as pl
from jax.experimental.pallas import tpu as pltpu

F32 = jnp.float32
BF16 = jnp.bfloat16
I32 = jnp.int32
HIGHEST = lax.Precision.HIGHEST

D_MODEL = 4096
RW_HEAD_DIM = 64
RW_HEADS = 32
RW_WIDTH = 2048
W_LORA = 96
A_LORA = 96
G_LORA = 256
RW_PROJ = 3 * RW_WIDTH + W_LORA + A_LORA + G_LORA
RW_GN_EPS = 64e-5
HEAD_DIM = 128
N_HEADS = 16
N_KV_HEADS = 4
GROUP = 4
ATT_WIDTH = 2048
KV_WIDTH = 512
ROT_DIM = 32
ROPE_THETA = 500000.0
IDX_HEADS = 16
IDX_DIM = 64
IDX_ROT_DIM = 16
TOPK_MAX = 256
PAGE_SIZE = 128
MEM_HEADS = 4
MEM_HEAD_DIM = 128
MEM_WIDTH = 512
N_GROUPS = 8
EXPERTS_PER_GROUP = 8
N_EXPERTS = 64
D_EXPERT = 512
LN_EPS = 1e-5
DEEPNORM_ALPHA = 2.0 ** 0.25
EXP_M05 = math.exp(-0.5)

LANES = 128
SUBLANES = 8
VMEM_LIMIT = 56 * 1024 * 1024

C_R, C_K, C_V = 0, 2048, 4096
A_Q, A_KA, A_VA, A_IQ = 0, 2048, 2560, 3072
G_A, G_B = 0, 4096
S_LORA, S_IKW, S_TOTAL = 0, 512, 640

INT_MIN = -(2 ** 31)


def _round_up(n, m):
    return -(-n // m) * m


def _cparams(sem):
    return pltpu.CompilerParams(dimension_semantics=sem, vmem_limit_bytes=VMEM_LIMIT)


def _dot(a, b, precision=None):
    return jnp.dot(a, b, preferred_element_type=F32, precision=precision)


def _dot_nt(a, b, precision=None):
    return lax.dot_general(a, b, (((1,), (1,)), ((), ())), preferred_element_type=F32, precision=precision)


def _sigmoid(x):
    return 1.0 / (1.0 + jnp.exp(-x))


def _mm_body(x_ref, w_ref, o_ref, acc_ref):
    k = pl.program_id(2)

    @pl.when(k == 0)
    def _():
        acc_ref[...] = jnp.zeros_like(acc_ref)

    acc_ref[...] += _dot(x_ref[...], w_ref[...])

    @pl.when(k == pl.num_programs(2) - 1)
    def _():
        o_ref[...] = acc_ref[...].astype(o_ref.dtype)


def _mm_fullk_body(x_ref, w_ref, o_ref):
    o_ref[...] = _dot(x_ref[...], w_ref[...]).astype(o_ref.dtype)


def _mm(x, w, tm, tn, tk, out_dtype=F32, name="mm"):
    m, kd = x.shape
    n = w.shape[1]
    if tk == kd:
        return pl.pallas_call(
            _mm_fullk_body,
            grid=(m // tm, n // tn),
            in_specs=[pl.BlockSpec((tm, kd), lambda i, j: (i, 0)),
                      pl.BlockSpec((kd, tn), lambda i, j: (0, j))],
            out_specs=pl.BlockSpec((tm, tn), lambda i, j: (i, j)),
            out_shape=jax.ShapeDtypeStruct((m, n), out_dtype),
            compiler_params=_cparams(("parallel", "parallel")),
            name=name,
        )(x, w)
    return pl.pallas_call(
        _mm_body,
        grid=(m // tm, n // tn, kd // tk),
        in_specs=[pl.BlockSpec((tm, tk), lambda i, j, k: (i, k)),
                  pl.BlockSpec((tk, tn), lambda i, j, k: (k, j))],
        out_specs=pl.BlockSpec((tm, tn), lambda i, j, k: (i, j)),
        out_shape=jax.ShapeDtypeStruct((m, n), out_dtype),
        scratch_shapes=[pltpu.VMEM((tm, tn), F32)],
        compiler_params=_cparams(("parallel", "parallel", "arbitrary")),
        name=name,
    )(x, w)


def _layer_norm_rows(x, g, b):
    mu = jnp.mean(x, axis=-1, keepdims=True)
    d = x - mu
    var = jnp.mean(d * d, axis=-1, keepdims=True)
    return d * lax.rsqrt(var + LN_EPS) * g + b


def _mm_ln_body(x_ref, w_ref, res_ref, g_ref, b_ref, o_ref, ob_ref, y_ref, *, tn):
    j = pl.program_id(1)
    y_ref[:, pl.ds(pl.multiple_of(j * tn, tn), tn)] = _dot(x_ref[...], w_ref[...])

    @pl.when(j == pl.num_programs(1) - 1)
    def _():
        y = _layer_norm_rows(DEEPNORM_ALPHA * res_ref[...] + y_ref[...], g_ref[...], b_ref[...])
        o_ref[...] = y
        ob_ref[...] = y.astype(BF16)


def _mm_ln(x, w, res, g, b, tm, tn, name="mm_ln"):
    m, kd = x.shape
    n = w.shape[1]
    return pl.pallas_call(
        functools.partial(_mm_ln_body, tn=tn),
        grid=(m // tm, n // tn),
        in_specs=[pl.BlockSpec((tm, kd), lambda i, j: (i, 0)),
                  pl.BlockSpec((kd, tn), lambda i, j: (0, j)),
                  pl.BlockSpec((tm, n), lambda i, j: (i, 0)),
                  pl.BlockSpec((1, n), lambda i, j: (0, 0)),
                  pl.BlockSpec((1, n), lambda i, j: (0, 0))],
        out_specs=[pl.BlockSpec((tm, n), lambda i, j: (i, 0)),
                   pl.BlockSpec((tm, n), lambda i, j: (i, 0))],
        out_shape=[jax.ShapeDtypeStruct((m, n), F32), jax.ShapeDtypeStruct((m, n), BF16)],
        scratch_shapes=[pltpu.VMEM((tm, n), F32)],
        compiler_params=_cparams(("parallel", "arbitrary")),
        name=name,
    )(x, w, res, g, b)


def _branch_merge_body(rw_ref, at_ref, wa_ref, wb_ref, ga_ref, gb_ref, o_ref):
    a = _dot(rw_ref[...], wa_ref[...])
    b = _dot(at_ref[...], wb_ref[...])
    o_ref[...] = (_sigmoid(ga_ref[...]) * a + _sigmoid(gb_ref[...]) * b).astype(o_ref.dtype)


def _branch_merge(rw, att, wa, wb, z, tm, tn):
    m = rw.shape[0]
    n = wa.shape[1]
    ga0, gb0 = G_A // tn, G_B // tn
    return pl.pallas_call(
        _branch_merge_body,
        grid=(m // tm, n // tn),
        in_specs=[pl.BlockSpec((tm, RW_WIDTH), lambda i, j: (i, 0)),
                  pl.BlockSpec((tm, ATT_WIDTH), lambda i, j: (i, 0)),
                  pl.BlockSpec((RW_WIDTH, tn), lambda i, j: (0, j)),
                  pl.BlockSpec((ATT_WIDTH, tn), lambda i, j: (0, j)),
                  pl.BlockSpec((tm, tn), lambda i, j: (i, ga0 + j)),
                  pl.BlockSpec((tm, tn), lambda i, j: (i, gb0 + j))],
        out_specs=pl.BlockSpec((tm, tn), lambda i, j: (i, j)),
        out_shape=jax.ShapeDtypeStruct((m, n), BF16),
        compiler_params=_cparams(("parallel", "parallel")),
        name="branch_merge",
    )(rw, att, wa, wb, z, z)


def _seg_sum(x, ind, ind_t):
    return _dot(_dot(x, ind, HIGHEST), ind_t, HIGHEST)


def _split_bf16(x, parts):
    out = []
    for _ in range(parts):
        t = x.astype(BF16)
        out.append(t)
        x = x - t.astype(F32)
    return out


def _seg_sum_quads(x, bd):
    outs = []
    for q in range(x.shape[1] // WKV_W):
        hi, lo = _split_bf16(x[:, q * WKV_W:(q + 1) * WKV_W], 2)
        outs.append(_dot(hi, bd) + _dot(lo, bd))
    return jnp.concatenate(outs, axis=1) if len(outs) > 1 else outs[0]


def _rwkv_tokens(zr, zk, zv, zl, pr, pk, pv, plo, prm, mu_l, w2, a2, g2, seg):
    r = zr + (pr - zr) * prm[0:1]
    kx = zk + (pk - zk) * prm[1:2]
    v = zv + (pv - zv) * prm[2:3]
    zsl = zl + (plo - zl) * mu_l
    tw = jnp.tanh(zsl[:, 0:128]).astype(BF16)
    xw = prm[3:4] + _dot(tw, w2)
    lw = -EXP_M05 * _sigmoid(xw)
    a = _sigmoid(prm[4:5] + _dot(zsl[:, 128:256].astype(BF16), a2))
    g = _dot(_sigmoid(zsl[:, 256:512]).astype(BF16), g2)
    kk = kx * prm[5:6]
    n2 = seg(kk * kk)
    kkn = kk / jnp.maximum(jnp.sqrt(n2), 1e-12)
    kmod = kx * (1.0 + (a - 1.0) * prm[6:7])
    return r, lw, kmod, v, kkn, a, g


def _rwkv_post(y, r, kmod, v, g, prm, seg):
    inv_n = 1.0 / RW_HEAD_DIM
    mean = seg(y) * inv_n
    d = y - mean
    var = seg(d * d) * inv_n
    yn = d * lax.rsqrt(var + RW_GN_EPS) * prm[8:9] + prm[9:10]
    bonus = seg(r * kmod * prm[7:8]) * v
    return (yn + bonus) * g


WKV_C = 64
WKV_HQ = 4
WKV_W = WKV_HQ * RW_HEAD_DIM
WKV_QPS = 8


def _wkv_chunk_body(zr_ref, zk_ref, zv_ref, zl_ref, prm_ref, mul_ref, w2_ref, a2_ref, g2_ref,
                    sel_ref, o_ref, so_ref, s_ref, cr_ref, ck_ref, cv_ref, cl_ref):
    c = pl.program_id(2)
    C = WKV_C
    W = WKV_W

    @pl.when(c == 0)
    def _():
        s_ref[...] = jnp.zeros_like(s_ref)
        cr_ref[...] = jnp.zeros_like(cr_ref)
        ck_ref[...] = jnp.zeros_like(ck_ref)
        cv_ref[...] = jnp.zeros_like(cv_ref)
        cl_ref[...] = jnp.zeros_like(cl_ref)

    rows = lax.broadcasted_iota(I32, (C, 1), 0)

    def shifted(z, carry_ref):
        prev = jnp.where(rows == 0, carry_ref[0:1, :], pltpu.roll(z, 1, 0))
        carry_ref[0:1, :] = z[C - 1:C, :]
        return prev

    zr, zk, zv, zl = zr_ref[...], zk_ref[...], zv_ref[...], zl_ref[...]
    pr, pk, pv, plo = shifted(zr, cr_ref), shifted(zk, ck_ref), shifted(zv, cv_ref), shifted(zl, cl_ref)
    prm = prm_ref[...]
    lane_head = lax.broadcasted_iota(I32, (1, W), 1) // RW_HEAD_DIM
    hv = lax.broadcasted_iota(I32, (W, 1), 0) // RW_HEAD_DIM
    bd = (hv == lane_head).astype(BF16)
    seg = lambda x: _seg_sum_quads(x, bd)
    r, lw, kmod, v, kkn, a, g = _rwkv_tokens(zr, zk, zv, zl, pr, pk, pv, plo, prm, mul_ref[0:1, :],
                                            w2_ref[...], a2_ref[...], g2_ref[...], seg)
    al = -kkn
    be = kkn * a

    ti = lax.broadcasted_iota(I32, (C, C), 0)
    tj = lax.broadcasted_iota(I32, (C, C), 1)
    tri = (tj <= ti).astype(BF16)
    cum = sum(_dot(tri, part) for part in _split_bf16(lw, 3))
    cum_l = cum[C - 1:C, :]
    p_inv = jnp.exp(-cum)
    p_rel = jnp.exp(cum_l - cum)
    ab = al * jnp.exp(cum - lw)
    rb = r * jnp.exp(cum)
    bt = (be * p_inv).astype(BF16)
    kt = (kmod * p_inv).astype(BF16)
    bk = jnp.concatenate([be * p_rel, kmod * p_rel], axis=0).astype(BF16)
    ar = jnp.concatenate([ab, rb], axis=0)
    pc = jnp.exp(cum_l)

    n4 = WKV_HQ * C
    bi = lax.broadcasted_iota(I32, (n4, n4), 0)
    bj = lax.broadcasted_iota(I32, (n4, n4), 1)
    same = (bi // C) == (bj // C)
    tri_s4 = same & ((bj % C) < (bi % C))
    tri_i4 = same & ((bj % C) <= (bi % C))
    eye4 = (bi == bj).astype(F32)
    masks = [lane_head == j for j in range(WKV_HQ)]

    def stack(x):
        return jnp.concatenate([jnp.where(m, x, jnp.zeros_like(x)) for m in masks], axis=0)

    def block_sum(x):
        return sum(x[j * C:(j + 1) * C] for j in range(WKV_HQ))

    def bdot(a, b):
        return _dot(a.astype(BF16), b.astype(BF16))

    qs = []
    for q in range(WKV_QPS):
        sl = slice(q * W, (q + 1) * W)
        v_q = v[:, sl]
        lhs = jnp.concatenate([stack(ab[:, sl]), stack(rb[:, sl])], axis=0).astype(BF16)
        abr = _dot_nt(lhs, stack(bt[:, sl]))
        akr = _dot_nt(lhs, stack(kt[:, sl]))
        qs.append(dict(sl=sl, v=v_q, v_s=stack(v_q), s0=s_ref[q],
                       x=jnp.where(tri_s4, abr[0:n4], 0.0), a_rb=jnp.where(tri_i4, abr[n4:2 * n4], 0.0),
                       a_ak=jnp.where(tri_s4, akr[0:n4], 0.0), a_rk=jnp.where(tri_i4, akr[n4:2 * n4], 0.0)))
    for d in qs:
        d['pw'] = [d['x']]
    for _ in range(5):
        for d in qs:
            d['pw'].append(bdot(d['pw'][-1], d['pw'][-1]))
    for d in qs:
        pw = d['pw']
        pr_ = [eye4 + pw[2 * i] + pw[2 * i + 1] + bdot(pw[2 * i], pw[2 * i + 1]) for i in range(3)]
        d['t'] = bdot(bdot(pr_[0], pr_[1]), pr_[2])
    for d in qs:
        gs = _dot_nt(ar[:, d['sl']].astype(BF16), d['s0'].astype(BF16))
        d['g_r'] = gs[C:2 * C]
        d['w_s'] = stack(gs[0:C]) + bdot(d['a_ak'], d['v_s'])
    for d in qs:
        d['u_s'] = bdot(d['t'], d['w_s'])
    ys = []
    for q, d in enumerate(qs):
        yv = bdot(jnp.concatenate([d['a_rb'], d['a_rk']], axis=1), jnp.concatenate([d['u_s'], d['v_s']], axis=0))
        ys.append(d['g_r'] + block_sum(yv))
        uv_t = jnp.concatenate([block_sum(d['u_s']), d['v']], axis=0).T.astype(BF16)
        upd = _dot(uv_t, bk[:, d['sl']])
        s_ref[q] = d['s0'] * pc[:, d['sl']] + jnp.where(hv == lane_head, upd, 0.0)

    y = jnp.concatenate(ys, axis=1) if WKV_QPS > 1 else ys[0]
    o_ref[...] = _rwkv_post(y, r, kmod, v, g, prm, seg).astype(o_ref.dtype)

    @pl.when(c == pl.num_programs(2) - 1)
    def _():
        for q in range(WKV_QPS):
            for j in range(WKV_HQ):
                rows_j = s_ref[q, j * RW_HEAD_DIM:(j + 1) * RW_HEAD_DIM, :]
                so_ref[0, q * WKV_HQ + j] = _dot(rows_j, sel_ref[j], HIGHEST)


def _wkv_prompt(z, zs, prm, mu_l, w2, a2, g2, sel, batch, seq):
    nc = seq // WKV_C
    WS = WKV_W * WKV_QPS
    nq = RW_WIDTH // WS
    row = lambda b, q, c: b * nc + c
    return pl.pallas_call(
        _wkv_chunk_body,
        grid=(batch, nq, nc),
        in_specs=[pl.BlockSpec((WKV_C, WS), lambda b, q, c: (row(b, q, c), C_R // WS + q)),
                  pl.BlockSpec((WKV_C, WS), lambda b, q, c: (row(b, q, c), C_K // WS + q)),
                  pl.BlockSpec((WKV_C, WS), lambda b, q, c: (row(b, q, c), C_V // WS + q)),
                  pl.BlockSpec((WKV_C, 512), lambda b, q, c: (row(b, q, c), S_LORA // 512)),
                  pl.BlockSpec((16, WS), lambda b, q, c: (0, q)),
                  pl.BlockSpec((8, 512), lambda b, q, c: (0, 0)),
                  pl.BlockSpec((128, WS), lambda b, q, c: (0, q)),
                  pl.BlockSpec((128, WS), lambda b, q, c: (0, q)),
                  pl.BlockSpec((256, WS), lambda b, q, c: (0, q)),
                  pl.BlockSpec((WKV_HQ, WKV_W, RW_HEAD_DIM), lambda b, q, c: (0, 0, 0))],
        out_specs=[pl.BlockSpec((WKV_C, WS), lambda b, q, c: (row(b, q, c), q)),
                   pl.BlockSpec((1, WKV_HQ * WKV_QPS, RW_HEAD_DIM, RW_HEAD_DIM), lambda b, q, c: (b, q, 0, 0))],
        out_shape=[jax.ShapeDtypeStruct((batch * seq, RW_WIDTH), BF16),
                   jax.ShapeDtypeStruct((batch, RW_HEADS, RW_HEAD_DIM, RW_HEAD_DIM), F32)],
        scratch_shapes=[pltpu.VMEM((WKV_QPS, WKV_W, WKV_W), F32), pltpu.VMEM((8, WS), F32), pltpu.VMEM((8, WS), F32),
                        pltpu.VMEM((8, WS), F32), pltpu.VMEM((8, 512), F32)],
        compiler_params=_cparams(("parallel", "parallel", "arbitrary")),
        name="wkv_prompt",
    )(z, z, z, zs, prm, mu_l, w2, a2, g2, sel)


def _wkv_tok_body(zr_ref, zk_ref, zv_ref, zl_ref, pr_ref, pk_ref, pv_ref, pl_ref, prm_ref, mul_ref, w2_ref, a2_ref,
                  g2_ref, ind_ref, indt_ref, r_ref, w_ref, al_ref, be_ref, km_ref, v_ref, g_ref, bo_ref):
    prm = prm_ref[...]
    ind, ind_t = ind_ref[...], indt_ref[...]
    seg = lambda x: _seg_sum(x, ind, ind_t)
    r, lw, kmod, v, kkn, a, g = _rwkv_tokens(zr_ref[...], zk_ref[...], zv_ref[...], zl_ref[...], pr_ref[...],
                                            pk_ref[...], pv_ref[...], pl_ref[...], prm, mul_ref[0:1, :],
                                            w2_ref[...], a2_ref[...], g2_ref[...], seg)
    r_ref[...] = r
    w_ref[...] = jnp.exp(lw)
    al_ref[...] = -kkn
    be_ref[...] = kkn * a
    km_ref[...] = kmod
    v_ref[...] = v
    g_ref[...] = g
    bo_ref[...] = seg(r * kmod * prm[7:8])


def _wkv_tokens_sample(z, zs, prev_r, prev_k, prev_v, prev_l, prm, mu_l, w2, a2, g2, ind, ind_t, row0, n):
    rb = row0 // n
    full = lambda a: pl.BlockSpec(a.shape, lambda i: (0,) * a.ndim)
    zspec = lambda w, c0: pl.BlockSpec((n, w), lambda i: (rb, c0 // w))
    return pl.pallas_call(
        _wkv_tok_body,
        grid=(1,),
        in_specs=[zspec(RW_WIDTH, C_R), zspec(RW_WIDTH, C_K), zspec(RW_WIDTH, C_V), zspec(512, S_LORA),
                  full(prev_r), full(prev_k), full(prev_v), full(prev_l), full(prm), full(mu_l), full(w2), full(a2),
                  full(g2), full(ind), full(ind_t)],
        out_specs=[pl.BlockSpec((n, RW_WIDTH), lambda i: (0, 0))] * 8,
        out_shape=[jax.ShapeDtypeStruct((n, RW_WIDTH), F32)] * 8,
        compiler_params=_cparams(("arbitrary",)),
        name="wkv_tokens_sample",
    )(z, z, z, zs, prev_r, prev_k, prev_v, prev_l, prm, mu_l, w2, a2, g2, ind, ind_t)


def _wkv_step_body(s_ref, w_ref, al_ref, be_ref, km_ref, r_ref, v_ref, g_ref, bo_ref, lnw_ref, lnb_ref, o_ref, so_ref):
    s = s_ref[...]
    vcol = v_ref[...]
    sa = jnp.sum(s * al_ref[...], axis=-1, keepdims=True)
    s2 = s * w_ref[...] + sa * be_ref[...] + vcol * km_ref[...]
    so_ref[...] = s2
    y = jnp.sum(s2 * r_ref[...], axis=-1, keepdims=True)
    mean = jnp.mean(y, axis=2, keepdims=True)
    d = y - mean
    var = jnp.mean(d * d, axis=2, keepdims=True)
    yn = d * lax.rsqrt(var + RW_GN_EPS) * lnw_ref[...] + lnb_ref[...]
    o_ref[...] = (yn + bo_ref[...] * vcol) * g_ref[...]


def _wkv_step(state, w, al, be, km, r, v, g, bo, lnw, lnb):
    n, h = state.shape[0], state.shape[1]
    hq = 2
    rowspec = pl.BlockSpec((n, hq, 1, RW_HEAD_DIM), lambda q: (0, q, 0, 0))
    colspec = pl.BlockSpec((n, hq, RW_HEAD_DIM, 1), lambda q: (0, q, 0, 0))
    pcol = pl.BlockSpec((1, hq, RW_HEAD_DIM, 1), lambda q: (0, q, 0, 0))
    sspec = pl.BlockSpec((n, hq, RW_HEAD_DIM, RW_HEAD_DIM), lambda q: (0, q, 0, 0))
    return pl.pallas_call(
        _wkv_step_body,
        grid=(h // hq,),
        in_specs=[sspec, rowspec, rowspec, rowspec, rowspec, rowspec, colspec, colspec, colspec, pcol, pcol],
        out_specs=[colspec, sspec],
        out_shape=[jax.ShapeDtypeStruct((n, h, RW_HEAD_DIM, 1), F32), jax.ShapeDtypeStruct(state.shape, F32)],
        compiler_params=_cparams(("parallel",)),
        name="wkv_step",
    )(state, w, al, be, km, r, v, g, bo, lnw, lnb)


def _rope_tables(pos, rot_dim, period):
    half = rot_dim // 2
    t = pos.shape[0]
    inv_freq = ROPE_THETA ** (-jnp.arange(half, dtype=F32) / half)
    ang = pos.astype(F32)[:, None] * inv_freq[None, :]
    cos, sin = jnp.cos(ang), jnp.sin(ang)
    zh = jnp.zeros((t, half), F32)
    rest = period - rot_dim
    c = jnp.concatenate([cos, cos, jnp.ones((t, rest), F32)], axis=1)
    s1 = jnp.concatenate([-sin, zh, jnp.zeros((t, rest), F32)], axis=1)
    s2 = jnp.concatenate([zh, sin, jnp.zeros((t, rest), F32)], axis=1)
    rep = LANES // period
    return jnp.stack([jnp.tile(a, (1, rep)) for a in (c, s1, s2)], axis=0)


def _rope(x, tab, half):
    n = x.shape[1]
    rep = n // LANES
    c, s1, s2 = [jnp.tile(tab[i], (1, rep)) if rep > 1 else tab[i] for i in range(3)]
    return x * c + pltpu.roll(x, n - half, 1) * s1 + pltpu.roll(x, half, 1) * s2


def _prep_body(q_ref, ka_ref, va_ref, iq_ref, ikw_ref, ta_ref, ti_ref, lnw_ref, lnb_ref,
               qo_ref, ko_ref, kb_ref, vb_ref, qio_ref, kio_ref, kid_ref):
    ta = ta_ref[...]
    ti = ti_ref[...]
    qo_ref[...] = (_rope(q_ref[...], ta, ROT_DIM // 2) * (HEAD_DIM ** -0.5)).astype(BF16)
    k = _rope(ka_ref[...], ta, ROT_DIM // 2)
    ko_ref[...] = k
    kb_ref[...] = k.astype(BF16)
    vb_ref[...] = va_ref[...].astype(BF16)
    qio_ref[...] = _rope(iq_ref[...], ti, IDX_ROT_DIM // 2).astype(BF16)
    x = ikw_ref[...]
    lane = lax.broadcasted_iota(I32, x.shape, 1)
    is_k = lane < IDX_DIM
    mu = jnp.sum(jnp.where(is_k, x, 0.0), axis=-1, keepdims=True) * (1.0 / IDX_DIM)
    d = jnp.where(is_k, x - mu, 0.0)
    var = jnp.sum(d * d, axis=-1, keepdims=True) * (1.0 / IDX_DIM)
    kn = d * lax.rsqrt(var + LN_EPS) * lnw_ref[...] + lnb_ref[...]
    kr = _rope(kn, ti, IDX_ROT_DIM // 2)
    kr = jnp.where(is_k, kr, 0.0)
    kio_ref[...] = jnp.where(is_k, kr, x * (IDX_HEADS ** -0.5))
    kid_ref[...] = (kr + pltpu.roll(kr, IDX_DIM, 1)).astype(BF16)


def _prep(z, zs, tab_a, tab_i, ln_w, ln_b, tm):
    m = z.shape[0]
    row = lambda w, c0: pl.BlockSpec((tm, w), lambda i: (i, c0 // w))
    outs = [(ATT_WIDTH, BF16), (KV_WIDTH, F32), (KV_WIDTH, BF16), (KV_WIDTH, BF16),
            (IDX_HEADS * IDX_DIM, BF16), (LANES, F32), (LANES, BF16)]
    return pl.pallas_call(
        _prep_body,
        grid=(m // tm,),
        in_specs=[row(ATT_WIDTH, A_Q), row(KV_WIDTH, A_KA), row(KV_WIDTH, A_VA), row(IDX_HEADS * IDX_DIM, A_IQ),
                  row(LANES, S_IKW),
                  pl.BlockSpec((3, tm, LANES), lambda i: (0, i, 0)),
                  pl.BlockSpec((3, tm, LANES), lambda i: (0, i, 0)),
                  pl.BlockSpec((1, LANES), lambda i: (0, 0)),
                  pl.BlockSpec((1, LANES), lambda i: (0, 0))],
        out_specs=[pl.BlockSpec((tm, w), lambda i: (i, 0)) for w, _ in outs],
        out_shape=[jax.ShapeDtypeStruct((m, w), dt) for w, dt in outs],
        compiler_params=_cparams(("parallel",)),
        name="prep",
    )(z, z, z, z, zs, tab_a, tab_i, ln_w, ln_b)


DSA_QB = 128
DSA_TK = 512
DSA_ATK = 256


def _float_key(s):
    b = pltpu.bitcast(s, I32)
    return b ^ ((b >> 31) & 0x7FFFFFFF)


def _kth_threshold(count_ge, topk):
    def step(b, thr):
        cand = thr + jnp.left_shift(jnp.int32(1), 31 - b)
        return jnp.where(count_ge(cand) >= topk, cand, thr)
    return lax.fori_loop(0, 32, step, jnp.full((DSA_QB, 1), INT_MIN, I32))


def _dsa_prompt_body(q_ref, kb_ref, vb_ref, qi_ref, kid_ref, kiw_ref, o_ref, keys_ref, jcut_ref, m_ref, l_ref,
                     acc_ref, qs_ref, *, topk, seq):
    i = pl.program_id(1)
    nt = (i * DSA_QB + DSA_QB + DSA_TK - 1) // DSA_TK
    qpos = i * DSA_QB + lax.broadcasted_iota(I32, (DSA_QB, 1), 0)
    col0 = lax.broadcasted_iota(I32, (1, DSA_TK), 1)
    lane = lax.broadcasted_iota(I32, (1, LANES), 1)
    kiw = kiw_ref[...]

    def score_tile(t, carry):
        kd = kid_ref[pl.ds(t * DSA_TK, DSA_TK), :]
        s = jnp.zeros((DSA_QB, DSA_TK), F32)
        for h in range(IDX_HEADS):
            qt = qi_ref[:, (h // 2) * LANES:(h // 2 + 1) * LANES]
            qh = jnp.where((lane // IDX_DIM) == (h % 2), qt, jnp.zeros_like(qt))
            d = _dot_nt(qh, kd) * (IDX_DIM ** -0.5)
            s = s + kiw[:, IDX_DIM + h:IDX_DIM + h + 1] * jnp.maximum(d, 0.0)
        valid = (t * DSA_TK + col0) <= qpos
        keys_ref[:, pl.ds(t * DSA_TK, DSA_TK)] = jnp.where(valid, _float_key(s), INT_MIN)
        return carry

    lax.fori_loop(0, nt, score_tile, 0)

    def count(pred):
        def body(t, acc):
            kt = keys_ref[:, pl.ds(t * DSA_TK, DSA_TK)]
            hit = pred(kt, t * DSA_TK + col0).astype(I32)
            for c in range(DSA_TK // LANES):
                acc = acc + hit[:, c * LANES:(c + 1) * LANES]
            return acc
        acc = lax.fori_loop(0, nt, body, jnp.zeros((DSA_QB, LANES), I32))
        return jnp.sum(acc, axis=-1, keepdims=True)

    thr = _kth_threshold(lambda c: count(lambda kt, col: kt >= c), topk)
    n_gt = count(lambda kt, col: kt > thr)
    n_eq = count(lambda kt, col: (kt == thr) & (col <= qpos))
    need = topk - n_gt
    jcut_ref[...] = jnp.full(jcut_ref.shape, seq, I32)
    excess = (n_eq > need) & (thr > INT_MIN)

    @pl.when(jnp.max(excess.astype(I32)) > 0)
    def _():
        def step(b, jm):
            cand = jm + jnp.left_shift(jnp.int32(1), 30 - b)
            c = count(lambda kt, col: (kt == thr) & (col <= qpos) & (col < cand))
            return jnp.where(c < need, cand, jm)
        jm = lax.fori_loop(0, 31, step, jnp.zeros((DSA_QB, 1), I32))
        jcut_ref[...] = jnp.broadcast_to(jnp.where(excess, jm, seq), jcut_ref.shape)

    jcut = jcut_ref[:, 0:1]

    m_ref[...] = jnp.full(m_ref.shape, -1e29, F32)
    l_ref[...] = jnp.zeros_like(l_ref)
    acc_ref[...] = jnp.zeros_like(acc_ref)
    for h in range(N_HEADS):
        qs_ref[h // GROUP, (h % GROUP) * DSA_QB:(h % GROUP + 1) * DSA_QB, :] = q_ref[:, h * HEAD_DIM:(h + 1) * HEAD_DIM]
    rep = DSA_ATK // LANES
    nta = (i * DSA_QB + DSA_QB + DSA_ATK - 1) // DSA_ATK
    cola0 = lax.broadcasted_iota(I32, (1, DSA_ATK), 1)

    def att_tile(t, carry):
        kt = keys_ref[:, pl.ds(t * DSA_ATK, DSA_ATK)]
        col = t * DSA_ATK + cola0
        sel = ((kt > thr) | ((kt == thr) & (col <= jcut))) & (col <= qpos)
        sel4 = jnp.concatenate([sel] * GROUP, axis=0)
        for g in range(N_KV_HEADS):
            k_t = kb_ref[pl.ds(t * DSA_ATK, DSA_ATK), g * HEAD_DIM:(g + 1) * HEAD_DIM]
            v_t = vb_ref[pl.ds(t * DSA_ATK, DSA_ATK), g * HEAD_DIM:(g + 1) * HEAD_DIM]
            s = jnp.where(sel4, _dot_nt(qs_ref[g], k_t), -1e30)
            m_old = m_ref[g]
            m_new = jnp.maximum(m_old, jnp.max(s, axis=-1, keepdims=True))
            p = jnp.exp(s - jnp.tile(m_new, (1, rep)))
            corr = jnp.exp(m_old - m_new)
            m_ref[g] = m_new
            l_ref[g] = l_ref[g] * corr + jnp.sum(p, axis=-1, keepdims=True)
            acc_ref[g] = acc_ref[g] * corr + _dot(p.astype(BF16), v_t)
        return carry

    lax.fori_loop(0, nta, att_tile, 0)
    for h in range(N_HEADS):
        rows = slice((h % GROUP) * DSA_QB, (h % GROUP + 1) * DSA_QB)
        o_ref[:, h * HEAD_DIM:(h + 1) * HEAD_DIM] = (acc_ref[h // GROUP, rows, :] / l_ref[h // GROUP, rows, :]).astype(o_ref.dtype)


def _dsa_prompt(qb, kb, vb, qib, kid, kiw, batch, seq, topk):
    nb = seq // DSA_QB
    body = functools.partial(_dsa_prompt_body, topk=topk, seq=seq)
    return pl.pallas_call(
        body,
        grid=(batch, nb),
        in_specs=[pl.BlockSpec((DSA_QB, ATT_WIDTH), lambda b, i: (b * nb + i, 0)),
                  pl.BlockSpec((seq, KV_WIDTH), lambda b, i: (b, 0)),
                  pl.BlockSpec((seq, KV_WIDTH), lambda b, i: (b, 0)),
                  pl.BlockSpec((DSA_QB, IDX_HEADS * IDX_DIM), lambda b, i: (b * nb + i, 0)),
                  pl.BlockSpec((seq, LANES), lambda b, i: (b, 0)),
                  pl.BlockSpec((DSA_QB, LANES), lambda b, i: (b * nb + i, 0))],
        out_specs=pl.BlockSpec((DSA_QB, ATT_WIDTH), lambda b, i: (b * nb + i, 0)),
        out_shape=jax.ShapeDtypeStruct((batch * seq, ATT_WIDTH), BF16),
        scratch_shapes=[pltpu.VMEM((DSA_QB, seq), I32), pltpu.VMEM((DSA_QB, LANES), I32),
                        pltpu.VMEM((N_KV_HEADS, GROUP * DSA_QB, LANES), F32),
                        pltpu.VMEM((N_KV_HEADS, GROUP * DSA_QB, LANES), F32),
                        pltpu.VMEM((N_KV_HEADS, GROUP * DSA_QB, HEAD_DIM), F32),
                        pltpu.VMEM((N_KV_HEADS, GROUP * DSA_QB, HEAD_DIM), BF16)],
        compiler_params=_cparams(("parallel", "arbitrary")),
        name="dsa_prompt",
    )(qb, kb, vb, qib, kid, kiw)


def _sel_sample_body(pt_ref, qi_ref, wi_ref, ks_ref, ck_hbm, pos_ref, ms_ref, kbuf, sc_ref, jm_ref, rk_ref, sem, *,
                     topk, npg):
    s = pl.program_id(0)
    slot = s % 2

    def page_copy(seq, p, sl):
        return pltpu.make_async_copy(ck_hbm.at[pt_ref[seq, p]], kbuf.at[sl, p], sem.at[sl])

    @pl.when(s == 0)
    def _():
        for p in range(npg):
            page_copy(0, p, 0).start()

    @pl.when(s + 1 < pl.num_programs(0))
    def _():
        for p in range(npg):
            page_copy(s + 1, p, 1 - slot).start()

    for p in range(npg):
        page_copy(s, p, slot).wait()

    qi, wi = qi_ref[0], wi_ref[0]
    for c in range(npg // SEL_CP):
        kp = kbuf[slot, c * SEL_CP:(c + 1) * SEL_CP].reshape(SEL_CP * PAGE_SIZE, IDX_DIM).astype(BF16)
        dp = _dot_nt(qi, kp) * (IDX_DIM ** -0.5)
        sc = jnp.sum(wi * jnp.maximum(dp, 0.0), axis=0, keepdims=True)
        for r in range(SEL_CP):
            sc_ref[c * SEL_CP + r:c * SEL_CP + r + 1, :] = sc[:, r * PAGE_SIZE:(r + 1) * PAGE_SIZE]
    keys = _float_key(sc_ref[...])
    d = jnp.sum(qi.astype(F32) * ks_ref[0].astype(F32), axis=-1, keepdims=True) * (IDX_DIM ** -0.5)
    s_self = jnp.sum(wi * jnp.maximum(d, 0.0), axis=0, keepdims=True)
    k_self = _float_key(s_self)
    lane = lax.broadcasted_iota(I32, (1, PAGE_SIZE), 1)
    pos = lax.broadcasted_iota(I32, keys.shape, 0) * PAGE_SIZE + lane

    def total(x):
        return jnp.sum(jnp.sum(x.astype(I32), axis=1, keepdims=True), axis=0, keepdims=True)

    def step(b, thr):
        cand = thr + jnp.left_shift(jnp.int32(1), 31 - b)
        c = total(keys >= cand) + (k_self >= cand).astype(I32)
        return jnp.where(c >= topk, cand, thr)

    thr = lax.fori_loop(0, 32, step, jnp.full((1, 1), INT_MIN, I32))
    need = topk - total(keys > thr) - (k_self > thr).astype(I32)
    eq = keys == thr
    jm_ref[...] = jnp.full(jm_ref.shape, npg * PAGE_SIZE, I32)

    @pl.when(jnp.max((total(eq) > need).astype(I32)) > 0)
    def _():
        def jstep(b, jm):
            cand = jm + jnp.left_shift(jnp.int32(1), 30 - b)
            return jnp.where(total(eq & (pos < cand)) < need, cand, jm)
        jm_ref[...] = jnp.broadcast_to(lax.fori_loop(0, 31, jstep, jnp.zeros((1, 1), I32)), jm_ref.shape)

    jm = jm_ref[0:1, 0:1]
    sel = (keys > thr) | (eq & (pos <= jm))
    self_sel = (k_self > thr) | ((k_self == thr) & (total(eq & (pos <= jm)) < need))
    ms_ref[0] = jnp.broadcast_to(self_sel.astype(F32), (1, LANES))

    ri = lax.broadcasted_iota(I32, (PAGE_SIZE, PAGE_SIZE), 0)
    ci = lax.broadcasted_iota(I32, (PAGE_SIZE, PAGE_SIZE), 1)
    sel_b = sel.astype(BF16)
    within = _dot(sel_b, (ri <= ci).astype(BF16))
    tot = _dot(sel_b, jnp.ones((PAGE_SIZE, PAGE_SIZE), BF16))
    pr_ = lax.broadcasted_iota(I32, (npg, npg), 0)
    pc_ = lax.broadcasted_iota(I32, (npg, npg), 1)
    before = _dot((pc_ < pr_).astype(BF16), tot.astype(BF16))
    rk_ref[...] = jnp.where(sel, (before + within).astype(I32) - 1, -1)
    jcol = lax.broadcasted_iota(I32, (topk, PAGE_SIZE), 0)
    lane_f = lax.broadcasted_iota(I32, (topk, PAGE_SIZE), 1).astype(F32)

    def gather_pos(p, carry):
        hi, lo = carry
        hit = jnp.broadcast_to(rk_ref[pl.ds(p, 1), :], (topk, PAGE_SIZE)) == jcol
        return hi + jnp.where(hit, jnp.asarray(p, F32), 0.0), lo + jnp.where(hit, lane_f, 0.0)

    zero = jnp.zeros((topk, PAGE_SIZE), F32)
    hi, lo = lax.fori_loop(0, npg, gather_pos, (zero, zero))
    ones8 = jnp.ones((SUBLANES, PAGE_SIZE), BF16)
    pos_row = _dot_nt(ones8, hi.astype(BF16)) * PAGE_SIZE + _dot_nt(ones8, lo.astype(BF16))
    pos_ref[0] = pos_row[0:1].astype(I32)


SEL_CP = 8


def _sel_sample(page_table, qi, wi, kself, cache_ki, topk):
    n, npg = page_table.shape
    assert npg % SEL_CP == 0 and npg <= PAGE_SIZE
    grid_spec = pltpu.PrefetchScalarGridSpec(
        num_scalar_prefetch=1,
        grid=(n,),
        in_specs=[pl.BlockSpec((1, IDX_HEADS, IDX_DIM), lambda s, pt: (s, 0, 0)),
                  pl.BlockSpec((1, IDX_HEADS, 1), lambda s, pt: (s, 0, 0)),
                  pl.BlockSpec((1, 1, IDX_DIM), lambda s, pt: (s, 0, 0)),
                  pl.BlockSpec(memory_space=pl.ANY)],
        out_specs=[pl.BlockSpec((1, 1, topk), lambda s, pt: (s, 0, 0)),
                   pl.BlockSpec((1, 1, LANES), lambda s, pt: (s, 0, 0))],
        scratch_shapes=[pltpu.VMEM((2, npg, PAGE_SIZE, IDX_DIM), F32), pltpu.VMEM((npg, PAGE_SIZE), F32),
                        pltpu.VMEM((8, LANES), I32), pltpu.VMEM((npg, PAGE_SIZE), I32),
                        pltpu.SemaphoreType.DMA((2,))],
    )
    return pl.pallas_call(
        functools.partial(_sel_sample_body, topk=topk, npg=npg),
        grid_spec=grid_spec,
        out_shape=[jax.ShapeDtypeStruct((n, 1, topk), I32), jax.ShapeDtypeStruct((n, 1, LANES), F32)],
        compiler_params=_cparams(("arbitrary",)),
        name="sel_sample",
    )(page_table, qi, wi, kself, cache_ki)


def _att_sel_body(pt_ref, pos_ref, q_ref, ms_ref, ks_ref, vs_ref, ck_hbm, cv_hbm, o_ref, kbuf, vbuf, sem, *, topk):
    s = pl.program_id(0)
    slot = s % 2

    def request(seq, sl):
        def body(j, c):
            pos = pos_ref[seq, j]
            pg = pt_ref[seq, pos // PAGE_SIZE]
            r = pos % PAGE_SIZE
            pltpu.make_async_copy(ck_hbm.at[pg, r], kbuf.at[sl, j], sem.at[0, sl]).start()
            pltpu.make_async_copy(cv_hbm.at[pg, r], vbuf.at[sl, j], sem.at[1, sl]).start()
            return c
        lax.fori_loop(0, topk, body, 0, unroll=8)

    @pl.when(s == 0)
    def _():
        request(0, 0)

    @pl.when(s + 1 < pl.num_programs(0))
    def _():
        request(s + 1, 1 - slot)

    for h in range(topk // PAGE_SIZE):
        rows = pl.ds(h * PAGE_SIZE, PAGE_SIZE)
        pltpu.make_async_copy(ck_hbm.at[0], kbuf.at[slot, rows], sem.at[0, slot]).wait()
        pltpu.make_async_copy(cv_hbm.at[0], vbuf.at[slot, rows], sem.at[1, slot]).wait()

    q = q_ref[0]
    row_g = lax.broadcasted_iota(I32, (N_HEADS, 1), 0) // GROUP
    lane_g = lax.broadcasted_iota(I32, (1, KV_WIDTH), 1) // HEAD_DIM
    q_bd = jnp.where(row_g == lane_g, jnp.tile(q, (1, N_KV_HEADS)), jnp.zeros((N_HEADS, KV_WIDTH), BF16))
    k2 = jnp.concatenate([kbuf[slot, :, g, :] for g in range(N_KV_HEADS)], axis=1).astype(BF16)
    v2 = jnp.concatenate([vbuf[slot, :, g, :] for g in range(N_KV_HEADS)], axis=1).astype(BF16)
    self_row = ms_ref[0]
    self_f = self_row[:, 0:1]
    n_past = topk - jnp.tile(self_row, (1, topk // LANES))
    valid = lax.broadcasted_iota(I32, (1, topk), 1).astype(F32) < n_past
    sc = jnp.where(valid, _dot_nt(q_bd, k2), -1e30)
    s1 = jnp.sum(q.astype(F32) * ks_ref[0].astype(F32), axis=-1, keepdims=True)
    s1 = jnp.where(self_f > 0.5, s1, -1e30)
    m = jnp.maximum(jnp.max(sc, axis=-1, keepdims=True), jnp.maximum(s1, -1e29))
    pr = jnp.exp(sc - m)
    p1 = jnp.exp(s1 - m)
    l = jnp.sum(pr, axis=-1, keepdims=True) + p1
    pv = _dot(pr.astype(BF16), v2)
    own = jnp.zeros((N_HEADS, HEAD_DIM), F32)
    for g in range(N_KV_HEADS):
        own = jnp.where(row_g == g, pv[:, g * HEAD_DIM:(g + 1) * HEAD_DIM], own)
    o_ref[0] = ((own + p1.astype(BF16).astype(F32) * vs_ref[0].astype(F32)) / l).astype(o_ref.dtype)


def _att_sel(page_table, pos_list, q, mself, kself, vself, cache_k, cache_v):
    n, topk = pos_list.shape
    assert topk % PAGE_SIZE == 0
    seqspec = lambda r, c: pl.BlockSpec((1, r, c), lambda s, pt, pos: (s, 0, 0))
    anyspec = pl.BlockSpec(memory_space=pl.ANY)
    rows = (2, topk, N_KV_HEADS, HEAD_DIM)
    grid_spec = pltpu.PrefetchScalarGridSpec(
        num_scalar_prefetch=2,
        grid=(n,),
        in_specs=[seqspec(N_HEADS, HEAD_DIM), seqspec(1, LANES), seqspec(N_HEADS, HEAD_DIM),
                  seqspec(N_HEADS, HEAD_DIM), anyspec, anyspec],
        out_specs=seqspec(N_HEADS, HEAD_DIM),
        scratch_shapes=[pltpu.VMEM(rows, F32), pltpu.VMEM(rows, F32), pltpu.SemaphoreType.DMA((2, 2))],
    )
    return pl.pallas_call(
        functools.partial(_att_sel_body, topk=topk),
        grid_spec=grid_spec,
        out_shape=jax.ShapeDtypeStruct((n, N_HEADS, HEAD_DIM), BF16),
        compiler_params=_cparams(("arbitrary",)),
        name="att_sel",
    )(page_table, pos_list, q, mself, kself, vself, cache_k, cache_v)


def _mem_att_prompt_body(q_ref, k_ref, v_ref, o_ref):
    scale = MEM_HEAD_DIM ** -0.5
    for h in range(MEM_HEADS):
        sl = slice(h * MEM_HEAD_DIM, (h + 1) * MEM_HEAD_DIM)
        s = _dot_nt(q_ref[:, sl], k_ref[:, sl]) * scale
        m = jnp.max(s, axis=-1, keepdims=True)
        e = jnp.exp(s - m)
        pr = e / jnp.sum(e, axis=-1, keepdims=True)
        o_ref[:, sl] = _dot(pr.astype(BF16), v_ref[:, sl]).astype(o_ref.dtype)


def _mem_att_prompt(mq, mk, mv, batch, seq, tq):
    m = mk.shape[0] // batch
    nb = seq // tq
    return pl.pallas_call(
        _mem_att_prompt_body,
        grid=(batch * nb,),
        in_specs=[pl.BlockSpec((tq, MEM_WIDTH), lambda i: (i, 0)),
                  pl.BlockSpec((m, MEM_WIDTH), lambda i: (i // nb, 0)),
                  pl.BlockSpec((m, MEM_WIDTH), lambda i: (i // nb, 0))],
        out_specs=pl.BlockSpec((tq, MEM_WIDTH), lambda i: (i, 0)),
        out_shape=jax.ShapeDtypeStruct((batch * seq, MEM_WIDTH), BF16),
        compiler_params=_cparams(("parallel",)),
        name="mem_att_prompt",
    )(mq, mk, mv)


def _mem_att_sample_body(q_ref, k_ref, v_ref, o_ref):
    scale = MEM_HEAD_DIM ** -0.5
    q = q_ref[0].astype(F32)
    for h in range(MEM_HEADS):
        sl = slice(h * MEM_HEAD_DIM, (h + 1) * MEM_HEAD_DIM)
        s = jnp.sum(k_ref[0, :, h, :] * q[:, sl], axis=-1, keepdims=True) * scale
        m = jnp.max(s, axis=0, keepdims=True)
        e = jnp.exp(s - m)
        pr = e / jnp.sum(e, axis=0, keepdims=True)
        o_ref[0, :, sl] = jnp.sum(pr * v_ref[0, :, h, :], axis=0, keepdims=True).astype(o_ref.dtype)


def _mem_att_sample(mq, mk, mv):
    n, m, nh, hd = mk.shape
    w = nh * hd
    return pl.pallas_call(
        _mem_att_sample_body,
        grid=(n,),
        in_specs=[pl.BlockSpec((1, 1, w), lambda s: (s, 0, 0)),
                  pl.BlockSpec((1, m, nh, hd), lambda s: (s, 0, 0, 0)),
                  pl.BlockSpec((1, m, nh, hd), lambda s: (s, 0, 0, 0))],
        out_specs=pl.BlockSpec((1, 1, w), lambda s: (s, 0, 0)),
        out_shape=jax.ShapeDtypeStruct((n, 1, w), BF16),
        compiler_params=_cparams(("parallel",)),
        name="mem_att_sample",
    )(mq, mk, mv)


def _router_body(x_ref, w_ref, b_ref, ei_ref, ew_ref, acc_ref):
    k = pl.program_id(1)

    @pl.when(k == 0)
    def _():
        acc_ref[...] = jnp.zeros_like(acc_ref)

    acc_ref[...] += _dot(x_ref[...], w_ref[...], HIGHEST)

    @pl.when(k == pl.num_programs(1) - 1)
    def _():
        lg = acc_ref[...] + b_ref[...]
        lane = lax.broadcasted_iota(I32, lg.shape, 1)
        neg = jnp.float32(-jnp.inf)
        is_g = lane < N_GROUPS
        glm = jnp.where(is_g, lg, neg)
        gmax = jnp.max(glm, axis=-1, keepdims=True)
        g_sel = jnp.min(jnp.where(glm == gmax, lane, LANES), axis=-1, keepdims=True)
        g_prob = 1.0 / jnp.sum(jnp.where(is_g, jnp.exp(lg - gmax), 0.0), axis=-1, keepdims=True)
        e_id = lane - N_GROUPS
        in_grp = (e_id >= 0) & (e_id < N_EXPERTS) & ((e_id // EXPERTS_PER_GROUP) == g_sel)
        el = jnp.where(in_grp, lg, neg)
        m1 = jnp.max(el, axis=-1, keepdims=True)
        i1 = jnp.min(jnp.where(in_grp & (el == m1), lane, LANES), axis=-1, keepdims=True)
        rest = in_grp & (lane != i1)
        el2 = jnp.where(rest, lg, neg)
        m2 = jnp.max(el2, axis=-1, keepdims=True)
        i2 = jnp.min(jnp.where(rest & (el2 == m2), lane, LANES), axis=-1, keepdims=True)
        t = jnp.exp(m2 - m1)
        w1 = g_prob / (1.0 + t)
        w2 = g_prob * t / (1.0 + t)
        ei_ref[...] = jnp.where(lane == 0, i1 - N_GROUPS, jnp.where(lane == 1, i2 - N_GROUPS, 0))
        ew_ref[...] = jnp.where(lane == 0, w1, jnp.where(lane == 1, w2, 0.0))


def _router(x, w, b, tm, tk):
    m, kd = x.shape
    return pl.pallas_call(
        _router_body,
        grid=(m // tm, kd // tk),
        in_specs=[pl.BlockSpec((tm, tk), lambda i, k: (i, k)),
                  pl.BlockSpec((tk, LANES), lambda i, k: (k, 0)),
                  pl.BlockSpec((1, LANES), lambda i, k: (0, 0))],
        out_specs=[pl.BlockSpec((tm, LANES), lambda i, k: (i, 0)), pl.BlockSpec((tm, LANES), lambda i, k: (i, 0))],
        out_shape=[jax.ShapeDtypeStruct((m, LANES), I32), jax.ShapeDtypeStruct((m, LANES), F32)],
        scratch_shapes=[pltpu.VMEM((tm, LANES), F32)],
        compiler_params=_cparams(("parallel", "arbitrary")),
        name="router",
    )(x, w, b)


MOE_BR = 128


def _gather_body(idx_ref, nblk_ref, src_ref, o_ref, sem):
    i = pl.program_id(0)
    g = o_ref.shape[0]

    def row_copy(src_row, dst_row):
        return pltpu.make_async_copy(src_ref.at[pl.ds(src_row, 1)], o_ref.at[pl.ds(dst_row, 1)], sem)

    @pl.when(i < nblk_ref[0])
    def _():
        def issue(r, c):
            row_copy(idx_ref[i * g + r], r).start()
            return c
        lax.fori_loop(0, g, issue, 0, unroll=8)
        pltpu.make_async_copy(src_ref.at[pl.ds(0, g)], o_ref, sem).wait()

    @pl.when(i >= nblk_ref[0])
    def _():
        o_ref[...] = jnp.zeros_like(o_ref)


def _gather_rows(src, idx, nblk, g):
    m = idx.shape[0]
    d = src.shape[1]
    grid_spec = pltpu.PrefetchScalarGridSpec(
        num_scalar_prefetch=2,
        grid=(m // g,),
        in_specs=[pl.BlockSpec(memory_space=pl.ANY)],
        out_specs=pl.BlockSpec((g, d), lambda i, idx, nb: (i, 0)),
        scratch_shapes=[pltpu.SemaphoreType.DMA(())],
    )
    return pl.pallas_call(
        _gather_body,
        grid_spec=grid_spec,
        out_shape=jax.ShapeDtypeStruct((m, d), src.dtype),
        compiler_params=_cparams(("arbitrary",)),
        name="gather_rows",
    )(idx, nblk, src)


def _expert_up_body(be_ref, nblk_ref, x_ref, wg_ref, wu_ref, h_ref, wgb_ref, wub_ref):
    i = pl.program_id(1)
    changed = jnp.logical_or(i == 0, be_ref[i] != be_ref[jnp.maximum(i - 1, 0)])

    @pl.when(jnp.logical_and(i < nblk_ref[0], changed))
    def _():
        wgb_ref[...] = wg_ref[0].astype(BF16)
        wub_ref[...] = wu_ref[0].astype(BF16)

    @pl.when(i < nblk_ref[0])
    def _():
        x = x_ref[...].astype(BF16)
        a = _dot(x, wgb_ref[...])
        u = _dot(x, wub_ref[...])
        h_ref[...] = (a * _sigmoid(a) * u).astype(h_ref.dtype)

    @pl.when(i >= nblk_ref[0])
    def _():
        h_ref[...] = jnp.zeros_like(h_ref)


def _expert_up(block_e, nblk, xs, w_gate, w_up, th):
    nr, d = xs.shape
    nb = nr // MOE_BR
    nh = D_EXPERT // th
    blk = lambda i, nbk: jnp.minimum(i, nbk[0] - 1)
    grid_spec = pltpu.PrefetchScalarGridSpec(
        num_scalar_prefetch=2,
        grid=(nh, nb),
        in_specs=[pl.BlockSpec((MOE_BR, d), lambda j, i, be, nbk: (blk(i, nbk), 0)),
                  pl.BlockSpec((1, d, th), lambda j, i, be, nbk: (be[blk(i, nbk)], 0, j)),
                  pl.BlockSpec((1, d, th), lambda j, i, be, nbk: (be[blk(i, nbk)], 0, j))],
        out_specs=pl.BlockSpec((MOE_BR, th), lambda j, i, be, nbk: (i, j)),
        scratch_shapes=[pltpu.VMEM((d, th), BF16), pltpu.VMEM((d, th), BF16)],
    )
    return pl.pallas_call(
        _expert_up_body,
        grid_spec=grid_spec,
        out_shape=jax.ShapeDtypeStruct((nr, D_EXPERT), BF16),
        compiler_params=_cparams(("arbitrary", "arbitrary")),
        name="expert_up",
    )(block_e, nblk, xs, w_gate, w_up)


def _expert_down_body(be_ref, nblk_ref, h_ref, wd_ref, y_ref, wdb_ref):
    i = pl.program_id(0)
    changed = jnp.logical_or(i == 0, be_ref[i] != be_ref[jnp.maximum(i - 1, 0)])

    @pl.when(jnp.logical_and(i < nblk_ref[0], changed))
    def _():
        wdb_ref[...] = wd_ref[0].astype(BF16)

    @pl.when(i < nblk_ref[0])
    def _():
        y_ref[...] = _dot(h_ref[...], wdb_ref[...])

    @pl.when(i >= nblk_ref[0])
    def _():
        y_ref[...] = jnp.zeros_like(y_ref)


def _expert_down(block_e, nblk, h, w_down):
    nr = h.shape[0]
    d = w_down.shape[2]
    blk = lambda i, nbk: jnp.minimum(i, nbk[0] - 1)
    grid_spec = pltpu.PrefetchScalarGridSpec(
        num_scalar_prefetch=2,
        grid=(nr // MOE_BR,),
        in_specs=[pl.BlockSpec((MOE_BR, D_EXPERT), lambda i, be, nbk: (blk(i, nbk), 0)),
                  pl.BlockSpec((1, D_EXPERT, d), lambda i, be, nbk: (be[blk(i, nbk)], 0, 0))],
        out_specs=pl.BlockSpec((MOE_BR, d), lambda i, be, nbk: (i, 0)),
        scratch_shapes=[pltpu.VMEM((D_EXPERT, d), BF16)],
    )
    return pl.pallas_call(
        _expert_down_body,
        grid_spec=grid_spec,
        out_shape=jax.ShapeDtypeStruct((nr, d), F32),
        compiler_params=_cparams(("arbitrary",)),
        name="expert_down",
    )(block_e, nblk, h, w_down)


def _combine_ln_body(x_ref, y0_ref, y1_ref, ew_ref, g_ref, b_ref, op_ref, os_ref, *, npb):
    i = pl.program_id(0)
    ew = ew_ref[...]
    ff = y0_ref[...] * ew[:, 0:1] + y1_ref[...] * ew[:, 1:2]
    y = _layer_norm_rows(DEEPNORM_ALPHA * x_ref[...] + ff, g_ref[...], b_ref[...])

    @pl.when(i < npb)
    def _():
        op_ref[...] = y

    @pl.when(i >= npb)
    def _():
        os_ref[...] = y


def _combine_ln(x, yg, ew, g, b, tm, n_prompt):
    m, d = x.shape
    nb = m // tm
    npb = n_prompt // tm
    return pl.pallas_call(
        functools.partial(_combine_ln_body, npb=npb),
        grid=(nb,),
        in_specs=[pl.BlockSpec((tm, d), lambda i: (i, 0)),
                  pl.BlockSpec((tm, d), lambda i: (i, 0)),
                  pl.BlockSpec((tm, d), lambda i: (nb + i, 0)),
                  pl.BlockSpec((tm, LANES), lambda i: (i, 0)),
                  pl.BlockSpec((1, d), lambda i: (0, 0)),
                  pl.BlockSpec((1, d), lambda i: (0, 0))],
        out_specs=[pl.BlockSpec((tm, d), lambda i: (jnp.minimum(i, npb - 1), 0)),
                   pl.BlockSpec((tm, d), lambda i: (jnp.maximum(i - npb, 0), 0))],
        out_shape=[jax.ShapeDtypeStruct((n_prompt, d), F32), jax.ShapeDtypeStruct((m - n_prompt, d), F32)],
        compiler_params=_cparams(("arbitrary",)),
        name="combine_ln",
    )(x, yg, yg, ew, g, b)


def _pad_cols(x, n):
    return jnp.pad(x, ((0, 0), (0, n - x.shape[1])))


def _split_w_in(w):
    o = [int(v) for v in np.cumsum([0, RW_PROJ, ATT_WIDTH + 2 * KV_WIDTH + IDX_HEADS * IDX_DIM, IDX_DIM + IDX_HEADS,
                                    2 * D_MODEL])]
    w_rkv = w[:, 0:3 * RW_WIDTH].astype(BF16)
    w_att = w[:, o[1]:o[2]].astype(BF16)
    w_gate = w[:, o[3]:o[4]].astype(BF16)
    w_small = jnp.concatenate([_lora_cols(w[:, 0:RW_PROJ]), _pad_cols(w[:, o[2]:o[3]], LANES)], axis=1).astype(BF16)
    return w_rkv, w_att, w_gate, w_small


def _lora_cols(x):
    return jnp.concatenate([_pad_cols(x[:, 6144:6240], 128), _pad_cols(x[:, 6240:6336], 128), x[:, 6336:6592]], axis=1)


def _pack_rwkv(rw_mu, rw_w0, rw_w2, rw_a0, rw_a2, rw_g2, rw_k_k, rw_k_a, rw_r_k, rw_ln_w, rw_ln_b):
    flat = lambda t: t.reshape(1, RW_WIDTH)
    mu = rw_mu.reshape(1, RW_PROJ)
    rows = [mu[:, 0:2048], mu[:, 2048:4096], mu[:, 4096:6144], flat(rw_w0), flat(rw_a0), flat(rw_k_k),
            flat(rw_k_a), flat(rw_r_k), flat(rw_ln_w), flat(rw_ln_b)]
    prm = jnp.pad(jnp.concatenate(rows, axis=0), ((0, 6), (0, 0)))
    mu_l = jnp.pad(_lora_cols(mu), ((0, 7), (0, 0)))
    w2 = jnp.pad(rw_w2, ((0, 128 - W_LORA), (0, 0))).astype(BF16)
    a2 = jnp.pad(rw_a2, ((0, 128 - A_LORA), (0, 0))).astype(BF16)
    g2 = rw_g2.astype(BF16)
    return prm, mu_l, w2, a2, g2


def _head_indicators(width):
    lane = np.arange(width)[:, None] // RW_HEAD_DIM
    ind = (lane == np.arange(128)[None, :]).astype(np.float32)
    return jnp.asarray(ind), jnp.asarray(ind.T)


def _head_selectors():
    sel = np.zeros((WKV_HQ, WKV_W, RW_HEAD_DIM), np.float32)
    for j in range(WKV_HQ):
        sel[j, j * RW_HEAD_DIM + np.arange(RW_HEAD_DIM), np.arange(RW_HEAD_DIM)] = 1.0
    return jnp.asarray(sel)


def kernel(x_prompt, x_sample, mem_prompt, cache_k, cache_v, cache_idx_k, page_table, state_wkv, state_shift, cache_mem_k, cache_mem_v, w_in, rw_mu, rw_w0, rw_w2, rw_a0, rw_a2, rw_g2, rw_k_k, rw_k_a, rw_r_k, rw_ln_w, rw_ln_b, idx_ln_w, idx_ln_b, w_branch_a, w_branch_b, w_out, ln1_w, ln1_b, w_mem_q, w_mem_k, w_mem_v, w_mem_o, ln2_w, ln2_b, w_router_grp, b_router_grp, w_router_exp, b_router_exp, w_exp_gate, w_exp_up, w_exp_down, ln3_w, ln3_b):
    B, S, D = x_prompt.shape
    DB, DS, _ = x_sample.shape
    assert DS == 1 and cache_k.shape[0] == 1
    TP = B * S
    T = TP + DB
    MP = _round_up(T, 640)
    past = page_table.shape[1] * PAGE_SIZE
    n_mem = mem_prompt.shape[1]
    row1 = lambda a: a.reshape(1, -1)

    def pad_rows(a):
        return jnp.concatenate([a, jnp.zeros((MP - a.shape[0],) + a.shape[1:], a.dtype)], axis=0)

    x_all = pad_rows(jnp.concatenate([x_prompt.reshape(TP, D), x_sample.reshape(DB, D)], axis=0))
    xb = x_all.astype(BF16)
    w_rkv, w_att, w_gate, w_small = _split_w_in(w_in[0])
    z_rkv = _mm(xb, w_rkv, 640, 1024, D, name="in_proj_rkv")
    z_att = _mm(xb, w_att, 640, 1024, D, name="in_proj_att")
    z_gate = _mm(xb, w_gate, 640, 1024, D, name="in_proj_gate")
    z_small = _mm(xb, w_small, 640, S_TOTAL, D, name="in_proj_small")

    prm, mu_l, w2, a2, g2 = _pack_rwkv(rw_mu[0], rw_w0[0], rw_w2[0], rw_a0[0], rw_a2[0], rw_g2[0], rw_k_k[0],
                                       rw_k_a[0], rw_r_k[0], rw_ln_w[0], rw_ln_b[0])
    rw_p, wkv_p = _wkv_prompt(z_rkv, z_small, prm, mu_l, w2, a2, g2, _head_selectors(), B, S)
    ss = state_shift[0]
    ind_f, indt_f = _head_indicators(RW_WIDTH)
    tok = _wkv_tokens_sample(z_rkv, z_small, ss[:, 0:2048], ss[:, 2048:4096], ss[:, 4096:6144], _lora_cols(ss), prm,
                             mu_l, w2, a2, g2, ind_f, indt_f, TP, DB)
    t_r, t_w, t_al, t_be, t_km, t_v, t_g, t_bo = tok
    rowv = lambda a: a.reshape(DB, RW_HEADS, 1, RW_HEAD_DIM)
    colv = lambda a: a.reshape(DB, RW_HEADS, RW_HEAD_DIM, 1)
    y_col, wkv_s = _wkv_step(state_wkv[0], rowv(t_w), rowv(t_al), rowv(t_be), rowv(t_km), rowv(t_r), colv(t_v),
                             colv(t_g), colv(t_bo), rw_ln_w[0].reshape(1, RW_HEADS, RW_HEAD_DIM, 1),
                             rw_ln_b[0].reshape(1, RW_HEADS, RW_HEAD_DIM, 1))
    rw_all = pad_rows(jnp.concatenate([rw_p, y_col.reshape(DB, RW_WIDTH).astype(BF16)], axis=0))

    pos = jnp.concatenate([jnp.tile(jnp.arange(S, dtype=I32), B), jnp.full((MP - TP,), past, I32)])
    tab_a = _rope_tables(pos, ROT_DIM, HEAD_DIM)
    tab_i = _rope_tables(pos, IDX_ROT_DIM, IDX_DIM)
    qb, k_rot, kb, vb, qib, kiw, kid = _prep(z_att, z_small, tab_a, tab_i, _pad_cols(row1(idx_ln_w[0]), LANES),
                                             _pad_cols(row1(idx_ln_b[0]), LANES), 128)
    att_p = _dsa_prompt(qb, kb, vb, qib, kid, kiw, B, S, min(TOPK_MAX, S // 4))
    qi_s = qib[TP:T].reshape(DB, IDX_HEADS, IDX_DIM)
    wi_s = kiw[TP:T, IDX_DIM:IDX_DIM + IDX_HEADS].reshape(DB, IDX_HEADS, 1)
    topk_s = min(TOPK_MAX, (past + DS) // 4)
    pos_sel, mself = _sel_sample(page_table, qi_s, wi_s, kid[TP:T, 0:IDX_DIM].reshape(DB, 1, IDX_DIM),
                                 cache_idx_k[0], topk_s)
    expand = lambda a: jnp.repeat(a[TP:T].reshape(DB, N_KV_HEADS, HEAD_DIM), GROUP, axis=1)
    att_s = _att_sel(page_table, pos_sel.reshape(DB, topk_s), qb[TP:T].reshape(DB, N_HEADS, HEAD_DIM), mself,
                     expand(kb), expand(vb), cache_k[0], cache_v[0])
    att_all = pad_rows(jnp.concatenate([att_p, att_s.reshape(DB, ATT_WIDTH)], axis=0))

    merged = _branch_merge(rw_all, att_all, w_branch_a[0].astype(BF16), w_branch_b[0].astype(BF16), z_gate, 640, 1024)
    x1, x1b = _mm_ln(merged, w_out[0].astype(BF16), x_all, row1(ln1_w[0]), row1(ln1_b[0]), 320, 512, name="out_ln1")

    mq = _mm(x1b, w_mem_q[0].astype(BF16), 640, MEM_WIDTH, D, out_dtype=BF16, name="mem_q")
    mem2d = mem_prompt.reshape(B * n_mem, D).astype(BF16)
    mem_k = _mm(mem2d, w_mem_k[0].astype(BF16), B * n_mem, MEM_WIDTH, D, name="mem_k")
    mem_v = _mm(mem2d, w_mem_v[0].astype(BF16), B * n_mem, MEM_WIDTH, D, name="mem_v")
    ma_p = _mem_att_prompt(mq, mem_k.astype(BF16), mem_v.astype(BF16), B, S, 512)
    ma_s = _mem_att_sample(mq[TP:T].reshape(DB, 1, MEM_WIDTH), cache_mem_k[0], cache_mem_v[0])
    ma_all = pad_rows(jnp.concatenate([ma_p, ma_s.reshape(DB, MEM_WIDTH)], axis=0))
    x2, _ = _mm_ln(ma_all, w_mem_o[0].astype(BF16), x1, row1(ln2_w[0]), row1(ln2_b[0]), 320, 512, name="mem_o_ln2")

    w_r = _pad_cols(jnp.concatenate([w_router_grp[0], w_router_exp[0]], axis=1), LANES)
    b_r = _pad_cols(row1(jnp.concatenate([b_router_grp[0], b_router_exp[0]])), LANES)
    e_idx, e_w = _router(x2, w_r, b_r, 640, 1024)
    n_assign = 2 * T
    flat_e = e_idx[:T, 0:2].reshape(n_assign)
    order = jnp.argsort(flat_e).astype(I32)
    rank = jnp.argsort(order).astype(I32)
    experts = jnp.arange(N_EXPERTS, dtype=I32)
    onehot = flat_e[:, None] == experts[None, :]
    counts = jnp.sum(onehot, axis=0, dtype=I32)
    padded = (counts + MOE_BR - 1) // MOE_BR * MOE_BR
    pad_end = jnp.cumsum(padded)
    pad_start = pad_end - padded
    start = jnp.cumsum(counts) - counts
    slot = (rank + jnp.sum(jnp.where(onehot, (pad_start - start)[None, :], 0), axis=1)).reshape(T, 2)
    n_blocks = -(-n_assign // MOE_BR) + N_EXPERTS
    blk_row0 = jnp.arange(n_blocks, dtype=I32) * MOE_BR
    block_e = jnp.minimum(jnp.sum(pad_end[None, :] <= blk_row0[:, None], axis=1, dtype=I32), N_EXPERTS - 1)
    blk_hot = block_e[:, None] == experts[None, :]
    pick = lambda tab: jnp.sum(jnp.where(blk_hot, tab[None, :], 0), axis=1)
    j_in_e = (blk_row0 - pick(pad_start))[:, None] + jnp.arange(MOE_BR, dtype=I32)[None, :]
    src = jnp.clip(pick(start)[:, None] + j_in_e, 0, n_assign - 1)
    row_token = jnp.where(j_in_e < pick(counts)[:, None], order[src] // 2, 0).reshape(n_blocks * MOE_BR)
    n_used = (pad_end[-1] // MOE_BR).astype(I32).reshape(1)
    xs = _gather_rows(x2, row_token, n_used, MOE_BR)
    hid = _expert_up(block_e, n_used, xs, w_exp_gate[0], w_exp_up[0], 256)
    y_rows = _expert_down(block_e, n_used, hid, w_exp_down[0])
    slot_pad = jnp.concatenate([jnp.pad(slot[:, 0], (0, MP - T)), jnp.pad(slot[:, 1], (0, MP - T))])
    y_tok = _gather_rows(y_rows, slot_pad, jnp.full((1,), 2 * MP // MOE_BR, I32), MOE_BR)
    y_p, y_s = _combine_ln(x2, y_tok, e_w, row1(ln3_w[0]), row1(ln3_b[0]), 128, TP)

    kv5 = lambda a, n, s: a.reshape(1, n, s, N_KV_HEADS, HEAD_DIM)
    va = z_att[:, A_VA:A_VA + KV_WIDTH]
    ki = kiw[:, 0:IDX_DIM]
    last = lambda a: jnp.concatenate([a[(b + 1) * S - 1:(b + 1) * S] for b in range(B)] + [a[TP:T]], axis=0)
    zl, zsl = last(z_rkv), last(z_small)
    shift_cols = jnp.concatenate([zl, zsl[:, S_LORA:S_LORA + W_LORA], zsl[:, S_LORA + 128:S_LORA + 128 + A_LORA],
                                  zsl[:, S_LORA + 256:S_LORA + 512]], axis=1)
    mem5 = lambda a: a.reshape(1, B, n_mem, MEM_HEADS, MEM_HEAD_DIM)
    return (y_p.reshape(B, S, D), y_s[:DB].reshape(DB, DS, D),
            kv5(k_rot[:TP], B, S), kv5(va[:TP], B, S), ki[:TP].reshape(1, B, S, IDX_DIM),
            wkv_p[None], shift_cols[:B][None], mem5(mem_k), mem5(mem_v),
            kv5(k_rot[TP:T], DB, DS), kv5(va[TP:T], DB, DS), ki[TP:T].reshape(1, DB, DS, IDX_DIM),
            wkv_s[None], shift_cols[B:][None])
```

```python
import functools
import math

import jax
import jax.numpy as jnp
import numpy as np
from jax import lax
from jax.experimental import pallas as pl
from jax.experimental.pallas import tpu as pltpu

F32 = jnp.float32
BF16 = jnp.bfloat16
I32 = jnp.int32
HIGHEST = lax.Precision.HIGHEST

D_MODEL = 4096
RW_HEAD_DIM = 64
RW_HEADS = 32
RW_WIDTH = 2048
W_LORA = 96
A_LORA = 96
G_LORA = 256
RW_PROJ = 3 * RW_WIDTH + W_LORA + A_LORA + G_LORA
RW_GN_EPS = 64e-5
HEAD_DIM = 128
N_HEADS = 16
N_KV_HEADS = 4
GROUP = 4
ATT_WIDTH = 2048
KV_WIDTH = 512
ROT_DIM = 32
ROPE_THETA = 500000.0
IDX_HEADS = 16
IDX_DIM = 64
IDX_ROT_DIM = 16
TOPK_MAX = 256
PAGE_SIZE = 128
MEM_HEADS = 4
MEM_HEAD_DIM = 128
MEM_WIDTH = 512
N_GROUPS = 8
EXPERTS_PER_GROUP = 8
N_EXPERTS = 64
D_EXPERT = 512
LN_EPS = 1e-5
DEEPNORM_ALPHA = 2.0 ** 0.25
EXP_M05 = math.exp(-0.5)

LANES = 128
SUBLANES = 8
VMEM_LIMIT = 56 * 1024 * 1024

C_R, C_K, C_V = 0, 2048, 4096
A_Q, A_KA, A_VA, A_IQ = 0, 2048, 2560, 3072
G_A, G_B = 0, 4096
S_LORA, S_IKW, S_TOTAL = 0, 512, 640

INT_MIN = -(2 ** 31)


def _round_up(n, m):
    return -(-n // m) * m


def _cparams(sem):
    return pltpu.CompilerParams(dimension_semantics=sem, vmem_limit_bytes=VMEM_LIMIT)


def _dot(a, b, precision=None):
    return jnp.dot(a, b, preferred_element_type=F32, precision=precision)


def _dot_nt(a, b, precision=None):
    return lax.dot_general(a, b, (((1,), (1,)), ((), ())), preferred_element_type=F32, precision=precision)


def _sigmoid(x):
    return 1.0 / (1.0 + jnp.exp(-x))


def _mm_body(x_ref, w_ref, o_ref, acc_ref):
    k = pl.program_id(2)

    @pl.when(k == 0)
    def _():
        acc_ref[...] = jnp.zeros_like(acc_ref)

    acc_ref[...] += _dot(x_ref[...], w_ref[...])

    @pl.when(k == pl.num_programs(2) - 1)
    def _():
        o_ref[...] = acc_ref[...].astype(o_ref.dtype)


def _mm_fullk_body(x_ref, w_ref, o_ref):
    o_ref[...] = _dot(x_ref[...], w_ref[...]).astype(o_ref.dtype)


def _mm(x, w, tm, tn, tk, out_dtype=F32, name="mm"):
    m, kd = x.shape
    n = w.shape[1]
    if tk == kd:
        return pl.pallas_call(
            _mm_fullk_body,
            grid=(m // tm, n // tn),
            in_specs=[pl.BlockSpec((tm, kd), lambda i, j: (i, 0)),
                      pl.BlockSpec((kd, tn), lambda i, j: (0, j))],
            out_specs=pl.BlockSpec((tm, tn), lambda i, j: (i, j)),
            out_shape=jax.ShapeDtypeStruct((m, n), out_dtype),
            compiler_params=_cparams(("parallel", "parallel")),
            name=name,
        )(x, w)
    return pl.pallas_call(
        _mm_body,
        grid=(m // tm, n // tn, kd // tk),
        in_specs=[pl.BlockSpec((tm, tk), lambda i, j, k: (i, k)),
                  pl.BlockSpec((tk, tn), lambda i, j, k: (k, j))],
        out_specs=pl.BlockSpec((tm, tn), lambda i, j, k: (i, j)),
        out_shape=jax.ShapeDtypeStruct((m, n), out_dtype),
        scratch_shapes=[pltpu.VMEM((tm, tn), F32)],
        compiler_params=_cparams(("parallel", "parallel", "arbitrary")),
        name=name,
    )(x, w)


def _layer_norm_rows(x, g, b):
    mu = jnp.mean(x, axis=-1, keepdims=True)
    d = x - mu
    var = jnp.mean(d * d, axis=-1, keepdims=True)
    return d * lax.rsqrt(var + LN_EPS) * g + b


def _mm_ln_body(x_ref, w_ref, res_ref, g_ref, b_ref, o_ref, ob_ref, y_ref, *, tn):
    j = pl.program_id(1)
    y_ref[:, pl.ds(pl.multiple_of(j * tn, tn), tn)] = _dot(x_ref[...], w_ref[...])

    @pl.when(j == pl.num_programs(1) - 1)
    def _():
        y = _layer_norm_rows(DEEPNORM_ALPHA * res_ref[...] + y_ref[...], g_ref[...], b_ref[...])
        o_ref[...] = y
        ob_ref[...] = y.astype(BF16)


def _mm_ln(x, w, res, g, b, tm, tn, name="mm_ln"):
    m, kd = x.shape
    n = w.shape[1]
    return pl.pallas_call(
        functools.partial(_mm_ln_body, tn=tn),
        grid=(m // tm, n // tn),
        in_specs=[pl.BlockSpec((tm, kd), lambda i, j: (i, 0)),
                  pl.BlockSpec((kd, tn), lambda i, j: (0, j)),
                  pl.BlockSpec((tm, n), lambda i, j: (i, 0)),
                  pl.BlockSpec((1, n), lambda i, j: (0, 0)),
                  pl.BlockSpec((1, n), lambda i, j: (0, 0))],
        out_specs=[pl.BlockSpec((tm, n), lambda i, j: (i, 0)),
                   pl.BlockSpec((tm, n), lambda i, j: (i, 0))],
        out_shape=[jax.ShapeDtypeStruct((m, n), F32), jax.ShapeDtypeStruct((m, n), BF16)],
        scratch_shapes=[pltpu.VMEM((tm, n), F32)],
        compiler_params=_cparams(("parallel", "arbitrary")),
        name=name,
    )(x, w, res, g, b)


def _branch_merge_body(rw_ref, at_ref, wa_ref, wb_ref, ga_ref, gb_ref, o_ref):
    a = _dot(rw_ref[...], wa_ref[...])
    b = _dot(at_ref[...], wb_ref[...])
    o_ref[...] = (_sigmoid(ga_ref[...]) * a + _sigmoid(gb_ref[...]) * b).astype(o_ref.dtype)


def _branch_merge(rw, att, wa, wb, z, tm, tn):
    m = rw.shape[0]
    n = wa.shape[1]
    ga0, gb0 = G_A // tn, G_B // tn
    return pl.pallas_call(
        _branch_merge_body,
        grid=(m // tm, n // tn),
        in_specs=[pl.BlockSpec((tm, RW_WIDTH), lambda i, j: (i, 0)),
                  pl.BlockSpec((tm, ATT_WIDTH), lambda i, j: (i, 0)),
                  pl.BlockSpec((RW_WIDTH, tn), lambda i, j: (0, j)),
                  pl.BlockSpec((ATT_WIDTH, tn), lambda i, j: (0, j)),
                  pl.BlockSpec((tm, tn), lambda i, j: (i, ga0 + j)),
                  pl.BlockSpec((tm, tn), lambda i, j: (i, gb0 + j))],
        out_specs=pl.BlockSpec((tm, tn), lambda i, j: (i, j)),
        out_shape=jax.ShapeDtypeStruct((m, n), BF16),
        compiler_params=_cparams(("parallel", "parallel")),
        name="branch_merge",
    )(rw, att, wa, wb, z, z)


def _seg_sum(x, ind, ind_t):
    return _dot(_dot(x, ind, HIGHEST), ind_t, HIGHEST)


def _split_bf16(x, parts):
    out = []
    for _ in range(parts):
        t = x.astype(BF16)
        out.append(t)
        x = x - t.astype(F32)
    return out


def _seg_sum_quads(x, bd):
    outs = []
    for q in range(x.shape[1] // WKV_W):
        hi, lo = _split_bf16(x[:, q * WKV_W:(q + 1) * WKV_W], 2)
        outs.append(_dot(hi, bd) + _dot(lo, bd))
    return jnp.concatenate(outs, axis=1) if len(outs) > 1 else outs[0]


def _rwkv_tokens(zr, zk, zv, zl, pr, pk, pv, plo, prm, mu_l, w2, a2, g2, seg):
    r = zr + (pr - zr) * prm[0:1]
    kx = zk + (pk - zk) * prm[1:2]
    v = zv + (pv - zv) * prm[2:3]
    zsl = zl + (plo - zl) * mu_l
    tw = jnp.tanh(zsl[:, 0:128]).astype(BF16)
    xw = prm[3:4] + _dot(tw, w2)
    lw = -EXP_M05 * _sigmoid(xw)
    a = _sigmoid(prm[4:5] + _dot(zsl[:, 128:256].astype(BF16), a2))
    g = _dot(_sigmoid(zsl[:, 256:512]).astype(BF16), g2)
    kk = kx * prm[5:6]
    n2 = seg(kk * kk)
    kkn = kk / jnp.maximum(jnp.sqrt(n2), 1e-12)
    kmod = kx * (1.0 + (a - 1.0) * prm[6:7])
    return r, lw, kmod, v, kkn, a, g


def _rwkv_post(y, r, kmod, v, g, prm, seg):
    inv_n = 1.0 / RW_HEAD_DIM
    mean = seg(y) * inv_n
    d = y - mean
    var = seg(d * d) * inv_n
    yn = d * lax.rsqrt(var + RW_GN_EPS) * prm[8:9] + prm[9:10]
    bonus = seg(r * kmod * prm[7:8]) * v
    return (yn + bonus) * g


WKV_C = 64
WKV_HQ = 4
WKV_W = WKV_HQ * RW_HEAD_DIM
WKV_QPS = 8


def _wkv_chunk_body(zr_ref, zk_ref, zv_ref, zl_ref, prm_ref, mul_ref, w2_ref, a2_ref, g2_ref,
                    sel_ref, o_ref, so_ref, s_ref, cr_ref, ck_ref, cv_ref, cl_ref):
    c = pl.program_id(2)
    C = WKV_C
    W = WKV_W

    @pl.when(c == 0)
    def _():
        s_ref[...] = jnp.zeros_like(s_ref)
        cr_ref[...] = jnp.zeros_like(cr_ref)
        ck_ref[...] = jnp.zeros_like(ck_ref)
        cv_ref[...] = jnp.zeros_like(cv_ref)
        cl_ref[...] = jnp.zeros_like(cl_ref)

    rows = lax.broadcasted_iota(I32, (C, 1), 0)

    def shifted(z, carry_ref):
        prev = jnp.where(rows == 0, carry_ref[0:1, :], pltpu.roll(z, 1, 0))
        carry_ref[0:1, :] = z[C - 1:C, :]
        return prev

    zr, zk, zv, zl = zr_ref[...], zk_ref[...], zv_ref[...], zl_ref[...]
    pr, pk, pv, plo = shifted(zr, cr_ref), shifted(zk, ck_ref), shifted(zv, cv_ref), shifted(zl, cl_ref)
    prm = prm_ref[...]
    lane_head = lax.broadcasted_iota(I32, (1, W), 1) // RW_HEAD_DIM
    hv = lax.broadcasted_iota(I32, (W, 1), 0) // RW_HEAD_DIM
    bd = (hv == lane_head).astype(BF16)
    seg = lambda x: _seg_sum_quads(x, bd)
    r, lw, kmod, v, kkn, a, g = _rwkv_tokens(zr, zk, zv, zl, pr, pk, pv, plo, prm, mul_ref[0:1, :],
                                            w2_ref[...], a2_ref[...], g2_ref[...], seg)
    al = -kkn
    be = kkn * a

    ti = lax.broadcasted_iota(I32, (C, C), 0)
    tj = lax.broadcasted_iota(I32, (C, C), 1)
    tri = (tj <= ti).astype(BF16)
    cum = sum(_dot(tri, part) for part in _split_bf16(lw, 3))
    cum_l = cum[C - 1:C, :]
    p_inv = jnp.exp(-cum)
    p_rel = jnp.exp(cum_l - cum)
    ab = al * jnp.exp(cum - lw)
    rb = r * jnp.exp(cum)
    bt = (be * p_inv).astype(BF16)
    kt = (kmod * p_inv).astype(BF16)
    bk = jnp.concatenate([be * p_rel, kmod * p_rel], axis=0).astype(BF16)
    ar = jnp.concatenate([ab, rb], axis=0)
    pc = jnp.exp(cum_l)

    n4 = WKV_HQ * C
    bi = lax.broadcasted_iota(I32, (n4, n4), 0)
    bj = lax.broadcasted_iota(I32, (n4, n4), 1)
    same = (bi // C) == (bj // C)
    tri_s4 = same & ((bj % C) < (bi % C))
    tri_i4 = same & ((bj % C) <= (bi % C))
    eye4 = (bi == bj).astype(F32)
    masks = [lane_head == j for j in range(WKV_HQ)]

    def stack(x):
        return jnp.concatenate([jnp.where(m, x, jnp.zeros_like(x)) for m in masks], axis=0)

    def block_sum(x):
        return sum(x[j * C:(j + 1) * C] for j in range(WKV_HQ))

    def bdot(a, b):
        return _dot(a.astype(BF16), b.astype(BF16))

    qs = []
    for q in range(WKV_QPS):
        sl = slice(q * W, (q + 1) * W)
        v_q = v[:, sl]
        lhs = jnp.concatenate([stack(ab[:, sl]), stack(rb[:, sl])], axis=0).astype(BF16)
        abr = _dot_nt(lhs, stack(bt[:, sl]))
        akr = _dot_nt(lhs, stack(kt[:, sl]))
        qs.append(dict(sl=sl, v=v_q, v_s=stack(v_q), s0=s_ref[q],
                       x=jnp.where(tri_s4, abr[0:n4], 0.0), a_rb=jnp.where(tri_i4, abr[n4:2 * n4], 0.0),
                       a_ak=jnp.where(tri_s4, akr[0:n4], 0.0), a_rk=jnp.where(tri_i4, akr[n4:2 * n4], 0.0)))
    for d in qs:
        d['pw'] = [d['x']]
    for _ in range(5):
        for d in qs:
            d['pw'].append(bdot(d['pw'][-1], d['pw'][-1]))
    for d in qs:
        pw = d['pw']
        pr_ = [eye4 + pw[2 * i] + pw[2 * i + 1] + bdot(pw[2 * i], pw[2 * i + 1]) for i in range(3)]
        d['t'] = bdot(bdot(pr_[0], pr_[1]), pr_[2])
    for d in qs:
        gs = _dot_nt(ar[:, d['sl']].astype(BF16), d['s0'].astype(BF16))
        d['g_r'] = gs[C:2 * C]
        d['w_s'] = stack(gs[0:C]) + bdot(d['a_ak'], d['v_s'])
    for d in qs:
        d['u_s'] = bdot(d['t'], d['w_s'])
    ys = []
    for q, d in enumerate(qs):
        yv = bdot(jnp.concatenate([d['a_rb'], d['a_rk']], axis=1), jnp.concatenate([d['u_s'], d['v_s']], axis=0))
        ys.append(d['g_r'] + block_sum(yv))
        uv_t = jnp.concatenate([block_sum(d['u_s']), d['v']], axis=0).T.astype(BF16)
        upd = _dot(uv_t, bk[:, d['sl']])
        s_ref[q] = d['s0'] * pc[:, d['sl']] + jnp.where(hv == lane_head, upd, 0.0)

    y = jnp.concatenate(ys, axis=1) if WKV_QPS > 1 else ys[0]
    o_ref[...] = _rwkv_post(y, r, kmod, v, g, prm, seg).astype(o_ref.dtype)

    @pl.when(c == pl.num_programs(2) - 1)
    def _():
        for q in range(WKV_QPS):
            for j in range(WKV_HQ):
                rows_j = s_ref[q, j * RW_HEAD_DIM:(j + 1) * RW_HEAD_DIM, :]
                so_ref[0, q * WKV_HQ + j] = _dot(rows_j, sel_ref[j], HIGHEST)


def _wkv_prompt(z, zs, prm, mu_l, w2, a2, g2, sel, batch, seq):
    nc = seq // WKV_C
    WS = WKV_W * WKV_QPS
    nq = RW_WIDTH // WS
    row = lambda b, q, c: b * nc + c
    return pl.pallas_call(
        _wkv_chunk_body,
        grid=(batch, nq, nc),
        in_specs=[pl.BlockSpec((WKV_C, WS), lambda b, q, c: (row(b, q, c), C_R // WS + q)),
                  pl.BlockSpec((WKV_C, WS), lambda b, q, c: (row(b, q, c), C_K // WS + q)),
                  pl.BlockSpec((WKV_C, WS), lambda b, q, c: (row(b, q, c), C_V // WS + q)),
                  pl.BlockSpec((WKV_C, 512), lambda b, q, c: (row(b, q, c), S_LORA // 512)),
                  pl.BlockSpec((16, WS), lambda b, q, c: (0, q)),
                  pl.BlockSpec((8, 512), lambda b, q, c: (0, 0)),
                  pl.BlockSpec((128, WS), lambda b, q, c: (0, q)),
                  pl.BlockSpec((128, WS), lambda b, q, c: (0, q)),
                  pl.BlockSpec((256, WS), lambda b, q, c: (0, q)),
                  pl.BlockSpec((WKV_HQ, WKV_W, RW_HEAD_DIM), lambda b, q, c: (0, 0, 0))],
        out_specs=[pl.BlockSpec((WKV_C, WS), lambda b, q, c: (row(b, q, c), q)),
                   pl.BlockSpec((1, WKV_HQ * WKV_QPS, RW_HEAD_DIM, RW_HEAD_DIM), lambda b, q, c: (b, q, 0, 0))],
        out_shape=[jax.ShapeDtypeStruct((batch * seq, RW_WIDTH), BF16),
                   jax.ShapeDtypeStruct((batch, RW_HEADS, RW_HEAD_DIM, RW_HEAD_DIM), F32)],
        scratch_shapes=[pltpu.VMEM((WKV_QPS, WKV_W, WKV_W), F32), pltpu.VMEM((8, WS), F32), pltpu.VMEM((8, WS), F32),
                        pltpu.VMEM((8, WS), F32), pltpu.VMEM((8, 512), F32)],
        compiler_params=_cparams(("parallel", "parallel", "arbitrary")),
        name="wkv_prompt",
    )(z, z, z, zs, prm, mu_l, w2, a2, g2, sel)


def _wkv_tok_body(zr_ref, zk_ref, zv_ref, zl_ref, pr_ref, pk_ref, pv_ref, pl_ref, prm_ref, mul_ref, w2_ref, a2_ref,
                  g2_ref, ind_ref, indt_ref, r_ref, w_ref, al_ref, be_ref, km_ref, v_ref, g_ref, bo_ref):
    prm = prm_ref[...]
    ind, ind_t = ind_ref[...], indt_ref[...]
    seg = lambda x: _seg_sum(x, ind, ind_t)
    r, lw, kmod, v, kkn, a, g = _rwkv_tokens(zr_ref[...], zk_ref[...], zv_ref[...], zl_ref[...], pr_ref[...],
                                            pk_ref[...], pv_ref[...], pl_ref[...], prm, mul_ref[0:1, :],
                                            w2_ref[...], a2_ref[...], g2_ref[...], seg)
    r_ref[...] = r
    w_ref[...] = jnp.exp(lw)
    al_ref[...] = -kkn
    be_ref[...] = kkn * a
    km_ref[...] = kmod
    v_ref[...] = v
    g_ref[...] = g
    bo_ref[...] = seg(r * kmod * prm[7:8])


def _wkv_tokens_sample(z, zs, prev_r, prev_k, prev_v, prev_l, prm, mu_l, w2, a2, g2, ind, ind_t, row0, n):
    rb = row0 // n
    full = lambda a: pl.BlockSpec(a.shape, lambda i: (0,) * a.ndim)
    zspec = lambda w, c0: pl.BlockSpec((n, w), lambda i: (rb, c0 // w))
    return pl.pallas_call(
        _wkv_tok_body,
        grid=(1,),
        in_specs=[zspec(RW_WIDTH, C_R), zspec(RW_WIDTH, C_K), zspec(RW_WIDTH, C_V), zspec(512, S_LORA),
                  full(prev_r), full(prev_k), full(prev_v), full(prev_l), full(prm), full(mu_l), full(w2), full(a2),
                  full(g2), full(ind), full(ind_t)],
        out_specs=[pl.BlockSpec((n, RW_WIDTH), lambda i: (0, 0))] * 8,
        out_shape=[jax.ShapeDtypeStruct((n, RW_WIDTH), F32)] * 8,
        compiler_params=_cparams(("arbitrary",)),
        name="wkv_tokens_sample",
    )(z, z, z, zs, prev_r, prev_k, prev_v, prev_l, prm, mu_l, w2, a2, g2, ind, ind_t)


def _wkv_step_body(s_ref, w_ref, al_ref, be_ref, km_ref, r_ref, v_ref, g_ref, bo_ref, lnw_ref, lnb_ref, o_ref, so_ref):
    s = s_ref[...]
    vcol = v_ref[...]
    sa = jnp.sum(s * al_ref[...], axis=-1, keepdims=True)
    s2 = s * w_ref[...] + sa * be_ref[...] + vcol * km_ref[...]
    so_ref[...] = s2
    y = jnp.sum(s2 * r_ref[...], axis=-1, keepdims=True)
    mean = jnp.mean(y, axis=2, keepdims=True)
    d = y - mean
    var = jnp.mean(d * d, axis=2, keepdims=True)
    yn = d * lax.rsqrt(var + RW_GN_EPS) * lnw_ref[...] + lnb_ref[...]
    o_ref[...] = (yn + bo_ref[...] * vcol) * g_ref[...]


def _wkv_step(state, w, al, be, km, r, v, g, bo, lnw, lnb):
    n, h = state.shape[0], state.shape[1]
    hq = 2
    rowspec = pl.BlockSpec((n, hq, 1, RW_HEAD_DIM), lambda q: (0, q, 0, 0))
    colspec = pl.BlockSpec((n, hq, RW_HEAD_DIM, 1), lambda q: (0, q, 0, 0))
    pcol = pl.BlockSpec((1, hq, RW_HEAD_DIM, 1), lambda q: (0, q, 0, 0))
    sspec = pl.BlockSpec((n, hq, RW_HEAD_DIM, RW_HEAD_DIM), lambda q: (0, q, 0, 0))
    return pl.pallas_call(
        _wkv_step_body,
        grid=(h // hq,),
        in_specs=[sspec, rowspec, rowspec, rowspec, rowspec, rowspec, colspec, colspec, colspec, pcol, pcol],
        out_specs=[colspec, sspec],
        out_shape=[jax.ShapeDtypeStruct((n, h, RW_HEAD_DIM, 1), F32), jax.ShapeDtypeStruct(state.shape, F32)],
        compiler_params=_cparams(("parallel",)),
        name="wkv_step",
    )(state, w, al, be, km, r, v, g, bo, lnw, lnb)


def _rope_tables(pos, rot_dim, period):
    half = rot_dim // 2
    t = pos.shape[0]
    inv_freq = ROPE_THETA ** (-jnp.arange(half, dtype=F32) / half)
    ang = pos.astype(F32)[:, None] * inv_freq[None, :]
    cos, sin = jnp.cos(ang), jnp.sin(ang)
    zh = jnp.zeros((t, half), F32)
    rest = period - rot_dim
    c = jnp.concatenate([cos, cos, jnp.ones((t, rest), F32)], axis=1)
    s1 = jnp.concatenate([-sin, zh, jnp.zeros((t, rest), F32)], axis=1)
    s2 = jnp.concatenate([zh, sin, jnp.zeros((t, rest), F32)], axis=1)
    rep = LANES // period
    return jnp.stack([jnp.tile(a, (1, rep)) for a in (c, s1, s2)], axis=0)


def _rope(x, tab, half):
    n = x.shape[1]
    rep = n // LANES
    c, s1, s2 = [jnp.tile(tab[i], (1, rep)) if rep > 1 else tab[i] for i in range(3)]
    return x * c + pltpu.roll(x, n - half, 1) * s1 + pltpu.roll(x, half, 1) * s2


def _prep_body(q_ref, ka_ref, va_ref, iq_ref, ikw_ref, ta_ref, ti_ref, lnw_ref, lnb_ref,
               qo_ref, ko_ref, kb_ref, vb_ref, qio_ref, kio_ref, kid_ref):
    ta = ta_ref[...]
    ti = ti_ref[...]
    qo_ref[...] = (_rope(q_ref[...], ta, ROT_DIM // 2) * (HEAD_DIM ** -0.5)).astype(BF16)
    k = _rope(ka_ref[...], ta, ROT_DIM // 2)
    ko_ref[...] = k
    kb_ref[...] = k.astype(BF16)
    vb_ref[...] = va_ref[...].astype(BF16)
    qio_ref[...] = _rope(iq_ref[...], ti, IDX_ROT_DIM // 2).astype(BF16)
    x = ikw_ref[...]
    lane = lax.broadcasted_iota(I32, x.shape, 1)
    is_k = lane < IDX_DIM
    mu = jnp.sum(jnp.where(is_k, x, 0.0), axis=-1, keepdims=True) * (1.0 / IDX_DIM)
    d = jnp.where(is_k, x - mu, 0.0)
    var = jnp.sum(d * d, axis=-1, keepdims=True) * (1.0 / IDX_DIM)
    kn = d * lax.rsqrt(var + LN_EPS) * lnw_ref[...] + lnb_ref[...]
    kr = _rope(kn, ti, IDX_ROT_DIM // 2)
    kr = jnp.where(is_k, kr, 0.0)
    kio_ref[...] = jnp.where(is_k, kr, x * (IDX_HEADS ** -0.5))
    kid_ref[...] = (kr + pltpu.roll(kr, IDX_DIM, 1)).astype(BF16)


def _prep(z, zs, tab_a, tab_i, ln_w, ln_b, tm):
    m = z.shape[0]
    row = lambda w, c0: pl.BlockSpec((tm, w), lambda i: (i, c0 // w))
    outs = [(ATT_WIDTH, BF16), (KV_WIDTH, F32), (KV_WIDTH, BF16), (KV_WIDTH, BF16),
            (IDX_HEADS * IDX_DIM, BF16), (LANES, F32), (LANES, BF16)]
    return pl.pallas_call(
        _prep_body,
        grid=(m // tm,),
        in_specs=[row(ATT_WIDTH, A_Q), row(KV_WIDTH, A_KA), row(KV_WIDTH, A_VA), row(IDX_HEADS * IDX_DIM, A_IQ),
                  row(LANES, S_IKW),
                  pl.BlockSpec((3, tm, LANES), lambda i: (0, i, 0)),
                  pl.BlockSpec((3, tm, LANES), lambda i: (0, i, 0)),
                  pl.BlockSpec((1, LANES), lambda i: (0, 0)),
                  pl.BlockSpec((1, LANES), lambda i: (0, 0))],
        out_specs=[pl.BlockSpec((tm, w), lambda i: (i, 0)) for w, _ in outs],
        out_shape=[jax.ShapeDtypeStruct((m, w), dt) for w, dt in outs],
        compiler_params=_cparams(("parallel",)),
        name="prep",
    )(z, z, z, z, zs, tab_a, tab_i, ln_w, ln_b)


DSA_QB = 128
DSA_TK = 512
DSA_ATK = 256


def _float_key(s):
    b = pltpu.bitcast(s, I32)
    return b ^ ((b >> 31) & 0x7FFFFFFF)


def _kth_threshold(count_ge, topk):
    def step(b, thr):
        cand = thr + jnp.left_shift(jnp.int32(1), 31 - b)
        return jnp.where(count_ge(cand) >= topk, cand, thr)
    return lax.fori_loop(0, 32, step, jnp.full((DSA_QB, 1), INT_MIN, I32))


def _dsa_prompt_body(q_ref, kb_ref, vb_ref, qi_ref, kid_ref, kiw_ref, o_ref, keys_ref, jcut_ref, m_ref, l_ref,
                     acc_ref, qs_ref, *, topk, seq):
    i = pl.program_id(1)
    nt = (i * DSA_QB + DSA_QB + DSA_TK - 1) // DSA_TK
    qpos = i * DSA_QB + lax.broadcasted_iota(I32, (DSA_QB, 1), 0)
    col0 = lax.broadcasted_iota(I32, (1, DSA_TK), 1)
    lane = lax.broadcasted_iota(I32, (1, LANES), 1)
    kiw = kiw_ref[...] * (IDX_DIM ** -0.5)

    def score_tile(t, carry):
        kd = kid_ref[pl.ds(t * DSA_TK, DSA_TK), :]
        s = jnp.zeros((DSA_QB, DSA_TK), F32)
        for h in range(IDX_HEADS):
            qt = qi_ref[:, (h // 2) * LANES:(h // 2 + 1) * LANES]
            qh = jnp.where((lane // IDX_DIM) == (h % 2), qt, jnp.zeros_like(qt))
            s = s + kiw[:, IDX_DIM + h:IDX_DIM + h + 1] * jnp.maximum(_dot_nt(qh, kd), 0.0)
        valid = (t * DSA_TK + col0) <= qpos
        keys_ref[:, pl.ds(t * DSA_TK, DSA_TK)] = jnp.where(valid, _float_key(s), INT_MIN)
        return carry

    lax.fori_loop(0, nt, score_tile, 0)

    def count(pred):
        def body(t, acc):
            kt = keys_ref[:, pl.ds(t * DSA_TK, DSA_TK)]
            hit = pred(kt, t * DSA_TK + col0).astype(I32)
            for c in range(DSA_TK // LANES):
                acc = acc + hit[:, c * LANES:(c + 1) * LANES]
            return acc
        acc = lax.fori_loop(0, nt, body, jnp.zeros((DSA_QB, LANES), I32))
        return jnp.sum(acc, axis=-1, keepdims=True)

    thr = _kth_threshold(lambda c: count(lambda kt, col: kt >= c), topk)
    n_gt = count(lambda kt, col: kt > thr)
    n_eq = count(lambda kt, col: (kt == thr) & (col <= qpos))
    need = topk - n_gt
    jcut_ref[...] = jnp.full(jcut_ref.shape, seq, I32)
    excess = (n_eq > need) & (thr > INT_MIN)

    @pl.when(jnp.max(excess.astype(I32)) > 0)
    def _():
        def step(b, jm):
            cand = jm + jnp.left_shift(jnp.int32(1), 30 - b)
            c = count(lambda kt, col: (kt == thr) & (col <= qpos) & (col < cand))
            return jnp.where(c < need, cand, jm)
        jm = lax.fori_loop(0, 31, step, jnp.zeros((DSA_QB, 1), I32))
        jcut_ref[...] = jnp.broadcast_to(jnp.where(excess, jm, seq), jcut_ref.shape)

    jcut = jcut_ref[:, 0:1]

    m_ref[...] = jnp.full(m_ref.shape, -1e29, F32)
    l_ref[...] = jnp.zeros_like(l_ref)
    acc_ref[...] = jnp.zeros_like(acc_ref)
    for h in range(N_HEADS):
        qs_ref[h // GROUP, (h % GROUP) * DSA_QB:(h % GROUP + 1) * DSA_QB, :] = q_ref[:, h * HEAD_DIM:(h + 1) * HEAD_DIM]
    rep = DSA_ATK // LANES
    nta = (i * DSA_QB + DSA_QB + DSA_ATK - 1) // DSA_ATK
    cola0 = lax.broadcasted_iota(I32, (1, DSA_ATK), 1)

    def att_tile(t, carry):
        kt = keys_ref[:, pl.ds(t * DSA_ATK, DSA_ATK)]
        col = t * DSA_ATK + cola0
        sel = ((kt > thr) | ((kt == thr) & (col <= jcut))) & (col <= qpos)
        sel4 = jnp.concatenate([sel] * GROUP, axis=0)
        for g in range(N_KV_HEADS):
            k_t = kb_ref[pl.ds(t * DSA_ATK, DSA_ATK), g * HEAD_DIM:(g + 1) * HEAD_DIM]
            v_t = vb_ref[pl.ds(t * DSA_ATK, DSA_ATK), g * HEAD_DIM:(g + 1) * HEAD_DIM]
            s = jnp.where(sel4, _dot_nt(qs_ref[g], k_t), -1e30)
            m_old = m_ref[g]
            m_new = jnp.maximum(m_old, jnp.max(s, axis=-1, keepdims=True))
            p = jnp.exp(s - jnp.tile(m_new, (1, rep)))
            corr = jnp.exp(m_old - m_new)
            m_ref[g] = m_new
            l_ref[g] = l_ref[g] * corr + jnp.sum(p, axis=-1, keepdims=True)
            acc_ref[g] = acc_ref[g] * corr + _dot(p.astype(BF16), v_t)
        return carry

    lax.fori_loop(0, nta, att_tile, 0)
    for h in range(N_HEADS):
        rows = slice((h % GROUP) * DSA_QB, (h % GROUP + 1) * DSA_QB)
        o_ref[:, h * HEAD_DIM:(h + 1) * HEAD_DIM] = (acc_ref[h // GROUP, rows, :] / l_ref[h // GROUP, rows, :]).astype(o_ref.dtype)


def _dsa_prompt(qb, kb, vb, qib, kid, kiw, batch, seq, topk):
    nb = seq // DSA_QB
    body = functools.partial(_dsa_prompt_body, topk=topk, seq=seq)
    return pl.pallas_call(
        body,
        grid=(batch, nb),
        in_specs=[pl.BlockSpec((DSA_QB, ATT_WIDTH), lambda b, i: (b * nb + i, 0)),
                  pl.BlockSpec((seq, KV_WIDTH), lambda b, i: (b, 0)),
                  pl.BlockSpec((seq, KV_WIDTH), lambda b, i: (b, 0)),
                  pl.BlockSpec((DSA_QB, IDX_HEADS * IDX_DIM), lambda b, i: (b * nb + i, 0)),
                  pl.BlockSpec((seq, LANES), lambda b, i: (b, 0)),
                  pl.BlockSpec((DSA_QB, LANES), lambda b, i: (b * nb + i, 0))],
        out_specs=pl.BlockSpec((DSA_QB, ATT_WIDTH), lambda b, i: (b * nb + i, 0)),
        out_shape=jax.ShapeDtypeStruct((batch * seq, ATT_WIDTH), BF16),
        scratch_shapes=[pltpu.VMEM((DSA_QB, seq), I32), pltpu.VMEM((DSA_QB, LANES), I32),
                        pltpu.VMEM((N_KV_HEADS, GROUP * DSA_QB, LANES), F32),
                        pltpu.VMEM((N_KV_HEADS, GROUP * DSA_QB, LANES), F32),
                        pltpu.VMEM((N_KV_HEADS, GROUP * DSA_QB, HEAD_DIM), F32),
                        pltpu.VMEM((N_KV_HEADS, GROUP * DSA_QB, HEAD_DIM), BF16)],
        compiler_params=_cparams(("parallel", "arbitrary")),
        name="dsa_prompt",
    )(qb, kb, vb, qib, kid, kiw)


def _sel_sample_body(pt_ref, qi_ref, wi_ref, ks_ref, ck_hbm, pos_ref, ms_ref, kbuf, sc_ref, jm_ref, rk_ref, sem, *,
                     topk, npg):
    s = pl.program_id(0)
    slot = s % 2
    U = SEL_SPS

    def page_copy(step, u, p, sl):
        return pltpu.make_async_copy(ck_hbm.at[pt_ref[step * U + u, p]], kbuf.at[sl, u, p], sem.at[sl])

    def request(step, sl):
        for u in range(U):
            for p in range(npg):
                page_copy(step, u, p, sl).start()

    @pl.when(s == 0)
    def _():
        request(0, 0)

    @pl.when(s + 1 < pl.num_programs(0))
    def _():
        request(s + 1, 1 - slot)

    for u in range(U):
        for p in range(npg):
            page_copy(s, u, p, slot).wait()

    lane = lax.broadcasted_iota(I32, (1, PAGE_SIZE), 1)
    pos = lax.broadcasted_iota(I32, (npg, PAGE_SIZE), 0) * PAGE_SIZE + lane

    def total(x):
        return jnp.sum(jnp.sum(x.astype(I32), axis=1, keepdims=True), axis=0, keepdims=True)

    keys, k_self = [], []
    for u in range(U):
        qi, wi = qi_ref[u], wi_ref[u] * (IDX_DIM ** -0.5)
        for c in range(npg // SEL_CP):
            kt = jnp.concatenate([kbuf[slot, u, c * SEL_CP + r] for r in range(SEL_CP)], axis=1).astype(BF16)
            sc = jnp.sum(wi * jnp.maximum(_dot(qi, kt), 0.0), axis=0, keepdims=True)
            for r in range(SEL_CP):
                sc_ref[u, c * SEL_CP + r:c * SEL_CP + r + 1, :] = sc[:, r * PAGE_SIZE:(r + 1) * PAGE_SIZE]
        keys.append(_float_key(sc_ref[u]))
        d = jnp.sum(qi.astype(F32) * ks_ref[u].astype(F32), axis=-1, keepdims=True)
        k_self.append(_float_key(jnp.sum(wi * jnp.maximum(d, 0.0), axis=0, keepdims=True)))

    def step(b, thrs):
        out = []
        for u in range(U):
            cand = thrs[u] + jnp.left_shift(jnp.int32(1), 31 - b)
            c = total(keys[u] >= cand) + (k_self[u] >= cand).astype(I32)
            out.append(jnp.where(c >= topk, cand, thrs[u]))
        return tuple(out)

    thrs = lax.fori_loop(0, 32, step, tuple(jnp.full((1, 1), INT_MIN, I32) for _ in range(U)))

    ri = lax.broadcasted_iota(I32, (PAGE_SIZE, PAGE_SIZE), 0)
    ci = lax.broadcasted_iota(I32, (PAGE_SIZE, PAGE_SIZE), 1)
    pr_ = lax.broadcasted_iota(I32, (npg, npg), 0)
    pc_ = lax.broadcasted_iota(I32, (npg, npg), 1)
    jcol = lax.broadcasted_iota(I32, (topk, PAGE_SIZE), 0)
    lane_f = lax.broadcasted_iota(I32, (topk, PAGE_SIZE), 1).astype(F32)
    ones8 = jnp.ones((SUBLANES, PAGE_SIZE), BF16)
    for u in range(U):
        thr = thrs[u]
        need = topk - total(keys[u] > thr) - (k_self[u] > thr).astype(I32)
        eq = keys[u] == thr
        jm_ref[u] = jnp.full((SUBLANES, LANES), npg * PAGE_SIZE, I32)

        @pl.when(jnp.max((total(eq) > need).astype(I32)) > 0)
        def _():
            def jstep(b, jm):
                cand = jm + jnp.left_shift(jnp.int32(1), 30 - b)
                return jnp.where(total(eq & (pos < cand)) < need, cand, jm)
            jm_ref[u] = jnp.broadcast_to(lax.fori_loop(0, 31, jstep, jnp.zeros((1, 1), I32)), (SUBLANES, LANES))

        jm = jm_ref[u, 0:1, 0:1]
        sel = (keys[u] > thr) | (eq & (pos <= jm))
        self_sel = (k_self[u] > thr) | ((k_self[u] == thr) & (total(eq & (pos <= jm)) < need))
        ms_ref[u] = jnp.broadcast_to(self_sel.astype(F32), (1, LANES))

        sel_b = sel.astype(BF16)
        within = _dot(sel_b, (ri <= ci).astype(BF16))
        tot = _dot(sel_b, jnp.ones((PAGE_SIZE, PAGE_SIZE), BF16))
        before = _dot((pc_ < pr_).astype(BF16), tot.astype(BF16))
        rk_ref[u] = jnp.where(sel, (before + within).astype(I32) - 1, -1)

        def gather_pos(p, carry):
            hi, lo = carry
            hit = jnp.broadcast_to(rk_ref[u, pl.ds(p, 1), :], (topk, PAGE_SIZE)) == jcol
            return hi + jnp.where(hit, jnp.asarray(p, F32), 0.0), lo + jnp.where(hit, lane_f, 0.0)

        zero = jnp.zeros((topk, PAGE_SIZE), F32)
        hi, lo = lax.fori_loop(0, npg, gather_pos, (zero, zero))
        pos_row = _dot_nt(ones8, hi.astype(BF16)) * PAGE_SIZE + _dot_nt(ones8, lo.astype(BF16))
        pos_ref[u] = pos_row[0:1].astype(I32)


SEL_CP = 8
SEL_SPS = 4


def _sel_sample(page_table, qi, wi, kself, cache_kt, topk):
    n, npg = page_table.shape
    U = SEL_SPS
    assert npg % SEL_CP == 0 and npg <= PAGE_SIZE and n % U == 0
    grid_spec = pltpu.PrefetchScalarGridSpec(
        num_scalar_prefetch=1,
        grid=(n // U,),
        in_specs=[pl.BlockSpec((U, IDX_HEADS, IDX_DIM), lambda s, pt: (s, 0, 0)),
                  pl.BlockSpec((U, IDX_HEADS, 1), lambda s, pt: (s, 0, 0)),
                  pl.BlockSpec((U, 1, IDX_DIM), lambda s, pt: (s, 0, 0)),
                  pl.BlockSpec(memory_space=pl.ANY)],
        out_specs=[pl.BlockSpec((U, 1, topk), lambda s, pt: (s, 0, 0)),
                   pl.BlockSpec((U, 1, LANES), lambda s, pt: (s, 0, 0))],
        scratch_shapes=[pltpu.VMEM((2, U, npg, IDX_DIM, PAGE_SIZE), F32), pltpu.VMEM((U, npg, PAGE_SIZE), F32),
                        pltpu.VMEM((U, SUBLANES, LANES), I32), pltpu.VMEM((U, npg, PAGE_SIZE), I32),
                        pltpu.SemaphoreType.DMA((2,))],
    )
    return pl.pallas_call(
        functools.partial(_sel_sample_body, topk=topk, npg=npg),
        grid_spec=grid_spec,
        out_shape=[jax.ShapeDtypeStruct((n, 1, topk), I32), jax.ShapeDtypeStruct((n, 1, LANES), F32)],
        compiler_params=_cparams(("arbitrary",)),
        name="sel_sample",
    )(page_table, qi, wi, kself, cache_kt)


def _att_sel_body(pt_ref, pos_ref, q_ref, ms_ref, ks_ref, vs_ref, ck_hbm, cv_hbm, o_ref, kbuf, vbuf, sem, *, topk):
    s = pl.program_id(0)
    slot = s % 2

    def request(seq, sl):
        def body(j, c):
            pos = pos_ref[seq, j]
            pg = pt_ref[seq, pos // PAGE_SIZE]
            r = pos % PAGE_SIZE
            pltpu.make_async_copy(ck_hbm.at[pg, r], kbuf.at[sl, j], sem.at[0, sl]).start()
            pltpu.make_async_copy(cv_hbm.at[pg, r], vbuf.at[sl, j], sem.at[1, sl]).start()
            return c
        lax.fori_loop(0, topk, body, 0, unroll=8)

    @pl.when(s == 0)
    def _():
        request(0, 0)

    @pl.when(s + 1 < pl.num_programs(0))
    def _():
        request(s + 1, 1 - slot)

    for h in range(topk // PAGE_SIZE):
        rows = pl.ds(h * PAGE_SIZE, PAGE_SIZE)
        pltpu.make_async_copy(ck_hbm.at[0], kbuf.at[slot, rows], sem.at[0, slot]).wait()
        pltpu.make_async_copy(cv_hbm.at[0], vbuf.at[slot, rows], sem.at[1, slot]).wait()

    q = q_ref[0]
    row_g = lax.broadcasted_iota(I32, (N_HEADS, 1), 0) // GROUP
    lane_g = lax.broadcasted_iota(I32, (1, KV_WIDTH), 1) // HEAD_DIM
    q_bd = jnp.where(row_g == lane_g, jnp.tile(q, (1, N_KV_HEADS)), jnp.zeros((N_HEADS, KV_WIDTH), BF16))
    k2 = jnp.concatenate([kbuf[slot, :, g, :] for g in range(N_KV_HEADS)], axis=1).astype(BF16)
    v2 = jnp.concatenate([vbuf[slot, :, g, :] for g in range(N_KV_HEADS)], axis=1).astype(BF16)
    self_row = ms_ref[0]
    self_f = self_row[:, 0:1]
    n_past = topk - jnp.tile(self_row, (1, topk // LANES))
    valid = lax.broadcasted_iota(I32, (1, topk), 1).astype(F32) < n_past
    sc = jnp.where(valid, _dot_nt(q_bd, k2), -1e30)
    s1 = jnp.sum(q.astype(F32) * ks_ref[0].astype(F32), axis=-1, keepdims=True)
    s1 = jnp.where(self_f > 0.5, s1, -1e30)
    m = jnp.maximum(jnp.max(sc, axis=-1, keepdims=True), jnp.maximum(s1, -1e29))
    pr = jnp.exp(sc - m)
    p1 = jnp.exp(s1 - m)
    l = jnp.sum(pr, axis=-1, keepdims=True) + p1
    pv = _dot(pr.astype(BF16), v2)
    own = jnp.zeros((N_HEADS, HEAD_DIM), F32)
    for g in range(N_KV_HEADS):
        own = jnp.where(row_g == g, pv[:, g * HEAD_DIM:(g + 1) * HEAD_DIM], own)
    o_ref[0] = ((own + p1.astype(BF16).astype(F32) * vs_ref[0].astype(F32)) / l).astype(o_ref.dtype)


def _att_sel(page_table, pos_list, q, mself, kself, vself, cache_k, cache_v):
    n, topk = pos_list.shape
    assert topk % PAGE_SIZE == 0
    seqspec = lambda r, c: pl.BlockSpec((1, r, c), lambda s, pt, pos: (s, 0, 0))
    anyspec = pl.BlockSpec(memory_space=pl.ANY)
    rows = (2, topk, N_KV_HEADS, HEAD_DIM)
    grid_spec = pltpu.PrefetchScalarGridSpec(
        num_scalar_prefetch=2,
        grid=(n,),
        in_specs=[seqspec(N_HEADS, HEAD_DIM), seqspec(1, LANES), seqspec(N_HEADS, HEAD_DIM),
                  seqspec(N_HEADS, HEAD_DIM), anyspec, anyspec],
        out_specs=seqspec(N_HEADS, HEAD_DIM),
        scratch_shapes=[pltpu.VMEM(rows, F32), pltpu.VMEM(rows, F32), pltpu.SemaphoreType.DMA((2, 2))],
    )
    return pl.pallas_call(
        functools.partial(_att_sel_body, topk=topk),
        grid_spec=grid_spec,
        out_shape=jax.ShapeDtypeStruct((n, N_HEADS, HEAD_DIM), BF16),
        compiler_params=_cparams(("arbitrary",)),
        name="att_sel",
    )(page_table, pos_list, q, mself, kself, vself, cache_k, cache_v)


def _mem_att_prompt_body(q_ref, k_ref, v_ref, o_ref):
    scale = MEM_HEAD_DIM ** -0.5
    for h in range(MEM_HEADS):
        sl = slice(h * MEM_HEAD_DIM, (h + 1) * MEM_HEAD_DIM)
        s = _dot_nt(q_ref[:, sl], k_ref[:, sl]) * scale
        m = jnp.max(s, axis=-1, keepdims=True)
        e = jnp.exp(s - m)
        pr = e / jnp.sum(e, axis=-1, keepdims=True)
        o_ref[:, sl] = _dot(pr.astype(BF16), v_ref[:, sl]).astype(o_ref.dtype)


def _mem_att_prompt(mq, mk, mv, batch, seq, tq):
    m = mk.shape[0] // batch
    nb = seq // tq
    return pl.pallas_call(
        _mem_att_prompt_body,
        grid=(batch * nb,),
        in_specs=[pl.BlockSpec((tq, MEM_WIDTH), lambda i: (i, 0)),
                  pl.BlockSpec((m, MEM_WIDTH), lambda i: (i // nb, 0)),
                  pl.BlockSpec((m, MEM_WIDTH), lambda i: (i // nb, 0))],
        out_specs=pl.BlockSpec((tq, MEM_WIDTH), lambda i: (i, 0)),
        out_shape=jax.ShapeDtypeStruct((batch * seq, MEM_WIDTH), BF16),
        compiler_params=_cparams(("parallel",)),
        name="mem_att_prompt",
    )(mq, mk, mv)


def _mem_att_sample_body(q_ref, k_ref, v_ref, o_ref):
    scale = MEM_HEAD_DIM ** -0.5
    q = q_ref[0].astype(F32)
    for h in range(MEM_HEADS):
        sl = slice(h * MEM_HEAD_DIM, (h + 1) * MEM_HEAD_DIM)
        s = jnp.sum(k_ref[0, :, h, :] * q[:, sl], axis=-1, keepdims=True) * scale
        m = jnp.max(s, axis=0, keepdims=True)
        e = jnp.exp(s - m)
        pr = e / jnp.sum(e, axis=0, keepdims=True)
        o_ref[0, :, sl] = jnp.sum(pr * v_ref[0, :, h, :], axis=0, keepdims=True).astype(o_ref.dtype)


def _mem_att_sample(mq, mk, mv):
    n, m, nh, hd = mk.shape
    w = nh * hd
    return pl.pallas_call(
        _mem_att_sample_body,
        grid=(n,),
        in_specs=[pl.BlockSpec((1, 1, w), lambda s: (s, 0, 0)),
                  pl.BlockSpec((1, m, nh, hd), lambda s: (s, 0, 0, 0)),
                  pl.BlockSpec((1, m, nh, hd), lambda s: (s, 0, 0, 0))],
        out_specs=pl.BlockSpec((1, 1, w), lambda s: (s, 0, 0)),
        out_shape=jax.ShapeDtypeStruct((n, 1, w), BF16),
        compiler_params=_cparams(("parallel",)),
        name="mem_att_sample",
    )(mq, mk, mv)


def _router_body(x_ref, w_ref, b_ref, ei_ref, ew_ref, acc_ref):
    k = pl.program_id(1)

    @pl.when(k == 0)
    def _():
        acc_ref[...] = jnp.zeros_like(acc_ref)

    acc_ref[...] += _dot(x_ref[...], w_ref[...], HIGHEST)

    @pl.when(k == pl.num_programs(1) - 1)
    def _():
        lg = acc_ref[...] + b_ref[...]
        lane = lax.broadcasted_iota(I32, lg.shape, 1)
        neg = jnp.float32(-jnp.inf)
        is_g = lane < N_GROUPS
        glm = jnp.where(is_g, lg, neg)
        gmax = jnp.max(glm, axis=-1, keepdims=True)
        g_sel = jnp.min(jnp.where(glm == gmax, lane, LANES), axis=-1, keepdims=True)
        g_prob = 1.0 / jnp.sum(jnp.where(is_g, jnp.exp(lg - gmax), 0.0), axis=-1, keepdims=True)
        e_id = lane - N_GROUPS
        in_grp = (e_id >= 0) & (e_id < N_EXPERTS) & ((e_id // EXPERTS_PER_GROUP) == g_sel)
        el = jnp.where(in_grp, lg, neg)
        m1 = jnp.max(el, axis=-1, keepdims=True)
        i1 = jnp.min(jnp.where(in_grp & (el == m1), lane, LANES), axis=-1, keepdims=True)
        rest = in_grp & (lane != i1)
        el2 = jnp.where(rest, lg, neg)
        m2 = jnp.max(el2, axis=-1, keepdims=True)
        i2 = jnp.min(jnp.where(rest & (el2 == m2), lane, LANES), axis=-1, keepdims=True)
        t = jnp.exp(m2 - m1)
        w1 = g_prob / (1.0 + t)
        w2 = g_prob * t / (1.0 + t)
        ei_ref[...] = jnp.where(lane == 0, i1 - N_GROUPS, jnp.where(lane == 1, i2 - N_GROUPS, 0))
        ew_ref[...] = jnp.where(lane == 0, w1, jnp.where(lane == 1, w2, 0.0))


def _router(x, w, b, tm, tk):
    m, kd = x.shape
    return pl.pallas_call(
        _router_body,
        grid=(m // tm, kd // tk),
        in_specs=[pl.BlockSpec((tm, tk), lambda i, k: (i, k)),
                  pl.BlockSpec((tk, LANES), lambda i, k: (k, 0)),
                  pl.BlockSpec((1, LANES), lambda i, k: (0, 0))],
        out_specs=[pl.BlockSpec((tm, LANES), lambda i, k: (i, 0)), pl.BlockSpec((tm, LANES), lambda i, k: (i, 0))],
        out_shape=[jax.ShapeDtypeStruct((m, LANES), I32), jax.ShapeDtypeStruct((m, LANES), F32)],
        scratch_shapes=[pltpu.VMEM((tm, LANES), F32)],
        compiler_params=_cparams(("parallel", "arbitrary")),
        name="router",
    )(x, w, b)


MOE_BR = 128


def _gather_body(idx_ref, nblk_ref, src_ref, o_ref, sem):
    i = pl.program_id(0)
    g = o_ref.shape[0]

    def row_copy(src_row, dst_row):
        return pltpu.make_async_copy(src_ref.at[pl.ds(src_row, 1)], o_ref.at[pl.ds(dst_row, 1)], sem)

    @pl.when(i < nblk_ref[0])
    def _():
        def issue(r, c):
            row_copy(idx_ref[i * g + r], r).start()
            return c
        lax.fori_loop(0, g, issue, 0, unroll=8)
        pltpu.make_async_copy(src_ref.at[pl.ds(0, g)], o_ref, sem).wait()

    @pl.when(i >= nblk_ref[0])
    def _():
        o_ref[...] = jnp.zeros_like(o_ref)


def _gather_rows(src, idx, nblk, g):
    m = idx.shape[0]
    tail = src.shape[1:]
    zeros = (0,) * len(tail)
    grid_spec = pltpu.PrefetchScalarGridSpec(
        num_scalar_prefetch=2,
        grid=(m // g,),
        in_specs=[pl.BlockSpec(memory_space=pl.ANY)],
        out_specs=pl.BlockSpec((g,) + tail, lambda i, idx, nb: (i,) + zeros),
        scratch_shapes=[pltpu.SemaphoreType.DMA(())],
    )
    return pl.pallas_call(
        _gather_body,
        grid_spec=grid_spec,
        out_shape=jax.ShapeDtypeStruct((m,) + tail, src.dtype),
        compiler_params=_cparams(("arbitrary",)),
        name="gather_rows",
    )(idx, nblk, src)


def _expert_up_body(be_ref, nblk_ref, x_ref, wg_ref, wu_ref, h_ref, wgb_ref, wub_ref):
    i = pl.program_id(1)
    changed = jnp.logical_or(i == 0, be_ref[i] != be_ref[jnp.maximum(i - 1, 0)])

    @pl.when(jnp.logical_and(i < nblk_ref[0], changed))
    def _():
        wgb_ref[...] = wg_ref[0].astype(BF16)
        wub_ref[...] = wu_ref[0].astype(BF16)

    @pl.when(i < nblk_ref[0])
    def _():
        x = x_ref[...].astype(BF16)
        a = _dot(x, wgb_ref[...])
        u = _dot(x, wub_ref[...])
        h_ref[...] = (a * _sigmoid(a) * u).astype(h_ref.dtype)

    @pl.when(i >= nblk_ref[0])
    def _():
        h_ref[...] = jnp.zeros_like(h_ref)


def _expert_up(block_e, nblk, xs, w_gate, w_up, th):
    nr, d = xs.shape
    nb = nr // MOE_BR
    nh = D_EXPERT // th
    blk = lambda i, nbk: jnp.minimum(i, nbk[0] - 1)
    grid_spec = pltpu.PrefetchScalarGridSpec(
        num_scalar_prefetch=2,
        grid=(nh, nb),
        in_specs=[pl.BlockSpec((MOE_BR, d), lambda j, i, be, nbk: (blk(i, nbk), 0)),
                  pl.BlockSpec((1, d, th), lambda j, i, be, nbk: (be[blk(i, nbk)], 0, j)),
                  pl.BlockSpec((1, d, th), lambda j, i, be, nbk: (be[blk(i, nbk)], 0, j))],
        out_specs=pl.BlockSpec((MOE_BR, th), lambda j, i, be, nbk: (i, j)),
        scratch_shapes=[pltpu.VMEM((d, th), BF16), pltpu.VMEM((d, th), BF16)],
    )
    return pl.pallas_call(
        _expert_up_body,
        grid_spec=grid_spec,
        out_shape=jax.ShapeDtypeStruct((nr, D_EXPERT), BF16),
        compiler_params=_cparams(("arbitrary", "arbitrary")),
        name="expert_up",
    )(block_e, nblk, xs, w_gate, w_up)


def _expert_down_body(be_ref, nblk_ref, h_ref, wd_ref, y_ref, wdb_ref):
    i = pl.program_id(0)
    changed = jnp.logical_or(i == 0, be_ref[i] != be_ref[jnp.maximum(i - 1, 0)])

    @pl.when(jnp.logical_and(i < nblk_ref[0], changed))
    def _():
        wdb_ref[...] = wd_ref[0].astype(BF16)

    @pl.when(i < nblk_ref[0])
    def _():
        y_ref[...] = _dot(h_ref[...], wdb_ref[...])

    @pl.when(i >= nblk_ref[0])
    def _():
        y_ref[...] = jnp.zeros_like(y_ref)


def _expert_down(block_e, nblk, h, w_down):
    nr = h.shape[0]
    d = w_down.shape[2]
    blk = lambda i, nbk: jnp.minimum(i, nbk[0] - 1)
    grid_spec = pltpu.PrefetchScalarGridSpec(
        num_scalar_prefetch=2,
        grid=(nr // MOE_BR,),
        in_specs=[pl.BlockSpec((MOE_BR, D_EXPERT), lambda i, be, nbk: (blk(i, nbk), 0)),
                  pl.BlockSpec((1, D_EXPERT, d), lambda i, be, nbk: (be[blk(i, nbk)], 0, 0))],
        out_specs=pl.BlockSpec((MOE_BR, d), lambda i, be, nbk: (i, 0)),
        scratch_shapes=[pltpu.VMEM((D_EXPERT, d), BF16)],
    )
    return pl.pallas_call(
        _expert_down_body,
        grid_spec=grid_spec,
        out_shape=jax.ShapeDtypeStruct((nr, d), F32),
        compiler_params=_cparams(("arbitrary",)),
        name="expert_down",
    )(block_e, nblk, h, w_down)


def _combine_ln_body(x_ref, y0_ref, y1_ref, ew_ref, g_ref, b_ref, op_ref, os_ref, *, npb):
    i = pl.program_id(0)
    ew = ew_ref[...]
    ff = y0_ref[...] * ew[:, 0:1] + y1_ref[...] * ew[:, 1:2]
    y = _layer_norm_rows(DEEPNORM_ALPHA * x_ref[...] + ff, g_ref[...], b_ref[...])

    @pl.when(i < npb)
    def _():
        op_ref[...] = y

    @pl.when(i >= npb)
    def _():
        os_ref[...] = y


def _combine_ln(x, yg, ew, g, b, tm, n_prompt):
    m, d = x.shape
    nb = m // tm
    npb = n_prompt // tm
    return pl.pallas_call(
        functools.partial(_combine_ln_body, npb=npb),
        grid=(nb,),
        in_specs=[pl.BlockSpec((tm, d), lambda i: (i, 0)),
                  pl.BlockSpec((tm, d), lambda i: (i, 0)),
                  pl.BlockSpec((tm, d), lambda i: (nb + i, 0)),
                  pl.BlockSpec((tm, LANES), lambda i: (i, 0)),
                  pl.BlockSpec((1, d), lambda i: (0, 0)),
                  pl.BlockSpec((1, d), lambda i: (0, 0))],
        out_specs=[pl.BlockSpec((tm, d), lambda i: (jnp.minimum(i, npb - 1), 0)),
                   pl.BlockSpec((tm, d), lambda i: (jnp.maximum(i - npb, 0), 0))],
        out_shape=[jax.ShapeDtypeStruct((n_prompt, d), F32), jax.ShapeDtypeStruct((m - n_prompt, d), F32)],
        compiler_params=_cparams(("arbitrary",)),
        name="combine_ln",
    )(x, yg, yg, ew, g, b)


def _pad_cols(x, n):
    return jnp.pad(x, ((0, 0), (0, n - x.shape[1])))


def _split_w_in(w):
    o = [int(v) for v in np.cumsum([0, RW_PROJ, ATT_WIDTH + 2 * KV_WIDTH + IDX_HEADS * IDX_DIM, IDX_DIM + IDX_HEADS,
                                    2 * D_MODEL])]
    w_rkv = w[:, 0:3 * RW_WIDTH].astype(BF16)
    w_att = w[:, o[1]:o[2]].astype(BF16)
    w_gate = w[:, o[3]:o[4]].astype(BF16)
    w_small = jnp.concatenate([_lora_cols(w[:, 0:RW_PROJ]), _pad_cols(w[:, o[2]:o[3]], LANES)], axis=1).astype(BF16)
    return w_rkv, w_att, w_gate, w_small


def _lora_cols(x):
    return jnp.concatenate([_pad_cols(x[:, 6144:6240], 128), _pad_cols(x[:, 6240:6336], 128), x[:, 6336:6592]], axis=1)


def _pack_rwkv(rw_mu, rw_w0, rw_w2, rw_a0, rw_a2, rw_g2, rw_k_k, rw_k_a, rw_r_k, rw_ln_w, rw_ln_b):
    flat = lambda t: t.reshape(1, RW_WIDTH)
    mu = rw_mu.reshape(1, RW_PROJ)
    rows = [mu[:, 0:2048], mu[:, 2048:4096], mu[:, 4096:6144], flat(rw_w0), flat(rw_a0), flat(rw_k_k),
            flat(rw_k_a), flat(rw_r_k), flat(rw_ln_w), flat(rw_ln_b)]
    prm = jnp.pad(jnp.concatenate(rows, axis=0), ((0, 6), (0, 0)))
    mu_l = jnp.pad(_lora_cols(mu), ((0, 7), (0, 0)))
    w2 = jnp.pad(rw_w2, ((0, 128 - W_LORA), (0, 0))).astype(BF16)
    a2 = jnp.pad(rw_a2, ((0, 128 - A_LORA), (0, 0))).astype(BF16)
    g2 = rw_g2.astype(BF16)
    return prm, mu_l, w2, a2, g2


def _head_indicators(width):
    lane = np.arange(width)[:, None] // RW_HEAD_DIM
    ind = (lane == np.arange(128)[None, :]).astype(np.float32)
    return jnp.asarray(ind), jnp.asarray(ind.T)


def _head_selectors():
    sel = np.zeros((WKV_HQ, WKV_W, RW_HEAD_DIM), np.float32)
    for j in range(WKV_HQ):
        sel[j, j * RW_HEAD_DIM + np.arange(RW_HEAD_DIM), np.arange(RW_HEAD_DIM)] = 1.0
    return jnp.asarray(sel)


def kernel(x_prompt, x_sample, mem_prompt, cache_k, cache_v, cache_idx_k, page_table, state_wkv, state_shift, cache_mem_k, cache_mem_v, w_in, rw_mu, rw_w0, rw_w2, rw_a0, rw_a2, rw_g2, rw_k_k, rw_k_a, rw_r_k, rw_ln_w, rw_ln_b, idx_ln_w, idx_ln_b, w_branch_a, w_branch_b, w_out, ln1_w, ln1_b, w_mem_q, w_mem_k, w_mem_v, w_mem_o, ln2_w, ln2_b, w_router_grp, b_router_grp, w_router_exp, b_router_exp, w_exp_gate, w_exp_up, w_exp_down, ln3_w, ln3_b):
    B, S, D = x_prompt.shape
    DB, DS, _ = x_sample.shape
    assert DS == 1 and cache_k.shape[0] == 1
    TP = B * S
    T = TP + DB
    MP = _round_up(T, 640)
    past = page_table.shape[1] * PAGE_SIZE
    n_mem = mem_prompt.shape[1]
    row1 = lambda a: a.reshape(1, -1)

    def pad_rows(a):
        return jnp.concatenate([a, jnp.zeros((MP - a.shape[0],) + a.shape[1:], a.dtype)], axis=0)

    x_all = pad_rows(jnp.concatenate([x_prompt.reshape(TP, D), x_sample.reshape(DB, D)], axis=0))
    xb = x_all.astype(BF16)
    w_rkv, w_att, w_gate, w_small = _split_w_in(w_in[0])
    z_rkv = _mm(xb, w_rkv, 640, 1024, D, name="in_proj_rkv")
    z_att = _mm(xb, w_att, 640, 1024, D, name="in_proj_att")
    z_gate = _mm(xb, w_gate, 640, 1024, D, name="in_proj_gate")
    z_small = _mm(xb, w_small, 640, S_TOTAL, D, name="in_proj_small")

    prm, mu_l, w2, a2, g2 = _pack_rwkv(rw_mu[0], rw_w0[0], rw_w2[0], rw_a0[0], rw_a2[0], rw_g2[0], rw_k_k[0],
                                       rw_k_a[0], rw_r_k[0], rw_ln_w[0], rw_ln_b[0])
    rw_p, wkv_p = _wkv_prompt(z_rkv, z_small, prm, mu_l, w2, a2, g2, _head_selectors(), B, S)
    ss = state_shift[0]
    ind_f, indt_f = _head_indicators(RW_WIDTH)
    tok = _wkv_tokens_sample(z_rkv, z_small, ss[:, 0:2048], ss[:, 2048:4096], ss[:, 4096:6144], _lora_cols(ss), prm,
                             mu_l, w2, a2, g2, ind_f, indt_f, TP, DB)
    t_r, t_w, t_al, t_be, t_km, t_v, t_g, t_bo = tok
    rowv = lambda a: a.reshape(DB, RW_HEADS, 1, RW_HEAD_DIM)
    colv = lambda a: a.reshape(DB, RW_HEADS, RW_HEAD_DIM, 1)
    y_col, wkv_s = _wkv_step(state_wkv[0], rowv(t_w), rowv(t_al), rowv(t_be), rowv(t_km), rowv(t_r), colv(t_v),
                             colv(t_g), colv(t_bo), rw_ln_w[0].reshape(1, RW_HEADS, RW_HEAD_DIM, 1),
                             rw_ln_b[0].reshape(1, RW_HEADS, RW_HEAD_DIM, 1))
    rw_all = pad_rows(jnp.concatenate([rw_p, y_col.reshape(DB, RW_WIDTH).astype(BF16)], axis=0))

    pos = jnp.concatenate([jnp.tile(jnp.arange(S, dtype=I32), B), jnp.full((MP - TP,), past, I32)])
    tab_a = _rope_tables(pos, ROT_DIM, HEAD_DIM)
    tab_i = _rope_tables(pos, IDX_ROT_DIM, IDX_DIM)
    qb, k_rot, kb, vb, qib, kiw, kid = _prep(z_att, z_small, tab_a, tab_i, _pad_cols(row1(idx_ln_w[0]), LANES),
                                             _pad_cols(row1(idx_ln_b[0]), LANES), 128)
    att_p = _dsa_prompt(qb, kb, vb, qib, kid, kiw, B, S, min(TOPK_MAX, S // 4))
    qi_s = qib[TP:T].reshape(DB, IDX_HEADS, IDX_DIM)
    wi_s = kiw[TP:T, IDX_DIM:IDX_DIM + IDX_HEADS].reshape(DB, IDX_HEADS, 1)
    topk_s = min(TOPK_MAX, (past + DS) // 4)
    pos_sel, mself = _sel_sample(page_table, qi_s, wi_s, kid[TP:T, 0:IDX_DIM].reshape(DB, 1, IDX_DIM),
                                 jnp.swapaxes(cache_idx_k[0], 1, 2), topk_s)
    expand = lambda a: jnp.repeat(a[TP:T].reshape(DB, N_KV_HEADS, HEAD_DIM), GROUP, axis=1)
    att_s = _att_sel(page_table, pos_sel.reshape(DB, topk_s), qb[TP:T].reshape(DB, N_HEADS, HEAD_DIM), mself,
                     expand(kb), expand(vb), cache_k[0], cache_v[0])
    att_all = pad_rows(jnp.concatenate([att_p, att_s.reshape(DB, ATT_WIDTH)], axis=0))

    merged = _branch_merge(rw_all, att_all, w_branch_a[0].astype(BF16), w_branch_b[0].astype(BF16), z_gate, 640, 1024)
    x1, x1b = _mm_ln(merged, w_out[0].astype(BF16), x_all, row1(ln1_w[0]), row1(ln1_b[0]), 320, 512, name="out_ln1")

    mq = _mm(x1b, w_mem_q[0].astype(BF16), 640, MEM_WIDTH, D, out_dtype=BF16, name="mem_q")
    mem2d = mem_prompt.reshape(B * n_mem, D).astype(BF16)
    mem_k = _mm(mem2d, w_mem_k[0].astype(BF16), B * n_mem, MEM_WIDTH, D, name="mem_k")
    mem_v = _mm(mem2d, w_mem_v[0].astype(BF16), B * n_mem, MEM_WIDTH, D, name="mem_v")
    ma_p = _mem_att_prompt(mq, mem_k.astype(BF16), mem_v.astype(BF16), B, S, 512)
    ma_s = _mem_att_sample(mq[TP:T].reshape(DB, 1, MEM_WIDTH), cache_mem_k[0], cache_mem_v[0])
    ma_all = pad_rows(jnp.concatenate([ma_p, ma_s.reshape(DB, MEM_WIDTH)], axis=0))
    x2, x2b = _mm_ln(ma_all, w_mem_o[0].astype(BF16), x1, row1(ln2_w[0]), row1(ln2_b[0]), 320, 512, name="mem_o_ln2")

    w_r = _pad_cols(jnp.concatenate([w_router_grp[0], w_router_exp[0]], axis=1), LANES)
    b_r = _pad_cols(row1(jnp.concatenate([b_router_grp[0], b_router_exp[0]])), LANES)
    e_idx, e_w = _router(x2, w_r, b_r, 640, 1024)
    n_assign = 2 * T
    flat_e = e_idx[:T, 0:2].reshape(n_assign)
    order = jnp.argsort(flat_e).astype(I32)
    rank = jnp.argsort(order).astype(I32)
    experts = jnp.arange(N_EXPERTS, dtype=I32)
    onehot = flat_e[:, None] == experts[None, :]
    counts = jnp.sum(onehot, axis=0, dtype=I32)
    padded = (counts + MOE_BR - 1) // MOE_BR * MOE_BR
    pad_end = jnp.cumsum(padded)
    pad_start = pad_end - padded
    start = jnp.cumsum(counts) - counts
    slot = (rank + jnp.sum(jnp.where(onehot, (pad_start - start)[None, :], 0), axis=1)).reshape(T, 2)
    n_blocks = -(-n_assign // MOE_BR) + N_EXPERTS
    blk_row0 = jnp.arange(n_blocks, dtype=I32) * MOE_BR
    block_e = jnp.minimum(jnp.sum(pad_end[None, :] <= blk_row0[:, None], axis=1, dtype=I32), N_EXPERTS - 1)
    blk_hot = block_e[:, None] == experts[None, :]
    pick = lambda tab: jnp.sum(jnp.where(blk_hot, tab[None, :], 0), axis=1)
    j_in_e = (blk_row0 - pick(pad_start))[:, None] + jnp.arange(MOE_BR, dtype=I32)[None, :]
    src = jnp.clip(pick(start)[:, None] + j_in_e, 0, n_assign - 1)
    row_token = jnp.where(j_in_e < pick(counts)[:, None], order[src] // 2, 0).reshape(n_blocks * MOE_BR)
    n_used = (pad_end[-1] // MOE_BR).astype(I32).reshape(1)
    xs = _gather_rows(x2b.reshape(MP, D // LANES, LANES), row_token, n_used, MOE_BR).reshape(n_blocks * MOE_BR, D)
    hid = _expert_up(block_e, n_used, xs, w_exp_gate[0], w_exp_up[0], 256)
    y_rows = _expert_down(block_e, n_used, hid, w_exp_down[0])
    slot_pad = jnp.concatenate([jnp.pad(slot[:, 0], (0, MP - T)), jnp.pad(slot[:, 1], (0, MP - T))])
    y_tok = _gather_rows(y_rows, slot_pad, jnp.full((1,), 2 * MP // MOE_BR, I32), MOE_BR)
    y_p, y_s = _combine_ln(x2, y_tok, e_w, row1(ln3_w[0]), row1(ln3_b[0]), 128, TP)

    kv5 = lambda a, n, s: a.reshape(1, n, s, N_KV_HEADS, HEAD_DIM)
    va = z_att[:, A_VA:A_VA + KV_WIDTH]
    ki = kiw[:, 0:IDX_DIM]
    last = lambda a: jnp.concatenate([a[(b + 1) * S - 1:(b + 1) * S] for b in range(B)] + [a[TP:T]], axis=0)
    zl, zsl = last(z_rkv), last(z_small)
    shift_cols = jnp.concatenate([zl, zsl[:, S_LORA:S_LORA + W_LORA], zsl[:, S_LORA + 128:S_LORA + 128 + A_LORA],
                                  zsl[:, S_LORA + 256:S_LORA + 512]], axis=1)
    mem5 = lambda a: a.reshape(1, B, n_mem, MEM_HEADS, MEM_HEAD_DIM)
    return (y_p.reshape(B, S, D), y_s[:DB].reshape(DB, DS, D),
            kv5(k_rot[:TP], B, S), kv5(va[:TP], B, S), ki[:TP].reshape(1, B, S, IDX_DIM),
            wkv_p[None], shift_cols[:B][None], mem5(mem_k), mem5(mem_v),
            kv5(k_rot[TP:T], DB, DS), kv5(va[TP:T], DB, DS), ki[TP:T].reshape(1, DB, DS, IDX_DIM),
            wkv_s[None], shift_cols[B:][None])
```

```python
import functools
import math

import jax
import jax.numpy as jnp
import numpy as np
from jax import lax
from jax.experimental import pallas as pl
from jax.experimental.pallas import tpu as pltpu

F32 = jnp.float32
BF16 = jnp.bfloat16
I32 = jnp.int32
HIGHEST = lax.Precision.HIGHEST

D_MODEL = 4096
RW_HEAD_DIM = 64
RW_HEADS = 32
RW_WIDTH = 2048
W_LORA = 96
A_LORA = 96
G_LORA = 256
RW_PROJ = 3 * RW_WIDTH + W_LORA + A_LORA + G_LORA
RW_GN_EPS = 64e-5
HEAD_DIM = 128
N_HEADS = 16
N_KV_HEADS = 4
GROUP = 4
ATT_WIDTH = 2048
KV_WIDTH = 512
ROT_DIM = 32
ROPE_THETA = 500000.0
IDX_HEADS = 16
IDX_DIM = 64
IDX_ROT_DIM = 16
TOPK_MAX = 256
PAGE_SIZE = 128
MEM_HEADS = 4
MEM_HEAD_DIM = 128
MEM_WIDTH = 512
N_GROUPS = 8
EXPERTS_PER_GROUP = 8
N_EXPERTS = 64
D_EXPERT = 512
LN_EPS = 1e-5
DEEPNORM_ALPHA = 2.0 ** 0.25
EXP_M05 = math.exp(-0.5)

LANES = 128
SUBLANES = 8
VMEM_LIMIT = 56 * 1024 * 1024

C_R, C_K, C_V = 0, 2048, 4096
A_Q, A_KA, A_VA, A_IQ = 0, 2048, 2560, 3072
G_A, G_B = 0, 4096
S_LORA, S_IKW, S_TOTAL = 0, 512, 640

INT_MIN = -(2 ** 31)


def _round_up(n, m):
    return -(-n // m) * m


def _cparams(sem):
    return pltpu.CompilerParams(dimension_semantics=sem, vmem_limit_bytes=VMEM_LIMIT)


def _dot(a, b, precision=None):
    return jnp.dot(a, b, preferred_element_type=F32, precision=precision)


def _dot_nt(a, b, precision=None):
    return lax.dot_general(a, b, (((1,), (1,)), ((), ())), preferred_element_type=F32, precision=precision)


def _sigmoid(x):
    return 1.0 / (1.0 + jnp.exp(-x))


def _mm_body(x_ref, w_ref, o_ref, acc_ref):
    k = pl.program_id(2)

    @pl.when(k == 0)
    def _():
        acc_ref[...] = jnp.zeros_like(acc_ref)

    acc_ref[...] += _dot(x_ref[...], w_ref[...])

    @pl.when(k == pl.num_programs(2) - 1)
    def _():
        o_ref[...] = acc_ref[...].astype(o_ref.dtype)


def _mm_fullk_body(x_ref, w_ref, o_ref):
    o_ref[...] = _dot(x_ref[...], w_ref[...]).astype(o_ref.dtype)


def _mm(x, w, tm, tn, tk, out_dtype=F32, name="mm"):
    m, kd = x.shape
    n = w.shape[1]
    if tk == kd:
        return pl.pallas_call(
            _mm_fullk_body,
            grid=(m // tm, n // tn),
            in_specs=[pl.BlockSpec((tm, kd), lambda i, j: (i, 0)),
                      pl.BlockSpec((kd, tn), lambda i, j: (0, j))],
            out_specs=pl.BlockSpec((tm, tn), lambda i, j: (i, j)),
            out_shape=jax.ShapeDtypeStruct((m, n), out_dtype),
            compiler_params=_cparams(("parallel", "parallel")),
            name=name,
        )(x, w)
    return pl.pallas_call(
        _mm_body,
        grid=(m // tm, n // tn, kd // tk),
        in_specs=[pl.BlockSpec((tm, tk), lambda i, j, k: (i, k)),
                  pl.BlockSpec((tk, tn), lambda i, j, k: (k, j))],
        out_specs=pl.BlockSpec((tm, tn), lambda i, j, k: (i, j)),
        out_shape=jax.ShapeDtypeStruct((m, n), out_dtype),
        scratch_shapes=[pltpu.VMEM((tm, tn), F32)],
        compiler_params=_cparams(("parallel", "parallel", "arbitrary")),
        name=name,
    )(x, w)


def _layer_norm_rows(x, g, b):
    mu = jnp.mean(x, axis=-1, keepdims=True)
    d = x - mu
    var = jnp.mean(d * d, axis=-1, keepdims=True)
    return d * lax.rsqrt(var + LN_EPS) * g + b


def _mm_ln_body(x_ref, w_ref, res_ref, g_ref, b_ref, o_ref, ob_ref, y_ref, *, tn):
    j = pl.program_id(1)
    y_ref[:, pl.ds(pl.multiple_of(j * tn, tn), tn)] = _dot(x_ref[...], w_ref[...])

    @pl.when(j == pl.num_programs(1) - 1)
    def _():
        y = _layer_norm_rows(DEEPNORM_ALPHA * res_ref[...] + y_ref[...], g_ref[...], b_ref[...])
        o_ref[...] = y
        ob_ref[...] = y.astype(BF16)


def _mm_ln(x, w, res, g, b, tm, tn, name="mm_ln"):
    m, kd = x.shape
    n = w.shape[1]
    return pl.pallas_call(
        functools.partial(_mm_ln_body, tn=tn),
        grid=(m // tm, n // tn),
        in_specs=[pl.BlockSpec((tm, kd), lambda i, j: (i, 0)),
                  pl.BlockSpec((kd, tn), lambda i, j: (0, j)),
                  pl.BlockSpec((tm, n), lambda i, j: (i, 0)),
                  pl.BlockSpec((1, n), lambda i, j: (0, 0)),
                  pl.BlockSpec((1, n), lambda i, j: (0, 0))],
        out_specs=[pl.BlockSpec((tm, n), lambda i, j: (i, 0)),
                   pl.BlockSpec((tm, n), lambda i, j: (i, 0))],
        out_shape=[jax.ShapeDtypeStruct((m, n), F32), jax.ShapeDtypeStruct((m, n), BF16)],
        scratch_shapes=[pltpu.VMEM((tm, n), F32)],
        compiler_params=_cparams(("parallel", "arbitrary")),
        name=name,
    )(x, w, res, g, b)


def _branch_merge_body(rw_ref, at_ref, wa_ref, wb_ref, ga_ref, gb_ref, o_ref):
    a = _dot(rw_ref[...], wa_ref[...])
    b = _dot(at_ref[...], wb_ref[...])
    o_ref[...] = (_sigmoid(ga_ref[...]) * a + _sigmoid(gb_ref[...]) * b).astype(o_ref.dtype)


def _branch_merge(rw, att, wa, wb, z, tm, tn):
    m = rw.shape[0]
    n = wa.shape[1]
    ga0, gb0 = G_A // tn, G_B // tn
    return pl.pallas_call(
        _branch_merge_body,
        grid=(m // tm, n // tn),
        in_specs=[pl.BlockSpec((tm, RW_WIDTH), lambda i, j: (i, 0)),
                  pl.BlockSpec((tm, ATT_WIDTH), lambda i, j: (i, 0)),
                  pl.BlockSpec((RW_WIDTH, tn), lambda i, j: (0, j)),
                  pl.BlockSpec((ATT_WIDTH, tn), lambda i, j: (0, j)),
                  pl.BlockSpec((tm, tn), lambda i, j: (i, ga0 + j)),
                  pl.BlockSpec((tm, tn), lambda i, j: (i, gb0 + j))],
        out_specs=pl.BlockSpec((tm, tn), lambda i, j: (i, j)),
        out_shape=jax.ShapeDtypeStruct((m, n), BF16),
        compiler_params=_cparams(("parallel", "parallel")),
        name="branch_merge",
    )(rw, att, wa, wb, z, z)


def _seg_sum(x, ind, ind_t):
    return _dot(_dot(x, ind, HIGHEST), ind_t, HIGHEST)


def _split_bf16(x, parts):
    out = []
    for _ in range(parts):
        t = x.astype(BF16)
        out.append(t)
        x = x - t.astype(F32)
    return out


def _seg_sum_quads(x, bd):
    outs = []
    for q in range(x.shape[1] // WKV_W):
        hi, lo = _split_bf16(x[:, q * WKV_W:(q + 1) * WKV_W], 2)
        outs.append(_dot(hi, bd) + _dot(lo, bd))
    return jnp.concatenate(outs, axis=1) if len(outs) > 1 else outs[0]


def _rwkv_tokens(zr, zk, zv, zl, pr, pk, pv, plo, prm, mu_l, w2, a2, g2, seg):
    r = zr + (pr - zr) * prm[0:1]
    kx = zk + (pk - zk) * prm[1:2]
    v = zv + (pv - zv) * prm[2:3]
    zsl = zl + (plo - zl) * mu_l
    tw = jnp.tanh(zsl[:, 0:128]).astype(BF16)
    xw = prm[3:4] + _dot(tw, w2)
    lw = -EXP_M05 * _sigmoid(xw)
    a = _sigmoid(prm[4:5] + _dot(zsl[:, 128:256].astype(BF16), a2))
    g = _dot(_sigmoid(zsl[:, 256:512]).astype(BF16), g2)
    kk = kx * prm[5:6]
    n2 = seg(kk * kk)
    kkn = kk / jnp.maximum(jnp.sqrt(n2), 1e-12)
    kmod = kx * (1.0 + (a - 1.0) * prm[6:7])
    return r, lw, kmod, v, kkn, a, g


def _rwkv_post(y, r, kmod, v, g, prm, seg):
    inv_n = 1.0 / RW_HEAD_DIM
    mean = seg(y) * inv_n
    d = y - mean
    var = seg(d * d) * inv_n
    yn = d * lax.rsqrt(var + RW_GN_EPS) * prm[8:9] + prm[9:10]
    bonus = seg(r * kmod * prm[7:8]) * v
    return (yn + bonus) * g


WKV_C = 64
WKV_HQ = 4
WKV_W = WKV_HQ * RW_HEAD_DIM
WKV_QPS = 8


def _wkv_chunk_body(zr_ref, zk_ref, zv_ref, zl_ref, prm_ref, mul_ref, w2_ref, a2_ref, g2_ref,
                    sel_ref, o_ref, so_ref, s_ref, cr_ref, ck_ref, cv_ref, cl_ref):
    c = pl.program_id(2)
    C = WKV_C
    W = WKV_W

    @pl.when(c == 0)
    def _():
        s_ref[...] = jnp.zeros_like(s_ref)
        cr_ref[...] = jnp.zeros_like(cr_ref)
        ck_ref[...] = jnp.zeros_like(ck_ref)
        cv_ref[...] = jnp.zeros_like(cv_ref)
        cl_ref[...] = jnp.zeros_like(cl_ref)

    rows = lax.broadcasted_iota(I32, (C, 1), 0)

    def shifted(z, carry_ref):
        prev = jnp.where(rows == 0, carry_ref[0:1, :], pltpu.roll(z, 1, 0))
        carry_ref[0:1, :] = z[C - 1:C, :]
        return prev

    zr, zk, zv, zl = zr_ref[...], zk_ref[...], zv_ref[...], zl_ref[...]
    pr, pk, pv, plo = shifted(zr, cr_ref), shifted(zk, ck_ref), shifted(zv, cv_ref), shifted(zl, cl_ref)
    prm = prm_ref[...]
    lane_head = lax.broadcasted_iota(I32, (1, W), 1) // RW_HEAD_DIM
    hv = lax.broadcasted_iota(I32, (W, 1), 0) // RW_HEAD_DIM
    bd = (hv == lane_head).astype(BF16)
    seg = lambda x: _seg_sum_quads(x, bd)
    r, lw, kmod, v, kkn, a, g = _rwkv_tokens(zr, zk, zv, zl, pr, pk, pv, plo, prm, mul_ref[0:1, :],
                                            w2_ref[...], a2_ref[...], g2_ref[...], seg)
    al = -kkn
    be = kkn * a

    ti = lax.broadcasted_iota(I32, (C, C), 0)
    tj = lax.broadcasted_iota(I32, (C, C), 1)
    tri = (tj <= ti).astype(BF16)
    cum = sum(_dot(tri, part) for part in _split_bf16(lw, 3))
    cum_l = cum[C - 1:C, :]
    p_inv = jnp.exp(-cum)
    p_rel = jnp.exp(cum_l - cum)
    ab = al * jnp.exp(cum - lw)
    rb = r * jnp.exp(cum)
    bt = (be * p_inv).astype(BF16)
    kt = (kmod * p_inv).astype(BF16)
    bk = jnp.concatenate([be * p_rel, kmod * p_rel], axis=0).astype(BF16)
    ar = jnp.concatenate([ab, rb], axis=0)
    pc = jnp.exp(cum_l)

    n4 = WKV_HQ * C
    bi = lax.broadcasted_iota(I32, (n4, n4), 0)
    bj = lax.broadcasted_iota(I32, (n4, n4), 1)
    same = (bi // C) == (bj // C)
    tri_s4 = same & ((bj % C) < (bi % C))
    tri_i4 = same & ((bj % C) <= (bi % C))
    eye4 = (bi == bj).astype(F32)
    masks = [lane_head == j for j in range(WKV_HQ)]

    def stack(x):
        return jnp.concatenate([jnp.where(m, x, jnp.zeros_like(x)) for m in masks], axis=0)

    def block_sum(x):
        return sum(x[j * C:(j + 1) * C] for j in range(WKV_HQ))

    def bdot(a, b):
        return _dot(a.astype(BF16), b.astype(BF16))

    qs = []
    for q in range(WKV_QPS):
        sl = slice(q * W, (q + 1) * W)
        v_q = v[:, sl]
        lhs = jnp.concatenate([stack(ab[:, sl]), stack(rb[:, sl])], axis=0).astype(BF16)
        abr = _dot_nt(lhs, stack(bt[:, sl]))
        akr = _dot_nt(lhs, stack(kt[:, sl]))
        qs.append(dict(sl=sl, v=v_q, v_s=stack(v_q), s0=s_ref[q],
                       x=jnp.where(tri_s4, abr[0:n4], 0.0), a_rb=jnp.where(tri_i4, abr[n4:2 * n4], 0.0),
                       a_ak=jnp.where(tri_s4, akr[0:n4], 0.0), a_rk=jnp.where(tri_i4, akr[n4:2 * n4], 0.0)))
    for d in qs:
        d['pw'] = [d['x']]
    for _ in range(5):
        for d in qs:
            d['pw'].append(bdot(d['pw'][-1], d['pw'][-1]))
    for d in qs:
        pw = d['pw']
        pr_ = [eye4 + pw[2 * i] + pw[2 * i + 1] + bdot(pw[2 * i], pw[2 * i + 1]) for i in range(3)]
        d['t'] = bdot(bdot(pr_[0], pr_[1]), pr_[2])
    for d in qs:
        gs = _dot_nt(ar[:, d['sl']].astype(BF16), d['s0'].astype(BF16))
        d['g_r'] = gs[C:2 * C]
        d['w_s'] = stack(gs[0:C]) + bdot(d['a_ak'], d['v_s'])
    for d in qs:
        d['u_s'] = bdot(d['t'], d['w_s'])
    ys = []
    for q, d in enumerate(qs):
        yv = bdot(jnp.concatenate([d['a_rb'], d['a_rk']], axis=1), jnp.concatenate([d['u_s'], d['v_s']], axis=0))
        ys.append(d['g_r'] + block_sum(yv))
        uv_t = jnp.concatenate([block_sum(d['u_s']), d['v']], axis=0).T.astype(BF16)
        upd = _dot(uv_t, bk[:, d['sl']])
        s_ref[q] = d['s0'] * pc[:, d['sl']] + jnp.where(hv == lane_head, upd, 0.0)

    y = jnp.concatenate(ys, axis=1) if WKV_QPS > 1 else ys[0]
    o_ref[...] = _rwkv_post(y, r, kmod, v, g, prm, seg).astype(o_ref.dtype)

    @pl.when(c == pl.num_programs(2) - 1)
    def _():
        for q in range(WKV_QPS):
            for j in range(WKV_HQ):
                rows_j = s_ref[q, j * RW_HEAD_DIM:(j + 1) * RW_HEAD_DIM, :]
                so_ref[0, q * WKV_HQ + j] = _dot(rows_j, sel_ref[j], HIGHEST)


def _wkv_prompt(z, zs, prm, mu_l, w2, a2, g2, sel, batch, seq):
    nc = seq // WKV_C
    WS = WKV_W * WKV_QPS
    nq = RW_WIDTH // WS
    row = lambda b, q, c: b * nc + c
    return pl.pallas_call(
        _wkv_chunk_body,
        grid=(batch, nq, nc),
        in_specs=[pl.BlockSpec((WKV_C, WS), lambda b, q, c: (row(b, q, c), C_R // WS + q)),
                  pl.BlockSpec((WKV_C, WS), lambda b, q, c: (row(b, q, c), C_K // WS + q)),
                  pl.BlockSpec((WKV_C, WS), lambda b, q, c: (row(b, q, c), C_V // WS + q)),
                  pl.BlockSpec((WKV_C, 512), lambda b, q, c: (row(b, q, c), S_LORA // 512)),
                  pl.BlockSpec((16, WS), lambda b, q, c: (0, q)),
                  pl.BlockSpec((8, 512), lambda b, q, c: (0, 0)),
                  pl.BlockSpec((128, WS), lambda b, q, c: (0, q)),
                  pl.BlockSpec((128, WS), lambda b, q, c: (0, q)),
                  pl.BlockSpec((256, WS), lambda b, q, c: (0, q)),
                  pl.BlockSpec((WKV_HQ, WKV_W, RW_HEAD_DIM), lambda b, q, c: (0, 0, 0))],
        out_specs=[pl.BlockSpec((WKV_C, WS), lambda b, q, c: (row(b, q, c), q)),
                   pl.BlockSpec((1, WKV_HQ * WKV_QPS, RW_HEAD_DIM, RW_HEAD_DIM), lambda b, q, c: (b, q, 0, 0))],
        out_shape=[jax.ShapeDtypeStruct((batch * seq, RW_WIDTH), BF16),
                   jax.ShapeDtypeStruct((batch, RW_HEADS, RW_HEAD_DIM, RW_HEAD_DIM), F32)],
        scratch_shapes=[pltpu.VMEM((WKV_QPS, WKV_W, WKV_W), F32), pltpu.VMEM((8, WS), F32), pltpu.VMEM((8, WS), F32),
                        pltpu.VMEM((8, WS), F32), pltpu.VMEM((8, 512), F32)],
        compiler_params=_cparams(("parallel", "parallel", "arbitrary")),
        name="wkv_prompt",
    )(z, z, z, zs, prm, mu_l, w2, a2, g2, sel)


def _wkv_tok_body(zr_ref, zk_ref, zv_ref, zl_ref, pr_ref, pk_ref, pv_ref, pl_ref, prm_ref, mul_ref, w2_ref, a2_ref,
                  g2_ref, ind_ref, indt_ref, r_ref, w_ref, al_ref, be_ref, km_ref, v_ref, g_ref, bo_ref):
    prm = prm_ref[...]
    ind, ind_t = ind_ref[...], indt_ref[...]
    seg = lambda x: _seg_sum(x, ind, ind_t)
    r, lw, kmod, v, kkn, a, g = _rwkv_tokens(zr_ref[...], zk_ref[...], zv_ref[...], zl_ref[...], pr_ref[...],
                                            pk_ref[...], pv_ref[...], pl_ref[...], prm, mul_ref[0:1, :],
                                            w2_ref[...], a2_ref[...], g2_ref[...], seg)
    r_ref[...] = r
    w_ref[...] = jnp.exp(lw)
    al_ref[...] = -kkn
    be_ref[...] = kkn * a
    km_ref[...] = kmod
    v_ref[...] = v
    g_ref[...] = g
    bo_ref[...] = seg(r * kmod * prm[7:8])


def _wkv_tokens_sample(z, zs, prev_r, prev_k, prev_v, prev_l, prm, mu_l, w2, a2, g2, ind, ind_t, row0, n):
    rb = row0 // n
    full = lambda a: pl.BlockSpec(a.shape, lambda i: (0,) * a.ndim)
    zspec = lambda w, c0: pl.BlockSpec((n, w), lambda i: (rb, c0 // w))
    return pl.pallas_call(
        _wkv_tok_body,
        grid=(1,),
        in_specs=[zspec(RW_WIDTH, C_R), zspec(RW_WIDTH, C_K), zspec(RW_WIDTH, C_V), zspec(512, S_LORA),
                  full(prev_r), full(prev_k), full(prev_v), full(prev_l), full(prm), full(mu_l), full(w2), full(a2),
                  full(g2), full(ind), full(ind_t)],
        out_specs=[pl.BlockSpec((n, RW_WIDTH), lambda i: (0, 0))] * 8,
        out_shape=[jax.ShapeDtypeStruct((n, RW_WIDTH), F32)] * 8,
        compiler_params=_cparams(("arbitrary",)),
        name="wkv_tokens_sample",
    )(z, z, z, zs, prev_r, prev_k, prev_v, prev_l, prm, mu_l, w2, a2, g2, ind, ind_t)


def _wkv_step_body(s_ref, w_ref, al_ref, be_ref, km_ref, r_ref, v_ref, g_ref, bo_ref, lnw_ref, lnb_ref, o_ref, so_ref):
    s = s_ref[...]
    vcol = v_ref[...]
    sa = jnp.sum(s * al_ref[...], axis=-1, keepdims=True)
    s2 = s * w_ref[...] + sa * be_ref[...] + vcol * km_ref[...]
    so_ref[...] = s2
    y = jnp.sum(s2 * r_ref[...], axis=-1, keepdims=True)
    mean = jnp.mean(y, axis=2, keepdims=True)
    d = y - mean
    var = jnp.mean(d * d, axis=2, keepdims=True)
    yn = d * lax.rsqrt(var + RW_GN_EPS) * lnw_ref[...] + lnb_ref[...]
    o_ref[...] = (yn + bo_ref[...] * vcol) * g_ref[...]


def _wkv_step(state, w, al, be, km, r, v, g, bo, lnw, lnb):
    n, h = state.shape[0], state.shape[1]
    hq = 2
    rowspec = pl.BlockSpec((n, hq, 1, RW_HEAD_DIM), lambda q: (0, q, 0, 0))
    colspec = pl.BlockSpec((n, hq, RW_HEAD_DIM, 1), lambda q: (0, q, 0, 0))
    pcol = pl.BlockSpec((1, hq, RW_HEAD_DIM, 1), lambda q: (0, q, 0, 0))
    sspec = pl.BlockSpec((n, hq, RW_HEAD_DIM, RW_HEAD_DIM), lambda q: (0, q, 0, 0))
    return pl.pallas_call(
        _wkv_step_body,
        grid=(h // hq,),
        in_specs=[sspec, rowspec, rowspec, rowspec, rowspec, rowspec, colspec, colspec, colspec, pcol, pcol],
        out_specs=[colspec, sspec],
        out_shape=[jax.ShapeDtypeStruct((n, h, RW_HEAD_DIM, 1), F32), jax.ShapeDtypeStruct(state.shape, F32)],
        compiler_params=_cparams(("parallel",)),
        name="wkv_step",
    )(state, w, al, be, km, r, v, g, bo, lnw, lnb)


def _rope_tables(pos, rot_dim, period):
    half = rot_dim // 2
    t = pos.shape[0]
    inv_freq = ROPE_THETA ** (-jnp.arange(half, dtype=F32) / half)
    ang = pos.astype(F32)[:, None] * inv_freq[None, :]
    cos, sin = jnp.cos(ang), jnp.sin(ang)
    zh = jnp.zeros((t, half), F32)
    rest = period - rot_dim
    c = jnp.concatenate([cos, cos, jnp.ones((t, rest), F32)], axis=1)
    s1 = jnp.concatenate([-sin, zh, jnp.zeros((t, rest), F32)], axis=1)
    s2 = jnp.concatenate([zh, sin, jnp.zeros((t, rest), F32)], axis=1)
    rep = LANES // period
    return jnp.stack([jnp.tile(a, (1, rep)) for a in (c, s1, s2)], axis=0)


def _rope(x, tab, half):
    n = x.shape[1]
    rep = n // LANES
    c, s1, s2 = [jnp.tile(tab[i], (1, rep)) if rep > 1 else tab[i] for i in range(3)]
    return x * c + pltpu.roll(x, n - half, 1) * s1 + pltpu.roll(x, half, 1) * s2


def _prep_body(q_ref, ka_ref, va_ref, iq_ref, ikw_ref, ta_ref, ti_ref, lnw_ref, lnb_ref,
               qo_ref, ko_ref, kb_ref, vb_ref, qio_ref, kio_ref, kid_ref, vbt_ref):
    ta = ta_ref[...]
    ti = ti_ref[...]
    qo_ref[...] = (_rope(q_ref[...], ta, ROT_DIM // 2) * (HEAD_DIM ** -0.5)).astype(BF16)
    k = _rope(ka_ref[...], ta, ROT_DIM // 2)
    ko_ref[...] = k
    kb_ref[...] = k.astype(BF16)
    vb_ref[...] = va_ref[...].astype(BF16)
    vbt_ref[...] = va_ref[...].T.astype(BF16)
    qio_ref[...] = _rope(iq_ref[...], ti, IDX_ROT_DIM // 2).astype(BF16)
    x = ikw_ref[...]
    lane = lax.broadcasted_iota(I32, x.shape, 1)
    is_k = lane < IDX_DIM
    mu = jnp.sum(jnp.where(is_k, x, 0.0), axis=-1, keepdims=True) * (1.0 / IDX_DIM)
    d = jnp.where(is_k, x - mu, 0.0)
    var = jnp.sum(d * d, axis=-1, keepdims=True) * (1.0 / IDX_DIM)
    kn = d * lax.rsqrt(var + LN_EPS) * lnw_ref[...] + lnb_ref[...]
    kr = _rope(kn, ti, IDX_ROT_DIM // 2)
    kr = jnp.where(is_k, kr, 0.0)
    kio_ref[...] = jnp.where(is_k, kr, x * (IDX_HEADS ** -0.5))
    kid_ref[...] = (kr + pltpu.roll(kr, IDX_DIM, 1)).astype(BF16)


def _prep(z, zs, tab_a, tab_i, ln_w, ln_b, tm):
    m = z.shape[0]
    row = lambda w, c0: pl.BlockSpec((tm, w), lambda i: (i, c0 // w))
    outs = [(ATT_WIDTH, BF16), (KV_WIDTH, F32), (KV_WIDTH, BF16), (KV_WIDTH, BF16),
            (IDX_HEADS * IDX_DIM, BF16), (LANES, F32), (LANES, BF16)]
    return pl.pallas_call(
        _prep_body,
        grid=(m // tm,),
        in_specs=[row(ATT_WIDTH, A_Q), row(KV_WIDTH, A_KA), row(KV_WIDTH, A_VA), row(IDX_HEADS * IDX_DIM, A_IQ),
                  row(LANES, S_IKW),
                  pl.BlockSpec((3, tm, LANES), lambda i: (0, i, 0)),
                  pl.BlockSpec((3, tm, LANES), lambda i: (0, i, 0)),
                  pl.BlockSpec((1, LANES), lambda i: (0, 0)),
                  pl.BlockSpec((1, LANES), lambda i: (0, 0))],
        out_specs=[pl.BlockSpec((tm, w), lambda i: (i, 0)) for w, _ in outs]
        + [pl.BlockSpec((KV_WIDTH, tm), lambda i: (0, i))],
        out_shape=[jax.ShapeDtypeStruct((m, w), dt) for w, dt in outs] + [jax.ShapeDtypeStruct((KV_WIDTH, m), BF16)],
        compiler_params=_cparams(("parallel",)),
        name="prep",
    )(z, z, z, z, zs, tab_a, tab_i, ln_w, ln_b)


DSA_QB = 128
DSA_TK = 512
DSA_ATK = 512


def _float_key(s):
    b = pltpu.bitcast(s, I32)
    return b ^ ((b >> 31) & 0x7FFFFFFF)


def _fold_rows(x, op):
    n = x.shape[0]
    while n > SUBLANES:
        n //= 2
        x = op(x[0:n], x[n:2 * n])
    return x


def _kth_threshold(count_ge, topk):
    def step(b, thr):
        cand = thr + jnp.left_shift(jnp.int32(1), 31 - b)
        return jnp.where(count_ge(cand) >= topk, cand, thr)
    return lax.fori_loop(0, 32, step, jnp.full((1, DSA_QB), INT_MIN, I32))


def _dsa_prompt_body(q_ref, kb_ref, vbt_ref, qi_ref, kid_ref, kiw_ref, o_ref, keys_ref, jcut_ref, acc_ref, qs_ref,
                     s_ref, p_ref, *, topk, seq):
    i = pl.program_id(1)
    nt = (i * DSA_QB + DSA_QB + DSA_TK - 1) // DSA_TK
    qpos = i * DSA_QB + lax.broadcasted_iota(I32, (1, DSA_QB), 1)
    row0 = lax.broadcasted_iota(I32, (DSA_TK, 1), 0)
    lane = lax.broadcasted_iota(I32, (1, LANES), 1)
    w_t = (kiw_ref[...] * (IDX_DIM ** -0.5)).T

    def score_tile(t, carry):
        kd = kid_ref[pl.ds(t * DSA_TK, DSA_TK), :]
        s = jnp.zeros((DSA_TK, DSA_QB), F32)
        for h in range(IDX_HEADS):
            qt = qi_ref[:, (h // 2) * LANES:(h // 2 + 1) * LANES]
            qh = jnp.where((lane // IDX_DIM) == (h % 2), qt, jnp.zeros_like(qt))
            s = s + w_t[IDX_DIM + h:IDX_DIM + h + 1, :] * jnp.maximum(_dot_nt(kd, qh), 0.0)
        valid = (t * DSA_TK + row0) <= qpos
        keys_ref[pl.ds(t * DSA_TK, DSA_TK), :] = jnp.where(valid, _float_key(s), INT_MIN)
        return carry

    lax.fori_loop(0, nt, score_tile, 0)

    def count(pred):
        def body(t, acc):
            kt = keys_ref[pl.ds(t * DSA_TK, DSA_TK), :]
            hit = pred(kt, t * DSA_TK + row0).astype(I32)
            return acc + _fold_rows(hit, jnp.add)
        acc = lax.fori_loop(0, nt, body, jnp.zeros((SUBLANES, DSA_QB), I32))
        return jnp.sum(acc, axis=0, keepdims=True)

    thr = _kth_threshold(lambda c: count(lambda kt, col: kt >= c), topk)
    n_gt = count(lambda kt, col: kt > thr)
    n_eq = count(lambda kt, col: (kt == thr) & (col <= qpos))
    need = topk - n_gt
    jcut_ref[...] = jnp.full(jcut_ref.shape, seq, I32)
    excess = (n_eq > need) & (thr > INT_MIN)

    @pl.when(jnp.max(excess.astype(I32)) > 0)
    def _():
        def step(b, jm):
            cand = jm + jnp.left_shift(jnp.int32(1), 30 - b)
            c = count(lambda kt, col: (kt == thr) & (col <= qpos) & (col < cand))
            return jnp.where(c < need, cand, jm)
        jm = lax.fori_loop(0, 31, step, jnp.zeros((1, DSA_QB), I32))
        jcut_ref[...] = jnp.broadcast_to(jnp.where(excess, jm, seq), jcut_ref.shape)

    jcut = jcut_ref[0:1, :]

    acc_ref[...] = jnp.zeros_like(acc_ref)
    for h in range(N_HEADS):
        qs_ref[h // GROUP, (h % GROUP) * DSA_QB:(h % GROUP + 1) * DSA_QB, :] = q_ref[:, h * HEAD_DIM:(h + 1) * HEAD_DIM]
    nta = (i * DSA_QB + DSA_QB + DSA_ATK - 1) // DSA_ATK
    rowa0 = lax.broadcasted_iota(I32, (DSA_ATK, 1), 0)

    def att_tile(t, carry):
        ms, ls = carry
        k0 = pl.multiple_of(t * DSA_ATK, DSA_ATK)
        kt = keys_ref[pl.ds(k0, DSA_ATK), :]
        kpos = k0 + rowa0
        sel = ((kt > thr) | ((kt == thr) & (kpos <= jcut))) & (kpos <= qpos)
        sel4 = jnp.concatenate([sel] * GROUP, axis=1)
        for g in range(N_KV_HEADS):
            k_t = kb_ref[pl.ds(k0, DSA_ATK), g * HEAD_DIM:(g + 1) * HEAD_DIM]
            s_ref[g] = _dot_nt(k_t, qs_ref[g])
        new_m, new_l, corrs = [], [], []
        for g in range(N_KV_HEADS):
            s = jnp.where(sel4, s_ref[g], -1e30)
            m_new = jnp.maximum(ms[g], jnp.max(_fold_rows(s, jnp.maximum), axis=0, keepdims=True))
            p = jnp.exp(s - m_new)
            corr = jnp.exp(ms[g] - m_new)
            new_l.append(ls[g] * corr + jnp.sum(_fold_rows(p, jnp.add), axis=0, keepdims=True))
            p_ref[g] = p.astype(BF16)
            new_m.append(m_new)
            corrs.append(corr)
        for g in range(N_KV_HEADS):
            vt_t = vbt_ref[g * HEAD_DIM:(g + 1) * HEAD_DIM, pl.ds(k0, DSA_ATK)]
            acc_ref[g] = acc_ref[g] * corrs[g] + _dot(vt_t, p_ref[g])
        return tuple(new_m), tuple(new_l)

    wq = GROUP * DSA_QB
    init = (tuple(jnp.full((1, wq), -1e29, F32) for _ in range(N_KV_HEADS)),
            tuple(jnp.zeros((1, wq), F32) for _ in range(N_KV_HEADS)))
    ms, ls = lax.fori_loop(0, nta, att_tile, init)
    for h in range(N_HEADS):
        g, c = h // GROUP, (h % GROUP) * DSA_QB
        o_t = acc_ref[g, :, c:c + DSA_QB] / ls[g][:, c:c + DSA_QB]
        o_ref[:, h * HEAD_DIM:(h + 1) * HEAD_DIM] = o_t.T.astype(o_ref.dtype)


def _dsa_prompt(qb, kb, vbt, qib, kid, kiw, batch, seq, topk):
    nb = seq // DSA_QB
    body = functools.partial(_dsa_prompt_body, topk=topk, seq=seq)
    return pl.pallas_call(
        body,
        grid=(batch, nb),
        in_specs=[pl.BlockSpec((DSA_QB, ATT_WIDTH), lambda b, i: (b * nb + i, 0)),
                  pl.BlockSpec((seq, KV_WIDTH), lambda b, i: (b, 0)),
                  pl.BlockSpec((KV_WIDTH, seq), lambda b, i: (0, b)),
                  pl.BlockSpec((DSA_QB, IDX_HEADS * IDX_DIM), lambda b, i: (b * nb + i, 0)),
                  pl.BlockSpec((seq, LANES), lambda b, i: (b, 0)),
                  pl.BlockSpec((DSA_QB, LANES), lambda b, i: (b * nb + i, 0))],
        out_specs=pl.BlockSpec((DSA_QB, ATT_WIDTH), lambda b, i: (b * nb + i, 0)),
        out_shape=jax.ShapeDtypeStruct((batch * seq, ATT_WIDTH), BF16),
        scratch_shapes=[pltpu.VMEM((seq, DSA_QB), I32), pltpu.VMEM((SUBLANES, DSA_QB), I32),
                        pltpu.VMEM((N_KV_HEADS, HEAD_DIM, GROUP * DSA_QB), F32),
                        pltpu.VMEM((N_KV_HEADS, GROUP * DSA_QB, HEAD_DIM), BF16),
                        pltpu.VMEM((N_KV_HEADS, DSA_ATK, GROUP * DSA_QB), F32),
                        pltpu.VMEM((N_KV_HEADS, DSA_ATK, GROUP * DSA_QB), BF16)],
        compiler_params=_cparams(("parallel", "arbitrary")),
        name="dsa_prompt",
    )(qb, kb, vbt, qib, kid, kiw)


def _sel_sample_body(pt_ref, qi_ref, wi_ref, ks_ref, ck_hbm, pos_ref, ms_ref, kbuf, sc_ref, jm_ref, rk_ref, sem, *,
                     topk, npg):
    s = pl.program_id(0)
    slot = s % 2
    U = SEL_SPS

    def page_copy(step, u, p, sl):
        return pltpu.make_async_copy(ck_hbm.at[pt_ref[step * U + u, p]], kbuf.at[sl, u, p], sem.at[sl])

    def request(step, sl):
        for u in range(U):
            for p in range(npg):
                page_copy(step, u, p, sl).start()

    @pl.when(s == 0)
    def _():
        request(0, 0)

    @pl.when(s + 1 < pl.num_programs(0))
    def _():
        request(s + 1, 1 - slot)

    for u in range(U):
        for p in range(npg):
            page_copy(s, u, p, slot).wait()

    lane = lax.broadcasted_iota(I32, (1, PAGE_SIZE), 1)
    pos = lax.broadcasted_iota(I32, (npg, PAGE_SIZE), 0) * PAGE_SIZE + lane

    def total(x):
        return jnp.sum(jnp.sum(x.astype(I32), axis=1, keepdims=True), axis=0, keepdims=True)

    keys, k_self = [], []
    for u in range(U):
        qi, wi = qi_ref[u], wi_ref[u] * (IDX_DIM ** -0.5)
        for c in range(npg // SEL_CP):
            kt = jnp.concatenate([kbuf[slot, u, c * SEL_CP + r] for r in range(SEL_CP)], axis=1).astype(BF16)
            sc = jnp.sum(wi * jnp.maximum(_dot(qi, kt), 0.0), axis=0, keepdims=True)
            for r in range(SEL_CP):
                sc_ref[u, c * SEL_CP + r:c * SEL_CP + r + 1, :] = sc[:, r * PAGE_SIZE:(r + 1) * PAGE_SIZE]
        keys.append(_float_key(sc_ref[u]))
        d = jnp.sum(qi.astype(F32) * ks_ref[u].astype(F32), axis=-1, keepdims=True)
        k_self.append(_float_key(jnp.sum(wi * jnp.maximum(d, 0.0), axis=0, keepdims=True)))

    def step(b, thrs):
        out = []
        for u in range(U):
            cand = thrs[u] + jnp.left_shift(jnp.int32(1), 31 - b)
            c = total(keys[u] >= cand) + (k_self[u] >= cand).astype(I32)
            out.append(jnp.where(c >= topk, cand, thrs[u]))
        return tuple(out)

    thrs = lax.fori_loop(0, 32, step, tuple(jnp.full((1, 1), INT_MIN, I32) for _ in range(U)))

    ri = lax.broadcasted_iota(I32, (PAGE_SIZE, PAGE_SIZE), 0)
    ci = lax.broadcasted_iota(I32, (PAGE_SIZE, PAGE_SIZE), 1)
    pr_ = lax.broadcasted_iota(I32, (npg, npg), 0)
    pc_ = lax.broadcasted_iota(I32, (npg, npg), 1)
    jcol = lax.broadcasted_iota(I32, (topk, PAGE_SIZE), 0)
    lane_f = lax.broadcasted_iota(I32, (topk, PAGE_SIZE), 1).astype(F32)
    ones8 = jnp.ones((SUBLANES, PAGE_SIZE), BF16)
    for u in range(U):
        thr = thrs[u]
        need = topk - total(keys[u] > thr) - (k_self[u] > thr).astype(I32)
        eq = keys[u] == thr
        jm_ref[u] = jnp.full((SUBLANES, LANES), npg * PAGE_SIZE, I32)

        @pl.when(jnp.max((total(eq) > need).astype(I32)) > 0)
        def _():
            def jstep(b, jm):
                cand = jm + jnp.left_shift(jnp.int32(1), 30 - b)
                return jnp.where(total(eq & (pos < cand)) < need, cand, jm)
            jm_ref[u] = jnp.broadcast_to(lax.fori_loop(0, 31, jstep, jnp.zeros((1, 1), I32)), (SUBLANES, LANES))

        jm = jm_ref[u, 0:1, 0:1]
        sel = (keys[u] > thr) | (eq & (pos <= jm))
        self_sel = (k_self[u] > thr) | ((k_self[u] == thr) & (total(eq & (pos <= jm)) < need))
        ms_ref[u] = jnp.broadcast_to(self_sel.astype(F32), (1, LANES))

        sel_b = sel.astype(BF16)
        within = _dot(sel_b, (ri <= ci).astype(BF16))
        tot = _dot(sel_b, jnp.ones((PAGE_SIZE, PAGE_SIZE), BF16))
        before = _dot((pc_ < pr_).astype(BF16), tot.astype(BF16))
        rk_ref[u] = jnp.where(sel, (before + within).astype(I32) - 1, -1)

        def gather_pos(p, carry):
            hi, lo = carry
            hit = jnp.broadcast_to(rk_ref[u, pl.ds(p, 1), :], (topk, PAGE_SIZE)) == jcol
            return hi + jnp.where(hit, jnp.asarray(p, F32), 0.0), lo + jnp.where(hit, lane_f, 0.0)

        zero = jnp.zeros((topk, PAGE_SIZE), F32)
        hi, lo = lax.fori_loop(0, npg, gather_pos, (zero, zero))
        pos_row = _dot_nt(ones8, hi.astype(BF16)) * PAGE_SIZE + _dot_nt(ones8, lo.astype(BF16))
        pos_ref[u] = pos_row[0:1].astype(I32)


SEL_CP = 8
SEL_SPS = 4


def _sel_sample(page_table, qi, wi, kself, cache_kt, topk):
    n, npg = page_table.shape
    U = SEL_SPS
    assert npg % SEL_CP == 0 and npg <= PAGE_SIZE and n % U == 0
    grid_spec = pltpu.PrefetchScalarGridSpec(
        num_scalar_prefetch=1,
        grid=(n // U,),
        in_specs=[pl.BlockSpec((U, IDX_HEADS, IDX_DIM), lambda s, pt: (s, 0, 0)),
                  pl.BlockSpec((U, IDX_HEADS, 1), lambda s, pt: (s, 0, 0)),
                  pl.BlockSpec((U, 1, IDX_DIM), lambda s, pt: (s, 0, 0)),
                  pl.BlockSpec(memory_space=pl.ANY)],
        out_specs=[pl.BlockSpec((U, 1, topk), lambda s, pt: (s, 0, 0)),
                   pl.BlockSpec((U, 1, LANES), lambda s, pt: (s, 0, 0))],
        scratch_shapes=[pltpu.VMEM((2, U, npg, IDX_DIM, PAGE_SIZE), F32), pltpu.VMEM((U, npg, PAGE_SIZE), F32),
                        pltpu.VMEM((U, SUBLANES, LANES), I32), pltpu.VMEM((U, npg, PAGE_SIZE), I32),
                        pltpu.SemaphoreType.DMA((2,))],
    )
    return pl.pallas_call(
        functools.partial(_sel_sample_body, topk=topk, npg=npg),
        grid_spec=grid_spec,
        out_shape=[jax.ShapeDtypeStruct((n, 1, topk), I32), jax.ShapeDtypeStruct((n, 1, LANES), F32)],
        compiler_params=_cparams(("arbitrary",)),
        name="sel_sample",
    )(page_table, qi, wi, kself, cache_kt)


def _att_sel_body(pt_ref, pos_ref, q_ref, ms_ref, ks_ref, vs_ref, ck_hbm, cv_hbm, o_ref, kbuf, vbuf, sem, *, topk):
    s = pl.program_id(0)
    slot = s % 2

    def request(seq, sl):
        def body(j, c):
            pos = pos_ref[seq, j]
            pg = pt_ref[seq, pos // PAGE_SIZE]
            r = pos % PAGE_SIZE
            pltpu.make_async_copy(ck_hbm.at[pg, r], kbuf.at[sl, j], sem.at[0, sl]).start()
            pltpu.make_async_copy(cv_hbm.at[pg, r], vbuf.at[sl, j], sem.at[1, sl]).start()
            return c
        lax.fori_loop(0, topk, body, 0, unroll=8)

    @pl.when(s == 0)
    def _():
        request(0, 0)

    @pl.when(s + 1 < pl.num_programs(0))
    def _():
        request(s + 1, 1 - slot)

    for h in range(topk // PAGE_SIZE):
        rows = pl.ds(h * PAGE_SIZE, PAGE_SIZE)
        pltpu.make_async_copy(ck_hbm.at[0], kbuf.at[slot, rows], sem.at[0, slot]).wait()
        pltpu.make_async_copy(cv_hbm.at[0], vbuf.at[slot, rows], sem.at[1, slot]).wait()

    q = q_ref[0]
    row_g = lax.broadcasted_iota(I32, (N_HEADS, 1), 0) // GROUP
    lane_g = lax.broadcasted_iota(I32, (1, KV_WIDTH), 1) // HEAD_DIM
    q_bd = jnp.where(row_g == lane_g, jnp.tile(q, (1, N_KV_HEADS)), jnp.zeros((N_HEADS, KV_WIDTH), BF16))
    k2 = jnp.concatenate([kbuf[slot, :, g, :] for g in range(N_KV_HEADS)], axis=1).astype(BF16)
    v2 = jnp.concatenate([vbuf[slot, :, g, :] for g in range(N_KV_HEADS)], axis=1).astype(BF16)
    self_row = ms_ref[0]
    self_f = self_row[:, 0:1]
    n_past = topk - jnp.tile(self_row, (1, topk // LANES))
    valid = lax.broadcasted_iota(I32, (1, topk), 1).astype(F32) < n_past
    sc = jnp.where(valid, _dot_nt(q_bd, k2), -1e30)
    s1 = jnp.sum(q.astype(F32) * ks_ref[0].astype(F32), axis=-1, keepdims=True)
    s1 = jnp.where(self_f > 0.5, s1, -1e30)
    m = jnp.maximum(jnp.max(sc, axis=-1, keepdims=True), jnp.maximum(s1, -1e29))
    pr = jnp.exp(sc - m)
    p1 = jnp.exp(s1 - m)
    l = jnp.sum(pr, axis=-1, keepdims=True) + p1
    pv = _dot(pr.astype(BF16), v2)
    own = jnp.zeros((N_HEADS, HEAD_DIM), F32)
    for g in range(N_KV_HEADS):
        own = jnp.where(row_g == g, pv[:, g * HEAD_DIM:(g + 1) * HEAD_DIM], own)
    o_ref[0] = ((own + p1.astype(BF16).astype(F32) * vs_ref[0].astype(F32)) / l).astype(o_ref.dtype)


def _att_sel(page_table, pos_list, q, mself, kself, vself, cache_k, cache_v):
    n, topk = pos_list.shape
    assert topk % PAGE_SIZE == 0
    seqspec = lambda r, c: pl.BlockSpec((1, r, c), lambda s, pt, pos: (s, 0, 0))
    anyspec = pl.BlockSpec(memory_space=pl.ANY)
    rows = (2, topk, N_KV_HEADS, HEAD_DIM)
    grid_spec = pltpu.PrefetchScalarGridSpec(
        num_scalar_prefetch=2,
        grid=(n,),
        in_specs=[seqspec(N_HEADS, HEAD_DIM), seqspec(1, LANES), seqspec(N_HEADS, HEAD_DIM),
                  seqspec(N_HEADS, HEAD_DIM), anyspec, anyspec],
        out_specs=seqspec(N_HEADS, HEAD_DIM),
        scratch_shapes=[pltpu.VMEM(rows, F32), pltpu.VMEM(rows, F32), pltpu.SemaphoreType.DMA((2, 2))],
    )
    return pl.pallas_call(
        functools.partial(_att_sel_body, topk=topk),
        grid_spec=grid_spec,
        out_shape=jax.ShapeDtypeStruct((n, N_HEADS, HEAD_DIM), BF16),
        compiler_params=_cparams(("arbitrary",)),
        name="att_sel",
    )(page_table, pos_list, q, mself, kself, vself, cache_k, cache_v)


def _mem_att_prompt_body(q_ref, k_ref, v_ref, o_ref):
    scale = MEM_HEAD_DIM ** -0.5
    for h in range(MEM_HEADS):
        sl = slice(h * MEM_HEAD_DIM, (h + 1) * MEM_HEAD_DIM)
        s = _dot_nt(q_ref[:, sl], k_ref[:, sl]) * scale
        m = jnp.max(s, axis=-1, keepdims=True)
        e = jnp.exp(s - m)
        pr = e / jnp.sum(e, axis=-1, keepdims=True)
        o_ref[:, sl] = _dot(pr.astype(BF16), v_ref[:, sl]).astype(o_ref.dtype)


def _mem_att_prompt(mq, mk, mv, batch, seq, tq):
    m = mk.shape[0] // batch
    nb = seq // tq
    return pl.pallas_call(
        _mem_att_prompt_body,
        grid=(batch * nb,),
        in_specs=[pl.BlockSpec((tq, MEM_WIDTH), lambda i: (i, 0)),
                  pl.BlockSpec((m, MEM_WIDTH), lambda i: (i // nb, 0)),
                  pl.BlockSpec((m, MEM_WIDTH), lambda i: (i // nb, 0))],
        out_specs=pl.BlockSpec((tq, MEM_WIDTH), lambda i: (i, 0)),
        out_shape=jax.ShapeDtypeStruct((batch * seq, MEM_WIDTH), BF16),
        compiler_params=_cparams(("parallel",)),
        name="mem_att_prompt",
    )(mq, mk, mv)


def _mem_att_sample_body(q_ref, k_ref, v_ref, o_ref):
    scale = MEM_HEAD_DIM ** -0.5
    q = q_ref[0].astype(F32)
    for h in range(MEM_HEADS):
        sl = slice(h * MEM_HEAD_DIM, (h + 1) * MEM_HEAD_DIM)
        s = jnp.sum(k_ref[0, :, h, :] * q[:, sl], axis=-1, keepdims=True) * scale
        m = jnp.max(s, axis=0, keepdims=True)
        e = jnp.exp(s - m)
        pr = e / jnp.sum(e, axis=0, keepdims=True)
        o_ref[0, :, sl] = jnp.sum(pr * v_ref[0, :, h, :], axis=0, keepdims=True).astype(o_ref.dtype)


def _mem_att_sample(mq, mk, mv):
    n, m, nh, hd = mk.shape
    w = nh * hd
    return pl.pallas_call(
        _mem_att_sample_body,
        grid=(n,),
        in_specs=[pl.BlockSpec((1, 1, w), lambda s: (s, 0, 0)),
                  pl.BlockSpec((1, m, nh, hd), lambda s: (s, 0, 0, 0)),
                  pl.BlockSpec((1, m, nh, hd), lambda s: (s, 0, 0, 0))],
        out_specs=pl.BlockSpec((1, 1, w), lambda s: (s, 0, 0)),
        out_shape=jax.ShapeDtypeStruct((n, 1, w), BF16),
        compiler_params=_cparams(("parallel",)),
        name="mem_att_sample",
    )(mq, mk, mv)


def _router_body(x_ref, w_ref, b_ref, ei_ref, ew_ref, acc_ref):
    k = pl.program_id(1)

    @pl.when(k == 0)
    def _():
        acc_ref[...] = jnp.zeros_like(acc_ref)

    acc_ref[...] += _dot(x_ref[...], w_ref[...], HIGHEST)

    @pl.when(k == pl.num_programs(1) - 1)
    def _():
        lg = acc_ref[...] + b_ref[...]
        lane = lax.broadcasted_iota(I32, lg.shape, 1)
        neg = jnp.float32(-jnp.inf)
        is_g = lane < N_GROUPS
        glm = jnp.where(is_g, lg, neg)
        gmax = jnp.max(glm, axis=-1, keepdims=True)
        g_sel = jnp.min(jnp.where(glm == gmax, lane, LANES), axis=-1, keepdims=True)
        g_prob = 1.0 / jnp.sum(jnp.where(is_g, jnp.exp(lg - gmax), 0.0), axis=-1, keepdims=True)
        e_id = lane - N_GROUPS
        in_grp = (e_id >= 0) & (e_id < N_EXPERTS) & ((e_id // EXPERTS_PER_GROUP) == g_sel)
        el = jnp.where(in_grp, lg, neg)
        m1 = jnp.max(el, axis=-1, keepdims=True)
        i1 = jnp.min(jnp.where(in_grp & (el == m1), lane, LANES), axis=-1, keepdims=True)
        rest = in_grp & (lane != i1)
        el2 = jnp.where(rest, lg, neg)
        m2 = jnp.max(el2, axis=-1, keepdims=True)
        i2 = jnp.min(jnp.where(rest & (el2 == m2), lane, LANES), axis=-1, keepdims=True)
        t = jnp.exp(m2 - m1)
        w1 = g_prob / (1.0 + t)
        w2 = g_prob * t / (1.0 + t)
        ei_ref[...] = jnp.where(lane == 0, i1 - N_GROUPS, jnp.where(lane == 1, i2 - N_GROUPS, 0))
        ew_ref[...] = jnp.where(lane == 0, w1, jnp.where(lane == 1, w2, 0.0))


def _router(x, w, b, tm, tk):
    m, kd = x.shape
    return pl.pallas_call(
        _router_body,
        grid=(m // tm, kd // tk),
        in_specs=[pl.BlockSpec((tm, tk), lambda i, k: (i, k)),
                  pl.BlockSpec((tk, LANES), lambda i, k: (k, 0)),
                  pl.BlockSpec((1, LANES), lambda i, k: (0, 0))],
        out_specs=[pl.BlockSpec((tm, LANES), lambda i, k: (i, 0)), pl.BlockSpec((tm, LANES), lambda i, k: (i, 0))],
        out_shape=[jax.ShapeDtypeStruct((m, LANES), I32), jax.ShapeDtypeStruct((m, LANES), F32)],
        scratch_shapes=[pltpu.VMEM((tm, LANES), F32)],
        compiler_params=_cparams(("parallel", "arbitrary")),
        name="router",
    )(x, w, b)


MOE_BR = 128
MOE_GB = 5


def _gather_body(idx_ref, nblk_ref, src_ref, o_ref, sem):
    i = pl.program_id(0)
    g = o_ref.shape[0]

    def row_copy(src_row, dst_row):
        return pltpu.make_async_copy(src_ref.at[pl.ds(src_row, 1)], o_ref.at[pl.ds(dst_row, 1)], sem)

    @pl.when(i < nblk_ref[0])
    def _():
        def issue(r, c):
            row_copy(idx_ref[i * g + r], r).start()
            return c
        lax.fori_loop(0, g, issue, 0, unroll=8)
        pltpu.make_async_copy(src_ref.at[pl.ds(0, g)], o_ref, sem).wait()

    @pl.when(i >= nblk_ref[0])
    def _():
        o_ref[...] = jnp.zeros_like(o_ref)


def _gather_rows(src, idx, nblk, g):
    m = idx.shape[0]
    tail = src.shape[1:]
    zeros = (0,) * len(tail)
    grid_spec = pltpu.PrefetchScalarGridSpec(
        num_scalar_prefetch=2,
        grid=(m // g,),
        in_specs=[pl.BlockSpec(memory_space=pl.ANY)],
        out_specs=pl.BlockSpec((g,) + tail, lambda i, idx, nb: (i,) + zeros),
        scratch_shapes=[pltpu.SemaphoreType.DMA(())],
    )
    return pl.pallas_call(
        _gather_body,
        grid_spec=grid_spec,
        out_shape=jax.ShapeDtypeStruct((m,) + tail, src.dtype),
        compiler_params=_cparams(("arbitrary",)),
        name="gather_rows",
    )(idx, nblk, src)


def _expert_up_body(be_ref, nblk_ref, x_ref, wg_ref, wu_ref, h_ref, wgb_ref, wub_ref):
    i = pl.program_id(1)
    changed = jnp.logical_or(i == 0, be_ref[i] != be_ref[jnp.maximum(i - 1, 0)])

    @pl.when(jnp.logical_and(i < nblk_ref[0], changed))
    def _():
        wgb_ref[...] = wg_ref[0].astype(BF16)
        wub_ref[...] = wu_ref[0].astype(BF16)

    @pl.when(i < nblk_ref[0])
    def _():
        x = x_ref[...].astype(BF16)
        a = _dot(x, wgb_ref[...])
        u = _dot(x, wub_ref[...])
        h_ref[...] = (a * _sigmoid(a) * u).astype(h_ref.dtype)

    @pl.when(i >= nblk_ref[0])
    def _():
        h_ref[...] = jnp.zeros_like(h_ref)


def _expert_up(block_e, nblk, xs, w_gate, w_up, th):
    nr, d = xs.shape
    nb = nr // MOE_BR
    nh = D_EXPERT // th
    blk = lambda i, nbk: jnp.minimum(i, nbk[0] - 1)
    grid_spec = pltpu.PrefetchScalarGridSpec(
        num_scalar_prefetch=2,
        grid=(nh, nb),
        in_specs=[pl.BlockSpec((MOE_BR, d), lambda j, i, be, nbk: (blk(i, nbk), 0)),
                  pl.BlockSpec((1, d, th), lambda j, i, be, nbk: (be[blk(i, nbk)], 0, j)),
                  pl.BlockSpec((1, d, th), lambda j, i, be, nbk: (be[blk(i, nbk)], 0, j))],
        out_specs=pl.BlockSpec((MOE_BR, th), lambda j, i, be, nbk: (i, j)),
        scratch_shapes=[pltpu.VMEM((d, th), BF16), pltpu.VMEM((d, th), BF16)],
    )
    return pl.pallas_call(
        _expert_up_body,
        grid_spec=grid_spec,
        out_shape=jax.ShapeDtypeStruct((nr, D_EXPERT), BF16),
        compiler_params=_cparams(("arbitrary", "arbitrary")),
        name="expert_up",
    )(block_e, nblk, xs, w_gate, w_up)


def _expert_down_body(be_ref, nblk_ref, h_ref, wd_ref, y_ref, wdb_ref):
    i = pl.program_id(0)
    changed = jnp.logical_or(i == 0, be_ref[i] != be_ref[jnp.maximum(i - 1, 0)])

    @pl.when(jnp.logical_and(i < nblk_ref[0], changed))
    def _():
        wdb_ref[...] = wd_ref[0].astype(BF16)

    @pl.when(i < nblk_ref[0])
    def _():
        y_ref[...] = _dot(h_ref[...], wdb_ref[...])

    @pl.when(i >= nblk_ref[0])
    def _():
        y_ref[...] = jnp.zeros_like(y_ref)


def _expert_down(block_e, nblk, h, w_down):
    nr = h.shape[0]
    d = w_down.shape[2]
    blk = lambda i, nbk: jnp.minimum(i, nbk[0] - 1)
    grid_spec = pltpu.PrefetchScalarGridSpec(
        num_scalar_prefetch=2,
        grid=(nr // MOE_BR,),
        in_specs=[pl.BlockSpec((MOE_BR, D_EXPERT), lambda i, be, nbk: (blk(i, nbk), 0)),
                  pl.BlockSpec((1, D_EXPERT, d), lambda i, be, nbk: (be[blk(i, nbk)], 0, 0))],
        out_specs=pl.BlockSpec((MOE_BR, d), lambda i, be, nbk: (i, 0)),
        scratch_shapes=[pltpu.VMEM((D_EXPERT, d), BF16)],
    )
    return pl.pallas_call(
        _expert_down_body,
        grid_spec=grid_spec,
        out_shape=jax.ShapeDtypeStruct((nr, d), F32),
        compiler_params=_cparams(("arbitrary",)),
        name="expert_down",
    )(block_e, nblk, h, w_down)


def _combine_ln_body(x_ref, y0_ref, y1_ref, ew_ref, g_ref, b_ref, op_ref, os_ref, *, npb):
    i = pl.program_id(0)
    ew = ew_ref[...]
    ff = y0_ref[...] * ew[:, 0:1] + y1_ref[...] * ew[:, 1:2]
    y = _layer_norm_rows(DEEPNORM_ALPHA * x_ref[...] + ff, g_ref[...], b_ref[...])

    @pl.when(i < npb)
    def _():
        op_ref[...] = y

    @pl.when(i >= npb)
    def _():
        os_ref[...] = y


def _combine_ln(x, yg, ew, g, b, tm, n_prompt):
    m, d = x.shape
    nb = m // tm
    npb = n_prompt // tm
    return pl.pallas_call(
        functools.partial(_combine_ln_body, npb=npb),
        grid=(nb,),
        in_specs=[pl.BlockSpec((tm, d), lambda i: (i, 0)),
                  pl.BlockSpec((tm, d), lambda i: (i, 0)),
                  pl.BlockSpec((tm, d), lambda i: (nb + i, 0)),
                  pl.BlockSpec((tm, LANES), lambda i: (i, 0)),
                  pl.BlockSpec((1, d), lambda i: (0, 0)),
                  pl.BlockSpec((1, d), lambda i: (0, 0))],
        out_specs=[pl.BlockSpec((tm, d), lambda i: (jnp.minimum(i, npb - 1), 0)),
                   pl.BlockSpec((tm, d), lambda i: (jnp.maximum(i - npb, 0), 0))],
        out_shape=[jax.ShapeDtypeStruct((n_prompt, d), F32), jax.ShapeDtypeStruct((m - n_prompt, d), F32)],
        compiler_params=_cparams(("arbitrary",)),
        name="combine_ln",
    )(x, yg, yg, ew, g, b)


def _pad_cols(x, n):
    return jnp.pad(x, ((0, 0), (0, n - x.shape[1])))


def _split_w_in(w):
    o = [int(v) for v in np.cumsum([0, RW_PROJ, ATT_WIDTH + 2 * KV_WIDTH + IDX_HEADS * IDX_DIM, IDX_DIM + IDX_HEADS,
                                    2 * D_MODEL])]
    w_rkv = w[:, 0:3 * RW_WIDTH].astype(BF16)
    w_att = w[:, o[1]:o[2]].astype(BF16)
    w_gate = w[:, o[3]:o[4]].astype(BF16)
    w_small = jnp.concatenate([_lora_cols(w[:, 0:RW_PROJ]), _pad_cols(w[:, o[2]:o[3]], LANES)], axis=1).astype(BF16)
    return w_rkv, w_att, w_gate, w_small


def _lora_cols(x):
    return jnp.concatenate([_pad_cols(x[:, 6144:6240], 128), _pad_cols(x[:, 6240:6336], 128), x[:, 6336:6592]], axis=1)


def _pack_rwkv(rw_mu, rw_w0, rw_w2, rw_a0, rw_a2, rw_g2, rw_k_k, rw_k_a, rw_r_k, rw_ln_w, rw_ln_b):
    flat = lambda t: t.reshape(1, RW_WIDTH)
    mu = rw_mu.reshape(1, RW_PROJ)
    rows = [mu[:, 0:2048], mu[:, 2048:4096], mu[:, 4096:6144], flat(rw_w0), flat(rw_a0), flat(rw_k_k),
            flat(rw_k_a), flat(rw_r_k), flat(rw_ln_w), flat(rw_ln_b)]
    prm = jnp.pad(jnp.concatenate(rows, axis=0), ((0, 6), (0, 0)))
    mu_l = jnp.pad(_lora_cols(mu), ((0, 7), (0, 0)))
    w2 = jnp.pad(rw_w2, ((0, 128 - W_LORA), (0, 0))).astype(BF16)
    a2 = jnp.pad(rw_a2, ((0, 128 - A_LORA), (0, 0))).astype(BF16)
    g2 = rw_g2.astype(BF16)
    return prm, mu_l, w2, a2, g2


def _head_indicators(width):
    lane = np.arange(width)[:, None] // RW_HEAD_DIM
    ind = (lane == np.arange(128)[None, :]).astype(np.float32)
    return jnp.asarray(ind), jnp.asarray(ind.T)


def _head_selectors():
    sel = np.zeros((WKV_HQ, WKV_W, RW_HEAD_DIM), np.float32)
    for j in range(WKV_HQ):
        sel[j, j * RW_HEAD_DIM + np.arange(RW_HEAD_DIM), np.arange(RW_HEAD_DIM)] = 1.0
    return jnp.asarray(sel)


def kernel(x_prompt, x_sample, mem_prompt, cache_k, cache_v, cache_idx_k, page_table, state_wkv, state_shift, cache_mem_k, cache_mem_v, w_in, rw_mu, rw_w0, rw_w2, rw_a0, rw_a2, rw_g2, rw_k_k, rw_k_a, rw_r_k, rw_ln_w, rw_ln_b, idx_ln_w, idx_ln_b, w_branch_a, w_branch_b, w_out, ln1_w, ln1_b, w_mem_q, w_mem_k, w_mem_v, w_mem_o, ln2_w, ln2_b, w_router_grp, b_router_grp, w_router_exp, b_router_exp, w_exp_gate, w_exp_up, w_exp_down, ln3_w, ln3_b):
    B, S, D = x_prompt.shape
    DB, DS, _ = x_sample.shape
    assert DS == 1 and cache_k.shape[0] == 1
    TP = B * S
    T = TP + DB
    MP = _round_up(T, 640)
    past = page_table.shape[1] * PAGE_SIZE
    n_mem = mem_prompt.shape[1]
    row1 = lambda a: a.reshape(1, -1)

    def pad_rows(a):
        return jnp.concatenate([a, jnp.zeros((MP - a.shape[0],) + a.shape[1:], a.dtype)], axis=0)

    x_all = pad_rows(jnp.concatenate([x_prompt.reshape(TP, D), x_sample.reshape(DB, D)], axis=0))
    xb = x_all.astype(BF16)
    w_rkv, w_att, w_gate, w_small = _split_w_in(w_in[0])
    z_rkv = _mm(xb, w_rkv, 640, 1024, D, name="in_proj_rkv")
    z_att = _mm(xb, w_att, 640, 1024, D, name="in_proj_att")
    z_gate = _mm(xb, w_gate, 640, 1024, D, name="in_proj_gate")
    z_small = _mm(xb, w_small, 640, S_TOTAL, D, name="in_proj_small")

    prm, mu_l, w2, a2, g2 = _pack_rwkv(rw_mu[0], rw_w0[0], rw_w2[0], rw_a0[0], rw_a2[0], rw_g2[0], rw_k_k[0],
                                       rw_k_a[0], rw_r_k[0], rw_ln_w[0], rw_ln_b[0])
    rw_p, wkv_p = _wkv_prompt(z_rkv, z_small, prm, mu_l, w2, a2, g2, _head_selectors(), B, S)
    ss = state_shift[0]
    ind_f, indt_f = _head_indicators(RW_WIDTH)
    tok = _wkv_tokens_sample(z_rkv, z_small, ss[:, 0:2048], ss[:, 2048:4096], ss[:, 4096:6144], _lora_cols(ss), prm,
                             mu_l, w2, a2, g2, ind_f, indt_f, TP, DB)
    t_r, t_w, t_al, t_be, t_km, t_v, t_g, t_bo = tok
    rowv = lambda a: a.reshape(DB, RW_HEADS, 1, RW_HEAD_DIM)
    colv = lambda a: a.reshape(DB, RW_HEADS, RW_HEAD_DIM, 1)
    y_col, wkv_s = _wkv_step(state_wkv[0], rowv(t_w), rowv(t_al), rowv(t_be), rowv(t_km), rowv(t_r), colv(t_v),
                             colv(t_g), colv(t_bo), rw_ln_w[0].reshape(1, RW_HEADS, RW_HEAD_DIM, 1),
                             rw_ln_b[0].reshape(1, RW_HEADS, RW_HEAD_DIM, 1))
    rw_all = pad_rows(jnp.concatenate([rw_p, y_col.reshape(DB, RW_WIDTH).astype(BF16)], axis=0))

    pos = jnp.concatenate([jnp.tile(jnp.arange(S, dtype=I32), B), jnp.full((MP - TP,), past, I32)])
    tab_a = _rope_tables(pos, ROT_DIM, HEAD_DIM)
    tab_i = _rope_tables(pos, IDX_ROT_DIM, IDX_DIM)
    qb, k_rot, kb, vb, qib, kiw, kid, vbt = _prep(z_att, z_small, tab_a, tab_i, _pad_cols(row1(idx_ln_w[0]), LANES),
                                                  _pad_cols(row1(idx_ln_b[0]), LANES), 128)
    att_p = _dsa_prompt(qb, kb, vbt, qib, kid, kiw, B, S, min(TOPK_MAX, S // 4))
    qi_s = qib[TP:T].reshape(DB, IDX_HEADS, IDX_DIM)
    wi_s = kiw[TP:T, IDX_DIM:IDX_DIM + IDX_HEADS].reshape(DB, IDX_HEADS, 1)
    topk_s = min(TOPK_MAX, (past + DS) // 4)
    pos_sel, mself = _sel_sample(page_table, qi_s, wi_s, kid[TP:T, 0:IDX_DIM].reshape(DB, 1, IDX_DIM),
                                 jnp.swapaxes(cache_idx_k[0], 1, 2), topk_s)
    expand = lambda a: jnp.repeat(a[TP:T].reshape(DB, N_KV_HEADS, HEAD_DIM), GROUP, axis=1)
    att_s = _att_sel(page_table, pos_sel.reshape(DB, topk_s), qb[TP:T].reshape(DB, N_HEADS, HEAD_DIM), mself,
                     expand(kb), expand(vb), cache_k[0], cache_v[0])
    att_all = pad_rows(jnp.concatenate([att_p, att_s.reshape(DB, ATT_WIDTH)], axis=0))

    merged = _branch_merge(rw_all, att_all, w_branch_a[0].astype(BF16), w_branch_b[0].astype(BF16), z_gate, 640, 1024)
    x1, x1b = _mm_ln(merged, w_out[0].astype(BF16), x_all, row1(ln1_w[0]), row1(ln1_b[0]), 320, 512, name="out_ln1")

    mq = _mm(x1b, w_mem_q[0].astype(BF16), 640, MEM_WIDTH, D, out_dtype=BF16, name="mem_q")
    mem2d = mem_prompt.reshape(B * n_mem, D).astype(BF16)
    mem_k = _mm(mem2d, w_mem_k[0].astype(BF16), B * n_mem, MEM_WIDTH, D, name="mem_k")
    mem_v = _mm(mem2d, w_mem_v[0].astype(BF16), B * n_mem, MEM_WIDTH, D, name="mem_v")
    ma_p = _mem_att_prompt(mq, mem_k.astype(BF16), mem_v.astype(BF16), B, S, 512)
    ma_s = _mem_att_sample(mq[TP:T].reshape(DB, 1, MEM_WIDTH), cache_mem_k[0], cache_mem_v[0])
    ma_all = pad_rows(jnp.concatenate([ma_p, ma_s.reshape(DB, MEM_WIDTH)], axis=0))
    x2, _ = _mm_ln(ma_all, w_mem_o[0].astype(BF16), x1, row1(ln2_w[0]), row1(ln2_b[0]), 320, 512, name="mem_o_ln2")

    w_r = _pad_cols(jnp.concatenate([w_router_grp[0], w_router_exp[0]], axis=1), LANES)
    b_r = _pad_cols(row1(jnp.concatenate([b_router_grp[0], b_router_exp[0]])), LANES)
    e_idx, e_w = _router(x2, w_r, b_r, 640, 1024)
    n_assign = 2 * T
    flat_e = e_idx[:T, 0:2].reshape(n_assign)
    order = jnp.argsort(flat_e).astype(I32)
    rank = jnp.argsort(order).astype(I32)
    experts = jnp.arange(N_EXPERTS, dtype=I32)
    onehot = flat_e[:, None] == experts[None, :]
    counts = jnp.sum(onehot, axis=0, dtype=I32)
    padded = (counts + MOE_BR - 1) // MOE_BR * MOE_BR
    pad_end = jnp.cumsum(padded)
    pad_start = pad_end - padded
    start = jnp.cumsum(counts) - counts
    slot = (rank + jnp.sum(jnp.where(onehot, (pad_start - start)[None, :], 0), axis=1)).reshape(T, 2)
    n_blocks = _round_up(-(-n_assign // MOE_BR) + N_EXPERTS, MOE_GB)
    blk_row0 = jnp.arange(n_blocks, dtype=I32) * MOE_BR
    block_e = jnp.minimum(jnp.sum(pad_end[None, :] <= blk_row0[:, None], axis=1, dtype=I32), N_EXPERTS - 1)
    blk_hot = block_e[:, None] == experts[None, :]
    pick = lambda tab: jnp.sum(jnp.where(blk_hot, tab[None, :], 0), axis=1)
    j_in_e = (blk_row0 - pick(pad_start))[:, None] + jnp.arange(MOE_BR, dtype=I32)[None, :]
    src = jnp.clip(pick(start)[:, None] + j_in_e, 0, n_assign - 1)
    row_token = jnp.where(j_in_e < pick(counts)[:, None], order[src] // 2, 0).reshape(n_blocks * MOE_BR)
    n_used = (pad_end[-1] // MOE_BR).astype(I32).reshape(1)
    g_rows = MOE_GB * MOE_BR
    xs = _gather_rows(x2, row_token, -(-n_used // MOE_GB), g_rows)
    hid = _expert_up(block_e, n_used, xs, w_exp_gate[0], w_exp_up[0], D_EXPERT)
    y_rows = _expert_down(block_e, n_used, hid, w_exp_down[0])
    slot_pad = jnp.concatenate([jnp.pad(slot[:, 0], (0, MP - T)), jnp.pad(slot[:, 1], (0, MP - T))])
    y_tok = _gather_rows(y_rows, slot_pad, jnp.full((1,), 2 * MP // g_rows, I32), g_rows)
    y_p, y_s = _combine_ln(x2, y_tok, e_w, row1(ln3_w[0]), row1(ln3_b[0]), 128, TP)

    kv5 = lambda a, n, s: a.reshape(1, n, s, N_KV_HEADS, HEAD_DIM)
    va = z_att[:, A_VA:A_VA + KV_WIDTH]
    ki = kiw[:, 0:IDX_DIM]
    last = lambda a: jnp.concatenate([a[(b + 1) * S - 1:(b + 1) * S] for b in range(B)] + [a[TP:T]], axis=0)
    zl, zsl = last(z_rkv), last(z_small)
    shift_cols = jnp.concatenate([zl, zsl[:, S_LORA:S_LORA + W_LORA], zsl[:, S_LORA + 128:S_LORA + 128 + A_LORA],
                                  zsl[:, S_LORA + 256:S_LORA + 512]], axis=1)
    mem5 = lambda a: a.reshape(1, B, n_mem, MEM_HEADS, MEM_HEAD_DIM)
    return (y_p.reshape(B, S, D), y_s[:DB].reshape(DB, DS, D),
            kv5(k_rot[:TP], B, S), kv5(va[:TP], B, S), ki[:TP].reshape(1, B, S, IDX_DIM),
            wkv_p[None], shift_cols[:B][None], mem5(mem_k), mem5(mem_v),
            kv5(k_rot[TP:T], DB, DS), kv5(va[TP:T], DB, DS), ki[TP:T].reshape(1, DB, DS, IDX_DIM),
            wkv_s[None], shift_cols[B:][None])
```

```python
import functools
import math

import jax
import jax.numpy as jnp
import numpy as np
from jax import lax
from jax.experimental import pallas as pl
from jax.experimental.pallas import tpu as pltpu

F32 = jnp.float32
BF16 = jnp.bfloat16
I32 = jnp.int32
HIGHEST = lax.Precision.HIGHEST

D_MODEL = 4096
RW_HEAD_DIM = 64
RW_HEADS = 32
RW_WIDTH = 2048
W_LORA = 96
A_LORA = 96
G_LORA = 256
RW_PROJ = 3 * RW_WIDTH + W_LORA + A_LORA + G_LORA
RW_GN_EPS = 64e-5
HEAD_DIM = 128
N_HEADS = 16
N_KV_HEADS = 4
GROUP = 4
ATT_WIDTH = 2048
KV_WIDTH = 512
ROT_DIM = 32
ROPE_THETA = 500000.0
IDX_HEADS = 16
IDX_DIM = 64
IDX_ROT_DIM = 16
TOPK_MAX = 256
PAGE_SIZE = 128
MEM_HEADS = 4
MEM_HEAD_DIM = 128
MEM_WIDTH = 512
N_GROUPS = 8
EXPERTS_PER_GROUP = 8
N_EXPERTS = 64
D_EXPERT = 512
LN_EPS = 1e-5
DEEPNORM_ALPHA = 2.0 ** 0.25
EXP_M05 = math.exp(-0.5)

LANES = 128
SUBLANES = 8
VMEM_LIMIT = 56 * 1024 * 1024

C_R, C_K, C_V = 0, 2048, 4096
A_Q, A_KA, A_VA, A_IQ = 0, 2048, 2560, 3072
G_A, G_B = 0, 4096
S_LORA, S_IKW, S_TOTAL = 0, 512, 640

INT_MIN = -(2 ** 31)


def _round_up(n, m):
    return -(-n // m) * m


def _cparams(sem):
    return pltpu.CompilerParams(dimension_semantics=sem, vmem_limit_bytes=VMEM_LIMIT)


def _dot(a, b, precision=None):
    return jnp.dot(a, b, preferred_element_type=F32, precision=precision)


def _dot_nt(a, b, precision=None):
    return lax.dot_general(a, b, (((1,), (1,)), ((), ())), preferred_element_type=F32, precision=precision)


def _sigmoid(x):
    return 1.0 / (1.0 + jnp.exp(-x))


def _mm_body(x_ref, w_ref, o_ref, acc_ref):
    k = pl.program_id(2)

    @pl.when(k == 0)
    def _():
        acc_ref[...] = jnp.zeros_like(acc_ref)

    acc_ref[...] += _dot(x_ref[...], w_ref[...])

    @pl.when(k == pl.num_programs(2) - 1)
    def _():
        o_ref[...] = acc_ref[...].astype(o_ref.dtype)


def _mm_fullk_body(x_ref, w_ref, o_ref):
    o_ref[...] = _dot(x_ref[...], w_ref[...]).astype(o_ref.dtype)


def _mm(x, w, tm, tn, tk, out_dtype=F32, name="mm"):
    m, kd = x.shape
    n = w.shape[1]
    if tk == kd:
        return pl.pallas_call(
            _mm_fullk_body,
            grid=(m // tm, n // tn),
            in_specs=[pl.BlockSpec((tm, kd), lambda i, j: (i, 0)),
                      pl.BlockSpec((kd, tn), lambda i, j: (0, j))],
            out_specs=pl.BlockSpec((tm, tn), lambda i, j: (i, j)),
            out_shape=jax.ShapeDtypeStruct((m, n), out_dtype),
            compiler_params=_cparams(("parallel", "parallel")),
            name=name,
        )(x, w)
    return pl.pallas_call(
        _mm_body,
        grid=(m // tm, n // tn, kd // tk),
        in_specs=[pl.BlockSpec((tm, tk), lambda i, j, k: (i, k)),
                  pl.BlockSpec((tk, tn), lambda i, j, k: (k, j))],
        out_specs=pl.BlockSpec((tm, tn), lambda i, j, k: (i, j)),
        out_shape=jax.ShapeDtypeStruct((m, n), out_dtype),
        scratch_shapes=[pltpu.VMEM((tm, tn), F32)],
        compiler_params=_cparams(("parallel", "parallel", "arbitrary")),
        name=name,
    )(x, w)


def _layer_norm_rows(x, g, b):
    mu = jnp.mean(x, axis=-1, keepdims=True)
    d = x - mu
    var = jnp.mean(d * d, axis=-1, keepdims=True)
    return d * lax.rsqrt(var + LN_EPS) * g + b


def _mm_ln_body(x_ref, w_ref, res_ref, g_ref, b_ref, o_ref, ob_ref, y_ref, *, tn):
    j = pl.program_id(1)
    y_ref[:, pl.ds(pl.multiple_of(j * tn, tn), tn)] = _dot(x_ref[...], w_ref[...])

    @pl.when(j == pl.num_programs(1) - 1)
    def _():
        y = _layer_norm_rows(DEEPNORM_ALPHA * res_ref[...] + y_ref[...], g_ref[...], b_ref[...])
        o_ref[...] = y
        ob_ref[...] = y.astype(BF16)


def _mm_ln(x, w, res, g, b, tm, tn, name="mm_ln"):
    m, kd = x.shape
    n = w.shape[1]
    return pl.pallas_call(
        functools.partial(_mm_ln_body, tn=tn),
        grid=(m // tm, n // tn),
        in_specs=[pl.BlockSpec((tm, kd), lambda i, j: (i, 0)),
                  pl.BlockSpec((kd, tn), lambda i, j: (0, j)),
                  pl.BlockSpec((tm, n), lambda i, j: (i, 0)),
                  pl.BlockSpec((1, n), lambda i, j: (0, 0)),
                  pl.BlockSpec((1, n), lambda i, j: (0, 0))],
        out_specs=[pl.BlockSpec((tm, n), lambda i, j: (i, 0)),
                   pl.BlockSpec((tm, n), lambda i, j: (i, 0))],
        out_shape=[jax.ShapeDtypeStruct((m, n), F32), jax.ShapeDtypeStruct((m, n), BF16)],
        scratch_shapes=[pltpu.VMEM((tm, n), F32)],
        compiler_params=_cparams(("parallel", "arbitrary")),
        name=name,
    )(x, w, res, g, b)


def _branch_merge_body(rw_ref, at_ref, wa_ref, wb_ref, ga_ref, gb_ref, o_ref):
    a = _dot(rw_ref[...], wa_ref[...])
    b = _dot(at_ref[...], wb_ref[...])
    o_ref[...] = (_sigmoid(ga_ref[...]) * a + _sigmoid(gb_ref[...]) * b).astype(o_ref.dtype)


def _branch_merge(rw, att, wa, wb, z, tm, tn):
    m = rw.shape[0]
    n = wa.shape[1]
    ga0, gb0 = G_A // tn, G_B // tn
    return pl.pallas_call(
        _branch_merge_body,
        grid=(m // tm, n // tn),
        in_specs=[pl.BlockSpec((tm, RW_WIDTH), lambda i, j: (i, 0)),
                  pl.BlockSpec((tm, ATT_WIDTH), lambda i, j: (i, 0)),
                  pl.BlockSpec((RW_WIDTH, tn), lambda i, j: (0, j)),
                  pl.BlockSpec((ATT_WIDTH, tn), lambda i, j: (0, j)),
                  pl.BlockSpec((tm, tn), lambda i, j: (i, ga0 + j)),
                  pl.BlockSpec((tm, tn), lambda i, j: (i, gb0 + j))],
        out_specs=pl.BlockSpec((tm, tn), lambda i, j: (i, j)),
        out_shape=jax.ShapeDtypeStruct((m, n), BF16),
        compiler_params=_cparams(("parallel", "parallel")),
        name="branch_merge",
    )(rw, att, wa, wb, z, z)


def _seg_sum(x, ind, ind_t):
    return _dot(_dot(x, ind, HIGHEST), ind_t, HIGHEST)


def _split_bf16(x, parts):
    out = []
    for _ in range(parts):
        t = x.astype(BF16)
        out.append(t)
        x = x - t.astype(F32)
    return out


def _seg_sum_quads(x, bd):
    outs = []
    for q in range(x.shape[1] // WKV_W):
        hi, lo = _split_bf16(x[:, q * WKV_W:(q + 1) * WKV_W], 2)
        outs.append(_dot(hi, bd) + _dot(lo, bd))
    return jnp.concatenate(outs, axis=1) if len(outs) > 1 else outs[0]


def _rwkv_tokens(zr, zk, zv, zl, pr, pk, pv, plo, prm, mu_l, w2, a2, g2, seg):
    r = zr + (pr - zr) * prm[0:1]
    kx = zk + (pk - zk) * prm[1:2]
    v = zv + (pv - zv) * prm[2:3]
    zsl = zl + (plo - zl) * mu_l
    tw = jnp.tanh(zsl[:, 0:128]).astype(BF16)
    xw = prm[3:4] + _dot(tw, w2)
    lw = -EXP_M05 * _sigmoid(xw)
    a = _sigmoid(prm[4:5] + _dot(zsl[:, 128:256].astype(BF16), a2))
    g = _dot(_sigmoid(zsl[:, 256:512]).astype(BF16), g2)
    kk = kx * prm[5:6]
    n2 = seg(kk * kk)
    kkn = kk / jnp.maximum(jnp.sqrt(n2), 1e-12)
    kmod = kx * (1.0 + (a - 1.0) * prm[6:7])
    return r, lw, kmod, v, kkn, a, g


def _rwkv_post(y, r, kmod, v, g, prm, seg):
    inv_n = 1.0 / RW_HEAD_DIM
    mean = seg(y) * inv_n
    d = y - mean
    var = seg(d * d) * inv_n
    yn = d * lax.rsqrt(var + RW_GN_EPS) * prm[8:9] + prm[9:10]
    bonus = seg(r * kmod * prm[7:8]) * v
    return (yn + bonus) * g


WKV_C = 64
WKV_HQ = 4
WKV_W = WKV_HQ * RW_HEAD_DIM
WKV_QPS = 8


def _wkv_chunk_body(zr_ref, zk_ref, zv_ref, zl_ref, prm_ref, mul_ref, w2_ref, a2_ref, g2_ref,
                    sel_ref, o_ref, so_ref, s_ref, cr_ref, ck_ref, cv_ref, cl_ref):
    c = pl.program_id(2)
    C = WKV_C
    W = WKV_W

    @pl.when(c == 0)
    def _():
        s_ref[...] = jnp.zeros_like(s_ref)
        cr_ref[...] = jnp.zeros_like(cr_ref)
        ck_ref[...] = jnp.zeros_like(ck_ref)
        cv_ref[...] = jnp.zeros_like(cv_ref)
        cl_ref[...] = jnp.zeros_like(cl_ref)

    rows = lax.broadcasted_iota(I32, (C, 1), 0)

    def shifted(z, carry_ref):
        prev = jnp.where(rows == 0, carry_ref[0:1, :], pltpu.roll(z, 1, 0))
        carry_ref[0:1, :] = z[C - 1:C, :]
        return prev

    zr, zk, zv, zl = zr_ref[...], zk_ref[...], zv_ref[...], zl_ref[...]
    pr, pk, pv, plo = shifted(zr, cr_ref), shifted(zk, ck_ref), shifted(zv, cv_ref), shifted(zl, cl_ref)
    prm = prm_ref[...]
    lane_head = lax.broadcasted_iota(I32, (1, W), 1) // RW_HEAD_DIM
    hv = lax.broadcasted_iota(I32, (W, 1), 0) // RW_HEAD_DIM
    bd = (hv == lane_head).astype(BF16)
    seg = lambda x: _seg_sum_quads(x, bd)
    r, lw, kmod, v, kkn, a, g = _rwkv_tokens(zr, zk, zv, zl, pr, pk, pv, plo, prm, mul_ref[0:1, :],
                                            w2_ref[...], a2_ref[...], g2_ref[...], seg)
    al = -kkn
    be = kkn * a

    ti = lax.broadcasted_iota(I32, (C, C), 0)
    tj = lax.broadcasted_iota(I32, (C, C), 1)
    tri = (tj <= ti).astype(BF16)
    cum = sum(_dot(tri, part) for part in _split_bf16(lw, 3))
    cum_l = cum[C - 1:C, :]
    p_inv = jnp.exp(-cum)
    p_rel = jnp.exp(cum_l - cum)
    ab = al * jnp.exp(cum - lw)
    rb = r * jnp.exp(cum)
    bt = (be * p_inv).astype(BF16)
    kt = (kmod * p_inv).astype(BF16)
    bk = jnp.concatenate([be * p_rel, kmod * p_rel], axis=0).astype(BF16)
    ar = jnp.concatenate([ab, rb], axis=0)
    pc = jnp.exp(cum_l)

    n4 = WKV_HQ * C
    bi = lax.broadcasted_iota(I32, (n4, n4), 0)
    bj = lax.broadcasted_iota(I32, (n4, n4), 1)
    same = (bi // C) == (bj // C)
    tri_s4 = same & ((bj % C) < (bi % C))
    tri_i4 = same & ((bj % C) <= (bi % C))
    eye4 = (bi == bj).astype(F32)
    masks = [lane_head == j for j in range(WKV_HQ)]

    def stack(x):
        return jnp.concatenate([jnp.where(m, x, jnp.zeros_like(x)) for m in masks], axis=0)

    def block_sum(x):
        return sum(x[j * C:(j + 1) * C] for j in range(WKV_HQ))

    def bdot(a, b):
        return _dot(a.astype(BF16), b.astype(BF16))

    qs = []
    for q in range(WKV_QPS):
        sl = slice(q * W, (q + 1) * W)
        v_q = v[:, sl]
        lhs = jnp.concatenate([stack(ab[:, sl]), stack(rb[:, sl])], axis=0).astype(BF16)
        abr = _dot_nt(lhs, stack(bt[:, sl]))
        akr = _dot_nt(lhs, stack(kt[:, sl]))
        qs.append(dict(sl=sl, v=v_q, v_s=stack(v_q), s0=s_ref[q],
                       x=jnp.where(tri_s4, abr[0:n4], 0.0), a_rb=jnp.where(tri_i4, abr[n4:2 * n4], 0.0),
                       a_ak=jnp.where(tri_s4, akr[0:n4], 0.0), a_rk=jnp.where(tri_i4, akr[n4:2 * n4], 0.0)))
    for d in qs:
        d['pw'] = [d['x']]
    for _ in range(5):
        for d in qs:
            d['pw'].append(bdot(d['pw'][-1], d['pw'][-1]))
    for d in qs:
        pw = d['pw']
        pr_ = [eye4 + pw[2 * i] + pw[2 * i + 1] + bdot(pw[2 * i], pw[2 * i + 1]) for i in range(3)]
        d['t'] = bdot(bdot(pr_[0], pr_[1]), pr_[2])
    for d in qs:
        gs = _dot_nt(ar[:, d['sl']].astype(BF16), d['s0'].astype(BF16))
        d['g_r'] = gs[C:2 * C]
        d['w_s'] = stack(gs[0:C]) + bdot(d['a_ak'], d['v_s'])
    for d in qs:
        d['u_s'] = bdot(d['t'], d['w_s'])
    ys = []
    for q, d in enumerate(qs):
        yv = bdot(jnp.concatenate([d['a_rb'], d['a_rk']], axis=1), jnp.concatenate([d['u_s'], d['v_s']], axis=0))
        ys.append(d['g_r'] + block_sum(yv))
        uv_t = jnp.concatenate([block_sum(d['u_s']), d['v']], axis=0).T.astype(BF16)
        upd = _dot(uv_t, bk[:, d['sl']])
        s_ref[q] = d['s0'] * pc[:, d['sl']] + jnp.where(hv == lane_head, upd, 0.0)

    y = jnp.concatenate(ys, axis=1) if WKV_QPS > 1 else ys[0]
    o_ref[...] = _rwkv_post(y, r, kmod, v, g, prm, seg).astype(o_ref.dtype)

    @pl.when(c == pl.num_programs(2) - 1)
    def _():
        for q in range(WKV_QPS):
            for j in range(WKV_HQ):
                rows_j = s_ref[q, j * RW_HEAD_DIM:(j + 1) * RW_HEAD_DIM, :]
                so_ref[0, q * WKV_HQ + j] = _dot(rows_j, sel_ref[j], HIGHEST)


def _wkv_prompt(z, zs, prm, mu_l, w2, a2, g2, sel, batch, seq):
    nc = seq // WKV_C
    WS = WKV_W * WKV_QPS
    nq = RW_WIDTH // WS
    row = lambda b, q, c: b * nc + c
    return pl.pallas_call(
        _wkv_chunk_body,
        grid=(batch, nq, nc),
        in_specs=[pl.BlockSpec((WKV_C, WS), lambda b, q, c: (row(b, q, c), C_R // WS + q)),
                  pl.BlockSpec((WKV_C, WS), lambda b, q, c: (row(b, q, c), C_K // WS + q)),
                  pl.BlockSpec((WKV_C, WS), lambda b, q, c: (row(b, q, c), C_V // WS + q)),
                  pl.BlockSpec((WKV_C, 512), lambda b, q, c: (row(b, q, c), S_LORA // 512)),
                  pl.BlockSpec((16, WS), lambda b, q, c: (0, q)),
                  pl.BlockSpec((8, 512), lambda b, q, c: (0, 0)),
                  pl.BlockSpec((128, WS), lambda b, q, c: (0, q)),
                  pl.BlockSpec((128, WS), lambda b, q, c: (0, q)),
                  pl.BlockSpec((256, WS), lambda b, q, c: (0, q)),
                  pl.BlockSpec((WKV_HQ, WKV_W, RW_HEAD_DIM), lambda b, q, c: (0, 0, 0))],
        out_specs=[pl.BlockSpec((WKV_C, WS), lambda b, q, c: (row(b, q, c), q)),
                   pl.BlockSpec((1, WKV_HQ * WKV_QPS, RW_HEAD_DIM, RW_HEAD_DIM), lambda b, q, c: (b, q, 0, 0))],
        out_shape=[jax.ShapeDtypeStruct((batch * seq, RW_WIDTH), BF16),
                   jax.ShapeDtypeStruct((batch, RW_HEADS, RW_HEAD_DIM, RW_HEAD_DIM), F32)],
        scratch_shapes=[pltpu.VMEM((WKV_QPS, WKV_W, WKV_W), F32), pltpu.VMEM((8, WS), F32), pltpu.VMEM((8, WS), F32),
                        pltpu.VMEM((8, WS), F32), pltpu.VMEM((8, 512), F32)],
        compiler_params=_cparams(("parallel", "parallel", "arbitrary")),
        name="wkv_prompt",
    )(z, z, z, zs, prm, mu_l, w2, a2, g2, sel)


def _wkv_tok_body(zr_ref, zk_ref, zv_ref, zl_ref, pr_ref, pk_ref, pv_ref, pl_ref, prm_ref, mul_ref, w2_ref, a2_ref,
                  g2_ref, ind_ref, indt_ref, r_ref, w_ref, al_ref, be_ref, km_ref, v_ref, g_ref, bo_ref):
    prm = prm_ref[...]
    ind, ind_t = ind_ref[...], indt_ref[...]
    seg = lambda x: _seg_sum(x, ind, ind_t)
    r, lw, kmod, v, kkn, a, g = _rwkv_tokens(zr_ref[...], zk_ref[...], zv_ref[...], zl_ref[...], pr_ref[...],
                                            pk_ref[...], pv_ref[...], pl_ref[...], prm, mul_ref[0:1, :],
                                            w2_ref[...], a2_ref[...], g2_ref[...], seg)
    r_ref[...] = r
    w_ref[...] = jnp.exp(lw)
    al_ref[...] = -kkn
    be_ref[...] = kkn * a
    km_ref[...] = kmod
    v_ref[...] = v
    g_ref[...] = g
    bo_ref[...] = seg(r * kmod * prm[7:8])


def _wkv_tokens_sample(z, zs, prev_r, prev_k, prev_v, prev_l, prm, mu_l, w2, a2, g2, ind, ind_t, row0, n):
    rb = row0 // n
    full = lambda a: pl.BlockSpec(a.shape, lambda i: (0,) * a.ndim)
    zspec = lambda w, c0: pl.BlockSpec((n, w), lambda i: (rb, c0 // w))
    return pl.pallas_call(
        _wkv_tok_body,
        grid=(1,),
        in_specs=[zspec(RW_WIDTH, C_R), zspec(RW_WIDTH, C_K), zspec(RW_WIDTH, C_V), zspec(512, S_LORA),
                  full(prev_r), full(prev_k), full(prev_v), full(prev_l), full(prm), full(mu_l), full(w2), full(a2),
                  full(g2), full(ind), full(ind_t)],
        out_specs=[pl.BlockSpec((n, RW_WIDTH), lambda i: (0, 0))] * 8,
        out_shape=[jax.ShapeDtypeStruct((n, RW_WIDTH), F32)] * 8,
        compiler_params=_cparams(("arbitrary",)),
        name="wkv_tokens_sample",
    )(z, z, z, zs, prev_r, prev_k, prev_v, prev_l, prm, mu_l, w2, a2, g2, ind, ind_t)


def _wkv_step_body(s_ref, w_ref, al_ref, be_ref, km_ref, r_ref, v_ref, g_ref, bo_ref, lnw_ref, lnb_ref, o_ref, so_ref):
    s = s_ref[...]
    vcol = v_ref[...]
    sa = jnp.sum(s * al_ref[...], axis=-1, keepdims=True)
    s2 = s * w_ref[...] + sa * be_ref[...] + vcol * km_ref[...]
    so_ref[...] = s2
    y = jnp.sum(s2 * r_ref[...], axis=-1, keepdims=True)
    mean = jnp.mean(y, axis=2, keepdims=True)
    d = y - mean
    var = jnp.mean(d * d, axis=2, keepdims=True)
    yn = d * lax.rsqrt(var + RW_GN_EPS) * lnw_ref[...] + lnb_ref[...]
    o_ref[...] = (yn + bo_ref[...] * vcol) * g_ref[...]


def _wkv_step(state, w, al, be, km, r, v, g, bo, lnw, lnb):
    n, h = state.shape[0], state.shape[1]
    hq = 2
    rowspec = pl.BlockSpec((n, hq, 1, RW_HEAD_DIM), lambda q: (0, q, 0, 0))
    colspec = pl.BlockSpec((n, hq, RW_HEAD_DIM, 1), lambda q: (0, q, 0, 0))
    pcol = pl.BlockSpec((1, hq, RW_HEAD_DIM, 1), lambda q: (0, q, 0, 0))
    sspec = pl.BlockSpec((n, hq, RW_HEAD_DIM, RW_HEAD_DIM), lambda q: (0, q, 0, 0))
    return pl.pallas_call(
        _wkv_step_body,
        grid=(h // hq,),
        in_specs=[sspec, rowspec, rowspec, rowspec, rowspec, rowspec, colspec, colspec, colspec, pcol, pcol],
        out_specs=[colspec, sspec],
        out_shape=[jax.ShapeDtypeStruct((n, h, RW_HEAD_DIM, 1), F32), jax.ShapeDtypeStruct(state.shape, F32)],
        compiler_params=_cparams(("parallel",)),
        name="wkv_step",
    )(state, w, al, be, km, r, v, g, bo, lnw, lnb)


def _rope_tables(pos, rot_dim, period):
    half = rot_dim // 2
    t = pos.shape[0]
    inv_freq = ROPE_THETA ** (-jnp.arange(half, dtype=F32) / half)
    ang = pos.astype(F32)[:, None] * inv_freq[None, :]
    cos, sin = jnp.cos(ang), jnp.sin(ang)
    zh = jnp.zeros((t, half), F32)
    rest = period - rot_dim
    c = jnp.concatenate([cos, cos, jnp.ones((t, rest), F32)], axis=1)
    s1 = jnp.concatenate([-sin, zh, jnp.zeros((t, rest), F32)], axis=1)
    s2 = jnp.concatenate([zh, sin, jnp.zeros((t, rest), F32)], axis=1)
    rep = LANES // period
    return jnp.stack([jnp.tile(a, (1, rep)) for a in (c, s1, s2)], axis=0)


def _rope(x, tab, half):
    n = x.shape[1]
    rep = n // LANES
    c, s1, s2 = [jnp.tile(tab[i], (1, rep)) if rep > 1 else tab[i] for i in range(3)]
    return x * c + pltpu.roll(x, n - half, 1) * s1 + pltpu.roll(x, half, 1) * s2


def _prep_body(q_ref, ka_ref, va_ref, iq_ref, ikw_ref, ta_ref, ti_ref, lnw_ref, lnb_ref,
               qo_ref, ko_ref, kb_ref, vb_ref, qio_ref, kio_ref, kid_ref, vbt_ref):
    ta = ta_ref[...]
    ti = ti_ref[...]
    qo_ref[...] = (_rope(q_ref[...], ta, ROT_DIM // 2) * (HEAD_DIM ** -0.5)).astype(BF16)
    k = _rope(ka_ref[...], ta, ROT_DIM // 2)
    ko_ref[...] = k
    kb_ref[...] = k.astype(BF16)
    vb_ref[...] = va_ref[...].astype(BF16)
    vbt_ref[...] = va_ref[...].T.astype(BF16)
    qio_ref[...] = _rope(iq_ref[...], ti, IDX_ROT_DIM // 2).astype(BF16)
    x = ikw_ref[...]
    lane = lax.broadcasted_iota(I32, x.shape, 1)
    is_k = lane < IDX_DIM
    mu = jnp.sum(jnp.where(is_k, x, 0.0), axis=-1, keepdims=True) * (1.0 / IDX_DIM)
    d = jnp.where(is_k, x - mu, 0.0)
    var = jnp.sum(d * d, axis=-1, keepdims=True) * (1.0 / IDX_DIM)
    kn = d * lax.rsqrt(var + LN_EPS) * lnw_ref[...] + lnb_ref[...]
    kr = _rope(kn, ti, IDX_ROT_DIM // 2)
    kr = jnp.where(is_k, kr, 0.0)
    kio_ref[...] = jnp.where(is_k, kr, x * (IDX_HEADS ** -0.5))
    kid_ref[...] = (kr + pltpu.roll(kr, IDX_DIM, 1)).astype(BF16)


def _prep(z, zs, tab_a, tab_i, ln_w, ln_b, tm):
    m = z.shape[0]
    row = lambda w, c0: pl.BlockSpec((tm, w), lambda i: (i, c0 // w))
    outs = [(ATT_WIDTH, BF16), (KV_WIDTH, F32), (KV_WIDTH, BF16), (KV_WIDTH, BF16),
            (IDX_HEADS * IDX_DIM, BF16), (LANES, F32), (LANES, BF16)]
    return pl.pallas_call(
        _prep_body,
        grid=(m // tm,),
        in_specs=[row(ATT_WIDTH, A_Q), row(KV_WIDTH, A_KA), row(KV_WIDTH, A_VA), row(IDX_HEADS * IDX_DIM, A_IQ),
                  row(LANES, S_IKW),
                  pl.BlockSpec((3, tm, LANES), lambda i: (0, i, 0)),
                  pl.BlockSpec((3, tm, LANES), lambda i: (0, i, 0)),
                  pl.BlockSpec((1, LANES), lambda i: (0, 0)),
                  pl.BlockSpec((1, LANES), lambda i: (0, 0))],
        out_specs=[pl.BlockSpec((tm, w), lambda i: (i, 0)) for w, _ in outs]
        + [pl.BlockSpec((KV_WIDTH, tm), lambda i: (0, i))],
        out_shape=[jax.ShapeDtypeStruct((m, w), dt) for w, dt in outs] + [jax.ShapeDtypeStruct((KV_WIDTH, m), BF16)],
        compiler_params=_cparams(("parallel",)),
        name="prep",
    )(z, z, z, z, zs, tab_a, tab_i, ln_w, ln_b)


DSA_QB = 128
DSA_TK = 512
DSA_ATK = 512


def _float_key(s):
    b = pltpu.bitcast(s, I32)
    return b ^ ((b >> 31) & 0x7FFFFFFF)


def _fold_rows(x, op):
    n = x.shape[0]
    while n > SUBLANES:
        n //= 2
        x = op(x[0:n], x[n:2 * n])
    return x


def _kth_threshold(count_ge, topk):
    def step(b, thr):
        cand = thr + jnp.left_shift(jnp.int32(1), 31 - b)
        return jnp.where(count_ge(cand) >= topk, cand, thr)
    return lax.fori_loop(0, 32, step, jnp.full((1, DSA_QB), INT_MIN, I32))


def _dsa_prompt_body(q_ref, kb_ref, vbt_ref, qi_ref, kid_ref, kiw_ref, o_ref, keys_ref, jcut_ref, acc_ref, qs_ref,
                     s_ref, p_ref, *, topk, seq):
    i = pl.program_id(1)
    nt = (i * DSA_QB + DSA_QB + DSA_TK - 1) // DSA_TK
    qpos = i * DSA_QB + lax.broadcasted_iota(I32, (1, DSA_QB), 1)
    row0 = lax.broadcasted_iota(I32, (DSA_TK, 1), 0)
    lane = lax.broadcasted_iota(I32, (1, LANES), 1)
    w_t = (kiw_ref[...] * (IDX_DIM ** -0.5)).T

    def score_tile(t, carry):
        kd = kid_ref[pl.ds(t * DSA_TK, DSA_TK), :]
        s = jnp.zeros((DSA_TK, DSA_QB), F32)
        for h in range(IDX_HEADS):
            qt = qi_ref[:, (h // 2) * LANES:(h // 2 + 1) * LANES]
            qh = jnp.where((lane // IDX_DIM) == (h % 2), qt, jnp.zeros_like(qt))
            s = s + w_t[IDX_DIM + h:IDX_DIM + h + 1, :] * jnp.maximum(_dot_nt(kd, qh), 0.0)
        valid = (t * DSA_TK + row0) <= qpos
        keys_ref[pl.ds(t * DSA_TK, DSA_TK), :] = jnp.where(valid, _float_key(s), INT_MIN)
        return carry

    lax.fori_loop(0, nt, score_tile, 0)

    def count(pred):
        def body(t, acc):
            kt = keys_ref[pl.ds(t * DSA_TK, DSA_TK), :]
            hit = pred(kt, t * DSA_TK + row0).astype(I32)
            return acc + _fold_rows(hit, jnp.add)
        acc = lax.fori_loop(0, nt, body, jnp.zeros((SUBLANES, DSA_QB), I32))
        return jnp.sum(acc, axis=0, keepdims=True)

    thr = _kth_threshold(lambda c: count(lambda kt, col: kt >= c), topk)
    n_gt = count(lambda kt, col: kt > thr)
    n_eq = count(lambda kt, col: (kt == thr) & (col <= qpos))
    need = topk - n_gt
    jcut_ref[...] = jnp.full(jcut_ref.shape, seq, I32)
    excess = (n_eq > need) & (thr > INT_MIN)

    @pl.when(jnp.max(excess.astype(I32)) > 0)
    def _():
        def step(b, jm):
            cand = jm + jnp.left_shift(jnp.int32(1), 30 - b)
            c = count(lambda kt, col: (kt == thr) & (col <= qpos) & (col < cand))
            return jnp.where(c < need, cand, jm)
        jm = lax.fori_loop(0, 31, step, jnp.zeros((1, DSA_QB), I32))
        jcut_ref[...] = jnp.broadcast_to(jnp.where(excess, jm, seq), jcut_ref.shape)

    jcut = jcut_ref[0:1, :]

    acc_ref[...] = jnp.zeros_like(acc_ref)
    for h in range(N_HEADS):
        qs_ref[h // GROUP, (h % GROUP) * DSA_QB:(h % GROUP + 1) * DSA_QB, :] = q_ref[:, h * HEAD_DIM:(h + 1) * HEAD_DIM]
    nta = (i * DSA_QB + DSA_QB + DSA_ATK - 1) // DSA_ATK
    rowa0 = lax.broadcasted_iota(I32, (DSA_ATK, 1), 0)

    def att_tile(t, carry):
        ms, ls = carry
        k0 = pl.multiple_of(t * DSA_ATK, DSA_ATK)
        kt = keys_ref[pl.ds(k0, DSA_ATK), :]
        kpos = k0 + rowa0
        sel = ((kt > thr) | ((kt == thr) & (kpos <= jcut))) & (kpos <= qpos)
        sel4 = jnp.concatenate([sel] * GROUP, axis=1)
        for g in range(N_KV_HEADS):
            k_t = kb_ref[pl.ds(k0, DSA_ATK), g * HEAD_DIM:(g + 1) * HEAD_DIM]
            s_ref[g] = _dot_nt(k_t, qs_ref[g])
        new_m, new_l, corrs = [], [], []
        for g in range(N_KV_HEADS):
            s = jnp.where(sel4, s_ref[g], -1e30)
            m_new = jnp.maximum(ms[g], jnp.max(_fold_rows(s, jnp.maximum), axis=0, keepdims=True))
            p = jnp.exp(s - m_new)
            corr = jnp.exp(ms[g] - m_new)
            new_l.append(ls[g] * corr + jnp.sum(_fold_rows(p, jnp.add), axis=0, keepdims=True))
            p_ref[g] = p.astype(BF16)
            new_m.append(m_new)
            corrs.append(corr)
        for g in range(N_KV_HEADS):
            vt_t = vbt_ref[g * HEAD_DIM:(g + 1) * HEAD_DIM, pl.ds(k0, DSA_ATK)]
            acc_ref[g] = acc_ref[g] * corrs[g] + _dot(vt_t, p_ref[g])
        return tuple(new_m), tuple(new_l)

    wq = GROUP * DSA_QB
    init = (tuple(jnp.full((1, wq), -1e29, F32) for _ in range(N_KV_HEADS)),
            tuple(jnp.zeros((1, wq), F32) for _ in range(N_KV_HEADS)))
    ms, ls = lax.fori_loop(0, nta, att_tile, init)
    for h in range(N_HEADS):
        g, c = h // GROUP, (h % GROUP) * DSA_QB
        o_t = acc_ref[g, :, c:c + DSA_QB] / ls[g][:, c:c + DSA_QB]
        o_ref[:, h * HEAD_DIM:(h + 1) * HEAD_DIM] = o_t.T.astype(o_ref.dtype)


def _dsa_prompt(qb, kb, vbt, qib, kid, kiw, batch, seq, topk):
    nb = seq // DSA_QB
    body = functools.partial(_dsa_prompt_body, topk=topk, seq=seq)
    return pl.pallas_call(
        body,
        grid=(batch, nb),
        in_specs=[pl.BlockSpec((DSA_QB, ATT_WIDTH), lambda b, i: (b * nb + i, 0)),
                  pl.BlockSpec((seq, KV_WIDTH), lambda b, i: (b, 0)),
                  pl.BlockSpec((KV_WIDTH, seq), lambda b, i: (0, b)),
                  pl.BlockSpec((DSA_QB, IDX_HEADS * IDX_DIM), lambda b, i: (b * nb + i, 0)),
                  pl.BlockSpec((seq, LANES), lambda b, i: (b, 0)),
                  pl.BlockSpec((DSA_QB, LANES), lambda b, i: (b * nb + i, 0))],
        out_specs=pl.BlockSpec((DSA_QB, ATT_WIDTH), lambda b, i: (b * nb + i, 0)),
        out_shape=jax.ShapeDtypeStruct((batch * seq, ATT_WIDTH), BF16),
        scratch_shapes=[pltpu.VMEM((seq, DSA_QB), I32), pltpu.VMEM((SUBLANES, DSA_QB), I32),
                        pltpu.VMEM((N_KV_HEADS, HEAD_DIM, GROUP * DSA_QB), F32),
                        pltpu.VMEM((N_KV_HEADS, GROUP * DSA_QB, HEAD_DIM), BF16),
                        pltpu.VMEM((N_KV_HEADS, DSA_ATK, GROUP * DSA_QB), F32),
                        pltpu.VMEM((N_KV_HEADS, DSA_ATK, GROUP * DSA_QB), BF16)],
        compiler_params=_cparams(("parallel", "arbitrary")),
        name="dsa_prompt",
    )(qb, kb, vbt, qib, kid, kiw)


def _sel_sample_body(pt_ref, qi_ref, wi_ref, ks_ref, ck_hbm, pos_ref, ms_ref, kbuf, sc_ref, jm_ref, rk_ref, sem, *,
                     topk, npg):
    s = pl.program_id(0)
    slot = s % 2
    U = SEL_SPS

    def page_copy(step, u, p, sl):
        return pltpu.make_async_copy(ck_hbm.at[pt_ref[step * U + u, p]], kbuf.at[sl, u, p], sem.at[sl])

    def request(step, sl):
        for u in range(U):
            for p in range(npg):
                page_copy(step, u, p, sl).start()

    @pl.when(s == 0)
    def _():
        request(0, 0)

    @pl.when(s + 1 < pl.num_programs(0))
    def _():
        request(s + 1, 1 - slot)

    for u in range(U):
        for p in range(npg):
            page_copy(s, u, p, slot).wait()

    lane = lax.broadcasted_iota(I32, (1, PAGE_SIZE), 1)
    pos = lax.broadcasted_iota(I32, (npg, PAGE_SIZE), 0) * PAGE_SIZE + lane

    def total(x):
        return jnp.sum(jnp.sum(x.astype(I32), axis=1, keepdims=True), axis=0, keepdims=True)

    keys, k_self = [], []
    for u in range(U):
        qi, wi = qi_ref[u], wi_ref[u] * (IDX_DIM ** -0.5)
        for c in range(npg // SEL_CP):
            kt = jnp.concatenate([kbuf[slot, u, c * SEL_CP + r] for r in range(SEL_CP)], axis=1).astype(BF16)
            sc = jnp.sum(wi * jnp.maximum(_dot(qi, kt), 0.0), axis=0, keepdims=True)
            for r in range(SEL_CP):
                sc_ref[u, c * SEL_CP + r:c * SEL_CP + r + 1, :] = sc[:, r * PAGE_SIZE:(r + 1) * PAGE_SIZE]
        keys.append(_float_key(sc_ref[u]))
        d = jnp.sum(qi.astype(F32) * ks_ref[u].astype(F32), axis=-1, keepdims=True)
        k_self.append(_float_key(jnp.sum(wi * jnp.maximum(d, 0.0), axis=0, keepdims=True)))

    def step(b, thrs):
        out = []
        for u in range(U):
            cand = thrs[u] + jnp.left_shift(jnp.int32(1), 31 - b)
            c = total(keys[u] >= cand) + (k_self[u] >= cand).astype(I32)
            out.append(jnp.where(c >= topk, cand, thrs[u]))
        return tuple(out)

    thrs = lax.fori_loop(0, 32, step, tuple(jnp.full((1, 1), INT_MIN, I32) for _ in range(U)))

    ri = lax.broadcasted_iota(I32, (PAGE_SIZE, PAGE_SIZE), 0)
    ci = lax.broadcasted_iota(I32, (PAGE_SIZE, PAGE_SIZE), 1)
    pr_ = lax.broadcasted_iota(I32, (npg, npg), 0)
    pc_ = lax.broadcasted_iota(I32, (npg, npg), 1)
    jcol = lax.broadcasted_iota(I32, (topk, PAGE_SIZE), 0)
    lane_f = lax.broadcasted_iota(I32, (topk, PAGE_SIZE), 1).astype(F32)
    ones8 = jnp.ones((SUBLANES, PAGE_SIZE), BF16)
    for u in range(U):
        thr = thrs[u]
        need = topk - total(keys[u] > thr) - (k_self[u] > thr).astype(I32)
        eq = keys[u] == thr
        jm_ref[u] = jnp.full((SUBLANES, LANES), npg * PAGE_SIZE, I32)

        @pl.when(jnp.max((total(eq) > need).astype(I32)) > 0)
        def _():
            def jstep(b, jm):
                cand = jm + jnp.left_shift(jnp.int32(1), 30 - b)
                return jnp.where(total(eq & (pos < cand)) < need, cand, jm)
            jm_ref[u] = jnp.broadcast_to(lax.fori_loop(0, 31, jstep, jnp.zeros((1, 1), I32)), (SUBLANES, LANES))

        jm = jm_ref[u, 0:1, 0:1]
        sel = (keys[u] > thr) | (eq & (pos <= jm))
        self_sel = (k_self[u] > thr) | ((k_self[u] == thr) & (total(eq & (pos <= jm)) < need))
        ms_ref[u] = jnp.broadcast_to(self_sel.astype(F32), (1, LANES))

        sel_b = sel.astype(BF16)
        within = _dot(sel_b, (ri <= ci).astype(BF16))
        tot = _dot(sel_b, jnp.ones((PAGE_SIZE, PAGE_SIZE), BF16))
        before = _dot((pc_ < pr_).astype(BF16), tot.astype(BF16))
        rk_ref[u] = jnp.where(sel, (before + within).astype(I32) - 1, -1)

        def gather_pos(p, carry):
            hi, lo = carry
            hit = jnp.broadcast_to(rk_ref[u, pl.ds(p, 1), :], (topk, PAGE_SIZE)) == jcol
            return hi + jnp.where(hit, jnp.asarray(p, F32), 0.0), lo + jnp.where(hit, lane_f, 0.0)

        zero = jnp.zeros((topk, PAGE_SIZE), F32)
        hi, lo = lax.fori_loop(0, npg, gather_pos, (zero, zero))
        pos_row = _dot_nt(ones8, hi.astype(BF16)) * PAGE_SIZE + _dot_nt(ones8, lo.astype(BF16))
        pos_ref[u] = pos_row[0:1].astype(I32)


SEL_CP = 8
SEL_SPS = 4


def _sel_sample(page_table, qi, wi, kself, cache_kt, topk):
    n, npg = page_table.shape
    U = SEL_SPS
    assert npg % SEL_CP == 0 and npg <= PAGE_SIZE and n % U == 0
    grid_spec = pltpu.PrefetchScalarGridSpec(
        num_scalar_prefetch=1,
        grid=(n // U,),
        in_specs=[pl.BlockSpec((U, IDX_HEADS, IDX_DIM), lambda s, pt: (s, 0, 0)),
                  pl.BlockSpec((U, IDX_HEADS, 1), lambda s, pt: (s, 0, 0)),
                  pl.BlockSpec((U, 1, IDX_DIM), lambda s, pt: (s, 0, 0)),
                  pl.BlockSpec(memory_space=pl.ANY)],
        out_specs=[pl.BlockSpec((U, 1, topk), lambda s, pt: (s, 0, 0)),
                   pl.BlockSpec((U, 1, LANES), lambda s, pt: (s, 0, 0))],
        scratch_shapes=[pltpu.VMEM((2, U, npg, IDX_DIM, PAGE_SIZE), F32), pltpu.VMEM((U, npg, PAGE_SIZE), F32),
                        pltpu.VMEM((U, SUBLANES, LANES), I32), pltpu.VMEM((U, npg, PAGE_SIZE), I32),
                        pltpu.SemaphoreType.DMA((2,))],
    )
    return pl.pallas_call(
        functools.partial(_sel_sample_body, topk=topk, npg=npg),
        grid_spec=grid_spec,
        out_shape=[jax.ShapeDtypeStruct((n, 1, topk), I32), jax.ShapeDtypeStruct((n, 1, LANES), F32)],
        compiler_params=_cparams(("arbitrary",)),
        name="sel_sample",
    )(page_table, qi, wi, kself, cache_kt)


def _att_sel_body(pt_ref, pos_ref, q_ref, ms_ref, ks_ref, vs_ref, ck_hbm, cv_hbm, o_ref, kbuf, vbuf, sem, *, topk):
    s = pl.program_id(0)
    slot = s % 2

    def request(seq, sl):
        def body(j, c):
            pos = pos_ref[seq, j]
            pg = pt_ref[seq, pos // PAGE_SIZE]
            r = pos % PAGE_SIZE
            pltpu.make_async_copy(ck_hbm.at[pg, r], kbuf.at[sl, j], sem.at[0, sl]).start()
            pltpu.make_async_copy(cv_hbm.at[pg, r], vbuf.at[sl, j], sem.at[1, sl]).start()
            return c
        lax.fori_loop(0, topk, body, 0, unroll=8)

    @pl.when(s == 0)
    def _():
        request(0, 0)

    @pl.when(s + 1 < pl.num_programs(0))
    def _():
        request(s + 1, 1 - slot)

    for h in range(topk // PAGE_SIZE):
        rows = pl.ds(h * PAGE_SIZE, PAGE_SIZE)
        pltpu.make_async_copy(ck_hbm.at[0], kbuf.at[slot, rows], sem.at[0, slot]).wait()
        pltpu.make_async_copy(cv_hbm.at[0], vbuf.at[slot, rows], sem.at[1, slot]).wait()

    q = q_ref[0]
    row_g = lax.broadcasted_iota(I32, (N_HEADS, 1), 0) // GROUP
    lane_g = lax.broadcasted_iota(I32, (1, KV_WIDTH), 1) // HEAD_DIM
    q_bd = jnp.where(row_g == lane_g, jnp.tile(q, (1, N_KV_HEADS)), jnp.zeros((N_HEADS, KV_WIDTH), BF16))
    k2 = jnp.concatenate([kbuf[slot, :, g, :] for g in range(N_KV_HEADS)], axis=1).astype(BF16)
    v2 = jnp.concatenate([vbuf[slot, :, g, :] for g in range(N_KV_HEADS)], axis=1).astype(BF16)
    self_row = ms_ref[0]
    self_f = self_row[:, 0:1]
    n_past = topk - jnp.tile(self_row, (1, topk // LANES))
    valid = lax.broadcasted_iota(I32, (1, topk), 1).astype(F32) < n_past
    sc = jnp.where(valid, _dot_nt(q_bd, k2), -1e30)
    s1 = jnp.sum(q.astype(F32) * ks_ref[0].astype(F32), axis=-1, keepdims=True)
    s1 = jnp.where(self_f > 0.5, s1, -1e30)
    m = jnp.maximum(jnp.max(sc, axis=-1, keepdims=True), jnp.maximum(s1, -1e29))
    pr = jnp.exp(sc - m)
    p1 = jnp.exp(s1 - m)
    l = jnp.sum(pr, axis=-1, keepdims=True) + p1
    pv = _dot(pr.astype(BF16), v2)
    own = jnp.zeros((N_HEADS, HEAD_DIM), F32)
    for g in range(N_KV_HEADS):
        own = jnp.where(row_g == g, pv[:, g * HEAD_DIM:(g + 1) * HEAD_DIM], own)
    o_ref[0] = ((own + p1.astype(BF16).astype(F32) * vs_ref[0].astype(F32)) / l).astype(o_ref.dtype)


def _att_sel(page_table, pos_list, q, mself, kself, vself, cache_k, cache_v):
    n, topk = pos_list.shape
    assert topk % PAGE_SIZE == 0
    seqspec = lambda r, c: pl.BlockSpec((1, r, c), lambda s, pt, pos: (s, 0, 0))
    anyspec = pl.BlockSpec(memory_space=pl.ANY)
    rows = (2, topk, N_KV_HEADS, HEAD_DIM)
    grid_spec = pltpu.PrefetchScalarGridSpec(
        num_scalar_prefetch=2,
        grid=(n,),
        in_specs=[seqspec(N_HEADS, HEAD_DIM), seqspec(1, LANES), seqspec(N_HEADS, HEAD_DIM),
                  seqspec(N_HEADS, HEAD_DIM), anyspec, anyspec],
        out_specs=seqspec(N_HEADS, HEAD_DIM),
        scratch_shapes=[pltpu.VMEM(rows, F32), pltpu.VMEM(rows, F32), pltpu.SemaphoreType.DMA((2, 2))],
    )
    return pl.pallas_call(
        functools.partial(_att_sel_body, topk=topk),
        grid_spec=grid_spec,
        out_shape=jax.ShapeDtypeStruct((n, N_HEADS, HEAD_DIM), BF16),
        compiler_params=_cparams(("arbitrary",)),
        name="att_sel",
    )(page_table, pos_list, q, mself, kself, vself, cache_k, cache_v)


def _mem_att_prompt_body(q_ref, k_ref, v_ref, o_ref):
    scale = MEM_HEAD_DIM ** -0.5
    for h in range(MEM_HEADS):
        sl = slice(h * MEM_HEAD_DIM, (h + 1) * MEM_HEAD_DIM)
        s = _dot_nt(q_ref[:, sl], k_ref[:, sl]) * scale
        m = jnp.max(s, axis=-1, keepdims=True)
        e = jnp.exp(s - m)
        pr = e / jnp.sum(e, axis=-1, keepdims=True)
        o_ref[:, sl] = _dot(pr.astype(BF16), v_ref[:, sl]).astype(o_ref.dtype)


def _mem_att_prompt(mq, mk, mv, batch, seq, tq):
    m = mk.shape[0] // batch
    nb = seq // tq
    return pl.pallas_call(
        _mem_att_prompt_body,
        grid=(batch * nb,),
        in_specs=[pl.BlockSpec((tq, MEM_WIDTH), lambda i: (i, 0)),
                  pl.BlockSpec((m, MEM_WIDTH), lambda i: (i // nb, 0)),
                  pl.BlockSpec((m, MEM_WIDTH), lambda i: (i // nb, 0))],
        out_specs=pl.BlockSpec((tq, MEM_WIDTH), lambda i: (i, 0)),
        out_shape=jax.ShapeDtypeStruct((batch * seq, MEM_WIDTH), BF16),
        compiler_params=_cparams(("parallel",)),
        name="mem_att_prompt",
    )(mq, mk, mv)


def _mem_att_sample_body(q_ref, k_ref, v_ref, o_ref):
    scale = MEM_HEAD_DIM ** -0.5
    q = q_ref[0].astype(F32)
    for h in range(MEM_HEADS):
        sl = slice(h * MEM_HEAD_DIM, (h + 1) * MEM_HEAD_DIM)
        s = jnp.sum(k_ref[0, :, h, :] * q[:, sl], axis=-1, keepdims=True) * scale
        m = jnp.max(s, axis=0, keepdims=True)
        e = jnp.exp(s - m)
        pr = e / jnp.sum(e, axis=0, keepdims=True)
        o_ref[0, :, sl] = jnp.sum(pr * v_ref[0, :, h, :], axis=0, keepdims=True).astype(o_ref.dtype)


def _mem_att_sample(mq, mk, mv):
    n, m, nh, hd = mk.shape
    w = nh * hd
    return pl.pallas_call(
        _mem_att_sample_body,
        grid=(n,),
        in_specs=[pl.BlockSpec((1, 1, w), lambda s: (s, 0, 0)),
                  pl.BlockSpec((1, m, nh, hd), lambda s: (s, 0, 0, 0)),
                  pl.BlockSpec((1, m, nh, hd), lambda s: (s, 0, 0, 0))],
        out_specs=pl.BlockSpec((1, 1, w), lambda s: (s, 0, 0)),
        out_shape=jax.ShapeDtypeStruct((n, 1, w), BF16),
        compiler_params=_cparams(("parallel",)),
        name="mem_att_sample",
    )(mq, mk, mv)


def _router_body(x_ref, w_ref, b_ref, ei_ref, ew_ref, acc_ref):
    k = pl.program_id(1)

    @pl.when(k == 0)
    def _():
        acc_ref[...] = jnp.zeros_like(acc_ref)

    acc_ref[...] += _dot(x_ref[...], w_ref[...], HIGHEST)

    @pl.when(k == pl.num_programs(1) - 1)
    def _():
        lg = acc_ref[...] + b_ref[...]
        lane = lax.broadcasted_iota(I32, lg.shape, 1)
        neg = jnp.float32(-jnp.inf)
        is_g = lane < N_GROUPS
        glm = jnp.where(is_g, lg, neg)
        gmax = jnp.max(glm, axis=-1, keepdims=True)
        g_sel = jnp.min(jnp.where(glm == gmax, lane, LANES), axis=-1, keepdims=True)
        g_prob = 1.0 / jnp.sum(jnp.where(is_g, jnp.exp(lg - gmax), 0.0), axis=-1, keepdims=True)
        e_id = lane - N_GROUPS
        in_grp = (e_id >= 0) & (e_id < N_EXPERTS) & ((e_id // EXPERTS_PER_GROUP) == g_sel)
        el = jnp.where(in_grp, lg, neg)
        m1 = jnp.max(el, axis=-1, keepdims=True)
        i1 = jnp.min(jnp.where(in_grp & (el == m1), lane, LANES), axis=-1, keepdims=True)
        rest = in_grp & (lane != i1)
        el2 = jnp.where(rest, lg, neg)
        m2 = jnp.max(el2, axis=-1, keepdims=True)
        i2 = jnp.min(jnp.where(rest & (el2 == m2), lane, LANES), axis=-1, keepdims=True)
        t = jnp.exp(m2 - m1)
        w1 = g_prob / (1.0 + t)
        w2 = g_prob * t / (1.0 + t)
        ei_ref[...] = jnp.where(lane == 0, i1 - N_GROUPS, jnp.where(lane == 1, i2 - N_GROUPS, 0))
        ew_ref[...] = jnp.where(lane == 0, w1, jnp.where(lane == 1, w2, 0.0))


def _router(x, w, b, tm, tk):
    m, kd = x.shape
    return pl.pallas_call(
        _router_body,
        grid=(m // tm, kd // tk),
        in_specs=[pl.BlockSpec((tm, tk), lambda i, k: (i, k)),
                  pl.BlockSpec((tk, LANES), lambda i, k: (k, 0)),
                  pl.BlockSpec((1, LANES), lambda i, k: (0, 0))],
        out_specs=[pl.BlockSpec((tm, LANES), lambda i, k: (i, 0)), pl.BlockSpec((tm, LANES), lambda i, k: (i, 0))],
        out_shape=[jax.ShapeDtypeStruct((m, LANES), I32), jax.ShapeDtypeStruct((m, LANES), F32)],
        scratch_shapes=[pltpu.VMEM((tm, LANES), F32)],
        compiler_params=_cparams(("parallel", "arbitrary")),
        name="router",
    )(x, w, b)


MOE_BR = 128


def _expert_up_body(be_ref, nblk_ref, tok_ref, x_hbm, wg_ref, wu_ref, h_ref, wgb_ref, wub_ref, xbuf, sem):
    i = pl.program_id(0)
    n_used = nblk_ref[0]
    slot = i % 2
    changed = jnp.logical_or(i == 0, be_ref[i] != be_ref[jnp.maximum(i - 1, 0)])

    def request(blk, sl):
        def issue(r, c):
            pltpu.make_async_copy(x_hbm.at[pl.ds(tok_ref[blk * MOE_BR + r], 1)], xbuf.at[sl, pl.ds(r, 1)],
                                  sem.at[sl]).start()
            return c
        lax.fori_loop(0, MOE_BR, issue, 0, unroll=8)

    @pl.when(jnp.logical_and(i == 0, n_used > 0))
    def _():
        request(0, 0)

    @pl.when(i + 1 < n_used)
    def _():
        request(i + 1, 1 - slot)

    @pl.when(jnp.logical_and(i < n_used, changed))
    def _():
        wgb_ref[...] = wg_ref[0].astype(BF16)
        wub_ref[...] = wu_ref[0].astype(BF16)

    @pl.when(i < n_used)
    def _():
        pltpu.make_async_copy(x_hbm.at[pl.ds(0, MOE_BR)], xbuf.at[slot], sem.at[slot]).wait()
        x = xbuf[slot].astype(BF16)
        a = _dot(x, wgb_ref[...])
        u = _dot(x, wub_ref[...])
        h_ref[...] = (a * _sigmoid(a) * u).astype(h_ref.dtype)

    @pl.when(i >= n_used)
    def _():
        h_ref[...] = jnp.zeros_like(h_ref)


def _expert_up(block_e, nblk, row_token, x, w_gate, w_up):
    nr = row_token.shape[0]
    d = x.shape[1]
    nb = nr // MOE_BR
    blk = lambda i, nbk: jnp.minimum(i, nbk[0] - 1)
    grid_spec = pltpu.PrefetchScalarGridSpec(
        num_scalar_prefetch=3,
        grid=(nb,),
        in_specs=[pl.BlockSpec(memory_space=pl.ANY),
                  pl.BlockSpec((1, d, D_EXPERT), lambda i, be, nbk, tok: (be[blk(i, nbk)], 0, 0)),
                  pl.BlockSpec((1, d, D_EXPERT), lambda i, be, nbk, tok: (be[blk(i, nbk)], 0, 0))],
        out_specs=pl.BlockSpec((MOE_BR, D_EXPERT), lambda i, be, nbk, tok: (i, 0)),
        scratch_shapes=[pltpu.VMEM((d, D_EXPERT), BF16), pltpu.VMEM((d, D_EXPERT), BF16),
                        pltpu.VMEM((2, MOE_BR, d), F32), pltpu.SemaphoreType.DMA((2,))],
    )
    return pl.pallas_call(
        _expert_up_body,
        grid_spec=grid_spec,
        out_shape=jax.ShapeDtypeStruct((nr, D_EXPERT), BF16),
        compiler_params=_cparams(("arbitrary",)),
        name="expert_up",
    )(block_e, nblk, row_token, x, w_gate, w_up)


def _expert_down_body(be_ref, nblk_ref, h_ref, wd_ref, y_ref, wdb_ref):
    i = pl.program_id(0)
    changed = jnp.logical_or(i == 0, be_ref[i] != be_ref[jnp.maximum(i - 1, 0)])

    @pl.when(jnp.logical_and(i < nblk_ref[0], changed))
    def _():
        wdb_ref[...] = wd_ref[0].astype(BF16)

    @pl.when(i < nblk_ref[0])
    def _():
        y_ref[...] = _dot(h_ref[...], wdb_ref[...])

    @pl.when(i >= nblk_ref[0])
    def _():
        y_ref[...] = jnp.zeros_like(y_ref)


def _expert_down(block_e, nblk, h, w_down):
    nr = h.shape[0]
    d = w_down.shape[2]
    blk = lambda i, nbk: jnp.minimum(i, nbk[0] - 1)
    grid_spec = pltpu.PrefetchScalarGridSpec(
        num_scalar_prefetch=2,
        grid=(nr // MOE_BR,),
        in_specs=[pl.BlockSpec((MOE_BR, D_EXPERT), lambda i, be, nbk: (blk(i, nbk), 0)),
                  pl.BlockSpec((1, D_EXPERT, d), lambda i, be, nbk: (be[blk(i, nbk)], 0, 0))],
        out_specs=pl.BlockSpec((MOE_BR, d), lambda i, be, nbk: (i, 0)),
        scratch_shapes=[pltpu.VMEM((D_EXPERT, d), BF16)],
    )
    return pl.pallas_call(
        _expert_down_body,
        grid_spec=grid_spec,
        out_shape=jax.ShapeDtypeStruct((nr, d), F32),
        compiler_params=_cparams(("arbitrary",)),
        name="expert_down",
    )(block_e, nblk, h, w_down)


def _combine_ln_body(slot_ref, x_ref, y_hbm, ew_ref, g_ref, b_ref, op_ref, os_ref, ybuf, sem, *, npb, tm, mp):
    i = pl.program_id(0)
    sl = i % 2

    def request(blk, s_):
        def issue(r, c):
            t = blk * tm + r
            pltpu.make_async_copy(y_hbm.at[pl.ds(slot_ref[t], 1)], ybuf.at[s_, pl.ds(r, 1)], sem.at[s_]).start()
            pltpu.make_async_copy(y_hbm.at[pl.ds(slot_ref[mp + t], 1)], ybuf.at[s_, pl.ds(tm + r, 1)],
                                  sem.at[s_]).start()
            return c
        lax.fori_loop(0, tm, issue, 0, unroll=8)

    @pl.when(i == 0)
    def _():
        request(0, 0)

    @pl.when(i + 1 < pl.num_programs(0))
    def _():
        request(i + 1, 1 - sl)

    pltpu.make_async_copy(y_hbm.at[pl.ds(0, 2 * tm)], ybuf.at[sl], sem.at[sl]).wait()
    ew = ew_ref[...]
    ff = ybuf[sl, 0:tm] * ew[:, 0:1] + ybuf[sl, tm:2 * tm] * ew[:, 1:2]
    y = _layer_norm_rows(DEEPNORM_ALPHA * x_ref[...] + ff, g_ref[...], b_ref[...])

    @pl.when(i < npb)
    def _():
        op_ref[...] = y

    @pl.when(i >= npb)
    def _():
        os_ref[...] = y


def _combine_ln(x, y_rows, slot2, ew, g, b, tm, n_prompt):
    m, d = x.shape
    nb = m // tm
    npb = n_prompt // tm
    grid_spec = pltpu.PrefetchScalarGridSpec(
        num_scalar_prefetch=1,
        grid=(nb,),
        in_specs=[pl.BlockSpec((tm, d), lambda i, sl: (i, 0)),
                  pl.BlockSpec(memory_space=pl.ANY),
                  pl.BlockSpec((tm, LANES), lambda i, sl: (i, 0)),
                  pl.BlockSpec((1, d), lambda i, sl: (0, 0)),
                  pl.BlockSpec((1, d), lambda i, sl: (0, 0))],
        out_specs=[pl.BlockSpec((tm, d), lambda i, sl: (jnp.minimum(i, npb - 1), 0)),
                   pl.BlockSpec((tm, d), lambda i, sl: (jnp.maximum(i - npb, 0), 0))],
        scratch_shapes=[pltpu.VMEM((2, 2 * tm, d), F32), pltpu.SemaphoreType.DMA((2,))],
    )
    return pl.pallas_call(
        functools.partial(_combine_ln_body, npb=npb, tm=tm, mp=m),
        grid_spec=grid_spec,
        out_shape=[jax.ShapeDtypeStruct((n_prompt, d), F32), jax.ShapeDtypeStruct((m - n_prompt, d), F32)],
        compiler_params=_cparams(("arbitrary",)),
        name="combine_ln",
    )(slot2, x, y_rows, ew, g, b)


def _pad_cols(x, n):
    return jnp.pad(x, ((0, 0), (0, n - x.shape[1])))


def _split_w_in(w):
    o = [int(v) for v in np.cumsum([0, RW_PROJ, ATT_WIDTH + 2 * KV_WIDTH + IDX_HEADS * IDX_DIM, IDX_DIM + IDX_HEADS,
                                    2 * D_MODEL])]
    w_rkv = w[:, 0:3 * RW_WIDTH].astype(BF16)
    w_att = w[:, o[1]:o[2]].astype(BF16)
    w_gate = w[:, o[3]:o[4]].astype(BF16)
    w_small = jnp.concatenate([_lora_cols(w[:, 0:RW_PROJ]), _pad_cols(w[:, o[2]:o[3]], LANES)], axis=1).astype(BF16)
    return w_rkv, w_att, w_gate, w_small


def _lora_cols(x):
    return jnp.concatenate([_pad_cols(x[:, 6144:6240], 128), _pad_cols(x[:, 6240:6336], 128), x[:, 6336:6592]], axis=1)


def _pack_rwkv(rw_mu, rw_w0, rw_w2, rw_a0, rw_a2, rw_g2, rw_k_k, rw_k_a, rw_r_k, rw_ln_w, rw_ln_b):
    flat = lambda t: t.reshape(1, RW_WIDTH)
    mu = rw_mu.reshape(1, RW_PROJ)
    rows = [mu[:, 0:2048], mu[:, 2048:4096], mu[:, 4096:6144], flat(rw_w0), flat(rw_a0), flat(rw_k_k),
            flat(rw_k_a), flat(rw_r_k), flat(rw_ln_w), flat(rw_ln_b)]
    prm = jnp.pad(jnp.concatenate(rows, axis=0), ((0, 6), (0, 0)))
    mu_l = jnp.pad(_lora_cols(mu), ((0, 7), (0, 0)))
    w2 = jnp.pad(rw_w2, ((0, 128 - W_LORA), (0, 0))).astype(BF16)
    a2 = jnp.pad(rw_a2, ((0, 128 - A_LORA), (0, 0))).astype(BF16)
    g2 = rw_g2.astype(BF16)
    return prm, mu_l, w2, a2, g2


def _head_indicators(width):
    lane = np.arange(width)[:, None] // RW_HEAD_DIM
    ind = (lane == np.arange(128)[None, :]).astype(np.float32)
    return jnp.asarray(ind), jnp.asarray(ind.T)


def _head_selectors():
    sel = np.zeros((WKV_HQ, WKV_W, RW_HEAD_DIM), np.float32)
    for j in range(WKV_HQ):
        sel[j, j * RW_HEAD_DIM + np.arange(RW_HEAD_DIM), np.arange(RW_HEAD_DIM)] = 1.0
    return jnp.asarray(sel)


def kernel(x_prompt, x_sample, mem_prompt, cache_k, cache_v, cache_idx_k, page_table, state_wkv, state_shift, cache_mem_k, cache_mem_v, w_in, rw_mu, rw_w0, rw_w2, rw_a0, rw_a2, rw_g2, rw_k_k, rw_k_a, rw_r_k, rw_ln_w, rw_ln_b, idx_ln_w, idx_ln_b, w_branch_a, w_branch_b, w_out, ln1_w, ln1_b, w_mem_q, w_mem_k, w_mem_v, w_mem_o, ln2_w, ln2_b, w_router_grp, b_router_grp, w_router_exp, b_router_exp, w_exp_gate, w_exp_up, w_exp_down, ln3_w, ln3_b):
    B, S, D = x_prompt.shape
    DB, DS, _ = x_sample.shape
    assert DS == 1 and cache_k.shape[0] == 1
    TP = B * S
    T = TP + DB
    MP = _round_up(T, 640)
    past = page_table.shape[1] * PAGE_SIZE
    n_mem = mem_prompt.shape[1]
    row1 = lambda a: a.reshape(1, -1)

    def pad_rows(a):
        return jnp.concatenate([a, jnp.zeros((MP - a.shape[0],) + a.shape[1:], a.dtype)], axis=0)

    x_all = pad_rows(jnp.concatenate([x_prompt.reshape(TP, D), x_sample.reshape(DB, D)], axis=0))
    xb = x_all.astype(BF16)
    w_rkv, w_att, w_gate, w_small = _split_w_in(w_in[0])
    z_rkv = _mm(xb, w_rkv, 640, 1024, D, name="in_proj_rkv")
    z_att = _mm(xb, w_att, 640, 1024, D, name="in_proj_att")
    z_gate = _mm(xb, w_gate, 640, 1024, D, name="in_proj_gate")
    z_small = _mm(xb, w_small, 640, S_TOTAL, D, name="in_proj_small")

    prm, mu_l, w2, a2, g2 = _pack_rwkv(rw_mu[0], rw_w0[0], rw_w2[0], rw_a0[0], rw_a2[0], rw_g2[0], rw_k_k[0],
                                       rw_k_a[0], rw_r_k[0], rw_ln_w[0], rw_ln_b[0])
    rw_p, wkv_p = _wkv_prompt(z_rkv, z_small, prm, mu_l, w2, a2, g2, _head_selectors(), B, S)
    ss = state_shift[0]
    ind_f, indt_f = _head_indicators(RW_WIDTH)
    tok = _wkv_tokens_sample(z_rkv, z_small, ss[:, 0:2048], ss[:, 2048:4096], ss[:, 4096:6144], _lora_cols(ss), prm,
                             mu_l, w2, a2, g2, ind_f, indt_f, TP, DB)
    t_r, t_w, t_al, t_be, t_km, t_v, t_g, t_bo = tok
    rowv = lambda a: a.reshape(DB, RW_HEADS, 1, RW_HEAD_DIM)
    colv = lambda a: a.reshape(DB, RW_HEADS, RW_HEAD_DIM, 1)
    y_col, wkv_s = _wkv_step(state_wkv[0], rowv(t_w), rowv(t_al), rowv(t_be), rowv(t_km), rowv(t_r), colv(t_v),
                             colv(t_g), colv(t_bo), rw_ln_w[0].reshape(1, RW_HEADS, RW_HEAD_DIM, 1),
                             rw_ln_b[0].reshape(1, RW_HEADS, RW_HEAD_DIM, 1))
    rw_all = pad_rows(jnp.concatenate([rw_p, y_col.reshape(DB, RW_WIDTH).astype(BF16)], axis=0))

    pos = jnp.concatenate([jnp.tile(jnp.arange(S, dtype=I32), B), jnp.full((MP - TP,), past, I32)])
    tab_a = _rope_tables(pos, ROT_DIM, HEAD_DIM)
    tab_i = _rope_tables(pos, IDX_ROT_DIM, IDX_DIM)
    qb, k_rot, kb, vb, qib, kiw, kid, vbt = _prep(z_att, z_small, tab_a, tab_i, _pad_cols(row1(idx_ln_w[0]), LANES),
                                                  _pad_cols(row1(idx_ln_b[0]), LANES), 128)
    att_p = _dsa_prompt(qb, kb, vbt, qib, kid, kiw, B, S, min(TOPK_MAX, S // 4))
    qi_s = qib[TP:T].reshape(DB, IDX_HEADS, IDX_DIM)
    wi_s = kiw[TP:T, IDX_DIM:IDX_DIM + IDX_HEADS].reshape(DB, IDX_HEADS, 1)
    topk_s = min(TOPK_MAX, (past + DS) // 4)
    pos_sel, mself = _sel_sample(page_table, qi_s, wi_s, kid[TP:T, 0:IDX_DIM].reshape(DB, 1, IDX_DIM),
                                 jnp.swapaxes(cache_idx_k[0], 1, 2), topk_s)
    expand = lambda a: jnp.repeat(a[TP:T].reshape(DB, N_KV_HEADS, HEAD_DIM), GROUP, axis=1)
    att_s = _att_sel(page_table, pos_sel.reshape(DB, topk_s), qb[TP:T].reshape(DB, N_HEADS, HEAD_DIM), mself,
                     expand(kb), expand(vb), cache_k[0], cache_v[0])
    att_all = pad_rows(jnp.concatenate([att_p, att_s.reshape(DB, ATT_WIDTH)], axis=0))

    merged = _branch_merge(rw_all, att_all, w_branch_a[0].astype(BF16), w_branch_b[0].astype(BF16), z_gate, 640, 1024)
    x1, x1b = _mm_ln(merged, w_out[0].astype(BF16), x_all, row1(ln1_w[0]), row1(ln1_b[0]), 320, 512, name="out_ln1")

    mq = _mm(x1b, w_mem_q[0].astype(BF16), 640, MEM_WIDTH, D, out_dtype=BF16, name="mem_q")
    mem2d = mem_prompt.reshape(B * n_mem, D).astype(BF16)
    mem_k = _mm(mem2d, w_mem_k[0].astype(BF16), B * n_mem, MEM_WIDTH, D, name="mem_k")
    mem_v = _mm(mem2d, w_mem_v[0].astype(BF16), B * n_mem, MEM_WIDTH, D, name="mem_v")
    ma_p = _mem_att_prompt(mq, mem_k.astype(BF16), mem_v.astype(BF16), B, S, 512)
    ma_s = _mem_att_sample(mq[TP:T].reshape(DB, 1, MEM_WIDTH), cache_mem_k[0], cache_mem_v[0])
    ma_all = pad_rows(jnp.concatenate([ma_p, ma_s.reshape(DB, MEM_WIDTH)], axis=0))
    x2, _ = _mm_ln(ma_all, w_mem_o[0].astype(BF16), x1, row1(ln2_w[0]), row1(ln2_b[0]), 320, 512, name="mem_o_ln2")

    w_r = _pad_cols(jnp.concatenate([w_router_grp[0], w_router_exp[0]], axis=1), LANES)
    b_r = _pad_cols(row1(jnp.concatenate([b_router_grp[0], b_router_exp[0]])), LANES)
    e_idx, e_w = _router(x2, w_r, b_r, 640, 1024)
    n_assign = 2 * T
    flat_e = e_idx[:T, 0:2].reshape(n_assign)
    order = jnp.argsort(flat_e).astype(I32)
    rank = jnp.argsort(order).astype(I32)
    experts = jnp.arange(N_EXPERTS, dtype=I32)
    onehot = flat_e[:, None] == experts[None, :]
    counts = jnp.sum(onehot, axis=0, dtype=I32)
    padded = (counts + MOE_BR - 1) // MOE_BR * MOE_BR
    pad_end = jnp.cumsum(padded)
    pad_start = pad_end - padded
    start = jnp.cumsum(counts) - counts
    slot = (rank + jnp.sum(jnp.where(onehot, (pad_start - start)[None, :], 0), axis=1)).reshape(T, 2)
    n_blocks = -(-n_assign // MOE_BR) + N_EXPERTS
    blk_row0 = jnp.arange(n_blocks, dtype=I32) * MOE_BR
    block_e = jnp.minimum(jnp.sum(pad_end[None, :] <= blk_row0[:, None], axis=1, dtype=I32), N_EXPERTS - 1)
    blk_hot = block_e[:, None] == experts[None, :]
    pick = lambda tab: jnp.sum(jnp.where(blk_hot, tab[None, :], 0), axis=1)
    j_in_e = (blk_row0 - pick(pad_start))[:, None] + jnp.arange(MOE_BR, dtype=I32)[None, :]
    src = jnp.clip(pick(start)[:, None] + j_in_e, 0, n_assign - 1)
    row_token = jnp.where(j_in_e < pick(counts)[:, None], order[src] // 2, 0).reshape(n_blocks * MOE_BR)
    n_used = (pad_end[-1] // MOE_BR).astype(I32).reshape(1)
    hid = _expert_up(block_e, n_used, row_token, x2, w_exp_gate[0], w_exp_up[0])
    y_rows = _expert_down(block_e, n_used, hid, w_exp_down[0])
    slot_pad = jnp.concatenate([jnp.pad(slot[:, 0], (0, MP - T)), jnp.pad(slot[:, 1], (0, MP - T))])
    y_p, y_s = _combine_ln(x2, y_rows, slot_pad, e_w, row1(ln3_w[0]), row1(ln3_b[0]), 128, TP)

    kv5 = lambda a, n, s: a.reshape(1, n, s, N_KV_HEADS, HEAD_DIM)
    va = z_att[:, A_VA:A_VA + KV_WIDTH]
    ki = kiw[:, 0:IDX_DIM]
    last = lambda a: jnp.concatenate([a[(b + 1) * S - 1:(b + 1) * S] for b in range(B)] + [a[TP:T]], axis=0)
    zl, zsl = last(z_rkv), last(z_small)
    shift_cols = jnp.concatenate([zl, zsl[:, S_LORA:S_LORA + W_LORA], zsl[:, S_LORA + 128:S_LORA + 128 + A_LORA],
                                  zsl[:, S_LORA + 256:S_LORA + 512]], axis=1)
    mem5 = lambda a: a.reshape(1, B, n_mem, MEM_HEADS, MEM_HEAD_DIM)
    return (y_p.reshape(B, S, D), y_s[:DB].reshape(DB, DS, D),
            kv5(k_rot[:TP], B, S), kv5(va[:TP], B, S), ki[:TP].reshape(1, B, S, IDX_DIM),
            wkv_p[None], shift_cols[:B][None], mem5(mem_k), mem5(mem_v),
            kv5(k_rot[TP:T], DB, DS), kv5(va[TP:T], DB, DS), ki[TP:T].reshape(1, DB, DS, IDX_DIM),
            wkv_s[None], shift_cols[B:][None])
```

```python
import functools
import math

import jax
import jax.numpy as jnp
import numpy as np
from jax import lax
from jax.experimental import pallas as pl
from jax.experimental.pallas import tpu as pltpu

F32 = jnp.float32
BF16 = jnp.bfloat16
I32 = jnp.int32
HIGHEST = lax.Precision.HIGHEST

D_MODEL = 4096
RW_HEAD_DIM = 64
RW_HEADS = 32
RW_WIDTH = 2048
W_LORA = 96
A_LORA = 96
G_LORA = 256
RW_PROJ = 3 * RW_WIDTH + W_LORA + A_LORA + G_LORA
RW_GN_EPS = 64e-5
HEAD_DIM = 128
N_HEADS = 16
N_KV_HEADS = 4
GROUP = 4
ATT_WIDTH = 2048
KV_WIDTH = 512
ROT_DIM = 32
ROPE_THETA = 500000.0
IDX_HEADS = 16
IDX_DIM = 64
IDX_ROT_DIM = 16
TOPK_MAX = 256
PAGE_SIZE = 128
MEM_HEADS = 4
MEM_HEAD_DIM = 128
MEM_WIDTH = 512
N_GROUPS = 8
EXPERTS_PER_GROUP = 8
N_EXPERTS = 64
D_EXPERT = 512
LN_EPS = 1e-5
DEEPNORM_ALPHA = 2.0 ** 0.25
EXP_M05 = math.exp(-0.5)

LANES = 128
SUBLANES = 8
VMEM_LIMIT = 56 * 1024 * 1024

DENSE_TM = 640
DENSE_TN = 1024
LN_TM = 320
LN_TN = 512
ROW_TM = 128

C_R, C_K, C_V = 0, 2048, 4096
A_Q, A_KA, A_VA, A_IQ = 0, 2048, 2560, 3072
G_A, G_B = 0, 4096
S_LORA, S_IKW, S_TOTAL = 0, 512, 640

INT_MIN = -(2 ** 31)


def _round_up(n, m):
    return -(-n // m) * m


def _cparams(sem):
    return pltpu.CompilerParams(dimension_semantics=sem, vmem_limit_bytes=VMEM_LIMIT)


def _dot(a, b, precision=None):
    return jnp.dot(a, b, preferred_element_type=F32, precision=precision)


def _dot_nt(a, b, precision=None):
    return lax.dot_general(a, b, (((1,), (1,)), ((), ())), preferred_element_type=F32, precision=precision)


def _sigmoid(x):
    return 1.0 / (1.0 + jnp.exp(-x))


def _mm_body(x_ref, w_ref, o_ref, acc_ref):
    k = pl.program_id(2)

    @pl.when(k == 0)
    def _():
        acc_ref[...] = jnp.zeros_like(acc_ref)

    acc_ref[...] += _dot(x_ref[...], w_ref[...])

    @pl.when(k == pl.num_programs(2) - 1)
    def _():
        o_ref[...] = acc_ref[...].astype(o_ref.dtype)


def _mm_fullk_body(x_ref, w_ref, o_ref):
    o_ref[...] = _dot(x_ref[...], w_ref[...]).astype(o_ref.dtype)


def _mm(x, w, tm, tn, tk, out_dtype=F32, name="mm"):
    m, kd = x.shape
    n = w.shape[1]
    if tk == kd:
        return pl.pallas_call(
            _mm_fullk_body,
            grid=(m // tm, n // tn),
            in_specs=[pl.BlockSpec((tm, kd), lambda i, j: (i, 0)),
                      pl.BlockSpec((kd, tn), lambda i, j: (0, j))],
            out_specs=pl.BlockSpec((tm, tn), lambda i, j: (i, j)),
            out_shape=jax.ShapeDtypeStruct((m, n), out_dtype),
            compiler_params=_cparams(("parallel", "parallel")),
            name=name,
        )(x, w)
    return pl.pallas_call(
        _mm_body,
        grid=(m // tm, n // tn, kd // tk),
        in_specs=[pl.BlockSpec((tm, tk), lambda i, j, k: (i, k)),
                  pl.BlockSpec((tk, tn), lambda i, j, k: (k, j))],
        out_specs=pl.BlockSpec((tm, tn), lambda i, j, k: (i, j)),
        out_shape=jax.ShapeDtypeStruct((m, n), out_dtype),
        scratch_shapes=[pltpu.VMEM((tm, tn), F32)],
        compiler_params=_cparams(("parallel", "parallel", "arbitrary")),
        name=name,
    )(x, w)


def _layer_norm_rows(x, g, b):
    mu = jnp.mean(x, axis=-1, keepdims=True)
    d = x - mu
    var = jnp.mean(d * d, axis=-1, keepdims=True)
    return d * lax.rsqrt(var + LN_EPS) * g + b


def _mm_ln_body(x_ref, w_ref, res_ref, g_ref, b_ref, o_ref, ob_ref, y_ref, *, tn):
    j = pl.program_id(1)
    y_ref[:, pl.ds(pl.multiple_of(j * tn, tn), tn)] = _dot(x_ref[...], w_ref[...])

    @pl.when(j == pl.num_programs(1) - 1)
    def _():
        y = _layer_norm_rows(DEEPNORM_ALPHA * res_ref[...] + y_ref[...], g_ref[...], b_ref[...])
        o_ref[...] = y
        ob_ref[...] = y.astype(BF16)


def _mm_ln(x, w, res, g, b, tm, tn, name="mm_ln"):
    m, kd = x.shape
    n = w.shape[1]
    return pl.pallas_call(
        functools.partial(_mm_ln_body, tn=tn),
        grid=(m // tm, n // tn),
        in_specs=[pl.BlockSpec((tm, kd), lambda i, j: (i, 0)),
                  pl.BlockSpec((kd, tn), lambda i, j: (0, j)),
                  pl.BlockSpec((tm, n), lambda i, j: (i, 0)),
                  pl.BlockSpec((1, n), lambda i, j: (0, 0)),
                  pl.BlockSpec((1, n), lambda i, j: (0, 0))],
        out_specs=[pl.BlockSpec((tm, n), lambda i, j: (i, 0)),
                   pl.BlockSpec((tm, n), lambda i, j: (i, 0))],
        out_shape=[jax.ShapeDtypeStruct((m, n), F32), jax.ShapeDtypeStruct((m, n), BF16)],
        scratch_shapes=[pltpu.VMEM((tm, n), F32)],
        compiler_params=_cparams(("parallel", "arbitrary")),
        name=name,
    )(x, w, res, g, b)


def _branch_merge_body(rw_ref, at_ref, wa_ref, wb_ref, ga_ref, gb_ref, o_ref):
    a = _dot(rw_ref[...], wa_ref[...])
    b = _dot(at_ref[...], wb_ref[...])
    o_ref[...] = (_sigmoid(ga_ref[...]) * a + _sigmoid(gb_ref[...]) * b).astype(o_ref.dtype)


def _branch_merge(rw, att, wa, wb, z, tm, tn):
    m = rw.shape[0]
    n = wa.shape[1]
    ga0, gb0 = G_A // tn, G_B // tn
    return pl.pallas_call(
        _branch_merge_body,
        grid=(m // tm, n // tn),
        in_specs=[pl.BlockSpec((tm, RW_WIDTH), lambda i, j: (i, 0)),
                  pl.BlockSpec((tm, ATT_WIDTH), lambda i, j: (i, 0)),
                  pl.BlockSpec((RW_WIDTH, tn), lambda i, j: (0, j)),
                  pl.BlockSpec((ATT_WIDTH, tn), lambda i, j: (0, j)),
                  pl.BlockSpec((tm, tn), lambda i, j: (i, ga0 + j)),
                  pl.BlockSpec((tm, tn), lambda i, j: (i, gb0 + j))],
        out_specs=pl.BlockSpec((tm, tn), lambda i, j: (i, j)),
        out_shape=jax.ShapeDtypeStruct((m, n), BF16),
        compiler_params=_cparams(("parallel", "parallel")),
        name="branch_merge",
    )(rw, att, wa, wb, z, z)


def _seg_sum(x, ind, ind_t):
    return _dot(_dot(x, ind, HIGHEST), ind_t, HIGHEST)


def _split_bf16(x, parts):
    out = []
    for _ in range(parts):
        t = x.astype(BF16)
        out.append(t)
        x = x - t.astype(F32)
    return out


def _seg_sum_quads(x, bd):
    outs = []
    for q in range(x.shape[1] // WKV_W):
        hi, lo = _split_bf16(x[:, q * WKV_W:(q + 1) * WKV_W], 2)
        outs.append(_dot(hi, bd) + _dot(lo, bd))
    return jnp.concatenate(outs, axis=1) if len(outs) > 1 else outs[0]


def _rwkv_tokens(zr, zk, zv, zl, pr, pk, pv, plo, prm, mu_l, w2, a2, g2, seg):
    r = zr + (pr - zr) * prm[0:1]
    kx = zk + (pk - zk) * prm[1:2]
    v = zv + (pv - zv) * prm[2:3]
    zsl = zl + (plo - zl) * mu_l
    tw = jnp.tanh(zsl[:, 0:128]).astype(BF16)
    xw = prm[3:4] + _dot(tw, w2)
    lw = -EXP_M05 * _sigmoid(xw)
    a = _sigmoid(prm[4:5] + _dot(zsl[:, 128:256].astype(BF16), a2))
    g = _dot(_sigmoid(zsl[:, 256:512]).astype(BF16), g2)
    kk = kx * prm[5:6]
    n2 = seg(kk * kk)
    kkn = kk / jnp.maximum(jnp.sqrt(n2), 1e-12)
    kmod = kx * (1.0 + (a - 1.0) * prm[6:7])
    return r, lw, kmod, v, kkn, a, g


def _rwkv_post(y, r, kmod, v, g, prm, seg):
    inv_n = 1.0 / RW_HEAD_DIM
    mean = seg(y) * inv_n
    d = y - mean
    var = seg(d * d) * inv_n
    yn = d * lax.rsqrt(var + RW_GN_EPS) * prm[8:9] + prm[9:10]
    bonus = seg(r * kmod * prm[7:8]) * v
    return (yn + bonus) * g


WKV_C = 64
WKV_HQ = 4
WKV_W = WKV_HQ * RW_HEAD_DIM
WKV_QPS = 8


def _wkv_chunk_body(zr_ref, zk_ref, zv_ref, zl_ref, prm_ref, mul_ref, w2_ref, a2_ref, g2_ref,
                    sel_ref, o_ref, so_ref, s_ref, cr_ref, ck_ref, cv_ref, cl_ref):
    c = pl.program_id(2)
    C = WKV_C
    W = WKV_W

    @pl.when(c == 0)
    def _():
        s_ref[...] = jnp.zeros_like(s_ref)
        cr_ref[...] = jnp.zeros_like(cr_ref)
        ck_ref[...] = jnp.zeros_like(ck_ref)
        cv_ref[...] = jnp.zeros_like(cv_ref)
        cl_ref[...] = jnp.zeros_like(cl_ref)

    rows = lax.broadcasted_iota(I32, (C, 1), 0)

    def shifted(z, carry_ref):
        prev = jnp.where(rows == 0, carry_ref[0:1, :], pltpu.roll(z, 1, 0))
        carry_ref[0:1, :] = z[C - 1:C, :]
        return prev

    zr, zk, zv, zl = zr_ref[...], zk_ref[...], zv_ref[...], zl_ref[...]
    pr, pk, pv, plo = shifted(zr, cr_ref), shifted(zk, ck_ref), shifted(zv, cv_ref), shifted(zl, cl_ref)
    prm = prm_ref[...]
    lane_head = lax.broadcasted_iota(I32, (1, W), 1) // RW_HEAD_DIM
    hv = lax.broadcasted_iota(I32, (W, 1), 0) // RW_HEAD_DIM
    bd = (hv == lane_head).astype(BF16)
    seg = lambda x: _seg_sum_quads(x, bd)
    r, lw, kmod, v, kkn, a, g = _rwkv_tokens(zr, zk, zv, zl, pr, pk, pv, plo, prm, mul_ref[0:1, :],
                                            w2_ref[...], a2_ref[...], g2_ref[...], seg)
    al = -kkn
    be = kkn * a

    ti = lax.broadcasted_iota(I32, (C, C), 0)
    tj = lax.broadcasted_iota(I32, (C, C), 1)
    tri = (tj <= ti).astype(BF16)
    cum = sum(_dot(tri, part) for part in _split_bf16(lw, 3))
    cum_l = cum[C - 1:C, :]
    p_inv = jnp.exp(-cum)
    p_rel = jnp.exp(cum_l - cum)
    ab = al * jnp.exp(cum - lw)
    rb = r * jnp.exp(cum)
    bt = (be * p_inv).astype(BF16)
    kt = (kmod * p_inv).astype(BF16)
    bk = jnp.concatenate([be * p_rel, kmod * p_rel], axis=0).astype(BF16)
    ar = jnp.concatenate([ab, rb], axis=0)
    pc = jnp.exp(cum_l)

    n4 = WKV_HQ * C
    bi = lax.broadcasted_iota(I32, (n4, n4), 0)
    bj = lax.broadcasted_iota(I32, (n4, n4), 1)
    same = (bi // C) == (bj // C)
    tri_s4 = same & ((bj % C) < (bi % C))
    tri_i4 = same & ((bj % C) <= (bi % C))
    eye4 = (bi == bj).astype(F32)
    masks = [lane_head == j for j in range(WKV_HQ)]

    def stack(x):
        return jnp.concatenate([jnp.where(m, x, jnp.zeros_like(x)) for m in masks], axis=0)

    def block_sum(x):
        return sum(x[j * C:(j + 1) * C] for j in range(WKV_HQ))

    def bdot(a, b):
        return _dot(a.astype(BF16), b.astype(BF16))

    qs = []
    for q in range(WKV_QPS):
        sl = slice(q * W, (q + 1) * W)
        v_q = v[:, sl]
        lhs = jnp.concatenate([stack(ab[:, sl]), stack(rb[:, sl])], axis=0).astype(BF16)
        abr = _dot_nt(lhs, stack(bt[:, sl]))
        akr = _dot_nt(lhs, stack(kt[:, sl]))
        qs.append(dict(sl=sl, v=v_q, v_s=stack(v_q), s0=s_ref[q],
                       x=jnp.where(tri_s4, abr[0:n4], 0.0), a_rb=jnp.where(tri_i4, abr[n4:2 * n4], 0.0),
                       a_ak=jnp.where(tri_s4, akr[0:n4], 0.0), a_rk=jnp.where(tri_i4, akr[n4:2 * n4], 0.0)))
    for d in qs:
        d['pw'] = [d['x']]
    for _ in range(5):
        for d in qs:
            d['pw'].append(bdot(d['pw'][-1], d['pw'][-1]))
    for d in qs:
        pw = d['pw']
        pr_ = [eye4 + pw[2 * i] + pw[2 * i + 1] + bdot(pw[2 * i], pw[2 * i + 1]) for i in range(3)]
        d['t'] = bdot(bdot(pr_[0], pr_[1]), pr_[2])
    for d in qs:
        gs = _dot_nt(ar[:, d['sl']].astype(BF16), d['s0'].astype(BF16))
        d['g_r'] = gs[C:2 * C]
        d['w_s'] = stack(gs[0:C]) + bdot(d['a_ak'], d['v_s'])
    for d in qs:
        d['u_s'] = bdot(d['t'], d['w_s'])
    ys = []
    for q, d in enumerate(qs):
        yv = bdot(jnp.concatenate([d['a_rb'], d['a_rk']], axis=1), jnp.concatenate([d['u_s'], d['v_s']], axis=0))
        ys.append(d['g_r'] + block_sum(yv))
        uv_t = jnp.concatenate([block_sum(d['u_s']), d['v']], axis=0).T.astype(BF16)
        upd = _dot(uv_t, bk[:, d['sl']])
        s_ref[q] = d['s0'] * pc[:, d['sl']] + jnp.where(hv == lane_head, upd, 0.0)

    y = jnp.concatenate(ys, axis=1) if WKV_QPS > 1 else ys[0]
    o_ref[...] = _rwkv_post(y, r, kmod, v, g, prm, seg).astype(o_ref.dtype)

    @pl.when(c == pl.num_programs(2) - 1)
    def _():
        for q in range(WKV_QPS):
            for j in range(WKV_HQ):
                rows_j = s_ref[q, j * RW_HEAD_DIM:(j + 1) * RW_HEAD_DIM, :]
                so_ref[0, q * WKV_HQ + j] = _dot(rows_j, sel_ref[j], HIGHEST)


def _wkv_prompt(z, zs, prm, mu_l, w2, a2, g2, sel, batch, seq):
    nc = seq // WKV_C
    WS = WKV_W * WKV_QPS
    nq = RW_WIDTH // WS
    row = lambda b, q, c: b * nc + c
    return pl.pallas_call(
        _wkv_chunk_body,
        grid=(batch, nq, nc),
        in_specs=[pl.BlockSpec((WKV_C, WS), lambda b, q, c: (row(b, q, c), C_R // WS + q)),
                  pl.BlockSpec((WKV_C, WS), lambda b, q, c: (row(b, q, c), C_K // WS + q)),
                  pl.BlockSpec((WKV_C, WS), lambda b, q, c: (row(b, q, c), C_V // WS + q)),
                  pl.BlockSpec((WKV_C, 512), lambda b, q, c: (row(b, q, c), S_LORA // 512)),
                  pl.BlockSpec((16, WS), lambda b, q, c: (0, q)),
                  pl.BlockSpec((8, 512), lambda b, q, c: (0, 0)),
                  pl.BlockSpec((128, WS), lambda b, q, c: (0, q)),
                  pl.BlockSpec((128, WS), lambda b, q, c: (0, q)),
                  pl.BlockSpec((256, WS), lambda b, q, c: (0, q)),
                  pl.BlockSpec((WKV_HQ, WKV_W, RW_HEAD_DIM), lambda b, q, c: (0, 0, 0))],
        out_specs=[pl.BlockSpec((WKV_C, WS), lambda b, q, c: (row(b, q, c), q)),
                   pl.BlockSpec((1, WKV_HQ * WKV_QPS, RW_HEAD_DIM, RW_HEAD_DIM), lambda b, q, c: (b, q, 0, 0))],
        out_shape=[jax.ShapeDtypeStruct((batch * seq, RW_WIDTH), BF16),
                   jax.ShapeDtypeStruct((batch, RW_HEADS, RW_HEAD_DIM, RW_HEAD_DIM), F32)],
        scratch_shapes=[pltpu.VMEM((WKV_QPS, WKV_W, WKV_W), F32), pltpu.VMEM((8, WS), F32), pltpu.VMEM((8, WS), F32),
                        pltpu.VMEM((8, WS), F32), pltpu.VMEM((8, 512), F32)],
        compiler_params=_cparams(("parallel", "parallel", "arbitrary")),
        name="wkv_prompt",
    )(z, z, z, zs, prm, mu_l, w2, a2, g2, sel)


def _wkv_tok_body(zr_ref, zk_ref, zv_ref, zl_ref, pr_ref, pk_ref, pv_ref, pl_ref, prm_ref, mul_ref, w2_ref, a2_ref,
                  g2_ref, ind_ref, indt_ref, r_ref, w_ref, al_ref, be_ref, km_ref, v_ref, g_ref, bo_ref):
    prm = prm_ref[...]
    ind, ind_t = ind_ref[...], indt_ref[...]
    seg = lambda x: _seg_sum(x, ind, ind_t)
    r, lw, kmod, v, kkn, a, g = _rwkv_tokens(zr_ref[...], zk_ref[...], zv_ref[...], zl_ref[...], pr_ref[...],
                                            pk_ref[...], pv_ref[...], pl_ref[...], prm, mul_ref[0:1, :],
                                            w2_ref[...], a2_ref[...], g2_ref[...], seg)
    r_ref[...] = r
    w_ref[...] = jnp.exp(lw)
    al_ref[...] = -kkn
    be_ref[...] = kkn * a
    km_ref[...] = kmod
    n = r.shape[0]
    pad = jnp.zeros((LANES - n, r.shape[1]), F32)
    for ref, val in ((v_ref, v), (g_ref, g), (bo_ref, seg(r * kmod * prm[7:8]))):
        ref[...] = jnp.concatenate([val, pad], axis=0).T


def _wkv_tokens_sample(z, zs, prev_r, prev_k, prev_v, prev_l, prm, mu_l, w2, a2, g2, ind, ind_t, row0, n):
    rb = row0 // n
    full = lambda a: pl.BlockSpec(a.shape, lambda i: (0,) * a.ndim)
    zspec = lambda w, c0: pl.BlockSpec((n, w), lambda i: (rb, c0 // w))
    return pl.pallas_call(
        _wkv_tok_body,
        grid=(1,),
        in_specs=[zspec(RW_WIDTH, C_R), zspec(RW_WIDTH, C_K), zspec(RW_WIDTH, C_V), zspec(512, S_LORA),
                  full(prev_r), full(prev_k), full(prev_v), full(prev_l), full(prm), full(mu_l), full(w2), full(a2),
                  full(g2), full(ind), full(ind_t)],
        out_specs=[pl.BlockSpec((n, RW_WIDTH), lambda i: (0, 0))] * 5
        + [pl.BlockSpec((RW_WIDTH, LANES), lambda i: (0, 0))] * 3,
        out_shape=[jax.ShapeDtypeStruct((n, RW_WIDTH), F32)] * 5 + [jax.ShapeDtypeStruct((RW_WIDTH, LANES), F32)] * 3,
        compiler_params=_cparams(("arbitrary",)),
        name="wkv_tokens_sample",
    )(z, z, z, zs, prev_r, prev_k, prev_v, prev_l, prm, mu_l, w2, a2, g2, ind, ind_t)


def _wkv_step_body(s_ref, w_ref, al_ref, be_ref, km_ref, r_ref, v_ref, g_ref, bo_ref, lnw_ref, lnb_ref, o_ref, so_ref):
    s = s_ref[...]
    n, hq = s.shape[0], s.shape[1]

    def columns(t_ref):
        t = t_ref[...]
        return jnp.stack([t[:, j:j + 1] for j in range(n)], axis=0).reshape(n, hq, RW_HEAD_DIM, 1)

    vcol, gcol, bocol = columns(v_ref), columns(g_ref), columns(bo_ref)
    sa = jnp.sum(s * al_ref[...], axis=-1, keepdims=True)
    s2 = s * w_ref[...] + sa * be_ref[...] + vcol * km_ref[...]
    so_ref[...] = s2
    y = jnp.sum(s2 * r_ref[...], axis=-1, keepdims=True)
    mean = jnp.mean(y, axis=2, keepdims=True)
    d = y - mean
    var = jnp.mean(d * d, axis=2, keepdims=True)
    yn = d * lax.rsqrt(var + RW_GN_EPS) * lnw_ref[...] + lnb_ref[...]
    o_ref[...] = (yn + bocol * vcol) * gcol


def _wkv_step(state, w, al, be, km, r, vt, gt, bot, lnw, lnb):
    n, h = state.shape[0], state.shape[1]
    hq = 2
    rowspec = pl.BlockSpec((n, hq, 1, RW_HEAD_DIM), lambda q: (0, q, 0, 0))
    colspec = pl.BlockSpec((n, hq, RW_HEAD_DIM, 1), lambda q: (0, q, 0, 0))
    tspec = pl.BlockSpec((hq * RW_HEAD_DIM, LANES), lambda q: (q, 0))
    pcol = pl.BlockSpec((1, hq, RW_HEAD_DIM, 1), lambda q: (0, q, 0, 0))
    sspec = pl.BlockSpec((n, hq, RW_HEAD_DIM, RW_HEAD_DIM), lambda q: (0, q, 0, 0))
    return pl.pallas_call(
        _wkv_step_body,
        grid=(h // hq,),
        in_specs=[sspec, rowspec, rowspec, rowspec, rowspec, rowspec, tspec, tspec, tspec, pcol, pcol],
        out_specs=[colspec, sspec],
        out_shape=[jax.ShapeDtypeStruct((n, h, RW_HEAD_DIM, 1), F32), jax.ShapeDtypeStruct(state.shape, F32)],
        compiler_params=_cparams(("parallel",)),
        name="wkv_step",
    )(state, w, al, be, km, r, vt, gt, bot, lnw, lnb)


def _rope_tables(pos, rot_dim, period):
    half = rot_dim // 2
    t = pos.shape[0]
    inv_freq = ROPE_THETA ** (-jnp.arange(half, dtype=F32) / half)
    ang = pos.astype(F32)[:, None] * inv_freq[None, :]
    cos, sin = jnp.cos(ang), jnp.sin(ang)
    zh = jnp.zeros((t, half), F32)
    rest = period - rot_dim
    c = jnp.concatenate([cos, cos, jnp.ones((t, rest), F32)], axis=1)
    s1 = jnp.concatenate([-sin, zh, jnp.zeros((t, rest), F32)], axis=1)
    s2 = jnp.concatenate([zh, sin, jnp.zeros((t, rest), F32)], axis=1)
    rep = LANES // period
    return jnp.stack([jnp.tile(a, (1, rep)) for a in (c, s1, s2)], axis=0)


def _rope(x, tab, half):
    n = x.shape[1]
    rep = n // LANES
    c, s1, s2 = [jnp.tile(tab[i], (1, rep)) if rep > 1 else tab[i] for i in range(3)]
    return x * c + pltpu.roll(x, n - half, 1) * s1 + pltpu.roll(x, half, 1) * s2


def _prep_body(q_ref, ka_ref, va_ref, iq_ref, ikw_ref, ta_ref, ti_ref, lnw_ref, lnb_ref,
               qo_ref, ko_ref, kb_ref, vb_ref, qio_ref, kio_ref, kid_ref, vbt_ref):
    ta = ta_ref[...]
    ti = ti_ref[...]
    qo_ref[...] = (_rope(q_ref[...], ta, ROT_DIM // 2) * (HEAD_DIM ** -0.5)).astype(BF16)
    k = _rope(ka_ref[...], ta, ROT_DIM // 2)
    ko_ref[...] = k
    kb_ref[...] = k.astype(BF16)
    vb_ref[...] = va_ref[...].astype(BF16)
    vbt_ref[...] = va_ref[...].T.astype(BF16)
    qio_ref[...] = _rope(iq_ref[...], ti, IDX_ROT_DIM // 2).astype(BF16)
    x = ikw_ref[...]
    lane = lax.broadcasted_iota(I32, x.shape, 1)
    is_k = lane < IDX_DIM
    mu = jnp.sum(jnp.where(is_k, x, 0.0), axis=-1, keepdims=True) * (1.0 / IDX_DIM)
    d = jnp.where(is_k, x - mu, 0.0)
    var = jnp.sum(d * d, axis=-1, keepdims=True) * (1.0 / IDX_DIM)
    kn = d * lax.rsqrt(var + LN_EPS) * lnw_ref[...] + lnb_ref[...]
    kr = _rope(kn, ti, IDX_ROT_DIM // 2)
    kr = jnp.where(is_k, kr, 0.0)
    kio_ref[...] = jnp.where(is_k, kr, x * (IDX_HEADS ** -0.5))
    kid_ref[...] = (kr + pltpu.roll(kr, IDX_DIM, 1)).astype(BF16)


def _prep(z, zs, tab_a, tab_i, ln_w, ln_b, tm):
    m = z.shape[0]
    row = lambda w, c0: pl.BlockSpec((tm, w), lambda i: (i, c0 // w))
    outs = [(ATT_WIDTH, BF16), (KV_WIDTH, F32), (KV_WIDTH, BF16), (KV_WIDTH, BF16),
            (IDX_HEADS * IDX_DIM, BF16), (LANES, F32), (LANES, BF16)]
    return pl.pallas_call(
        _prep_body,
        grid=(m // tm,),
        in_specs=[row(ATT_WIDTH, A_Q), row(KV_WIDTH, A_KA), row(KV_WIDTH, A_VA), row(IDX_HEADS * IDX_DIM, A_IQ),
                  row(LANES, S_IKW),
                  pl.BlockSpec((3, tm, LANES), lambda i: (0, i, 0)),
                  pl.BlockSpec((3, tm, LANES), lambda i: (0, i, 0)),
                  pl.BlockSpec((1, LANES), lambda i: (0, 0)),
                  pl.BlockSpec((1, LANES), lambda i: (0, 0))],
        out_specs=[pl.BlockSpec((tm, w), lambda i: (i, 0)) for w, _ in outs]
        + [pl.BlockSpec((KV_WIDTH, tm), lambda i: (0, i))],
        out_shape=[jax.ShapeDtypeStruct((m, w), dt) for w, dt in outs] + [jax.ShapeDtypeStruct((KV_WIDTH, m), BF16)],
        compiler_params=_cparams(("parallel",)),
        name="prep",
    )(z, z, z, z, zs, tab_a, tab_i, ln_w, ln_b)


DSA_QB = 128
DSA_TK = 512
DSA_ATK = 512


def _float_key(s):
    b = pltpu.bitcast(s, I32)
    return b ^ ((b >> 31) & 0x7FFFFFFF)


def _fold_rows(x, op):
    n = x.shape[0]
    while n > SUBLANES:
        n //= 2
        x = op(x[0:n], x[n:2 * n])
    return x


def _kth_threshold(count_ge, topk):
    def step(b, thr):
        cand = thr + jnp.left_shift(jnp.int32(1), 31 - b)
        return jnp.where(count_ge(cand) >= topk, cand, thr)
    return lax.fori_loop(0, 32, step, jnp.full((1, DSA_QB), INT_MIN, I32))


def _dsa_prompt_body(q_ref, kb_ref, vbt_ref, qi_ref, kid_ref, kiw_ref, o_ref, keys_ref, jcut_ref, acc_ref, qs_ref,
                     s_ref, p_ref, *, topk, seq):
    i = pl.program_id(1)
    nt = (i * DSA_QB + DSA_QB + DSA_TK - 1) // DSA_TK
    qpos = i * DSA_QB + lax.broadcasted_iota(I32, (1, DSA_QB), 1)
    row0 = lax.broadcasted_iota(I32, (DSA_TK, 1), 0)
    lane = lax.broadcasted_iota(I32, (1, LANES), 1)
    w_t = (kiw_ref[...] * (IDX_DIM ** -0.5)).T

    def score_tile(t, carry):
        kd = kid_ref[pl.ds(t * DSA_TK, DSA_TK), :]
        s = jnp.zeros((DSA_TK, DSA_QB), F32)
        for h in range(IDX_HEADS):
            qt = qi_ref[:, (h // 2) * LANES:(h // 2 + 1) * LANES]
            qh = jnp.where((lane // IDX_DIM) == (h % 2), qt, jnp.zeros_like(qt))
            s = s + w_t[IDX_DIM + h:IDX_DIM + h + 1, :] * jnp.maximum(_dot_nt(kd, qh), 0.0)
        valid = (t * DSA_TK + row0) <= qpos
        keys_ref[pl.ds(t * DSA_TK, DSA_TK), :] = jnp.where(valid, _float_key(s), INT_MIN)
        return carry

    lax.fori_loop(0, nt, score_tile, 0)

    def count(pred):
        def body(t, acc):
            kt = keys_ref[pl.ds(t * DSA_TK, DSA_TK), :]
            hit = pred(kt, t * DSA_TK + row0).astype(I32)
            return acc + _fold_rows(hit, jnp.add)
        acc = lax.fori_loop(0, nt, body, jnp.zeros((SUBLANES, DSA_QB), I32))
        return jnp.sum(acc, axis=0, keepdims=True)

    thr = _kth_threshold(lambda c: count(lambda kt, col: kt >= c), topk)
    n_gt = count(lambda kt, col: kt > thr)
    n_eq = count(lambda kt, col: (kt == thr) & (col <= qpos))
    need = topk - n_gt
    jcut_ref[...] = jnp.full(jcut_ref.shape, seq, I32)
    excess = (n_eq > need) & (thr > INT_MIN)

    @pl.when(jnp.max(excess.astype(I32)) > 0)
    def _():
        def step(b, jm):
            cand = jm + jnp.left_shift(jnp.int32(1), 30 - b)
            c = count(lambda kt, col: (kt == thr) & (col <= qpos) & (col < cand))
            return jnp.where(c < need, cand, jm)
        jm = lax.fori_loop(0, 31, step, jnp.zeros((1, DSA_QB), I32))
        jcut_ref[...] = jnp.broadcast_to(jnp.where(excess, jm, seq), jcut_ref.shape)

    jcut = jcut_ref[0:1, :]

    acc_ref[...] = jnp.zeros_like(acc_ref)
    for h in range(N_HEADS):
        qs_ref[h // GROUP, (h % GROUP) * DSA_QB:(h % GROUP + 1) * DSA_QB, :] = q_ref[:, h * HEAD_DIM:(h + 1) * HEAD_DIM]
    nta = (i * DSA_QB + DSA_QB + DSA_ATK - 1) // DSA_ATK
    rowa0 = lax.broadcasted_iota(I32, (DSA_ATK, 1), 0)

    def att_tile(t, carry):
        ms, ls = carry
        k0 = pl.multiple_of(t * DSA_ATK, DSA_ATK)
        kt = keys_ref[pl.ds(k0, DSA_ATK), :]
        kpos = k0 + rowa0
        sel = ((kt > thr) | ((kt == thr) & (kpos <= jcut))) & (kpos <= qpos)
        sel4 = jnp.concatenate([sel] * GROUP, axis=1)
        for g in range(N_KV_HEADS):
            k_t = kb_ref[pl.ds(k0, DSA_ATK), g * HEAD_DIM:(g + 1) * HEAD_DIM]
            s_ref[g] = _dot_nt(k_t, qs_ref[g])
        new_m, new_l, corrs = [], [], []
        for g in range(N_KV_HEADS):
            s = jnp.where(sel4, s_ref[g], -1e30)
            m_new = jnp.maximum(ms[g], jnp.max(_fold_rows(s, jnp.maximum), axis=0, keepdims=True))
            p = jnp.exp(s - m_new)
            corr = jnp.exp(ms[g] - m_new)
            new_l.append(ls[g] * corr + jnp.sum(_fold_rows(p, jnp.add), axis=0, keepdims=True))
            p_ref[g] = p.astype(BF16)
            new_m.append(m_new)
            corrs.append(corr)
        for g in range(N_KV_HEADS):
            vt_t = vbt_ref[g * HEAD_DIM:(g + 1) * HEAD_DIM, pl.ds(k0, DSA_ATK)]
            acc_ref[g] = acc_ref[g] * corrs[g] + _dot(vt_t, p_ref[g])
        return tuple(new_m), tuple(new_l)

    wq = GROUP * DSA_QB
    init = (tuple(jnp.full((1, wq), -1e29, F32) for _ in range(N_KV_HEADS)),
            tuple(jnp.zeros((1, wq), F32) for _ in range(N_KV_HEADS)))
    ms, ls = lax.fori_loop(0, nta, att_tile, init)
    for h in range(N_HEADS):
        g, c = h // GROUP, (h % GROUP) * DSA_QB
        o_t = acc_ref[g, :, c:c + DSA_QB] / ls[g][:, c:c + DSA_QB]
        o_ref[:, h * HEAD_DIM:(h + 1) * HEAD_DIM] = o_t.T.astype(o_ref.dtype)


def _dsa_prompt(qb, kb, vbt, qib, kid, kiw, batch, seq, topk):
    nb = seq // DSA_QB
    body = functools.partial(_dsa_prompt_body, topk=topk, seq=seq)
    return pl.pallas_call(
        body,
        grid=(batch, nb),
        in_specs=[pl.BlockSpec((DSA_QB, ATT_WIDTH), lambda b, i: (b * nb + i, 0)),
                  pl.BlockSpec((seq, KV_WIDTH), lambda b, i: (b, 0)),
                  pl.BlockSpec((KV_WIDTH, seq), lambda b, i: (0, b)),
                  pl.BlockSpec((DSA_QB, IDX_HEADS * IDX_DIM), lambda b, i: (b * nb + i, 0)),
                  pl.BlockSpec((seq, LANES), lambda b, i: (b, 0)),
                  pl.BlockSpec((DSA_QB, LANES), lambda b, i: (b * nb + i, 0))],
        out_specs=pl.BlockSpec((DSA_QB, ATT_WIDTH), lambda b, i: (b * nb + i, 0)),
        out_shape=jax.ShapeDtypeStruct((batch * seq, ATT_WIDTH), BF16),
        scratch_shapes=[pltpu.VMEM((seq, DSA_QB), I32), pltpu.VMEM((SUBLANES, DSA_QB), I32),
                        pltpu.VMEM((N_KV_HEADS, HEAD_DIM, GROUP * DSA_QB), F32),
                        pltpu.VMEM((N_KV_HEADS, GROUP * DSA_QB, HEAD_DIM), BF16),
                        pltpu.VMEM((N_KV_HEADS, DSA_ATK, GROUP * DSA_QB), F32),
                        pltpu.VMEM((N_KV_HEADS, DSA_ATK, GROUP * DSA_QB), BF16)],
        compiler_params=_cparams(("parallel", "arbitrary")),
        name="dsa_prompt",
    )(qb, kb, vbt, qib, kid, kiw)


def _sel_sample_body(pt_ref, qi_ref, wi_ref, ks_ref, ck_hbm, pos_ref, ms_ref, kbuf, sc_ref, jm_ref, rk_ref, sem, *,
                     topk, npg):
    s = pl.program_id(0)
    slot = s % 2
    U = SEL_SPS

    def page_copy(step, u, p, sl):
        return pltpu.make_async_copy(ck_hbm.at[pt_ref[step * U + u, p]], kbuf.at[sl, u, p], sem.at[sl])

    def request(step, sl):
        for u in range(U):
            for p in range(npg):
                page_copy(step, u, p, sl).start()

    @pl.when(s == 0)
    def _():
        request(0, 0)

    @pl.when(s + 1 < pl.num_programs(0))
    def _():
        request(s + 1, 1 - slot)

    for u in range(U):
        for p in range(npg):
            page_copy(s, u, p, slot).wait()

    lane = lax.broadcasted_iota(I32, (1, PAGE_SIZE), 1)
    pos = lax.broadcasted_iota(I32, (npg, PAGE_SIZE), 0) * PAGE_SIZE + lane

    def total(x):
        return jnp.sum(jnp.sum(x.astype(I32), axis=1, keepdims=True), axis=0, keepdims=True)

    keys, k_self = [], []
    for u in range(U):
        qi, wi = qi_ref[u], wi_ref[u] * (IDX_DIM ** -0.5)
        for c in range(npg // SEL_CP):
            kt = jnp.concatenate([kbuf[slot, u, c * SEL_CP + r] for r in range(SEL_CP)], axis=1).astype(BF16)
            sc = jnp.sum(wi * jnp.maximum(_dot(qi, kt), 0.0), axis=0, keepdims=True)
            for r in range(SEL_CP):
                sc_ref[u, c * SEL_CP + r:c * SEL_CP + r + 1, :] = sc[:, r * PAGE_SIZE:(r + 1) * PAGE_SIZE]
        keys.append(_float_key(sc_ref[u]))
        d = jnp.sum(qi.astype(F32) * ks_ref[u].astype(F32), axis=-1, keepdims=True)
        k_self.append(_float_key(jnp.sum(wi * jnp.maximum(d, 0.0), axis=0, keepdims=True)))

    def step(b, thrs):
        out = []
        for u in range(U):
            cand = thrs[u] + jnp.left_shift(jnp.int32(1), 31 - b)
            c = total(keys[u] >= cand) + (k_self[u] >= cand).astype(I32)
            out.append(jnp.where(c >= topk, cand, thrs[u]))
        return tuple(out)

    thrs = lax.fori_loop(0, 32, step, tuple(jnp.full((1, 1), INT_MIN, I32) for _ in range(U)))

    ri = lax.broadcasted_iota(I32, (PAGE_SIZE, PAGE_SIZE), 0)
    ci = lax.broadcasted_iota(I32, (PAGE_SIZE, PAGE_SIZE), 1)
    pr_ = lax.broadcasted_iota(I32, (npg, npg), 0)
    pc_ = lax.broadcasted_iota(I32, (npg, npg), 1)
    jcol = lax.broadcasted_iota(I32, (topk, PAGE_SIZE), 0)
    lane_f = lax.broadcasted_iota(I32, (topk, PAGE_SIZE), 1).astype(F32)
    ones8 = jnp.ones((SUBLANES, PAGE_SIZE), BF16)
    for u in range(U):
        thr = thrs[u]
        need = topk - total(keys[u] > thr) - (k_self[u] > thr).astype(I32)
        eq = keys[u] == thr
        jm_ref[u] = jnp.full((SUBLANES, LANES), npg * PAGE_SIZE, I32)

        @pl.when(jnp.max((total(eq) > need).astype(I32)) > 0)
        def _():
            def jstep(b, jm):
                cand = jm + jnp.left_shift(jnp.int32(1), 30 - b)
                return jnp.where(total(eq & (pos < cand)) < need, cand, jm)
            jm_ref[u] = jnp.broadcast_to(lax.fori_loop(0, 31, jstep, jnp.zeros((1, 1), I32)), (SUBLANES, LANES))

        jm = jm_ref[u, 0:1, 0:1]
        sel = (keys[u] > thr) | (eq & (pos <= jm))
        self_sel = (k_self[u] > thr) | ((k_self[u] == thr) & (total(eq & (pos <= jm)) < need))
        ms_ref[u] = jnp.broadcast_to(self_sel.astype(F32), (1, LANES))

        sel_b = sel.astype(BF16)
        within = _dot(sel_b, (ri <= ci).astype(BF16))
        tot = _dot(sel_b, jnp.ones((PAGE_SIZE, PAGE_SIZE), BF16))
        before = _dot((pc_ < pr_).astype(BF16), tot.astype(BF16))
        rk_ref[u] = jnp.where(sel, (before + within).astype(I32) - 1, -1)

        def gather_pos(p, carry):
            hi, lo = carry
            hit = jnp.broadcast_to(rk_ref[u, pl.ds(p, 1), :], (topk, PAGE_SIZE)) == jcol
            return hi + jnp.where(hit, jnp.asarray(p, F32), 0.0), lo + jnp.where(hit, lane_f, 0.0)

        zero = jnp.zeros((topk, PAGE_SIZE), F32)
        hi, lo = lax.fori_loop(0, npg, gather_pos, (zero, zero))
        pos_row = _dot_nt(ones8, hi.astype(BF16)) * PAGE_SIZE + _dot_nt(ones8, lo.astype(BF16))
        pos_ref[u] = pos_row[0:1].astype(I32)


SEL_CP = 8
SEL_SPS = 4


def _sel_sample(page_table, qi, wi, kself, cache_kt, topk):
    n, npg = page_table.shape
    U = SEL_SPS
    assert npg % SEL_CP == 0 and npg <= PAGE_SIZE and n % U == 0
    grid_spec = pltpu.PrefetchScalarGridSpec(
        num_scalar_prefetch=1,
        grid=(n // U,),
        in_specs=[pl.BlockSpec((U, IDX_HEADS, IDX_DIM), lambda s, pt: (s, 0, 0)),
                  pl.BlockSpec((U, IDX_HEADS, 1), lambda s, pt: (s, 0, 0)),
                  pl.BlockSpec((U, 1, IDX_DIM), lambda s, pt: (s, 0, 0)),
                  pl.BlockSpec(memory_space=pl.ANY)],
        out_specs=[pl.BlockSpec((U, 1, topk), lambda s, pt: (s, 0, 0)),
                   pl.BlockSpec((U, 1, LANES), lambda s, pt: (s, 0, 0))],
        scratch_shapes=[pltpu.VMEM((2, U, npg, IDX_DIM, PAGE_SIZE), F32), pltpu.VMEM((U, npg, PAGE_SIZE), F32),
                        pltpu.VMEM((U, SUBLANES, LANES), I32), pltpu.VMEM((U, npg, PAGE_SIZE), I32),
                        pltpu.SemaphoreType.DMA((2,))],
    )
    return pl.pallas_call(
        functools.partial(_sel_sample_body, topk=topk, npg=npg),
        grid_spec=grid_spec,
        out_shape=[jax.ShapeDtypeStruct((n, 1, topk), I32), jax.ShapeDtypeStruct((n, 1, LANES), F32)],
        compiler_params=_cparams(("arbitrary",)),
        name="sel_sample",
    )(page_table, qi, wi, kself, cache_kt)


def _att_sel_body(pt_ref, pos_ref, q_ref, ms_ref, ks_ref, vs_ref, ck_hbm, cv_hbm, o_ref, kbuf, vbuf, sem, *, topk):
    s = pl.program_id(0)
    slot = s % 2

    def request(seq, sl):
        def body(j, c):
            pos = pos_ref[seq, j]
            pg = pt_ref[seq, pos // PAGE_SIZE]
            r = pos % PAGE_SIZE
            pltpu.make_async_copy(ck_hbm.at[pg, r], kbuf.at[sl, j], sem.at[0, sl]).start()
            pltpu.make_async_copy(cv_hbm.at[pg, r], vbuf.at[sl, j], sem.at[1, sl]).start()
            return c
        lax.fori_loop(0, topk, body, 0, unroll=8)

    @pl.when(s == 0)
    def _():
        request(0, 0)

    @pl.when(s + 1 < pl.num_programs(0))
    def _():
        request(s + 1, 1 - slot)

    for h in range(topk // PAGE_SIZE):
        rows = pl.ds(h * PAGE_SIZE, PAGE_SIZE)
        pltpu.make_async_copy(ck_hbm.at[0], kbuf.at[slot, rows], sem.at[0, slot]).wait()
        pltpu.make_async_copy(cv_hbm.at[0], vbuf.at[slot, rows], sem.at[1, slot]).wait()

    q = q_ref[0]
    row_g = lax.broadcasted_iota(I32, (N_HEADS, 1), 0) // GROUP
    lane_g = lax.broadcasted_iota(I32, (1, KV_WIDTH), 1) // HEAD_DIM
    q_bd = jnp.where(row_g == lane_g, jnp.tile(q, (1, N_KV_HEADS)), jnp.zeros((N_HEADS, KV_WIDTH), BF16))
    k2 = jnp.concatenate([kbuf[slot, :, g, :] for g in range(N_KV_HEADS)], axis=1).astype(BF16)
    v2 = jnp.concatenate([vbuf[slot, :, g, :] for g in range(N_KV_HEADS)], axis=1).astype(BF16)
    self_row = ms_ref[0]
    self_f = self_row[:, 0:1]
    n_past = topk - jnp.tile(self_row, (1, topk // LANES))
    valid = lax.broadcasted_iota(I32, (1, topk), 1).astype(F32) < n_past
    sc = jnp.where(valid, _dot_nt(q_bd, k2), -1e30)
    s1 = jnp.sum(q.astype(F32) * ks_ref[0].astype(F32), axis=-1, keepdims=True)
    s1 = jnp.where(self_f > 0.5, s1, -1e30)
    m = jnp.maximum(jnp.max(sc, axis=-1, keepdims=True), jnp.maximum(s1, -1e29))
    pr = jnp.exp(sc - m)
    p1 = jnp.exp(s1 - m)
    l = jnp.sum(pr, axis=-1, keepdims=True) + p1
    pv = _dot(pr.astype(BF16), v2)
    own = jnp.zeros((N_HEADS, HEAD_DIM), F32)
    for g in range(N_KV_HEADS):
        own = jnp.where(row_g == g, pv[:, g * HEAD_DIM:(g + 1) * HEAD_DIM], own)
    o_ref[0] = ((own + p1.astype(BF16).astype(F32) * vs_ref[0].astype(F32)) / l).astype(o_ref.dtype)


def _att_sel(page_table, pos_list, q, mself, kself, vself, cache_k, cache_v):
    n, topk = pos_list.shape
    assert topk % PAGE_SIZE == 0
    seqspec = lambda r, c: pl.BlockSpec((1, r, c), lambda s, pt, pos: (s, 0, 0))
    anyspec = pl.BlockSpec(memory_space=pl.ANY)
    rows = (2, topk, N_KV_HEADS, HEAD_DIM)
    grid_spec = pltpu.PrefetchScalarGridSpec(
        num_scalar_prefetch=2,
        grid=(n,),
        in_specs=[seqspec(N_HEADS, HEAD_DIM), seqspec(1, LANES), seqspec(N_HEADS, HEAD_DIM),
                  seqspec(N_HEADS, HEAD_DIM), anyspec, anyspec],
        out_specs=seqspec(N_HEADS, HEAD_DIM),
        scratch_shapes=[pltpu.VMEM(rows, F32), pltpu.VMEM(rows, F32), pltpu.SemaphoreType.DMA((2, 2))],
    )
    return pl.pallas_call(
        functools.partial(_att_sel_body, topk=topk),
        grid_spec=grid_spec,
        out_shape=jax.ShapeDtypeStruct((n, N_HEADS, HEAD_DIM), BF16),
        compiler_params=_cparams(("arbitrary",)),
        name="att_sel",
    )(page_table, pos_list, q, mself, kself, vself, cache_k, cache_v)


def _mem_att_prompt_body(q_ref, k_ref, v_ref, o_ref):
    scale = MEM_HEAD_DIM ** -0.5
    for h in range(MEM_HEADS):
        sl = slice(h * MEM_HEAD_DIM, (h + 1) * MEM_HEAD_DIM)
        s = _dot_nt(q_ref[:, sl], k_ref[:, sl]) * scale
        m = jnp.max(s, axis=-1, keepdims=True)
        e = jnp.exp(s - m)
        pr = e / jnp.sum(e, axis=-1, keepdims=True)
        o_ref[:, sl] = _dot(pr.astype(BF16), v_ref[:, sl]).astype(o_ref.dtype)


def _mem_att_prompt(mq, mk, mv, batch, seq, tq):
    m = mk.shape[0] // batch
    nb = seq // tq
    return pl.pallas_call(
        _mem_att_prompt_body,
        grid=(batch * nb,),
        in_specs=[pl.BlockSpec((tq, MEM_WIDTH), lambda i: (i, 0)),
                  pl.BlockSpec((m, MEM_WIDTH), lambda i: (i // nb, 0)),
                  pl.BlockSpec((m, MEM_WIDTH), lambda i: (i // nb, 0))],
        out_specs=pl.BlockSpec((tq, MEM_WIDTH), lambda i: (i, 0)),
        out_shape=jax.ShapeDtypeStruct((batch * seq, MEM_WIDTH), BF16),
        compiler_params=_cparams(("parallel",)),
        name="mem_att_prompt",
    )(mq, mk, mv)


def _mem_att_sample_body(q_ref, k_ref, v_ref, o_ref):
    scale = MEM_HEAD_DIM ** -0.5
    q = q_ref[0].astype(F32)
    for h in range(MEM_HEADS):
        sl = slice(h * MEM_HEAD_DIM, (h + 1) * MEM_HEAD_DIM)
        s = jnp.sum(k_ref[0, :, h, :] * q[:, sl], axis=-1, keepdims=True) * scale
        m = jnp.max(s, axis=0, keepdims=True)
        e = jnp.exp(s - m)
        pr = e / jnp.sum(e, axis=0, keepdims=True)
        o_ref[0, :, sl] = jnp.sum(pr * v_ref[0, :, h, :], axis=0, keepdims=True).astype(o_ref.dtype)


def _mem_att_sample(mq, mk, mv):
    n, m, nh, hd = mk.shape
    w = nh * hd
    return pl.pallas_call(
        _mem_att_sample_body,
        grid=(n,),
        in_specs=[pl.BlockSpec((1, 1, w), lambda s: (s, 0, 0)),
                  pl.BlockSpec((1, m, nh, hd), lambda s: (s, 0, 0, 0)),
                  pl.BlockSpec((1, m, nh, hd), lambda s: (s, 0, 0, 0))],
        out_specs=pl.BlockSpec((1, 1, w), lambda s: (s, 0, 0)),
        out_shape=jax.ShapeDtypeStruct((n, 1, w), BF16),
        compiler_params=_cparams(("parallel",)),
        name="mem_att_sample",
    )(mq, mk, mv)


def _router_body(x_ref, w_ref, b_ref, ei_ref, ew_ref, acc_ref):
    k = pl.program_id(1)

    @pl.when(k == 0)
    def _():
        acc_ref[...] = jnp.zeros_like(acc_ref)

    acc_ref[...] += _dot(x_ref[...], w_ref[...], HIGHEST)

    @pl.when(k == pl.num_programs(1) - 1)
    def _():
        lg = acc_ref[...] + b_ref[...]
        lane = lax.broadcasted_iota(I32, lg.shape, 1)
        neg = jnp.float32(-jnp.inf)
        is_g = lane < N_GROUPS
        glm = jnp.where(is_g, lg, neg)
        gmax = jnp.max(glm, axis=-1, keepdims=True)
        g_sel = jnp.min(jnp.where(glm == gmax, lane, LANES), axis=-1, keepdims=True)
        g_prob = 1.0 / jnp.sum(jnp.where(is_g, jnp.exp(lg - gmax), 0.0), axis=-1, keepdims=True)
        e_id = lane - N_GROUPS
        in_grp = (e_id >= 0) & (e_id < N_EXPERTS) & ((e_id // EXPERTS_PER_GROUP) == g_sel)
        el = jnp.where(in_grp, lg, neg)
        m1 = jnp.max(el, axis=-1, keepdims=True)
        i1 = jnp.min(jnp.where(in_grp & (el == m1), lane, LANES), axis=-1, keepdims=True)
        rest = in_grp & (lane != i1)
        el2 = jnp.where(rest, lg, neg)
        m2 = jnp.max(el2, axis=-1, keepdims=True)
        i2 = jnp.min(jnp.where(rest & (el2 == m2), lane, LANES), axis=-1, keepdims=True)
        t = jnp.exp(m2 - m1)
        w1 = g_prob / (1.0 + t)
        w2 = g_prob * t / (1.0 + t)
        ei_ref[...] = jnp.where(lane == 0, i1 - N_GROUPS, jnp.where(lane == 1, i2 - N_GROUPS, 0))
        ew_ref[...] = jnp.where(lane == 0, w1, jnp.where(lane == 1, w2, 0.0))


def _router(x, w, b, tm, tk):
    m, kd = x.shape
    return pl.pallas_call(
        _router_body,
        grid=(m // tm, kd // tk),
        in_specs=[pl.BlockSpec((tm, tk), lambda i, k: (i, k)),
                  pl.BlockSpec((tk, LANES), lambda i, k: (k, 0)),
                  pl.BlockSpec((1, LANES), lambda i, k: (0, 0))],
        out_specs=[pl.BlockSpec((tm, LANES), lambda i, k: (i, 0)), pl.BlockSpec((tm, LANES), lambda i, k: (i, 0))],
        out_shape=[jax.ShapeDtypeStruct((m, LANES), I32), jax.ShapeDtypeStruct((m, LANES), F32)],
        scratch_shapes=[pltpu.VMEM((tm, LANES), F32)],
        compiler_params=_cparams(("parallel", "arbitrary")),
        name="router",
    )(x, w, b)


MOE_BR = 128


def _expert_up_body(be_ref, nblk_ref, tok_ref, x_hbm, wg_ref, wu_ref, h_ref, wgb_ref, wub_ref, xbuf, sem):
    i = pl.program_id(0)
    n_used = nblk_ref[0]
    slot = i % 2
    changed = jnp.logical_or(i == 0, be_ref[i] != be_ref[jnp.maximum(i - 1, 0)])

    def request(blk, sl):
        def issue(r, c):
            pltpu.make_async_copy(x_hbm.at[pl.ds(tok_ref[blk * MOE_BR + r], 1)], xbuf.at[sl, pl.ds(r, 1)],
                                  sem.at[sl]).start()
            return c
        lax.fori_loop(0, MOE_BR, issue, 0, unroll=8)

    @pl.when(jnp.logical_and(i == 0, n_used > 0))
    def _():
        request(0, 0)

    @pl.when(i + 1 < n_used)
    def _():
        request(i + 1, 1 - slot)

    @pl.when(jnp.logical_and(i < n_used, changed))
    def _():
        wgb_ref[...] = wg_ref[0].astype(BF16)
        wub_ref[...] = wu_ref[0].astype(BF16)

    @pl.when(i < n_used)
    def _():
        pltpu.make_async_copy(x_hbm.at[pl.ds(0, MOE_BR)], xbuf.at[slot], sem.at[slot]).wait()
        x = xbuf[slot].astype(BF16)
        a = _dot(x, wgb_ref[...])
        u = _dot(x, wub_ref[...])
        h_ref[...] = (a * _sigmoid(a) * u).astype(h_ref.dtype)

    @pl.when(i >= n_used)
    def _():
        h_ref[...] = jnp.zeros_like(h_ref)


def _expert_up(block_e, nblk, row_token, x, w_gate, w_up):
    nr = row_token.shape[0]
    d = x.shape[1]
    nb = nr // MOE_BR
    blk = lambda i, nbk: jnp.minimum(i, nbk[0] - 1)
    grid_spec = pltpu.PrefetchScalarGridSpec(
        num_scalar_prefetch=3,
        grid=(nb,),
        in_specs=[pl.BlockSpec(memory_space=pl.ANY),
                  pl.BlockSpec((1, d, D_EXPERT), lambda i, be, nbk, tok: (be[blk(i, nbk)], 0, 0)),
                  pl.BlockSpec((1, d, D_EXPERT), lambda i, be, nbk, tok: (be[blk(i, nbk)], 0, 0))],
        out_specs=pl.BlockSpec((MOE_BR, D_EXPERT), lambda i, be, nbk, tok: (i, 0)),
        scratch_shapes=[pltpu.VMEM((d, D_EXPERT), BF16), pltpu.VMEM((d, D_EXPERT), BF16),
                        pltpu.VMEM((2, MOE_BR, d), F32), pltpu.SemaphoreType.DMA((2,))],
    )
    return pl.pallas_call(
        _expert_up_body,
        grid_spec=grid_spec,
        out_shape=jax.ShapeDtypeStruct((nr, D_EXPERT), BF16),
        compiler_params=_cparams(("arbitrary",)),
        name="expert_up",
    )(block_e, nblk, row_token, x, w_gate, w_up)


def _expert_down_body(be_ref, nblk_ref, h_ref, wd_ref, y_ref, wdb_ref):
    i = pl.program_id(0)
    changed = jnp.logical_or(i == 0, be_ref[i] != be_ref[jnp.maximum(i - 1, 0)])

    @pl.when(jnp.logical_and(i < nblk_ref[0], changed))
    def _():
        wdb_ref[...] = wd_ref[0].astype(BF16)

    @pl.when(i < nblk_ref[0])
    def _():
        y_ref[...] = _dot(h_ref[...], wdb_ref[...])

    @pl.when(i >= nblk_ref[0])
    def _():
        y_ref[...] = jnp.zeros_like(y_ref)


def _expert_down(block_e, nblk, h, w_down):
    nr = h.shape[0]
    d = w_down.shape[2]
    blk = lambda i, nbk: jnp.minimum(i, nbk[0] - 1)
    grid_spec = pltpu.PrefetchScalarGridSpec(
        num_scalar_prefetch=2,
        grid=(nr // MOE_BR,),
        in_specs=[pl.BlockSpec((MOE_BR, D_EXPERT), lambda i, be, nbk: (blk(i, nbk), 0)),
                  pl.BlockSpec((1, D_EXPERT, d), lambda i, be, nbk: (be[blk(i, nbk)], 0, 0))],
        out_specs=pl.BlockSpec((MOE_BR, d), lambda i, be, nbk: (i, 0)),
        scratch_shapes=[pltpu.VMEM((D_EXPERT, d), BF16)],
    )
    return pl.pallas_call(
        _expert_down_body,
        grid_spec=grid_spec,
        out_shape=jax.ShapeDtypeStruct((nr, d), F32),
        compiler_params=_cparams(("arbitrary",)),
        name="expert_down",
    )(block_e, nblk, h, w_down)


def _combine_ln_body(slot_ref, x_ref, y_hbm, ew_ref, g_ref, b_ref, op_ref, os_ref, ybuf, sem, *, npb, tm, mp):
    i = pl.program_id(0)
    sl = i % 2

    def request(blk, s_):
        def issue(r, c):
            t = blk * tm + r
            pltpu.make_async_copy(y_hbm.at[pl.ds(slot_ref[t], 1)], ybuf.at[s_, pl.ds(r, 1)], sem.at[s_]).start()
            pltpu.make_async_copy(y_hbm.at[pl.ds(slot_ref[mp + t], 1)], ybuf.at[s_, pl.ds(tm + r, 1)],
                                  sem.at[s_]).start()
            return c
        lax.fori_loop(0, tm, issue, 0, unroll=8)

    @pl.when(i == 0)
    def _():
        request(0, 0)

    @pl.when(i + 1 < pl.num_programs(0))
    def _():
        request(i + 1, 1 - sl)

    pltpu.make_async_copy(y_hbm.at[pl.ds(0, 2 * tm)], ybuf.at[sl], sem.at[sl]).wait()
    ew = ew_ref[...]
    ff = ybuf[sl, 0:tm] * ew[:, 0:1] + ybuf[sl, tm:2 * tm] * ew[:, 1:2]
    y = _layer_norm_rows(DEEPNORM_ALPHA * x_ref[...] + ff, g_ref[...], b_ref[...])

    @pl.when(i < npb)
    def _():
        op_ref[...] = y

    @pl.when(i >= npb)
    def _():
        os_ref[...] = y


def _combine_ln(x, y_rows, slot2, ew, g, b, tm, n_prompt):
    m, d = x.shape
    nb = m // tm
    npb = n_prompt // tm
    grid_spec = pltpu.PrefetchScalarGridSpec(
        num_scalar_prefetch=1,
        grid=(nb,),
        in_specs=[pl.BlockSpec((tm, d), lambda i, sl: (i, 0)),
                  pl.BlockSpec(memory_space=pl.ANY),
                  pl.BlockSpec((tm, LANES), lambda i, sl: (i, 0)),
                  pl.BlockSpec((1, d), lambda i, sl: (0, 0)),
                  pl.BlockSpec((1, d), lambda i, sl: (0, 0))],
        out_specs=[pl.BlockSpec((tm, d), lambda i, sl: (jnp.minimum(i, npb - 1), 0)),
                   pl.BlockSpec((tm, d), lambda i, sl: (jnp.maximum(i - npb, 0), 0))],
        scratch_shapes=[pltpu.VMEM((2, 2 * tm, d), F32), pltpu.SemaphoreType.DMA((2,))],
    )
    return pl.pallas_call(
        functools.partial(_combine_ln_body, npb=npb, tm=tm, mp=m),
        grid_spec=grid_spec,
        out_shape=[jax.ShapeDtypeStruct((n_prompt, d), F32), jax.ShapeDtypeStruct((m - n_prompt, d), F32)],
        compiler_params=_cparams(("arbitrary",)),
        name="combine_ln",
    )(slot2, x, y_rows, ew, g, b)


def _pad_cols(x, n):
    return jnp.pad(x, ((0, 0), (0, n - x.shape[1])))


def _split_w_in(w):
    o = [int(v) for v in np.cumsum([0, RW_PROJ, ATT_WIDTH + 2 * KV_WIDTH + IDX_HEADS * IDX_DIM, IDX_DIM + IDX_HEADS,
                                    2 * D_MODEL])]
    w_rkv = w[:, 0:3 * RW_WIDTH].astype(BF16)
    w_att = w[:, o[1]:o[2]].astype(BF16)
    w_gate = w[:, o[3]:o[4]].astype(BF16)
    w_small = jnp.concatenate([_lora_cols(w[:, 0:RW_PROJ]), _pad_cols(w[:, o[2]:o[3]], LANES)], axis=1).astype(BF16)
    return w_rkv, w_att, w_gate, w_small


def _lora_cols(x):
    return jnp.concatenate([_pad_cols(x[:, 6144:6240], 128), _pad_cols(x[:, 6240:6336], 128), x[:, 6336:6592]], axis=1)


def _pack_rwkv(rw_mu, rw_w0, rw_w2, rw_a0, rw_a2, rw_g2, rw_k_k, rw_k_a, rw_r_k, rw_ln_w, rw_ln_b):
    flat = lambda t: t.reshape(1, RW_WIDTH)
    mu = rw_mu.reshape(1, RW_PROJ)
    rows = [mu[:, 0:2048], mu[:, 2048:4096], mu[:, 4096:6144], flat(rw_w0), flat(rw_a0), flat(rw_k_k),
            flat(rw_k_a), flat(rw_r_k), flat(rw_ln_w), flat(rw_ln_b)]
    prm = jnp.pad(jnp.concatenate(rows, axis=0), ((0, 6), (0, 0)))
    mu_l = jnp.pad(_lora_cols(mu), ((0, 7), (0, 0)))
    w2 = jnp.pad(rw_w2, ((0, 128 - W_LORA), (0, 0))).astype(BF16)
    a2 = jnp.pad(rw_a2, ((0, 128 - A_LORA), (0, 0))).astype(BF16)
    g2 = rw_g2.astype(BF16)
    return prm, mu_l, w2, a2, g2


def _head_indicators(width):
    lane = np.arange(width)[:, None] // RW_HEAD_DIM
    ind = (lane == np.arange(128)[None, :]).astype(np.float32)
    return jnp.asarray(ind), jnp.asarray(ind.T)


def _head_selectors():
    sel = np.zeros((WKV_HQ, WKV_W, RW_HEAD_DIM), np.float32)
    for j in range(WKV_HQ):
        sel[j, j * RW_HEAD_DIM + np.arange(RW_HEAD_DIM), np.arange(RW_HEAD_DIM)] = 1.0
    return jnp.asarray(sel)


def kernel(x_prompt, x_sample, mem_prompt, cache_k, cache_v, cache_idx_k, page_table, state_wkv, state_shift, cache_mem_k, cache_mem_v, w_in, rw_mu, rw_w0, rw_w2, rw_a0, rw_a2, rw_g2, rw_k_k, rw_k_a, rw_r_k, rw_ln_w, rw_ln_b, idx_ln_w, idx_ln_b, w_branch_a, w_branch_b, w_out, ln1_w, ln1_b, w_mem_q, w_mem_k, w_mem_v, w_mem_o, ln2_w, ln2_b, w_router_grp, b_router_grp, w_router_exp, b_router_exp, w_exp_gate, w_exp_up, w_exp_down, ln3_w, ln3_b):
    B, S, D = x_prompt.shape
    DB, DS, _ = x_sample.shape
    assert DS == 1 and cache_k.shape[0] == 1
    TP = B * S
    T = TP + DB
    MP = _round_up(T, DENSE_TM)
    past = page_table.shape[1] * PAGE_SIZE
    n_mem = mem_prompt.shape[1]
    row1 = lambda a: a.reshape(1, -1)

    def pad_rows(a):
        return jnp.concatenate([a, jnp.zeros((MP - a.shape[0],) + a.shape[1:], a.dtype)], axis=0)

    x_all = pad_rows(jnp.concatenate([x_prompt.reshape(TP, D), x_sample.reshape(DB, D)], axis=0))
    xb = x_all.astype(BF16)
    w_rkv, w_att, w_gate, w_small = _split_w_in(w_in[0])
    z_rkv = _mm(xb, w_rkv, DENSE_TM, DENSE_TN, D, name="in_proj_rkv")
    z_att = _mm(xb, w_att, DENSE_TM, DENSE_TN, D, name="in_proj_att")
    z_gate = _mm(xb, w_gate, DENSE_TM, DENSE_TN, D, name="in_proj_gate")
    z_small = _mm(xb, w_small, DENSE_TM, S_TOTAL, D, name="in_proj_small")

    prm, mu_l, w2, a2, g2 = _pack_rwkv(rw_mu[0], rw_w0[0], rw_w2[0], rw_a0[0], rw_a2[0], rw_g2[0], rw_k_k[0],
                                       rw_k_a[0], rw_r_k[0], rw_ln_w[0], rw_ln_b[0])
    rw_p, wkv_p = _wkv_prompt(z_rkv, z_small, prm, mu_l, w2, a2, g2, _head_selectors(), B, S)
    ss = state_shift[0]
    ind_f, indt_f = _head_indicators(RW_WIDTH)
    tok = _wkv_tokens_sample(z_rkv, z_small, ss[:, 0:2048], ss[:, 2048:4096], ss[:, 4096:6144], _lora_cols(ss), prm,
                             mu_l, w2, a2, g2, ind_f, indt_f, TP, DB)
    t_r, t_w, t_al, t_be, t_km, t_vt, t_gt, t_bot = tok
    rowv = lambda a: a.reshape(DB, RW_HEADS, 1, RW_HEAD_DIM)
    y_col, wkv_s = _wkv_step(state_wkv[0], rowv(t_w), rowv(t_al), rowv(t_be), rowv(t_km), rowv(t_r), t_vt, t_gt,
                             t_bot, rw_ln_w[0].reshape(1, RW_HEADS, RW_HEAD_DIM, 1),
                             rw_ln_b[0].reshape(1, RW_HEADS, RW_HEAD_DIM, 1))
    rw_all = pad_rows(jnp.concatenate([rw_p, y_col.reshape(DB, RW_WIDTH).astype(BF16)], axis=0))

    pos = jnp.concatenate([jnp.tile(jnp.arange(S, dtype=I32), B), jnp.full((MP - TP,), past, I32)])
    tab_a = _rope_tables(pos, ROT_DIM, HEAD_DIM)
    tab_i = _rope_tables(pos, IDX_ROT_DIM, IDX_DIM)
    qb, k_rot, kb, vb, qib, kiw, kid, vbt = _prep(z_att, z_small, tab_a, tab_i, _pad_cols(row1(idx_ln_w[0]), LANES),
                                                  _pad_cols(row1(idx_ln_b[0]), LANES), ROW_TM)
    att_p = _dsa_prompt(qb, kb, vbt, qib, kid, kiw, B, S, min(TOPK_MAX, S // 4))
    qi_s = qib[TP:T].reshape(DB, IDX_HEADS, IDX_DIM)
    wi_s = kiw[TP:T, IDX_DIM:IDX_DIM + IDX_HEADS].reshape(DB, IDX_HEADS, 1)
    topk_s = min(TOPK_MAX, (past + DS) // 4)
    pos_sel, mself = _sel_sample(page_table, qi_s, wi_s, kid[TP:T, 0:IDX_DIM].reshape(DB, 1, IDX_DIM),
                                 jnp.swapaxes(cache_idx_k[0], 1, 2), topk_s)
    expand = lambda a: jnp.repeat(a[TP:T].reshape(DB, N_KV_HEADS, HEAD_DIM), GROUP, axis=1)
    att_s = _att_sel(page_table, pos_sel.reshape(DB, topk_s), qb[TP:T].reshape(DB, N_HEADS, HEAD_DIM), mself,
                     expand(kb), expand(vb), cache_k[0], cache_v[0])
    att_all = pad_rows(jnp.concatenate([att_p, att_s.reshape(DB, ATT_WIDTH)], axis=0))

    merged = _branch_merge(rw_all, att_all, w_branch_a[0].astype(BF16), w_branch_b[0].astype(BF16), z_gate, DENSE_TM,
                           DENSE_TN)
    x1, x1b = _mm_ln(merged, w_out[0].astype(BF16), x_all, row1(ln1_w[0]), row1(ln1_b[0]), LN_TM, LN_TN,
                     name="out_ln1")

    mq = _mm(x1b, w_mem_q[0].astype(BF16), DENSE_TM, MEM_WIDTH, D, out_dtype=BF16, name="mem_q")
    mem2d = mem_prompt.reshape(B * n_mem, D).astype(BF16)
    mem_k = _mm(mem2d, w_mem_k[0].astype(BF16), B * n_mem, MEM_WIDTH, D, name="mem_k")
    mem_v = _mm(mem2d, w_mem_v[0].astype(BF16), B * n_mem, MEM_WIDTH, D, name="mem_v")
    ma_p = _mem_att_prompt(mq, mem_k.astype(BF16), mem_v.astype(BF16), B, S, 512)
    ma_s = _mem_att_sample(mq[TP:T].reshape(DB, 1, MEM_WIDTH), cache_mem_k[0], cache_mem_v[0])
    ma_all = pad_rows(jnp.concatenate([ma_p, ma_s.reshape(DB, MEM_WIDTH)], axis=0))
    x2, _ = _mm_ln(ma_all, w_mem_o[0].astype(BF16), x1, row1(ln2_w[0]), row1(ln2_b[0]), LN_TM, LN_TN,
                   name="mem_o_ln2")

    w_r = _pad_cols(jnp.concatenate([w_router_grp[0], w_router_exp[0]], axis=1), LANES)
    b_r = _pad_cols(row1(jnp.concatenate([b_router_grp[0], b_router_exp[0]])), LANES)
    e_idx, e_w = _router(x2, w_r, b_r, DENSE_TM, DENSE_TN)
    n_assign = 2 * T
    flat_e = e_idx[:T, 0:2].reshape(n_assign)
    order = jnp.argsort(flat_e).astype(I32)
    rank = jnp.argsort(order).astype(I32)
    experts = jnp.arange(N_EXPERTS, dtype=I32)
    onehot = flat_e[:, None] == experts[None, :]
    counts = jnp.sum(onehot, axis=0, dtype=I32)
    padded = (counts + MOE_BR - 1) // MOE_BR * MOE_BR
    pad_end = jnp.cumsum(padded)
    pad_start = pad_end - padded
    start = jnp.cumsum(counts) - counts
    slot = (rank + jnp.sum(jnp.where(onehot, (pad_start - start)[None, :], 0), axis=1)).reshape(T, 2)
    n_blocks = -(-n_assign // MOE_BR) + N_EXPERTS
    blk_row0 = jnp.arange(n_blocks, dtype=I32) * MOE_BR
    block_e = jnp.minimum(jnp.sum(pad_end[None, :] <= blk_row0[:, None], axis=1, dtype=I32), N_EXPERTS - 1)
    blk_hot = block_e[:, None] == experts[None, :]
    pick = lambda tab: jnp.sum(jnp.where(blk_hot, tab[None, :], 0), axis=1)
    j_in_e = (blk_row0 - pick(pad_start))[:, None] + jnp.arange(MOE_BR, dtype=I32)[None, :]
    src = jnp.clip(pick(start)[:, None] + j_in_e, 0, n_assign - 1)
    row_token = jnp.where(j_in_e < pick(counts)[:, None], order[src] // 2, 0).reshape(n_blocks * MOE_BR)
    n_used = (pad_end[-1] // MOE_BR).astype(I32).reshape(1)
    hid = _expert_up(block_e, n_used, row_token, x2, w_exp_gate[0], w_exp_up[0])
    y_rows = _expert_down(block_e, n_used, hid, w_exp_down[0])
    slot_pad = jnp.concatenate([jnp.pad(slot[:, 0], (0, MP - T)), jnp.pad(slot[:, 1], (0, MP - T))])
    y_p, y_s = _combine_ln(x2, y_rows, slot_pad, e_w, row1(ln3_w[0]), row1(ln3_b[0]), ROW_TM, TP)

    kv5 = lambda a, n, s: a.reshape(1, n, s, N_KV_HEADS, HEAD_DIM)
    va = z_att[:, A_VA:A_VA + KV_WIDTH]
    ki = kiw[:, 0:IDX_DIM]
    last = lambda a: jnp.concatenate([a[(b + 1) * S - 1:(b + 1) * S] for b in range(B)] + [a[TP:T]], axis=0)
    zl, zsl = last(z_rkv), last(z_small)
    shift_cols = jnp.concatenate([zl, zsl[:, S_LORA:S_LORA + W_LORA], zsl[:, S_LORA + 128:S_LORA + 128 + A_LORA],
                                  zsl[:, S_LORA + 256:S_LORA + 512]], axis=1)
    mem5 = lambda a: a.reshape(1, B, n_mem, MEM_HEADS, MEM_HEAD_DIM)
    return (y_p.reshape(B, S, D), y_s[:DB].reshape(DB, DS, D),
            kv5(k_rot[:TP], B, S), kv5(va[:TP], B, S), ki[:TP].reshape(1, B, S, IDX_DIM),
            wkv_p[None], shift_cols[:B][None], mem5(mem_k), mem5(mem_v),
            kv5(k_rot[TP:T], DB, DS), kv5(va[TP:T], DB, DS), ki[TP:T].reshape(1, DB, DS, IDX_DIM),
            wkv_s[None], shift_cols[B:][None])
```

```python
import functools
import math

import jax
import jax.numpy as jnp
import numpy as np
from jax import lax
from jax.experimental import pallas as pl
from jax.experimental.pallas import tpu as pltpu

F32 = jnp.float32
BF16 = jnp.bfloat16
I32 = jnp.int32
HIGHEST = lax.Precision.HIGHEST

D_MODEL = 4096
RW_HEAD_DIM = 64
RW_HEADS = 32
RW_WIDTH = 2048
W_LORA = 96
A_LORA = 96
G_LORA = 256
RW_PROJ = 3 * RW_WIDTH + W_LORA + A_LORA + G_LORA
RW_GN_EPS = 64e-5
HEAD_DIM = 128
N_HEADS = 16
N_KV_HEADS = 4
GROUP = 4
ATT_WIDTH = 2048
KV_WIDTH = 512
ROT_DIM = 32
ROPE_THETA = 500000.0
IDX_HEADS = 16
IDX_DIM = 64
IDX_ROT_DIM = 16
TOPK_MAX = 256
PAGE_SIZE = 128
MEM_HEADS = 4
MEM_HEAD_DIM = 128
MEM_WIDTH = 512
N_GROUPS = 8
EXPERTS_PER_GROUP = 8
N_EXPERTS = 64
D_EXPERT = 512
LN_EPS = 1e-5
DEEPNORM_ALPHA = 2.0 ** 0.25
EXP_M05 = math.exp(-0.5)

LANES = 128
SUBLANES = 8
VMEM_LIMIT = 56 * 1024 * 1024

DENSE_TM = 640
DENSE_TN = 1024
LN_TM = 320
LN_TN = 512
ROW_TM = 128

C_R, C_K, C_V = 0, 2048, 4096
A_Q, A_KA, A_VA, A_IQ = 0, 2048, 2560, 3072
G_A, G_B = 0, 4096
S_LORA, S_IKW, S_TOTAL = 0, 512, 640

INT_MIN = -(2 ** 31)


def _round_up(n, m):
    return -(-n // m) * m


def _cparams(sem):
    return pltpu.CompilerParams(dimension_semantics=sem, vmem_limit_bytes=VMEM_LIMIT)


def _dot(a, b, precision=None):
    return jnp.dot(a, b, preferred_element_type=F32, precision=precision)


def _dot_nt(a, b, precision=None):
    return lax.dot_general(a, b, (((1,), (1,)), ((), ())), preferred_element_type=F32, precision=precision)


def _sigmoid(x):
    return 1.0 / (1.0 + jnp.exp(-x))


def _mm_body(x_ref, w_ref, o_ref, acc_ref):
    k = pl.program_id(2)

    @pl.when(k == 0)
    def _():
        acc_ref[...] = jnp.zeros_like(acc_ref)

    acc_ref[...] += _dot(x_ref[...], w_ref[...])

    @pl.when(k == pl.num_programs(2) - 1)
    def _():
        o_ref[...] = acc_ref[...].astype(o_ref.dtype)


def _mm_fullk_body(x_ref, w_ref, o_ref):
    o_ref[...] = _dot(x_ref[...], w_ref[...]).astype(o_ref.dtype)


def _mm(x, w, tm, tn, tk, out_dtype=F32, name="mm"):
    m, kd = x.shape
    n = w.shape[1]
    if tk == kd:
        return pl.pallas_call(
            _mm_fullk_body,
            grid=(m // tm, n // tn),
            in_specs=[pl.BlockSpec((tm, kd), lambda i, j: (i, 0)),
                      pl.BlockSpec((kd, tn), lambda i, j: (0, j))],
            out_specs=pl.BlockSpec((tm, tn), lambda i, j: (i, j)),
            out_shape=jax.ShapeDtypeStruct((m, n), out_dtype),
            compiler_params=_cparams(("parallel", "parallel")),
            name=name,
        )(x, w)
    return pl.pallas_call(
        _mm_body,
        grid=(m // tm, n // tn, kd // tk),
        in_specs=[pl.BlockSpec((tm, tk), lambda i, j, k: (i, k)),
                  pl.BlockSpec((tk, tn), lambda i, j, k: (k, j))],
        out_specs=pl.BlockSpec((tm, tn), lambda i, j, k: (i, j)),
        out_shape=jax.ShapeDtypeStruct((m, n), out_dtype),
        scratch_shapes=[pltpu.VMEM((tm, tn), F32)],
        compiler_params=_cparams(("parallel", "parallel", "arbitrary")),
        name=name,
    )(x, w)


def _layer_norm_rows(x, g, b):
    mu = jnp.mean(x, axis=-1, keepdims=True)
    d = x - mu
    var = jnp.mean(d * d, axis=-1, keepdims=True)
    return d * lax.rsqrt(var + LN_EPS) * g + b


def _mm_ln_body(x_ref, w_ref, res_ref, g_ref, b_ref, o_ref, ob_ref, y_ref, *, tn):
    j = pl.program_id(1)
    y_ref[:, pl.ds(pl.multiple_of(j * tn, tn), tn)] = _dot(x_ref[...], w_ref[...])

    @pl.when(j == pl.num_programs(1) - 1)
    def _():
        y = _layer_norm_rows(DEEPNORM_ALPHA * res_ref[...] + y_ref[...], g_ref[...], b_ref[...])
        o_ref[...] = y
        ob_ref[...] = y.astype(BF16)


def _mm_ln(x, w, res, g, b, tm, tn, name="mm_ln"):
    m, kd = x.shape
    n = w.shape[1]
    return pl.pallas_call(
        functools.partial(_mm_ln_body, tn=tn),
        grid=(m // tm, n // tn),
        in_specs=[pl.BlockSpec((tm, kd), lambda i, j: (i, 0)),
                  pl.BlockSpec((kd, tn), lambda i, j: (0, j)),
                  pl.BlockSpec((tm, n), lambda i, j: (i, 0)),
                  pl.BlockSpec((1, n), lambda i, j: (0, 0)),
                  pl.BlockSpec((1, n), lambda i, j: (0, 0))],
        out_specs=[pl.BlockSpec((tm, n), lambda i, j: (i, 0)),
                   pl.BlockSpec((tm, n), lambda i, j: (i, 0))],
        out_shape=[jax.ShapeDtypeStruct((m, n), F32), jax.ShapeDtypeStruct((m, n), BF16)],
        scratch_shapes=[pltpu.VMEM((tm, n), F32)],
        compiler_params=_cparams(("parallel", "arbitrary")),
        name=name,
    )(x, w, res, g, b)


def _branch_merge_body(rw_ref, at_ref, wa_ref, wb_ref, ga_ref, gb_ref, o_ref):
    a = _dot(rw_ref[...], wa_ref[...])
    b = _dot(at_ref[...], wb_ref[...])
    o_ref[...] = (_sigmoid(ga_ref[...]) * a + _sigmoid(gb_ref[...]) * b).astype(o_ref.dtype)


def _branch_merge(rw, att, wa, wb, z, tm, tn):
    m = rw.shape[0]
    n = wa.shape[1]
    ga0, gb0 = G_A // tn, G_B // tn
    return pl.pallas_call(
        _branch_merge_body,
        grid=(m // tm, n // tn),
        in_specs=[pl.BlockSpec((tm, RW_WIDTH), lambda i, j: (i, 0)),
                  pl.BlockSpec((tm, ATT_WIDTH), lambda i, j: (i, 0)),
                  pl.BlockSpec((RW_WIDTH, tn), lambda i, j: (0, j)),
                  pl.BlockSpec((ATT_WIDTH, tn), lambda i, j: (0, j)),
                  pl.BlockSpec((tm, tn), lambda i, j: (i, ga0 + j)),
                  pl.BlockSpec((tm, tn), lambda i, j: (i, gb0 + j))],
        out_specs=pl.BlockSpec((tm, tn), lambda i, j: (i, j)),
        out_shape=jax.ShapeDtypeStruct((m, n), BF16),
        compiler_params=_cparams(("parallel", "parallel")),
        name="branch_merge",
    )(rw, att, wa, wb, z, z)


def _seg_sum(x, ind, ind_t):
    return _dot(_dot(x, ind, HIGHEST), ind_t, HIGHEST)


def _split_bf16(x, parts):
    out = []
    for _ in range(parts):
        t = x.astype(BF16)
        out.append(t)
        x = x - t.astype(F32)
    return out


def _seg_sum_quads(x, bd):
    outs = []
    for q in range(x.shape[1] // WKV_W):
        hi, lo = _split_bf16(x[:, q * WKV_W:(q + 1) * WKV_W], 2)
        outs.append(_dot(hi, bd) + _dot(lo, bd))
    return jnp.concatenate(outs, axis=1) if len(outs) > 1 else outs[0]


def _rwkv_tokens(zr, zk, zv, zl, pr, pk, pv, plo, prm, mu_l, w2, a2, g2, seg):
    r = zr + (pr - zr) * prm[0:1]
    kx = zk + (pk - zk) * prm[1:2]
    v = zv + (pv - zv) * prm[2:3]
    zsl = zl + (plo - zl) * mu_l
    tw = jnp.tanh(zsl[:, 0:128]).astype(BF16)
    xw = prm[3:4] + _dot(tw, w2)
    lw = -EXP_M05 * _sigmoid(xw)
    a = _sigmoid(prm[4:5] + _dot(zsl[:, 128:256].astype(BF16), a2))
    g = _dot(_sigmoid(zsl[:, 256:512]).astype(BF16), g2)
    kk = kx * prm[5:6]
    n2 = seg(kk * kk)
    kkn = kk / jnp.maximum(jnp.sqrt(n2), 1e-12)
    kmod = kx * (1.0 + (a - 1.0) * prm[6:7])
    return r, lw, kmod, v, kkn, a, g


def _rwkv_post(y, r, kmod, v, g, prm, seg):
    inv_n = 1.0 / RW_HEAD_DIM
    mean = seg(y) * inv_n
    d = y - mean
    var = seg(d * d) * inv_n
    yn = d * lax.rsqrt(var + RW_GN_EPS) * prm[8:9] + prm[9:10]
    bonus = seg(r * kmod * prm[7:8]) * v
    return (yn + bonus) * g


WKV_C = 64
WKV_HQ = 4
WKV_W = WKV_HQ * RW_HEAD_DIM
WKV_QPS = 8


def _wkv_chunk_body(zr_ref, zk_ref, zv_ref, zl_ref, prm_ref, mul_ref, w2_ref, a2_ref, g2_ref,
                    sel_ref, o_ref, so_ref, s_ref, cr_ref, ck_ref, cv_ref, cl_ref):
    c = pl.program_id(2)
    C = WKV_C
    W = WKV_W

    @pl.when(c == 0)
    def _():
        s_ref[...] = jnp.zeros_like(s_ref)
        cr_ref[...] = jnp.zeros_like(cr_ref)
        ck_ref[...] = jnp.zeros_like(ck_ref)
        cv_ref[...] = jnp.zeros_like(cv_ref)
        cl_ref[...] = jnp.zeros_like(cl_ref)

    rows = lax.broadcasted_iota(I32, (C, 1), 0)

    def shifted(z, carry_ref):
        prev = jnp.where(rows == 0, carry_ref[0:1, :], pltpu.roll(z, 1, 0))
        carry_ref[0:1, :] = z[C - 1:C, :]
        return prev

    zr, zk, zv, zl = zr_ref[...], zk_ref[...], zv_ref[...], zl_ref[...]
    pr, pk, pv, plo = shifted(zr, cr_ref), shifted(zk, ck_ref), shifted(zv, cv_ref), shifted(zl, cl_ref)
    prm = prm_ref[...]
    lane_head = lax.broadcasted_iota(I32, (1, W), 1) // RW_HEAD_DIM
    hv = lax.broadcasted_iota(I32, (W, 1), 0) // RW_HEAD_DIM
    bd = (hv == lane_head).astype(BF16)
    seg = lambda x: _seg_sum_quads(x, bd)
    r, lw, kmod, v, kkn, a, g = _rwkv_tokens(zr, zk, zv, zl, pr, pk, pv, plo, prm, mul_ref[0:1, :],
                                            w2_ref[...], a2_ref[...], g2_ref[...], seg)
    al = -kkn
    be = kkn * a

    ti = lax.broadcasted_iota(I32, (C, C), 0)
    tj = lax.broadcasted_iota(I32, (C, C), 1)
    tri = (tj <= ti).astype(BF16)
    cum = sum(_dot(tri, part) for part in _split_bf16(lw, 3))
    cum_l = cum[C - 1:C, :]
    p_inv = jnp.exp(-cum)
    p_rel = jnp.exp(cum_l - cum)
    ab = al * jnp.exp(cum - lw)
    rb = r * jnp.exp(cum)
    bt = (be * p_inv).astype(BF16)
    kt = (kmod * p_inv).astype(BF16)
    bk = jnp.concatenate([be * p_rel, kmod * p_rel], axis=0).astype(BF16)
    ar = jnp.concatenate([ab, rb], axis=0)
    pc = jnp.exp(cum_l)

    n4 = WKV_HQ * C
    bi = lax.broadcasted_iota(I32, (n4, n4), 0)
    bj = lax.broadcasted_iota(I32, (n4, n4), 1)
    same = (bi // C) == (bj // C)
    tri_s4 = same & ((bj % C) < (bi % C))
    tri_i4 = same & ((bj % C) <= (bi % C))
    eye4 = (bi == bj).astype(F32)
    masks = [lane_head == j for j in range(WKV_HQ)]

    def stack(x):
        return jnp.concatenate([jnp.where(m, x, jnp.zeros_like(x)) for m in masks], axis=0)

    def block_sum(x):
        return sum(x[j * C:(j + 1) * C] for j in range(WKV_HQ))

    def bdot(a, b):
        return _dot(a.astype(BF16), b.astype(BF16))

    qs = []
    for q in range(WKV_QPS):
        sl = slice(q * W, (q + 1) * W)
        v_q = v[:, sl]
        lhs = jnp.concatenate([stack(ab[:, sl]), stack(rb[:, sl])], axis=0).astype(BF16)
        abr = _dot_nt(lhs, stack(bt[:, sl]))
        akr = _dot_nt(lhs, stack(kt[:, sl]))
        qs.append(dict(sl=sl, v=v_q, v_s=stack(v_q), s0=s_ref[q],
                       x=jnp.where(tri_s4, abr[0:n4], 0.0), a_rb=jnp.where(tri_i4, abr[n4:2 * n4], 0.0),
                       a_ak=jnp.where(tri_s4, akr[0:n4], 0.0), a_rk=jnp.where(tri_i4, akr[n4:2 * n4], 0.0)))
    for d in qs:
        d['pw'] = [d['x']]
    for _ in range(5):
        for d in qs:
            d['pw'].append(bdot(d['pw'][-1], d['pw'][-1]))
    for d in qs:
        pw = d['pw']
        pr_ = [eye4 + pw[2 * i] + pw[2 * i + 1] + bdot(pw[2 * i], pw[2 * i + 1]) for i in range(3)]
        d['t'] = bdot(bdot(pr_[0], pr_[1]), pr_[2])
    for d in qs:
        gs = _dot_nt(ar[:, d['sl']].astype(BF16), d['s0'].astype(BF16))
        d['g_r'] = gs[C:2 * C]
        d['w_s'] = stack(gs[0:C]) + bdot(d['a_ak'], d['v_s'])
    for d in qs:
        d['u_s'] = bdot(d['t'], d['w_s'])
    ys = []
    for q, d in enumerate(qs):
        yv = bdot(jnp.concatenate([d['a_rb'], d['a_rk']], axis=1), jnp.concatenate([d['u_s'], d['v_s']], axis=0))
        ys.append(d['g_r'] + block_sum(yv))
        uv_t = jnp.concatenate([block_sum(d['u_s']), d['v']], axis=0).T.astype(BF16)
        upd = _dot(uv_t, bk[:, d['sl']])
        s_ref[q] = d['s0'] * pc[:, d['sl']] + jnp.where(hv == lane_head, upd, 0.0)

    y = jnp.concatenate(ys, axis=1) if WKV_QPS > 1 else ys[0]
    o_ref[...] = _rwkv_post(y, r, kmod, v, g, prm, seg).astype(o_ref.dtype)

    @pl.when(c == pl.num_programs(2) - 1)
    def _():
        for q in range(WKV_QPS):
            for j in range(WKV_HQ):
                rows_j = s_ref[q, j * RW_HEAD_DIM:(j + 1) * RW_HEAD_DIM, :]
                so_ref[0, q * WKV_HQ + j] = _dot(rows_j, sel_ref[j], HIGHEST)


def _wkv_prompt(z, zs, prm, mu_l, w2, a2, g2, sel, batch, seq):
    nc = seq // WKV_C
    WS = WKV_W * WKV_QPS
    nq = RW_WIDTH // WS
    row = lambda b, q, c: b * nc + c
    return pl.pallas_call(
        _wkv_chunk_body,
        grid=(batch, nq, nc),
        in_specs=[pl.BlockSpec((WKV_C, WS), lambda b, q, c: (row(b, q, c), C_R // WS + q)),
                  pl.BlockSpec((WKV_C, WS), lambda b, q, c: (row(b, q, c), C_K // WS + q)),
                  pl.BlockSpec((WKV_C, WS), lambda b, q, c: (row(b, q, c), C_V // WS + q)),
                  pl.BlockSpec((WKV_C, 512), lambda b, q, c: (row(b, q, c), S_LORA // 512)),
                  pl.BlockSpec((16, WS), lambda b, q, c: (0, q)),
                  pl.BlockSpec((8, 512), lambda b, q, c: (0, 0)),
                  pl.BlockSpec((128, WS), lambda b, q, c: (0, q)),
                  pl.BlockSpec((128, WS), lambda b, q, c: (0, q)),
                  pl.BlockSpec((256, WS), lambda b, q, c: (0, q)),
                  pl.BlockSpec((WKV_HQ, WKV_W, RW_HEAD_DIM), lambda b, q, c: (0, 0, 0))],
        out_specs=[pl.BlockSpec((WKV_C, WS), lambda b, q, c: (row(b, q, c), q)),
                   pl.BlockSpec((1, WKV_HQ * WKV_QPS, RW_HEAD_DIM, RW_HEAD_DIM), lambda b, q, c: (b, q, 0, 0))],
        out_shape=[jax.ShapeDtypeStruct((batch * seq, RW_WIDTH), BF16),
                   jax.ShapeDtypeStruct((batch, RW_HEADS, RW_HEAD_DIM, RW_HEAD_DIM), F32)],
        scratch_shapes=[pltpu.VMEM((WKV_QPS, WKV_W, WKV_W), F32), pltpu.VMEM((8, WS), F32), pltpu.VMEM((8, WS), F32),
                        pltpu.VMEM((8, WS), F32), pltpu.VMEM((8, 512), F32)],
        compiler_params=_cparams(("parallel", "parallel", "arbitrary")),
        name="wkv_prompt",
    )(z, z, z, zs, prm, mu_l, w2, a2, g2, sel)


def _wkv_tok_body(zr_ref, zk_ref, zv_ref, zl_ref, pr_ref, pk_ref, pv_ref, pl_ref, prm_ref, mul_ref, w2_ref, a2_ref,
                  g2_ref, ind_ref, indt_ref, r_ref, w_ref, al_ref, be_ref, km_ref, v_ref, g_ref, bo_ref):
    prm = prm_ref[...]
    ind, ind_t = ind_ref[...], indt_ref[...]
    seg = lambda x: _seg_sum(x, ind, ind_t)
    r, lw, kmod, v, kkn, a, g = _rwkv_tokens(zr_ref[...], zk_ref[...], zv_ref[...], zl_ref[...], pr_ref[...],
                                            pk_ref[...], pv_ref[...], pl_ref[...], prm, mul_ref[0:1, :],
                                            w2_ref[...], a2_ref[...], g2_ref[...], seg)
    r_ref[...] = r
    w_ref[...] = jnp.exp(lw)
    al_ref[...] = -kkn
    be_ref[...] = kkn * a
    km_ref[...] = kmod
    n = r.shape[0]
    pad = jnp.zeros((LANES - n, r.shape[1]), F32)
    for ref, val in ((v_ref, v), (g_ref, g), (bo_ref, seg(r * kmod * prm[7:8]))):
        ref[...] = jnp.concatenate([val, pad], axis=0).T


def _wkv_tokens_sample(z, zs, prev_r, prev_k, prev_v, prev_l, prm, mu_l, w2, a2, g2, ind, ind_t, row0, n):
    rb = row0 // n
    full = lambda a: pl.BlockSpec(a.shape, lambda i: (0,) * a.ndim)
    zspec = lambda w, c0: pl.BlockSpec((n, w), lambda i: (rb, c0 // w))
    return pl.pallas_call(
        _wkv_tok_body,
        grid=(1,),
        in_specs=[zspec(RW_WIDTH, C_R), zspec(RW_WIDTH, C_K), zspec(RW_WIDTH, C_V), zspec(512, S_LORA),
                  full(prev_r), full(prev_k), full(prev_v), full(prev_l), full(prm), full(mu_l), full(w2), full(a2),
                  full(g2), full(ind), full(ind_t)],
        out_specs=[pl.BlockSpec((n, RW_WIDTH), lambda i: (0, 0))] * 5
        + [pl.BlockSpec((RW_WIDTH, LANES), lambda i: (0, 0))] * 3,
        out_shape=[jax.ShapeDtypeStruct((n, RW_WIDTH), F32)] * 5 + [jax.ShapeDtypeStruct((RW_WIDTH, LANES), F32)] * 3,
        compiler_params=_cparams(("arbitrary",)),
        name="wkv_tokens_sample",
    )(z, z, z, zs, prev_r, prev_k, prev_v, prev_l, prm, mu_l, w2, a2, g2, ind, ind_t)


def _wkv_step_body(s_ref, w_ref, al_ref, be_ref, km_ref, r_ref, v_ref, g_ref, bo_ref, lnw_ref, lnb_ref, o_ref, so_ref):
    s = s_ref[...]
    n, hq = s.shape[0], s.shape[1]

    def columns(t_ref):
        t = t_ref[...]
        return jnp.stack([t[:, j:j + 1] for j in range(n)], axis=0).reshape(n, hq, RW_HEAD_DIM, 1)

    vcol, gcol, bocol = columns(v_ref), columns(g_ref), columns(bo_ref)
    sa = jnp.sum(s * al_ref[...], axis=-1, keepdims=True)
    s2 = s * w_ref[...] + sa * be_ref[...] + vcol * km_ref[...]
    so_ref[...] = s2
    y = jnp.sum(s2 * r_ref[...], axis=-1, keepdims=True)
    mean = jnp.mean(y, axis=2, keepdims=True)
    d = y - mean
    var = jnp.mean(d * d, axis=2, keepdims=True)
    yn = d * lax.rsqrt(var + RW_GN_EPS) * lnw_ref[...] + lnb_ref[...]
    o_ref[...] = (yn + bocol * vcol) * gcol


def _wkv_step(state, w, al, be, km, r, vt, gt, bot, lnw, lnb):
    n, h = state.shape[0], state.shape[1]
    hq = 2
    rowspec = pl.BlockSpec((n, hq, 1, RW_HEAD_DIM), lambda q: (0, q, 0, 0))
    colspec = pl.BlockSpec((n, hq, RW_HEAD_DIM, 1), lambda q: (0, q, 0, 0))
    tspec = pl.BlockSpec((hq * RW_HEAD_DIM, LANES), lambda q: (q, 0))
    pcol = pl.BlockSpec((1, hq, RW_HEAD_DIM, 1), lambda q: (0, q, 0, 0))
    sspec = pl.BlockSpec((n, hq, RW_HEAD_DIM, RW_HEAD_DIM), lambda q: (0, q, 0, 0))
    return pl.pallas_call(
        _wkv_step_body,
        grid=(h // hq,),
        in_specs=[sspec, rowspec, rowspec, rowspec, rowspec, rowspec, tspec, tspec, tspec, pcol, pcol],
        out_specs=[colspec, sspec],
        out_shape=[jax.ShapeDtypeStruct((n, h, RW_HEAD_DIM, 1), F32), jax.ShapeDtypeStruct(state.shape, F32)],
        compiler_params=_cparams(("parallel",)),
        name="wkv_step",
    )(state, w, al, be, km, r, vt, gt, bot, lnw, lnb)


def _rope_tables(pos, rot_dim, period):
    half = rot_dim // 2
    t = pos.shape[0]
    inv_freq = ROPE_THETA ** (-jnp.arange(half, dtype=F32) / half)
    ang = pos.astype(F32)[:, None] * inv_freq[None, :]
    cos, sin = jnp.cos(ang), jnp.sin(ang)
    zh = jnp.zeros((t, half), F32)
    rest = period - rot_dim
    c = jnp.concatenate([cos, cos, jnp.ones((t, rest), F32)], axis=1)
    s1 = jnp.concatenate([-sin, zh, jnp.zeros((t, rest), F32)], axis=1)
    s2 = jnp.concatenate([zh, sin, jnp.zeros((t, rest), F32)], axis=1)
    rep = LANES // period
    return jnp.stack([jnp.tile(a, (1, rep)) for a in (c, s1, s2)], axis=0)


def _rope(x, tab, half):
    n = x.shape[1]
    rep = n // LANES
    c, s1, s2 = [jnp.tile(tab[i], (1, rep)) if rep > 1 else tab[i] for i in range(3)]
    return x * c + pltpu.roll(x, n - half, 1) * s1 + pltpu.roll(x, half, 1) * s2


def _prep_body(q_ref, ka_ref, va_ref, iq_ref, ikw_ref, ta_ref, ti_ref, lnw_ref, lnb_ref,
               qo_ref, ko_ref, kb_ref, vb_ref, qio_ref, kio_ref, kid_ref, vbt_ref):
    ta = ta_ref[...]
    ti = ti_ref[...]
    qo_ref[...] = (_rope(q_ref[...], ta, ROT_DIM // 2) * (HEAD_DIM ** -0.5)).astype(BF16)
    k = _rope(ka_ref[...], ta, ROT_DIM // 2)
    ko_ref[...] = k
    kb_ref[...] = k.astype(BF16)
    vb_ref[...] = va_ref[...].astype(BF16)
    vbt_ref[...] = va_ref[...].T.astype(BF16)
    qio_ref[...] = _rope(iq_ref[...], ti, IDX_ROT_DIM // 2).astype(BF16)
    x = ikw_ref[...]
    lane = lax.broadcasted_iota(I32, x.shape, 1)
    is_k = lane < IDX_DIM
    mu = jnp.sum(jnp.where(is_k, x, 0.0), axis=-1, keepdims=True) * (1.0 / IDX_DIM)
    d = jnp.where(is_k, x - mu, 0.0)
    var = jnp.sum(d * d, axis=-1, keepdims=True) * (1.0 / IDX_DIM)
    kn = d * lax.rsqrt(var + LN_EPS) * lnw_ref[...] + lnb_ref[...]
    kr = _rope(kn, ti, IDX_ROT_DIM // 2)
    kr = jnp.where(is_k, kr, 0.0)
    kio_ref[...] = jnp.where(is_k, kr, x * (IDX_HEADS ** -0.5))
    kid_ref[...] = (kr + pltpu.roll(kr, IDX_DIM, 1)).astype(BF16)


def _prep(z, zs, tab_a, tab_i, ln_w, ln_b, tm):
    m = z.shape[0]
    row = lambda w, c0: pl.BlockSpec((tm, w), lambda i: (i, c0 // w))
    outs = [(ATT_WIDTH, BF16), (KV_WIDTH, F32), (KV_WIDTH, BF16), (KV_WIDTH, BF16),
            (IDX_HEADS * IDX_DIM, BF16), (LANES, F32), (LANES, BF16)]
    return pl.pallas_call(
        _prep_body,
        grid=(m // tm,),
        in_specs=[row(ATT_WIDTH, A_Q), row(KV_WIDTH, A_KA), row(KV_WIDTH, A_VA), row(IDX_HEADS * IDX_DIM, A_IQ),
                  row(LANES, S_IKW),
                  pl.BlockSpec((3, tm, LANES), lambda i: (0, i, 0)),
                  pl.BlockSpec((3, tm, LANES), lambda i: (0, i, 0)),
                  pl.BlockSpec((1, LANES), lambda i: (0, 0)),
                  pl.BlockSpec((1, LANES), lambda i: (0, 0))],
        out_specs=[pl.BlockSpec((tm, w), lambda i: (i, 0)) for w, _ in outs]
        + [pl.BlockSpec((KV_WIDTH, tm), lambda i: (0, i))],
        out_shape=[jax.ShapeDtypeStruct((m, w), dt) for w, dt in outs] + [jax.ShapeDtypeStruct((KV_WIDTH, m), BF16)],
        compiler_params=_cparams(("parallel",)),
        name="prep",
    )(z, z, z, z, zs, tab_a, tab_i, ln_w, ln_b)


DSA_QB = 128
DSA_TK = 512
DSA_ATK = 512


def _float_key(s):
    b = pltpu.bitcast(s, I32)
    return b ^ ((b >> 31) & 0x7FFFFFFF)


def _fold_rows(x, op):
    n = x.shape[0]
    while n > SUBLANES:
        n //= 2
        x = op(x[0:n], x[n:2 * n])
    return x


def _kth_threshold(count_ge, topk):
    def step(b, thr):
        cand = thr + jnp.left_shift(jnp.int32(1), 31 - b)
        return jnp.where(count_ge(cand) >= topk, cand, thr)
    return lax.fori_loop(0, 32, step, jnp.full((1, DSA_QB), INT_MIN, I32))


def _dsa_prompt_body(q_ref, kb_ref, vbt_ref, qi_ref, kid_ref, kiw_ref, o_ref, keys_ref, jcut_ref, acc_ref, qs_ref,
                     s_ref, p_ref, *, topk, seq):
    i = pl.program_id(1)
    nt = (i * DSA_QB + DSA_QB + DSA_TK - 1) // DSA_TK
    qpos = i * DSA_QB + lax.broadcasted_iota(I32, (1, DSA_QB), 1)
    row0 = lax.broadcasted_iota(I32, (DSA_TK, 1), 0)
    lane = lax.broadcasted_iota(I32, (1, LANES), 1)
    w_t = (kiw_ref[...] * (IDX_DIM ** -0.5)).T

    def score_tile(t, carry):
        kd = kid_ref[pl.ds(t * DSA_TK, DSA_TK), :]
        s = jnp.zeros((DSA_TK, DSA_QB), F32)
        for h in range(IDX_HEADS):
            qt = qi_ref[:, (h // 2) * LANES:(h // 2 + 1) * LANES]
            qh = jnp.where((lane // IDX_DIM) == (h % 2), qt, jnp.zeros_like(qt))
            s = s + w_t[IDX_DIM + h:IDX_DIM + h + 1, :] * jnp.maximum(_dot_nt(kd, qh), 0.0)
        valid = (t * DSA_TK + row0) <= qpos
        keys_ref[pl.ds(t * DSA_TK, DSA_TK), :] = jnp.where(valid, _float_key(s), INT_MIN)
        return carry

    lax.fori_loop(0, nt, score_tile, 0)

    def count(pred):
        def body(t, acc):
            kt = keys_ref[pl.ds(t * DSA_TK, DSA_TK), :]
            hit = pred(kt, t * DSA_TK + row0).astype(I32)
            return acc + _fold_rows(hit, jnp.add)
        acc = lax.fori_loop(0, nt, body, jnp.zeros((SUBLANES, DSA_QB), I32))
        return jnp.sum(acc, axis=0, keepdims=True)

    thr = _kth_threshold(lambda c: count(lambda kt, col: kt >= c), topk)
    n_gt = count(lambda kt, col: kt > thr)
    n_eq = count(lambda kt, col: (kt == thr) & (col <= qpos))
    need = topk - n_gt
    jcut_ref[...] = jnp.full(jcut_ref.shape, seq, I32)
    excess = (n_eq > need) & (thr > INT_MIN)

    @pl.when(jnp.max(excess.astype(I32)) > 0)
    def _():
        def step(b, jm):
            cand = jm + jnp.left_shift(jnp.int32(1), 30 - b)
            c = count(lambda kt, col: (kt == thr) & (col <= qpos) & (col < cand))
            return jnp.where(c < need, cand, jm)
        jm = lax.fori_loop(0, 31, step, jnp.zeros((1, DSA_QB), I32))
        jcut_ref[...] = jnp.broadcast_to(jnp.where(excess, jm, seq), jcut_ref.shape)

    jcut = jcut_ref[0:1, :]

    acc_ref[...] = jnp.zeros_like(acc_ref)
    for h in range(N_HEADS):
        qs_ref[h // GROUP, (h % GROUP) * DSA_QB:(h % GROUP + 1) * DSA_QB, :] = q_ref[:, h * HEAD_DIM:(h + 1) * HEAD_DIM]
    nta = (i * DSA_QB + DSA_QB + DSA_ATK - 1) // DSA_ATK
    rowa0 = lax.broadcasted_iota(I32, (DSA_ATK, 1), 0)

    def att_tile(t, carry):
        ms, ls = carry
        k0 = pl.multiple_of(t * DSA_ATK, DSA_ATK)
        kt = keys_ref[pl.ds(k0, DSA_ATK), :]
        kpos = k0 + rowa0
        sel = ((kt > thr) | ((kt == thr) & (kpos <= jcut))) & (kpos <= qpos)
        sel4 = jnp.concatenate([sel] * GROUP, axis=1)
        for g in range(N_KV_HEADS):
            k_t = kb_ref[pl.ds(k0, DSA_ATK), g * HEAD_DIM:(g + 1) * HEAD_DIM]
            s_ref[g] = _dot_nt(k_t, qs_ref[g])
        new_m, new_l, corrs = [], [], []
        for g in range(N_KV_HEADS):
            s = jnp.where(sel4, s_ref[g], -1e30)
            m_new = jnp.maximum(ms[g], jnp.max(_fold_rows(s, jnp.maximum), axis=0, keepdims=True))
            p = jnp.exp(s - m_new)
            corr = jnp.exp(ms[g] - m_new)
            new_l.append(ls[g] * corr + jnp.sum(_fold_rows(p, jnp.add), axis=0, keepdims=True))
            p_ref[g] = p.astype(BF16)
            new_m.append(m_new)
            corrs.append(corr)
        for g in range(N_KV_HEADS):
            vt_t = vbt_ref[g * HEAD_DIM:(g + 1) * HEAD_DIM, pl.ds(k0, DSA_ATK)]
            acc_ref[g] = acc_ref[g] * corrs[g] + _dot(vt_t, p_ref[g])
        return tuple(new_m), tuple(new_l)

    wq = GROUP * DSA_QB
    init = (tuple(jnp.full((1, wq), -1e29, F32) for _ in range(N_KV_HEADS)),
            tuple(jnp.zeros((1, wq), F32) for _ in range(N_KV_HEADS)))
    ms, ls = lax.fori_loop(0, nta, att_tile, init)
    for h in range(N_HEADS):
        g, c = h // GROUP, (h % GROUP) * DSA_QB
        o_t = acc_ref[g, :, c:c + DSA_QB] / ls[g][:, c:c + DSA_QB]
        o_ref[:, h * HEAD_DIM:(h + 1) * HEAD_DIM] = o_t.T.astype(o_ref.dtype)


def _dsa_prompt(qb, kb, vbt, qib, kid, kiw, batch, seq, topk):
    nb = seq // DSA_QB
    body = functools.partial(_dsa_prompt_body, topk=topk, seq=seq)
    return pl.pallas_call(
        body,
        grid=(batch, nb),
        in_specs=[pl.BlockSpec((DSA_QB, ATT_WIDTH), lambda b, i: (b * nb + i, 0)),
                  pl.BlockSpec((seq, KV_WIDTH), lambda b, i: (b, 0)),
                  pl.BlockSpec((KV_WIDTH, seq), lambda b, i: (0, b)),
                  pl.BlockSpec((DSA_QB, IDX_HEADS * IDX_DIM), lambda b, i: (b * nb + i, 0)),
                  pl.BlockSpec((seq, LANES), lambda b, i: (b, 0)),
                  pl.BlockSpec((DSA_QB, LANES), lambda b, i: (b * nb + i, 0))],
        out_specs=pl.BlockSpec((DSA_QB, ATT_WIDTH), lambda b, i: (b * nb + i, 0)),
        out_shape=jax.ShapeDtypeStruct((batch * seq, ATT_WIDTH), BF16),
        scratch_shapes=[pltpu.VMEM((seq, DSA_QB), I32), pltpu.VMEM((SUBLANES, DSA_QB), I32),
                        pltpu.VMEM((N_KV_HEADS, HEAD_DIM, GROUP * DSA_QB), F32),
                        pltpu.VMEM((N_KV_HEADS, GROUP * DSA_QB, HEAD_DIM), BF16),
                        pltpu.VMEM((N_KV_HEADS, DSA_ATK, GROUP * DSA_QB), F32),
                        pltpu.VMEM((N_KV_HEADS, DSA_ATK, GROUP * DSA_QB), BF16)],
        compiler_params=_cparams(("parallel", "arbitrary")),
        name="dsa_prompt",
    )(qb, kb, vbt, qib, kid, kiw)


def _sel_sample_body(pt_ref, qi_ref, wi_ref, ks_ref, ck_hbm, pos_ref, ms_ref, kbuf, sc_ref, jm_ref, rk_ref, sem, *,
                     topk, npg):
    s = pl.program_id(0)
    slot = s % 2
    U = SEL_SPS

    def page_copy(step, u, p, sl):
        return pltpu.make_async_copy(ck_hbm.at[pt_ref[step * U + u, p]], kbuf.at[sl, u, p], sem.at[sl])

    def request(step, sl):
        for u in range(U):
            for p in range(npg):
                page_copy(step, u, p, sl).start()

    @pl.when(s == 0)
    def _():
        request(0, 0)

    @pl.when(s + 1 < pl.num_programs(0))
    def _():
        request(s + 1, 1 - slot)

    for u in range(U):
        for p in range(npg):
            page_copy(s, u, p, slot).wait()

    lane = lax.broadcasted_iota(I32, (1, PAGE_SIZE), 1)
    pos = lax.broadcasted_iota(I32, (npg, PAGE_SIZE), 0) * PAGE_SIZE + lane

    def total(x):
        return jnp.sum(jnp.sum(x.astype(I32), axis=1, keepdims=True), axis=0, keepdims=True)

    keys, k_self = [], []
    for u in range(U):
        qi, wi = qi_ref[u], wi_ref[u] * (IDX_DIM ** -0.5)
        for c in range(npg // SEL_CP):
            kt = jnp.concatenate([kbuf[slot, u, c * SEL_CP + r] for r in range(SEL_CP)], axis=1).astype(BF16)
            sc = jnp.sum(wi * jnp.maximum(_dot(qi, kt), 0.0), axis=0, keepdims=True)
            for r in range(SEL_CP):
                sc_ref[u, c * SEL_CP + r:c * SEL_CP + r + 1, :] = sc[:, r * PAGE_SIZE:(r + 1) * PAGE_SIZE]
        keys.append(_float_key(sc_ref[u]))
        d = jnp.sum(qi.astype(F32) * ks_ref[u].astype(F32), axis=-1, keepdims=True)
        k_self.append(_float_key(jnp.sum(wi * jnp.maximum(d, 0.0), axis=0, keepdims=True)))

    def step(b, thrs):
        out = []
        for u in range(U):
            cand = thrs[u] + jnp.left_shift(jnp.int32(1), 31 - b)
            c = total(keys[u] >= cand) + (k_self[u] >= cand).astype(I32)
            out.append(jnp.where(c >= topk, cand, thrs[u]))
        return tuple(out)

    thrs = lax.fori_loop(0, 32, step, tuple(jnp.full((1, 1), INT_MIN, I32) for _ in range(U)))

    ri = lax.broadcasted_iota(I32, (PAGE_SIZE, PAGE_SIZE), 0)
    ci = lax.broadcasted_iota(I32, (PAGE_SIZE, PAGE_SIZE), 1)
    pr_ = lax.broadcasted_iota(I32, (npg, npg), 0)
    pc_ = lax.broadcasted_iota(I32, (npg, npg), 1)
    jcol = lax.broadcasted_iota(I32, (topk, PAGE_SIZE), 0)
    lane_f = lax.broadcasted_iota(I32, (topk, PAGE_SIZE), 1).astype(F32)
    ones8 = jnp.ones((SUBLANES, PAGE_SIZE), BF16)
    for u in range(U):
        thr = thrs[u]
        need = topk - total(keys[u] > thr) - (k_self[u] > thr).astype(I32)
        eq = keys[u] == thr
        jm_ref[u] = jnp.full((SUBLANES, LANES), npg * PAGE_SIZE, I32)

        @pl.when(jnp.max((total(eq) > need).astype(I32)) > 0)
        def _():
            def jstep(b, jm):
                cand = jm + jnp.left_shift(jnp.int32(1), 30 - b)
                return jnp.where(total(eq & (pos < cand)) < need, cand, jm)
            jm_ref[u] = jnp.broadcast_to(lax.fori_loop(0, 31, jstep, jnp.zeros((1, 1), I32)), (SUBLANES, LANES))

        jm = jm_ref[u, 0:1, 0:1]
        sel = (keys[u] > thr) | (eq & (pos <= jm))
        self_sel = (k_self[u] > thr) | ((k_self[u] == thr) & (total(eq & (pos <= jm)) < need))
        ms_ref[u] = jnp.broadcast_to(self_sel.astype(F32), (1, LANES))

        sel_b = sel.astype(BF16)
        within = _dot(sel_b, (ri <= ci).astype(BF16))
        tot = _dot(sel_b, jnp.ones((PAGE_SIZE, PAGE_SIZE), BF16))
        before = _dot((pc_ < pr_).astype(BF16), tot.astype(BF16))
        rk_ref[u] = jnp.where(sel, (before + within).astype(I32) - 1, -1)

        def gather_pos(p, carry):
            hi, lo = carry
            hit = jnp.broadcast_to(rk_ref[u, pl.ds(p, 1), :], (topk, PAGE_SIZE)) == jcol
            return hi + jnp.where(hit, jnp.asarray(p, F32), 0.0), lo + jnp.where(hit, lane_f, 0.0)

        zero = jnp.zeros((topk, PAGE_SIZE), F32)
        hi, lo = lax.fori_loop(0, npg, gather_pos, (zero, zero))
        pos_row = _dot_nt(ones8, hi.astype(BF16)) * PAGE_SIZE + _dot_nt(ones8, lo.astype(BF16))
        pos_ref[u] = pos_row[0:1].astype(I32)


SEL_CP = 8
SEL_SPS = 4


def _sel_sample(page_table, qi, wi, kself, cache_kt, topk):
    n, npg = page_table.shape
    U = SEL_SPS
    assert npg % SEL_CP == 0 and npg <= PAGE_SIZE and n % U == 0
    grid_spec = pltpu.PrefetchScalarGridSpec(
        num_scalar_prefetch=1,
        grid=(n // U,),
        in_specs=[pl.BlockSpec((U, IDX_HEADS, IDX_DIM), lambda s, pt: (s, 0, 0)),
                  pl.BlockSpec((U, IDX_HEADS, 1), lambda s, pt: (s, 0, 0)),
                  pl.BlockSpec((U, 1, IDX_DIM), lambda s, pt: (s, 0, 0)),
                  pl.BlockSpec(memory_space=pl.ANY)],
        out_specs=[pl.BlockSpec((U, 1, topk), lambda s, pt: (s, 0, 0)),
                   pl.BlockSpec((U, 1, LANES), lambda s, pt: (s, 0, 0))],
        scratch_shapes=[pltpu.VMEM((2, U, npg, IDX_DIM, PAGE_SIZE), F32), pltpu.VMEM((U, npg, PAGE_SIZE), F32),
                        pltpu.VMEM((U, SUBLANES, LANES), I32), pltpu.VMEM((U, npg, PAGE_SIZE), I32),
                        pltpu.SemaphoreType.DMA((2,))],
    )
    return pl.pallas_call(
        functools.partial(_sel_sample_body, topk=topk, npg=npg),
        grid_spec=grid_spec,
        out_shape=[jax.ShapeDtypeStruct((n, 1, topk), I32), jax.ShapeDtypeStruct((n, 1, LANES), F32)],
        compiler_params=_cparams(("arbitrary",)),
        name="sel_sample",
    )(page_table, qi, wi, kself, cache_kt)


def _att_sel_body(pt_ref, pos_ref, q_ref, ms_ref, ks_ref, vs_ref, ck_hbm, cv_hbm, o_ref, kbuf, vbuf, sem, *, topk):
    s = pl.program_id(0)
    slot = s % 2

    def request(seq, sl):
        def body(j, c):
            pos = pos_ref[seq, j]
            pg = pt_ref[seq, pos // PAGE_SIZE]
            r = pos % PAGE_SIZE
            pltpu.make_async_copy(ck_hbm.at[pg, r], kbuf.at[sl, j], sem.at[0, sl]).start()
            pltpu.make_async_copy(cv_hbm.at[pg, r], vbuf.at[sl, j], sem.at[1, sl]).start()
            return c
        lax.fori_loop(0, topk, body, 0, unroll=8)

    @pl.when(s == 0)
    def _():
        request(0, 0)

    @pl.when(s + 1 < pl.num_programs(0))
    def _():
        request(s + 1, 1 - slot)

    for h in range(topk // PAGE_SIZE):
        rows = pl.ds(h * PAGE_SIZE, PAGE_SIZE)
        pltpu.make_async_copy(ck_hbm.at[0], kbuf.at[slot, rows], sem.at[0, slot]).wait()
        pltpu.make_async_copy(cv_hbm.at[0], vbuf.at[slot, rows], sem.at[1, slot]).wait()

    q = q_ref[0]
    row_g = lax.broadcasted_iota(I32, (N_HEADS, 1), 0) // GROUP
    lane_g = lax.broadcasted_iota(I32, (1, KV_WIDTH), 1) // HEAD_DIM
    q_bd = jnp.where(row_g == lane_g, jnp.tile(q, (1, N_KV_HEADS)), jnp.zeros((N_HEADS, KV_WIDTH), BF16))
    k2 = jnp.concatenate([kbuf[slot, :, g, :] for g in range(N_KV_HEADS)], axis=1).astype(BF16)
    v2 = jnp.concatenate([vbuf[slot, :, g, :] for g in range(N_KV_HEADS)], axis=1).astype(BF16)
    self_row = ms_ref[0]
    self_f = self_row[:, 0:1]
    n_past = topk - jnp.tile(self_row, (1, topk // LANES))
    valid = lax.broadcasted_iota(I32, (1, topk), 1).astype(F32) < n_past
    sc = jnp.where(valid, _dot_nt(q_bd, k2), -1e30)
    s1 = jnp.sum(q.astype(F32) * ks_ref[0].astype(F32), axis=-1, keepdims=True)
    s1 = jnp.where(self_f > 0.5, s1, -1e30)
    m = jnp.maximum(jnp.max(sc, axis=-1, keepdims=True), jnp.maximum(s1, -1e29))
    pr = jnp.exp(sc - m)
    p1 = jnp.exp(s1 - m)
    l = jnp.sum(pr, axis=-1, keepdims=True) + p1
    pv = _dot(pr.astype(BF16), v2)
    own = jnp.zeros((N_HEADS, HEAD_DIM), F32)
    for g in range(N_KV_HEADS):
        own = jnp.where(row_g == g, pv[:, g * HEAD_DIM:(g + 1) * HEAD_DIM], own)
    o_ref[0] = ((own + p1.astype(BF16).astype(F32) * vs_ref[0].astype(F32)) / l).astype(o_ref.dtype)


def _att_sel(page_table, pos_list, q, mself, kself, vself, cache_k, cache_v):
    n, topk = pos_list.shape
    assert topk % PAGE_SIZE == 0
    seqspec = lambda r, c: pl.BlockSpec((1, r, c), lambda s, pt, pos: (s, 0, 0))
    anyspec = pl.BlockSpec(memory_space=pl.ANY)
    rows = (2, topk, N_KV_HEADS, HEAD_DIM)
    grid_spec = pltpu.PrefetchScalarGridSpec(
        num_scalar_prefetch=2,
        grid=(n,),
        in_specs=[seqspec(N_HEADS, HEAD_DIM), seqspec(1, LANES), seqspec(N_HEADS, HEAD_DIM),
                  seqspec(N_HEADS, HEAD_DIM), anyspec, anyspec],
        out_specs=seqspec(N_HEADS, HEAD_DIM),
        scratch_shapes=[pltpu.VMEM(rows, F32), pltpu.VMEM(rows, F32), pltpu.SemaphoreType.DMA((2, 2))],
    )
    return pl.pallas_call(
        functools.partial(_att_sel_body, topk=topk),
        grid_spec=grid_spec,
        out_shape=jax.ShapeDtypeStruct((n, N_HEADS, HEAD_DIM), BF16),
        compiler_params=_cparams(("arbitrary",)),
        name="att_sel",
    )(page_table, pos_list, q, mself, kself, vself, cache_k, cache_v)


def _mem_att_prompt_body(q_ref, k_ref, v_ref, o_ref):
    scale = MEM_HEAD_DIM ** -0.5
    for h in range(MEM_HEADS):
        sl = slice(h * MEM_HEAD_DIM, (h + 1) * MEM_HEAD_DIM)
        s = _dot_nt(q_ref[:, sl], k_ref[:, sl]) * scale
        m = jnp.max(s, axis=-1, keepdims=True)
        e = jnp.exp(s - m)
        pr = e / jnp.sum(e, axis=-1, keepdims=True)
        o_ref[:, sl] = _dot(pr.astype(BF16), v_ref[:, sl]).astype(o_ref.dtype)


def _mem_att_prompt(mq, mk, mv, batch, seq, tq):
    m = mk.shape[0] // batch
    nb = seq // tq
    return pl.pallas_call(
        _mem_att_prompt_body,
        grid=(batch * nb,),
        in_specs=[pl.BlockSpec((tq, MEM_WIDTH), lambda i: (i, 0)),
                  pl.BlockSpec((m, MEM_WIDTH), lambda i: (i // nb, 0)),
                  pl.BlockSpec((m, MEM_WIDTH), lambda i: (i // nb, 0))],
        out_specs=pl.BlockSpec((tq, MEM_WIDTH), lambda i: (i, 0)),
        out_shape=jax.ShapeDtypeStruct((batch * seq, MEM_WIDTH), BF16),
        compiler_params=_cparams(("parallel",)),
        name="mem_att_prompt",
    )(mq, mk, mv)


def _mem_att_sample_body(q_ref, k_ref, v_ref, o_ref):
    scale = MEM_HEAD_DIM ** -0.5
    q = q_ref[0].astype(F32)
    for h in range(MEM_HEADS):
        sl = slice(h * MEM_HEAD_DIM, (h + 1) * MEM_HEAD_DIM)
        s = jnp.sum(k_ref[0, :, h, :] * q[:, sl], axis=-1, keepdims=True) * scale
        m = jnp.max(s, axis=0, keepdims=True)
        e = jnp.exp(s - m)
        pr = e / jnp.sum(e, axis=0, keepdims=True)
        o_ref[0, :, sl] = jnp.sum(pr * v_ref[0, :, h, :], axis=0, keepdims=True).astype(o_ref.dtype)


def _mem_att_sample(mq, mk, mv):
    n, m, nh, hd = mk.shape
    w = nh * hd
    return pl.pallas_call(
        _mem_att_sample_body,
        grid=(n,),
        in_specs=[pl.BlockSpec((1, 1, w), lambda s: (s, 0, 0)),
                  pl.BlockSpec((1, m, nh, hd), lambda s: (s, 0, 0, 0)),
                  pl.BlockSpec((1, m, nh, hd), lambda s: (s, 0, 0, 0))],
        out_specs=pl.BlockSpec((1, 1, w), lambda s: (s, 0, 0)),
        out_shape=jax.ShapeDtypeStruct((n, 1, w), BF16),
        compiler_params=_cparams(("parallel",)),
        name="mem_att_sample",
    )(mq, mk, mv)


def _router_body(x_ref, w_ref, b_ref, ei_ref, ew_ref, acc_ref):
    k = pl.program_id(1)

    @pl.when(k == 0)
    def _():
        acc_ref[...] = jnp.zeros_like(acc_ref)

    acc_ref[...] += _dot(x_ref[...], w_ref[...], HIGHEST)

    @pl.when(k == pl.num_programs(1) - 1)
    def _():
        lg = acc_ref[...] + b_ref[...]
        lane = lax.broadcasted_iota(I32, lg.shape, 1)
        neg = jnp.float32(-jnp.inf)
        is_g = lane < N_GROUPS
        glm = jnp.where(is_g, lg, neg)
        gmax = jnp.max(glm, axis=-1, keepdims=True)
        g_sel = jnp.min(jnp.where(glm == gmax, lane, LANES), axis=-1, keepdims=True)
        g_prob = 1.0 / jnp.sum(jnp.where(is_g, jnp.exp(lg - gmax), 0.0), axis=-1, keepdims=True)
        e_id = lane - N_GROUPS
        in_grp = (e_id >= 0) & (e_id < N_EXPERTS) & ((e_id // EXPERTS_PER_GROUP) == g_sel)
        el = jnp.where(in_grp, lg, neg)
        m1 = jnp.max(el, axis=-1, keepdims=True)
        i1 = jnp.min(jnp.where(in_grp & (el == m1), lane, LANES), axis=-1, keepdims=True)
        rest = in_grp & (lane != i1)
        el2 = jnp.where(rest, lg, neg)
        m2 = jnp.max(el2, axis=-1, keepdims=True)
        i2 = jnp.min(jnp.where(rest & (el2 == m2), lane, LANES), axis=-1, keepdims=True)
        t = jnp.exp(m2 - m1)
        w1 = g_prob / (1.0 + t)
        w2 = g_prob * t / (1.0 + t)
        ei_ref[...] = jnp.where(lane == 0, i1 - N_GROUPS, jnp.where(lane == 1, i2 - N_GROUPS, 0))
        ew_ref[...] = jnp.where(lane == 0, w1, jnp.where(lane == 1, w2, 0.0))


def _router(x, w, b, tm, tk):
    m, kd = x.shape
    return pl.pallas_call(
        _router_body,
        grid=(m // tm, kd // tk),
        in_specs=[pl.BlockSpec((tm, tk), lambda i, k: (i, k)),
                  pl.BlockSpec((tk, LANES), lambda i, k: (k, 0)),
                  pl.BlockSpec((1, LANES), lambda i, k: (0, 0))],
        out_specs=[pl.BlockSpec((tm, LANES), lambda i, k: (i, 0)), pl.BlockSpec((tm, LANES), lambda i, k: (i, 0))],
        out_shape=[jax.ShapeDtypeStruct((m, LANES), I32), jax.ShapeDtypeStruct((m, LANES), F32)],
        scratch_shapes=[pltpu.VMEM((tm, LANES), F32)],
        compiler_params=_cparams(("parallel", "arbitrary")),
        name="router",
    )(x, w, b)


MOE_BR = 256


def _expert_up_body(be_ref, nblk_ref, tok_ref, x_hbm, wg_ref, wu_ref, h_ref, wgb_ref, wub_ref, xbuf, sem):
    i = pl.program_id(0)
    n_used = nblk_ref[0]
    slot = i % 2
    changed = jnp.logical_or(i == 0, be_ref[i] != be_ref[jnp.maximum(i - 1, 0)])

    def request(blk, sl):
        def issue(r, c):
            pltpu.make_async_copy(x_hbm.at[pl.ds(tok_ref[blk * MOE_BR + r], 1)], xbuf.at[sl, pl.ds(r, 1)],
                                  sem.at[sl]).start()
            return c
        lax.fori_loop(0, MOE_BR, issue, 0, unroll=8)

    @pl.when(jnp.logical_and(i == 0, n_used > 0))
    def _():
        request(0, 0)

    @pl.when(i + 1 < n_used)
    def _():
        request(i + 1, 1 - slot)

    @pl.when(jnp.logical_and(i < n_used, changed))
    def _():
        wgb_ref[...] = wg_ref[0].astype(BF16)
        wub_ref[...] = wu_ref[0].astype(BF16)

    @pl.when(i < n_used)
    def _():
        pltpu.make_async_copy(x_hbm.at[pl.ds(0, MOE_BR)], xbuf.at[slot], sem.at[slot]).wait()
        x = xbuf[slot].astype(BF16)
        a = _dot(x, wgb_ref[...])
        u = _dot(x, wub_ref[...])
        h_ref[...] = (a * _sigmoid(a) * u).astype(h_ref.dtype)

    @pl.when(i >= n_used)
    def _():
        h_ref[...] = jnp.zeros_like(h_ref)


def _expert_up(block_e, nblk, row_token, x, w_gate, w_up):
    nr = row_token.shape[0]
    d = x.shape[1]
    nb = nr // MOE_BR
    blk = lambda i, nbk: jnp.minimum(i, nbk[0] - 1)
    grid_spec = pltpu.PrefetchScalarGridSpec(
        num_scalar_prefetch=3,
        grid=(nb,),
        in_specs=[pl.BlockSpec(memory_space=pl.ANY),
                  pl.BlockSpec((1, d, D_EXPERT), lambda i, be, nbk, tok: (be[blk(i, nbk)], 0, 0)),
                  pl.BlockSpec((1, d, D_EXPERT), lambda i, be, nbk, tok: (be[blk(i, nbk)], 0, 0))],
        out_specs=pl.BlockSpec((MOE_BR, D_EXPERT), lambda i, be, nbk, tok: (i, 0)),
        scratch_shapes=[pltpu.VMEM((d, D_EXPERT), BF16), pltpu.VMEM((d, D_EXPERT), BF16),
                        pltpu.VMEM((2, MOE_BR, d), F32), pltpu.SemaphoreType.DMA((2,))],
    )
    return pl.pallas_call(
        _expert_up_body,
        grid_spec=grid_spec,
        out_shape=jax.ShapeDtypeStruct((nr, D_EXPERT), BF16),
        compiler_params=_cparams(("arbitrary",)),
        name="expert_up",
    )(block_e, nblk, row_token, x, w_gate, w_up)


def _expert_down_body(be_ref, nblk_ref, h_ref, wd_ref, y_ref, wdb_ref):
    i = pl.program_id(0)
    changed = jnp.logical_or(i == 0, be_ref[i] != be_ref[jnp.maximum(i - 1, 0)])

    @pl.when(jnp.logical_and(i < nblk_ref[0], changed))
    def _():
        wdb_ref[...] = wd_ref[0].astype(BF16)

    @pl.when(i < nblk_ref[0])
    def _():
        y_ref[...] = _dot(h_ref[...], wdb_ref[...])

    @pl.when(i >= nblk_ref[0])
    def _():
        y_ref[...] = jnp.zeros_like(y_ref)


def _expert_down(block_e, nblk, h, w_down):
    nr = h.shape[0]
    d = w_down.shape[2]
    blk = lambda i, nbk: jnp.minimum(i, nbk[0] - 1)
    grid_spec = pltpu.PrefetchScalarGridSpec(
        num_scalar_prefetch=2,
        grid=(nr // MOE_BR,),
        in_specs=[pl.BlockSpec((MOE_BR, D_EXPERT), lambda i, be, nbk: (blk(i, nbk), 0)),
                  pl.BlockSpec((1, D_EXPERT, d), lambda i, be, nbk: (be[blk(i, nbk)], 0, 0))],
        out_specs=pl.BlockSpec((MOE_BR, d), lambda i, be, nbk: (i, 0)),
        scratch_shapes=[pltpu.VMEM((D_EXPERT, d), BF16)],
    )
    return pl.pallas_call(
        _expert_down_body,
        grid_spec=grid_spec,
        out_shape=jax.ShapeDtypeStruct((nr, d), F32),
        compiler_params=_cparams(("arbitrary",)),
        name="expert_down",
    )(block_e, nblk, h, w_down)


def _combine_ln_body(slot_ref, x_ref, y_hbm, ew_ref, g_ref, b_ref, op_ref, os_ref, ybuf, sem, *, npb, tm, mp):
    i = pl.program_id(0)
    sl = i % 2

    def request(blk, s_):
        def issue(r, c):
            t = blk * tm + r
            pltpu.make_async_copy(y_hbm.at[pl.ds(slot_ref[t], 1)], ybuf.at[s_, pl.ds(r, 1)], sem.at[s_]).start()
            pltpu.make_async_copy(y_hbm.at[pl.ds(slot_ref[mp + t], 1)], ybuf.at[s_, pl.ds(tm + r, 1)],
                                  sem.at[s_]).start()
            return c
        lax.fori_loop(0, tm, issue, 0, unroll=8)

    @pl.when(i == 0)
    def _():
        request(0, 0)

    @pl.when(i + 1 < pl.num_programs(0))
    def _():
        request(i + 1, 1 - sl)

    pltpu.make_async_copy(y_hbm.at[pl.ds(0, 2 * tm)], ybuf.at[sl], sem.at[sl]).wait()
    ew = ew_ref[...]
    ff = ybuf[sl, 0:tm] * ew[:, 0:1] + ybuf[sl, tm:2 * tm] * ew[:, 1:2]
    y = _layer_norm_rows(DEEPNORM_ALPHA * x_ref[...] + ff, g_ref[...], b_ref[...])

    @pl.when(i < npb)
    def _():
        op_ref[...] = y

    @pl.when(i >= npb)
    def _():
        os_ref[...] = y


def _combine_ln(x, y_rows, slot2, ew, g, b, tm, n_prompt):
    m, d = x.shape
    nb = m // tm
    npb = n_prompt // tm
    grid_spec = pltpu.PrefetchScalarGridSpec(
        num_scalar_prefetch=1,
        grid=(nb,),
        in_specs=[pl.BlockSpec((tm, d), lambda i, sl: (i, 0)),
                  pl.BlockSpec(memory_space=pl.ANY),
                  pl.BlockSpec((tm, LANES), lambda i, sl: (i, 0)),
                  pl.BlockSpec((1, d), lambda i, sl: (0, 0)),
                  pl.BlockSpec((1, d), lambda i, sl: (0, 0))],
        out_specs=[pl.BlockSpec((tm, d), lambda i, sl: (jnp.minimum(i, npb - 1), 0)),
                   pl.BlockSpec((tm, d), lambda i, sl: (jnp.maximum(i - npb, 0), 0))],
        scratch_shapes=[pltpu.VMEM((2, 2 * tm, d), F32), pltpu.SemaphoreType.DMA((2,))],
    )
    return pl.pallas_call(
        functools.partial(_combine_ln_body, npb=npb, tm=tm, mp=m),
        grid_spec=grid_spec,
        out_shape=[jax.ShapeDtypeStruct((n_prompt, d), F32), jax.ShapeDtypeStruct((m - n_prompt, d), F32)],
        compiler_params=_cparams(("arbitrary",)),
        name="combine_ln",
    )(slot2, x, y_rows, ew, g, b)


def _pad_cols(x, n):
    return jnp.pad(x, ((0, 0), (0, n - x.shape[1])))


def _split_w_in(w):
    o = [int(v) for v in np.cumsum([0, RW_PROJ, ATT_WIDTH + 2 * KV_WIDTH + IDX_HEADS * IDX_DIM, IDX_DIM + IDX_HEADS,
                                    2 * D_MODEL])]
    w_rkv = w[:, 0:3 * RW_WIDTH].astype(BF16)
    w_att = w[:, o[1]:o[2]].astype(BF16)
    w_gate = w[:, o[3]:o[4]].astype(BF16)
    w_small = jnp.concatenate([_lora_cols(w[:, 0:RW_PROJ]), _pad_cols(w[:, o[2]:o[3]], LANES)], axis=1).astype(BF16)
    return w_rkv, w_att, w_gate, w_small


def _lora_cols(x):
    return jnp.concatenate([_pad_cols(x[:, 6144:6240], 128), _pad_cols(x[:, 6240:6336], 128), x[:, 6336:6592]], axis=1)


def _pack_rwkv(rw_mu, rw_w0, rw_w2, rw_a0, rw_a2, rw_g2, rw_k_k, rw_k_a, rw_r_k, rw_ln_w, rw_ln_b):
    flat = lambda t: t.reshape(1, RW_WIDTH)
    mu = rw_mu.reshape(1, RW_PROJ)
    rows = [mu[:, 0:2048], mu[:, 2048:4096], mu[:, 4096:6144], flat(rw_w0), flat(rw_a0), flat(rw_k_k),
            flat(rw_k_a), flat(rw_r_k), flat(rw_ln_w), flat(rw_ln_b)]
    prm = jnp.pad(jnp.concatenate(rows, axis=0), ((0, 6), (0, 0)))
    mu_l = jnp.pad(_lora_cols(mu), ((0, 7), (0, 0)))
    w2 = jnp.pad(rw_w2, ((0, 128 - W_LORA), (0, 0))).astype(BF16)
    a2 = jnp.pad(rw_a2, ((0, 128 - A_LORA), (0, 0))).astype(BF16)
    g2 = rw_g2.astype(BF16)
    return prm, mu_l, w2, a2, g2


def _head_indicators(width):
    lane = np.arange(width)[:, None] // RW_HEAD_DIM
    ind = (lane == np.arange(128)[None, :]).astype(np.float32)
    return jnp.asarray(ind), jnp.asarray(ind.T)


def _head_selectors():
    sel = np.zeros((WKV_HQ, WKV_W, RW_HEAD_DIM), np.float32)
    for j in range(WKV_HQ):
        sel[j, j * RW_HEAD_DIM + np.arange(RW_HEAD_DIM), np.arange(RW_HEAD_DIM)] = 1.0
    return jnp.asarray(sel)


def kernel(x_prompt, x_sample, mem_prompt, cache_k, cache_v, cache_idx_k, page_table, state_wkv, state_shift, cache_mem_k, cache_mem_v, w_in, rw_mu, rw_w0, rw_w2, rw_a0, rw_a2, rw_g2, rw_k_k, rw_k_a, rw_r_k, rw_ln_w, rw_ln_b, idx_ln_w, idx_ln_b, w_branch_a, w_branch_b, w_out, ln1_w, ln1_b, w_mem_q, w_mem_k, w_mem_v, w_mem_o, ln2_w, ln2_b, w_router_grp, b_router_grp, w_router_exp, b_router_exp, w_exp_gate, w_exp_up, w_exp_down, ln3_w, ln3_b):
    B, S, D = x_prompt.shape
    DB, DS, _ = x_sample.shape
    assert DS == 1 and cache_k.shape[0] == 1
    TP = B * S
    T = TP + DB
    MP = _round_up(T, DENSE_TM)
    past = page_table.shape[1] * PAGE_SIZE
    n_mem = mem_prompt.shape[1]
    row1 = lambda a: a.reshape(1, -1)

    def pad_rows(a):
        return jnp.concatenate([a, jnp.zeros((MP - a.shape[0],) + a.shape[1:], a.dtype)], axis=0)

    x_all = pad_rows(jnp.concatenate([x_prompt.reshape(TP, D), x_sample.reshape(DB, D)], axis=0))
    xb = x_all.astype(BF16)
    w_rkv, w_att, w_gate, w_small = _split_w_in(w_in[0])
    z_rkv = _mm(xb, w_rkv, DENSE_TM, DENSE_TN, D, name="in_proj_rkv")
    z_att = _mm(xb, w_att, DENSE_TM, DENSE_TN, D, name="in_proj_att")
    z_gate = _mm(xb, w_gate, DENSE_TM, DENSE_TN, D, name="in_proj_gate")
    z_small = _mm(xb, w_small, DENSE_TM, S_TOTAL, D, name="in_proj_small")

    prm, mu_l, w2, a2, g2 = _pack_rwkv(rw_mu[0], rw_w0[0], rw_w2[0], rw_a0[0], rw_a2[0], rw_g2[0], rw_k_k[0],
                                       rw_k_a[0], rw_r_k[0], rw_ln_w[0], rw_ln_b[0])
    rw_p, wkv_p = _wkv_prompt(z_rkv, z_small, prm, mu_l, w2, a2, g2, _head_selectors(), B, S)
    ss = state_shift[0]
    ind_f, indt_f = _head_indicators(RW_WIDTH)
    tok = _wkv_tokens_sample(z_rkv, z_small, ss[:, 0:2048], ss[:, 2048:4096], ss[:, 4096:6144], _lora_cols(ss), prm,
                             mu_l, w2, a2, g2, ind_f, indt_f, TP, DB)
    t_r, t_w, t_al, t_be, t_km, t_vt, t_gt, t_bot = tok
    rowv = lambda a: a.reshape(DB, RW_HEADS, 1, RW_HEAD_DIM)
    y_col, wkv_s = _wkv_step(state_wkv[0], rowv(t_w), rowv(t_al), rowv(t_be), rowv(t_km), rowv(t_r), t_vt, t_gt,
                             t_bot, rw_ln_w[0].reshape(1, RW_HEADS, RW_HEAD_DIM, 1),
                             rw_ln_b[0].reshape(1, RW_HEADS, RW_HEAD_DIM, 1))
    rw_all = pad_rows(jnp.concatenate([rw_p, y_col.reshape(DB, RW_WIDTH).astype(BF16)], axis=0))

    pos = jnp.concatenate([jnp.tile(jnp.arange(S, dtype=I32), B), jnp.full((MP - TP,), past, I32)])
    tab_a = _rope_tables(pos, ROT_DIM, HEAD_DIM)
    tab_i = _rope_tables(pos, IDX_ROT_DIM, IDX_DIM)
    qb, k_rot, kb, vb, qib, kiw, kid, vbt = _prep(z_att, z_small, tab_a, tab_i, _pad_cols(row1(idx_ln_w[0]), LANES),
                                                  _pad_cols(row1(idx_ln_b[0]), LANES), ROW_TM)
    att_p = _dsa_prompt(qb, kb, vbt, qib, kid, kiw, B, S, min(TOPK_MAX, S // 4))
    qi_s = qib[TP:T].reshape(DB, IDX_HEADS, IDX_DIM)
    wi_s = kiw[TP:T, IDX_DIM:IDX_DIM + IDX_HEADS].reshape(DB, IDX_HEADS, 1)
    topk_s = min(TOPK_MAX, (past + DS) // 4)
    pos_sel, mself = _sel_sample(page_table, qi_s, wi_s, kid[TP:T, 0:IDX_DIM].reshape(DB, 1, IDX_DIM),
                                 jnp.swapaxes(cache_idx_k[0], 1, 2), topk_s)
    expand = lambda a: jnp.repeat(a[TP:T].reshape(DB, N_KV_HEADS, HEAD_DIM), GROUP, axis=1)
    att_s = _att_sel(page_table, pos_sel.reshape(DB, topk_s), qb[TP:T].reshape(DB, N_HEADS, HEAD_DIM), mself,
                     expand(kb), expand(vb), cache_k[0], cache_v[0])
    att_all = pad_rows(jnp.concatenate([att_p, att_s.reshape(DB, ATT_WIDTH)], axis=0))

    merged = _branch_merge(rw_all, att_all, w_branch_a[0].astype(BF16), w_branch_b[0].astype(BF16), z_gate, DENSE_TM,
                           DENSE_TN)
    x1, x1b = _mm_ln(merged, w_out[0].astype(BF16), x_all, row1(ln1_w[0]), row1(ln1_b[0]), LN_TM, LN_TN,
                     name="out_ln1")

    mq = _mm(x1b, w_mem_q[0].astype(BF16), DENSE_TM, MEM_WIDTH, D, out_dtype=BF16, name="mem_q")
    mem2d = mem_prompt.reshape(B * n_mem, D).astype(BF16)
    mem_k = _mm(mem2d, w_mem_k[0].astype(BF16), B * n_mem, MEM_WIDTH, D, name="mem_k")
    mem_v = _mm(mem2d, w_mem_v[0].astype(BF16), B * n_mem, MEM_WIDTH, D, name="mem_v")
    ma_p = _mem_att_prompt(mq, mem_k.astype(BF16), mem_v.astype(BF16), B, S, 512)
    ma_s = _mem_att_sample(mq[TP:T].reshape(DB, 1, MEM_WIDTH), cache_mem_k[0], cache_mem_v[0])
    ma_all = pad_rows(jnp.concatenate([ma_p, ma_s.reshape(DB, MEM_WIDTH)], axis=0))
    x2, _ = _mm_ln(ma_all, w_mem_o[0].astype(BF16), x1, row1(ln2_w[0]), row1(ln2_b[0]), LN_TM, LN_TN,
                   name="mem_o_ln2")

    w_r = _pad_cols(jnp.concatenate([w_router_grp[0], w_router_exp[0]], axis=1), LANES)
    b_r = _pad_cols(row1(jnp.concatenate([b_router_grp[0], b_router_exp[0]])), LANES)
    e_idx, e_w = _router(x2, w_r, b_r, DENSE_TM, DENSE_TN)
    n_assign = 2 * T
    flat_e = e_idx[:T, 0:2].reshape(n_assign)
    order = jnp.argsort(flat_e).astype(I32)
    rank = jnp.argsort(order).astype(I32)
    experts = jnp.arange(N_EXPERTS, dtype=I32)
    onehot = flat_e[:, None] == experts[None, :]
    counts = jnp.sum(onehot, axis=0, dtype=I32)
    padded = (counts + MOE_BR - 1) // MOE_BR * MOE_BR
    pad_end = jnp.cumsum(padded)
    pad_start = pad_end - padded
    start = jnp.cumsum(counts) - counts
    slot = (rank + jnp.sum(jnp.where(onehot, (pad_start - start)[None, :], 0), axis=1)).reshape(T, 2)
    n_blocks = -(-n_assign // MOE_BR) + N_EXPERTS
    blk_row0 = jnp.arange(n_blocks, dtype=I32) * MOE_BR
    block_e = jnp.minimum(jnp.sum(pad_end[None, :] <= blk_row0[:, None], axis=1, dtype=I32), N_EXPERTS - 1)
    blk_hot = block_e[:, None] == experts[None, :]
    pick = lambda tab: jnp.sum(jnp.where(blk_hot, tab[None, :], 0), axis=1)
    j_in_e = (blk_row0 - pick(pad_start))[:, None] + jnp.arange(MOE_BR, dtype=I32)[None, :]
    src = jnp.clip(pick(start)[:, None] + j_in_e, 0, n_assign - 1)
    row_token = jnp.where(j_in_e < pick(counts)[:, None], order[src] // 2, 0).reshape(n_blocks * MOE_BR)
    n_used = (pad_end[-1] // MOE_BR).astype(I32).reshape(1)
    hid = _expert_up(block_e, n_used, row_token, x2, w_exp_gate[0], w_exp_up[0])
    y_rows = _expert_down(block_e, n_used, hid, w_exp_down[0])
    slot_pad = jnp.concatenate([jnp.pad(slot[:, 0], (0, MP - T)), jnp.pad(slot[:, 1], (0, MP - T))])
    y_p, y_s = _combine_ln(x2, y_rows, slot_pad, e_w, row1(ln3_w[0]), row1(ln3_b[0]), ROW_TM, TP)

    kv5 = lambda a, n, s: a.reshape(1, n, s, N_KV_HEADS, HEAD_DIM)
    va = z_att[:, A_VA:A_VA + KV_WIDTH]
    ki = kiw[:, 0:IDX_DIM]
    last = lambda a: jnp.concatenate([a[(b + 1) * S - 1:(b + 1) * S] for b in range(B)] + [a[TP:T]], axis=0)
    zl, zsl = last(z_rkv), last(z_small)
    shift_cols = jnp.concatenate([zl, zsl[:, S_LORA:S_LORA + W_LORA], zsl[:, S_LORA + 128:S_LORA + 128 + A_LORA],
                                  zsl[:, S_LORA + 256:S_LORA + 512]], axis=1)
    mem5 = lambda a: a.reshape(1, B, n_mem, MEM_HEADS, MEM_HEAD_DIM)
    return (y_p.reshape(B, S, D), y_s[:DB].reshape(DB, DS, D),
            kv5(k_rot[:TP], B, S), kv5(va[:TP], B, S), ki[:TP].reshape(1, B, S, IDX_DIM),
            wkv_p[None], shift_cols[:B][None], mem5(mem_k), mem5(mem_v),
            kv5(k_rot[TP:T], DB, DS), kv5(va[TP:T], DB, DS), ki[TP:T].reshape(1, DB, DS, IDX_DIM),
            wkv_s[None], shift_cols[B:][None])
```

```python
import functools
import math

import jax
import jax.numpy as jnp
import numpy as np
from jax import lax
from jax.experimental import pallas as pl
from jax.experimental.pallas import tpu as pltpu

F32 = jnp.float32
BF16 = jnp.bfloat16
I32 = jnp.int32
HIGHEST = lax.Precision.HIGHEST

D_MODEL = 4096
RW_HEAD_DIM = 64
RW_HEADS = 32
RW_WIDTH = 2048
W_LORA = 96
A_LORA = 96
G_LORA = 256
RW_PROJ = 3 * RW_WIDTH + W_LORA + A_LORA + G_LORA
RW_GN_EPS = 64e-5
HEAD_DIM = 128
N_HEADS = 16
N_KV_HEADS = 4
GROUP = 4
ATT_WIDTH = 2048
KV_WIDTH = 512
ROT_DIM = 32
ROPE_THETA = 500000.0
IDX_HEADS = 16
IDX_DIM = 64
IDX_ROT_DIM = 16
TOPK_MAX = 256
PAGE_SIZE = 128
MEM_HEADS = 4
MEM_HEAD_DIM = 128
MEM_WIDTH = 512
N_GROUPS = 8
EXPERTS_PER_GROUP = 8
N_EXPERTS = 64
D_EXPERT = 512
LN_EPS = 1e-5
DEEPNORM_ALPHA = 2.0 ** 0.25
EXP_M05 = math.exp(-0.5)

LANES = 128
SUBLANES = 8
VMEM_LIMIT = 56 * 1024 * 1024

DENSE_TM = 640
DENSE_TN = 1024
LN_TM = 320
LN_TN = 512
ROW_TM = 128

C_R, C_K, C_V = 0, 2048, 4096
A_Q, A_KA, A_VA, A_IQ = 0, 2048, 2560, 3072
G_A, G_B = 0, 4096
S_LORA, S_IKW, S_TOTAL = 0, 512, 640

INT_MIN = -(2 ** 31)


def _round_up(n, m):
    return -(-n // m) * m


def _cparams(sem):
    return pltpu.CompilerParams(dimension_semantics=sem, vmem_limit_bytes=VMEM_LIMIT)


def _dot(a, b, precision=None):
    return jnp.dot(a, b, preferred_element_type=F32, precision=precision)


def _dot_nt(a, b, precision=None):
    return lax.dot_general(a, b, (((1,), (1,)), ((), ())), preferred_element_type=F32, precision=precision)


def _sigmoid(x):
    return 1.0 / (1.0 + jnp.exp(-x))


def _mm_body(x_ref, w_ref, o_ref, acc_ref):
    k = pl.program_id(2)

    @pl.when(k == 0)
    def _():
        acc_ref[...] = jnp.zeros_like(acc_ref)

    acc_ref[...] += _dot(x_ref[...], w_ref[...])

    @pl.when(k == pl.num_programs(2) - 1)
    def _():
        o_ref[...] = acc_ref[...].astype(o_ref.dtype)


def _mm_fullk_body(x_ref, w_ref, o_ref):
    o_ref[...] = _dot(x_ref[...], w_ref[...]).astype(o_ref.dtype)


def _mm(x, w, tm, tn, tk, out_dtype=F32, name="mm"):
    m, kd = x.shape
    n = w.shape[1]
    if tk == kd:
        return pl.pallas_call(
            _mm_fullk_body,
            grid=(m // tm, n // tn),
            in_specs=[pl.BlockSpec((tm, kd), lambda i, j: (i, 0)),
                      pl.BlockSpec((kd, tn), lambda i, j: (0, j))],
            out_specs=pl.BlockSpec((tm, tn), lambda i, j: (i, j)),
            out_shape=jax.ShapeDtypeStruct((m, n), out_dtype),
            compiler_params=_cparams(("parallel", "parallel")),
            name=name,
        )(x, w)
    return pl.pallas_call(
        _mm_body,
        grid=(m // tm, n // tn, kd // tk),
        in_specs=[pl.BlockSpec((tm, tk), lambda i, j, k: (i, k)),
                  pl.BlockSpec((tk, tn), lambda i, j, k: (k, j))],
        out_specs=pl.BlockSpec((tm, tn), lambda i, j, k: (i, j)),
        out_shape=jax.ShapeDtypeStruct((m, n), out_dtype),
        scratch_shapes=[pltpu.VMEM((tm, tn), F32)],
        compiler_params=_cparams(("parallel", "parallel", "arbitrary")),
        name=name,
    )(x, w)


def _layer_norm_rows(x, g, b):
    mu = jnp.mean(x, axis=-1, keepdims=True)
    d = x - mu
    var = jnp.mean(d * d, axis=-1, keepdims=True)
    return d * lax.rsqrt(var + LN_EPS) * g + b


def _mm_ln_body(x_ref, w_ref, res_ref, g_ref, b_ref, o_ref, ob_ref, y_ref, *, tn):
    j = pl.program_id(1)
    y_ref[:, pl.ds(pl.multiple_of(j * tn, tn), tn)] = _dot(x_ref[...], w_ref[...])

    @pl.when(j == pl.num_programs(1) - 1)
    def _():
        y = _layer_norm_rows(DEEPNORM_ALPHA * res_ref[...] + y_ref[...], g_ref[...], b_ref[...])
        o_ref[...] = y
        ob_ref[...] = y.astype(BF16)


def _mm_ln(x, w, res, g, b, tm, tn, name="mm_ln"):
    m, kd = x.shape
    n = w.shape[1]
    return pl.pallas_call(
        functools.partial(_mm_ln_body, tn=tn),
        grid=(m // tm, n // tn),
        in_specs=[pl.BlockSpec((tm, kd), lambda i, j: (i, 0)),
                  pl.BlockSpec((kd, tn), lambda i, j: (0, j)),
                  pl.BlockSpec((tm, n), lambda i, j: (i, 0)),
                  pl.BlockSpec((1, n), lambda i, j: (0, 0)),
                  pl.BlockSpec((1, n), lambda i, j: (0, 0))],
        out_specs=[pl.BlockSpec((tm, n), lambda i, j: (i, 0)),
                   pl.BlockSpec((tm, n), lambda i, j: (i, 0))],
        out_shape=[jax.ShapeDtypeStruct((m, n), F32), jax.ShapeDtypeStruct((m, n), BF16)],
        scratch_shapes=[pltpu.VMEM((tm, n), F32)],
        compiler_params=_cparams(("parallel", "arbitrary")),
        name=name,
    )(x, w, res, g, b)


def _branch_merge_body(rw_ref, at_ref, wa_ref, wb_ref, ga_ref, gb_ref, o_ref):
    a = _dot(rw_ref[...], wa_ref[...])
    b = _dot(at_ref[...], wb_ref[...])
    o_ref[...] = (_sigmoid(ga_ref[...]) * a + _sigmoid(gb_ref[...]) * b).astype(o_ref.dtype)


def _branch_merge(rw, att, wa, wb, z, tm, tn):
    m = rw.shape[0]
    n = wa.shape[1]
    ga0, gb0 = G_A // tn, G_B // tn
    return pl.pallas_call(
        _branch_merge_body,
        grid=(m // tm, n // tn),
        in_specs=[pl.BlockSpec((tm, RW_WIDTH), lambda i, j: (i, 0)),
                  pl.BlockSpec((tm, ATT_WIDTH), lambda i, j: (i, 0)),
                  pl.BlockSpec((RW_WIDTH, tn), lambda i, j: (0, j)),
                  pl.BlockSpec((ATT_WIDTH, tn), lambda i, j: (0, j)),
                  pl.BlockSpec((tm, tn), lambda i, j: (i, ga0 + j)),
                  pl.BlockSpec((tm, tn), lambda i, j: (i, gb0 + j))],
        out_specs=pl.BlockSpec((tm, tn), lambda i, j: (i, j)),
        out_shape=jax.ShapeDtypeStruct((m, n), BF16),
        compiler_params=_cparams(("parallel", "parallel")),
        name="branch_merge",
    )(rw, att, wa, wb, z, z)


def _seg_sum(x, ind, ind_t):
    return _dot(_dot(x, ind, HIGHEST), ind_t, HIGHEST)


def _split_bf16(x, parts):
    out = []
    for _ in range(parts):
        t = x.astype(BF16)
        out.append(t)
        x = x - t.astype(F32)
    return out


def _seg_sum_quads(x, bd):
    outs = []
    for q in range(x.shape[1] // WKV_W):
        hi, lo = _split_bf16(x[:, q * WKV_W:(q + 1) * WKV_W], 2)
        outs.append(_dot(hi, bd) + _dot(lo, bd))
    return jnp.concatenate(outs, axis=1) if len(outs) > 1 else outs[0]


def _rwkv_tokens(zr, zk, zv, zl, pr, pk, pv, plo, prm, mu_l, w2, a2, g2, seg):
    r = zr + (pr - zr) * prm[0:1]
    kx = zk + (pk - zk) * prm[1:2]
    v = zv + (pv - zv) * prm[2:3]
    zsl = zl + (plo - zl) * mu_l
    tw = jnp.tanh(zsl[:, 0:128]).astype(BF16)
    xw = prm[3:4] + _dot(tw, w2)
    lw = -EXP_M05 * _sigmoid(xw)
    a = _sigmoid(prm[4:5] + _dot(zsl[:, 128:256].astype(BF16), a2))
    g = _dot(_sigmoid(zsl[:, 256:512]).astype(BF16), g2)
    kk = kx * prm[5:6]
    n2 = seg(kk * kk)
    kkn = kk / jnp.maximum(jnp.sqrt(n2), 1e-12)
    kmod = kx * (1.0 + (a - 1.0) * prm[6:7])
    return r, lw, kmod, v, kkn, a, g


def _rwkv_post(y, r, kmod, v, g, prm, seg):
    inv_n = 1.0 / RW_HEAD_DIM
    mean = seg(y) * inv_n
    d = y - mean
    var = seg(d * d) * inv_n
    yn = d * lax.rsqrt(var + RW_GN_EPS) * prm[8:9] + prm[9:10]
    bonus = seg(r * kmod * prm[7:8]) * v
    return (yn + bonus) * g


WKV_C = 64
WKV_HQ = 4
WKV_W = WKV_HQ * RW_HEAD_DIM
WKV_QPS = 8


def _wkv_chunk_body(zr_ref, zk_ref, zv_ref, zl_ref, prm_ref, mul_ref, w2_ref, a2_ref, g2_ref,
                    sel_ref, o_ref, so_ref, s_ref, cr_ref, ck_ref, cv_ref, cl_ref):
    c = pl.program_id(2)
    C = WKV_C
    W = WKV_W

    @pl.when(c == 0)
    def _():
        s_ref[...] = jnp.zeros_like(s_ref)
        cr_ref[...] = jnp.zeros_like(cr_ref)
        ck_ref[...] = jnp.zeros_like(ck_ref)
        cv_ref[...] = jnp.zeros_like(cv_ref)
        cl_ref[...] = jnp.zeros_like(cl_ref)

    rows = lax.broadcasted_iota(I32, (C, 1), 0)

    def shifted(z, carry_ref):
        prev = jnp.where(rows == 0, carry_ref[0:1, :], pltpu.roll(z, 1, 0))
        carry_ref[0:1, :] = z[C - 1:C, :]
        return prev

    zr, zk, zv, zl = zr_ref[...], zk_ref[...], zv_ref[...], zl_ref[...]
    pr, pk, pv, plo = shifted(zr, cr_ref), shifted(zk, ck_ref), shifted(zv, cv_ref), shifted(zl, cl_ref)
    prm = prm_ref[...]
    lane_head = lax.broadcasted_iota(I32, (1, W), 1) // RW_HEAD_DIM
    hv = lax.broadcasted_iota(I32, (W, 1), 0) // RW_HEAD_DIM
    bd = (hv == lane_head).astype(BF16)
    seg = lambda x: _seg_sum_quads(x, bd)
    r, lw, kmod, v, kkn, a, g = _rwkv_tokens(zr, zk, zv, zl, pr, pk, pv, plo, prm, mul_ref[0:1, :],
                                            w2_ref[...], a2_ref[...], g2_ref[...], seg)
    al = -kkn
    be = kkn * a

    ti = lax.broadcasted_iota(I32, (C, C), 0)
    tj = lax.broadcasted_iota(I32, (C, C), 1)
    tri = (tj <= ti).astype(BF16)
    cum = sum(_dot(tri, part) for part in _split_bf16(lw, 3))
    cum_l = cum[C - 1:C, :]
    p_inv = jnp.exp(-cum)
    p_rel = jnp.exp(cum_l - cum)
    ab = al * jnp.exp(cum - lw)
    rb = r * jnp.exp(cum)
    bt = (be * p_inv).astype(BF16)
    kt = (kmod * p_inv).astype(BF16)
    bk = jnp.concatenate([be * p_rel, kmod * p_rel], axis=0).astype(BF16)
    ar = jnp.concatenate([ab, rb], axis=0)
    pc = jnp.exp(cum_l)

    n4 = WKV_HQ * C
    bi = lax.broadcasted_iota(I32, (n4, n4), 0)
    bj = lax.broadcasted_iota(I32, (n4, n4), 1)
    same = (bi // C) == (bj // C)
    tri_s4 = same & ((bj % C) < (bi % C))
    tri_i4 = same & ((bj % C) <= (bi % C))
    eye4 = (bi == bj).astype(F32)
    masks = [lane_head == j for j in range(WKV_HQ)]

    def stack(x):
        return jnp.concatenate([jnp.where(m, x, jnp.zeros_like(x)) for m in masks], axis=0)

    def block_sum(x):
        return sum(x[j * C:(j + 1) * C] for j in range(WKV_HQ))

    def bdot(a, b):
        return _dot(a.astype(BF16), b.astype(BF16))

    qs = []
    for q in range(WKV_QPS):
        sl = slice(q * W, (q + 1) * W)
        v_q = v[:, sl]
        lhs = jnp.concatenate([stack(ab[:, sl]), stack(rb[:, sl])], axis=0).astype(BF16)
        abr = _dot_nt(lhs, stack(bt[:, sl]))
        akr = _dot_nt(lhs, stack(kt[:, sl]))
        qs.append(dict(sl=sl, v=v_q, v_s=stack(v_q), s0=s_ref[q],
                       x=jnp.where(tri_s4, abr[0:n4], 0.0), a_rb=jnp.where(tri_i4, abr[n4:2 * n4], 0.0),
                       a_ak=jnp.where(tri_s4, akr[0:n4], 0.0), a_rk=jnp.where(tri_i4, akr[n4:2 * n4], 0.0)))
    for d in qs:
        d['pw'] = [d['x']]
    for _ in range(5):
        for d in qs:
            d['pw'].append(bdot(d['pw'][-1], d['pw'][-1]))
    for d in qs:
        pw = d['pw']
        pr_ = [eye4 + pw[2 * i] + pw[2 * i + 1] + bdot(pw[2 * i], pw[2 * i + 1]) for i in range(3)]
        d['t'] = bdot(bdot(pr_[0], pr_[1]), pr_[2])
    for d in qs:
        gs = _dot_nt(ar[:, d['sl']].astype(BF16), d['s0'].astype(BF16))
        d['g_r'] = gs[C:2 * C]
        d['w_s'] = stack(gs[0:C]) + bdot(d['a_ak'], d['v_s'])
    for d in qs:
        d['u_s'] = bdot(d['t'], d['w_s'])
    ys = []
    for q, d in enumerate(qs):
        yv = bdot(jnp.concatenate([d['a_rb'], d['a_rk']], axis=1), jnp.concatenate([d['u_s'], d['v_s']], axis=0))
        ys.append(d['g_r'] + block_sum(yv))
        uv_t = jnp.concatenate([block_sum(d['u_s']), d['v']], axis=0).T.astype(BF16)
        upd = _dot(uv_t, bk[:, d['sl']])
        s_ref[q] = d['s0'] * pc[:, d['sl']] + jnp.where(hv == lane_head, upd, 0.0)

    y = jnp.concatenate(ys, axis=1) if WKV_QPS > 1 else ys[0]
    o_ref[...] = _rwkv_post(y, r, kmod, v, g, prm, seg).astype(o_ref.dtype)

    @pl.when(c == pl.num_programs(2) - 1)
    def _():
        for q in range(WKV_QPS):
            for j in range(WKV_HQ):
                rows_j = s_ref[q, j * RW_HEAD_DIM:(j + 1) * RW_HEAD_DIM, :]
                so_ref[0, q * WKV_HQ + j] = _dot(rows_j, sel_ref[j], HIGHEST)


def _wkv_prompt(z, zs, prm, mu_l, w2, a2, g2, sel, batch, seq):
    nc = seq // WKV_C
    WS = WKV_W * WKV_QPS
    nq = RW_WIDTH // WS
    row = lambda b, q, c: b * nc + c
    return pl.pallas_call(
        _wkv_chunk_body,
        grid=(batch, nq, nc),
        in_specs=[pl.BlockSpec((WKV_C, WS), lambda b, q, c: (row(b, q, c), C_R // WS + q)),
                  pl.BlockSpec((WKV_C, WS), lambda b, q, c: (row(b, q, c), C_K // WS + q)),
                  pl.BlockSpec((WKV_C, WS), lambda b, q, c: (row(b, q, c), C_V // WS + q)),
                  pl.BlockSpec((WKV_C, 512), lambda b, q, c: (row(b, q, c), S_LORA // 512)),
                  pl.BlockSpec((16, WS), lambda b, q, c: (0, q)),
                  pl.BlockSpec((8, 512), lambda b, q, c: (0, 0)),
                  pl.BlockSpec((128, WS), lambda b, q, c: (0, q)),
                  pl.BlockSpec((128, WS), lambda b, q, c: (0, q)),
                  pl.BlockSpec((256, WS), lambda b, q, c: (0, q)),
                  pl.BlockSpec((WKV_HQ, WKV_W, RW_HEAD_DIM), lambda b, q, c: (0, 0, 0))],
        out_specs=[pl.BlockSpec((WKV_C, WS), lambda b, q, c: (row(b, q, c), q)),
                   pl.BlockSpec((1, WKV_HQ * WKV_QPS, RW_HEAD_DIM, RW_HEAD_DIM), lambda b, q, c: (b, q, 0, 0))],
        out_shape=[jax.ShapeDtypeStruct((batch * seq, RW_WIDTH), BF16),
                   jax.ShapeDtypeStruct((batch, RW_HEADS, RW_HEAD_DIM, RW_HEAD_DIM), F32)],
        scratch_shapes=[pltpu.VMEM((WKV_QPS, WKV_W, WKV_W), F32), pltpu.VMEM((8, WS), F32), pltpu.VMEM((8, WS), F32),
                        pltpu.VMEM((8, WS), F32), pltpu.VMEM((8, 512), F32)],
        compiler_params=_cparams(("parallel", "parallel", "arbitrary")),
        name="wkv_prompt",
    )(z, z, z, zs, prm, mu_l, w2, a2, g2, sel)


def _wkv_tok_body(zr_ref, zk_ref, zv_ref, zl_ref, pr_ref, pk_ref, pv_ref, pl_ref, prm_ref, mul_ref, w2_ref, a2_ref,
                  g2_ref, ind_ref, indt_ref, r_ref, w_ref, al_ref, be_ref, km_ref, v_ref, g_ref, bo_ref):
    prm = prm_ref[...]
    ind, ind_t = ind_ref[...], indt_ref[...]
    seg = lambda x: _seg_sum(x, ind, ind_t)
    r, lw, kmod, v, kkn, a, g = _rwkv_tokens(zr_ref[...], zk_ref[...], zv_ref[...], zl_ref[...], pr_ref[...],
                                            pk_ref[...], pv_ref[...], pl_ref[...], prm, mul_ref[0:1, :],
                                            w2_ref[...], a2_ref[...], g2_ref[...], seg)
    r_ref[...] = r
    w_ref[...] = jnp.exp(lw)
    al_ref[...] = -kkn
    be_ref[...] = kkn * a
    km_ref[...] = kmod
    n = r.shape[0]
    pad = jnp.zeros((LANES - n, r.shape[1]), F32)
    for ref, val in ((v_ref, v), (g_ref, g), (bo_ref, seg(r * kmod * prm[7:8]))):
        ref[...] = jnp.concatenate([val, pad], axis=0).T


def _wkv_tokens_sample(z, zs, prev_r, prev_k, prev_v, prev_l, prm, mu_l, w2, a2, g2, ind, ind_t, row0, n):
    rb = row0 // n
    full = lambda a: pl.BlockSpec(a.shape, lambda i: (0,) * a.ndim)
    zspec = lambda w, c0: pl.BlockSpec((n, w), lambda i: (rb, c0 // w))
    return pl.pallas_call(
        _wkv_tok_body,
        grid=(1,),
        in_specs=[zspec(RW_WIDTH, C_R), zspec(RW_WIDTH, C_K), zspec(RW_WIDTH, C_V), zspec(512, S_LORA),
                  full(prev_r), full(prev_k), full(prev_v), full(prev_l), full(prm), full(mu_l), full(w2), full(a2),
                  full(g2), full(ind), full(ind_t)],
        out_specs=[pl.BlockSpec((n, RW_WIDTH), lambda i: (0, 0))] * 5
        + [pl.BlockSpec((RW_WIDTH, LANES), lambda i: (0, 0))] * 3,
        out_shape=[jax.ShapeDtypeStruct((n, RW_WIDTH), F32)] * 5 + [jax.ShapeDtypeStruct((RW_WIDTH, LANES), F32)] * 3,
        compiler_params=_cparams(("arbitrary",)),
        name="wkv_tokens_sample",
    )(z, z, z, zs, prev_r, prev_k, prev_v, prev_l, prm, mu_l, w2, a2, g2, ind, ind_t)


def _wkv_step_body(s_ref, w_ref, al_ref, be_ref, km_ref, r_ref, v_ref, g_ref, bo_ref, lnw_ref, lnb_ref, o_ref, so_ref):
    s = s_ref[...]
    n, hq = s.shape[0], s.shape[1]

    def columns(t_ref):
        t = t_ref[...]
        return jnp.stack([t[:, j:j + 1] for j in range(n)], axis=0).reshape(n, hq, RW_HEAD_DIM, 1)

    vcol, gcol, bocol = columns(v_ref), columns(g_ref), columns(bo_ref)
    sa = jnp.sum(s * al_ref[...], axis=-1, keepdims=True)
    s2 = s * w_ref[...] + sa * be_ref[...] + vcol * km_ref[...]
    so_ref[...] = s2
    y = jnp.sum(s2 * r_ref[...], axis=-1, keepdims=True)
    mean = jnp.mean(y, axis=2, keepdims=True)
    d = y - mean
    var = jnp.mean(d * d, axis=2, keepdims=True)
    yn = d * lax.rsqrt(var + RW_GN_EPS) * lnw_ref[...] + lnb_ref[...]
    o_ref[...] = (yn + bocol * vcol) * gcol


def _wkv_step(state, w, al, be, km, r, vt, gt, bot, lnw, lnb):
    n, h = state.shape[0], state.shape[1]
    hq = 2
    rowspec = pl.BlockSpec((n, hq, 1, RW_HEAD_DIM), lambda q: (0, q, 0, 0))
    colspec = pl.BlockSpec((n, hq, RW_HEAD_DIM, 1), lambda q: (0, q, 0, 0))
    tspec = pl.BlockSpec((hq * RW_HEAD_DIM, LANES), lambda q: (q, 0))
    pcol = pl.BlockSpec((1, hq, RW_HEAD_DIM, 1), lambda q: (0, q, 0, 0))
    sspec = pl.BlockSpec((n, hq, RW_HEAD_DIM, RW_HEAD_DIM), lambda q: (0, q, 0, 0))
    return pl.pallas_call(
        _wkv_step_body,
        grid=(h // hq,),
        in_specs=[sspec, rowspec, rowspec, rowspec, rowspec, rowspec, tspec, tspec, tspec, pcol, pcol],
        out_specs=[colspec, sspec],
        out_shape=[jax.ShapeDtypeStruct((n, h, RW_HEAD_DIM, 1), F32), jax.ShapeDtypeStruct(state.shape, F32)],
        compiler_params=_cparams(("parallel",)),
        name="wkv_step",
    )(state, w, al, be, km, r, vt, gt, bot, lnw, lnb)


def _rope_tables(pos, rot_dim, period):
    half = rot_dim // 2
    t = pos.shape[0]
    inv_freq = ROPE_THETA ** (-jnp.arange(half, dtype=F32) / half)
    ang = pos.astype(F32)[:, None] * inv_freq[None, :]
    cos, sin = jnp.cos(ang), jnp.sin(ang)
    zh = jnp.zeros((t, half), F32)
    rest = period - rot_dim
    c = jnp.concatenate([cos, cos, jnp.ones((t, rest), F32)], axis=1)
    s1 = jnp.concatenate([-sin, zh, jnp.zeros((t, rest), F32)], axis=1)
    s2 = jnp.concatenate([zh, sin, jnp.zeros((t, rest), F32)], axis=1)
    rep = LANES // period
    return jnp.stack([jnp.tile(a, (1, rep)) for a in (c, s1, s2)], axis=0)


def _rope(x, tab, half):
    n = x.shape[1]
    rep = n // LANES
    c, s1, s2 = [jnp.tile(tab[i], (1, rep)) if rep > 1 else tab[i] for i in range(3)]
    return x * c + pltpu.roll(x, n - half, 1) * s1 + pltpu.roll(x, half, 1) * s2


def _prep_body(q_ref, ka_ref, va_ref, iq_ref, ikw_ref, ta_ref, ti_ref, lnw_ref, lnb_ref,
               qo_ref, ko_ref, kb_ref, vb_ref, qio_ref, kio_ref, kid_ref, vbt_ref):
    ta = ta_ref[...]
    ti = ti_ref[...]
    qo_ref[...] = (_rope(q_ref[...], ta, ROT_DIM // 2) * (HEAD_DIM ** -0.5)).astype(BF16)
    k = _rope(ka_ref[...], ta, ROT_DIM // 2)
    ko_ref[...] = k
    kb_ref[...] = k.astype(BF16)
    vb_ref[...] = va_ref[...].astype(BF16)
    vbt_ref[...] = va_ref[...].T.astype(BF16)
    qio_ref[...] = _rope(iq_ref[...], ti, IDX_ROT_DIM // 2).astype(BF16)
    x = ikw_ref[...]
    lane = lax.broadcasted_iota(I32, x.shape, 1)
    is_k = lane < IDX_DIM
    mu = jnp.sum(jnp.where(is_k, x, 0.0), axis=-1, keepdims=True) * (1.0 / IDX_DIM)
    d = jnp.where(is_k, x - mu, 0.0)
    var = jnp.sum(d * d, axis=-1, keepdims=True) * (1.0 / IDX_DIM)
    kn = d * lax.rsqrt(var + LN_EPS) * lnw_ref[...] + lnb_ref[...]
    kr = _rope(kn, ti, IDX_ROT_DIM // 2)
    kr = jnp.where(is_k, kr, 0.0)
    kio_ref[...] = jnp.where(is_k, kr, x * (IDX_HEADS ** -0.5))
    kid_ref[...] = (kr + pltpu.roll(kr, IDX_DIM, 1)).astype(BF16)


def _prep(z, zs, tab_a, tab_i, ln_w, ln_b, tm):
    m = z.shape[0]
    row = lambda w, c0: pl.BlockSpec((tm, w), lambda i: (i, c0 // w))
    outs = [(ATT_WIDTH, BF16), (KV_WIDTH, F32), (KV_WIDTH, BF16), (KV_WIDTH, BF16),
            (IDX_HEADS * IDX_DIM, BF16), (LANES, F32), (LANES, BF16)]
    return pl.pallas_call(
        _prep_body,
        grid=(m // tm,),
        in_specs=[row(ATT_WIDTH, A_Q), row(KV_WIDTH, A_KA), row(KV_WIDTH, A_VA), row(IDX_HEADS * IDX_DIM, A_IQ),
                  row(LANES, S_IKW),
                  pl.BlockSpec((3, tm, LANES), lambda i: (0, i, 0)),
                  pl.BlockSpec((3, tm, LANES), lambda i: (0, i, 0)),
                  pl.BlockSpec((1, LANES), lambda i: (0, 0)),
                  pl.BlockSpec((1, LANES), lambda i: (0, 0))],
        out_specs=[pl.BlockSpec((tm, w), lambda i: (i, 0)) for w, _ in outs]
        + [pl.BlockSpec((KV_WIDTH, tm), lambda i: (0, i))],
        out_shape=[jax.ShapeDtypeStruct((m, w), dt) for w, dt in outs] + [jax.ShapeDtypeStruct((KV_WIDTH, m), BF16)],
        compiler_params=_cparams(("parallel",)),
        name="prep",
    )(z, z, z, z, zs, tab_a, tab_i, ln_w, ln_b)


DSA_QB = 128
DSA_TK = 512
DSA_ATK = 512


def _float_key(s):
    b = pltpu.bitcast(s, I32)
    return b ^ ((b >> 31) & 0x7FFFFFFF)


def _fold_rows(x, op):
    n = x.shape[0]
    while n > SUBLANES:
        n //= 2
        x = op(x[0:n], x[n:2 * n])
    return x


def _kth_threshold(count_ge, topk):
    def step(b, thr):
        cand = thr + jnp.left_shift(jnp.int32(1), 31 - b)
        return jnp.where(count_ge(cand) >= topk, cand, thr)
    return lax.fori_loop(0, 32, step, jnp.full((1, DSA_QB), INT_MIN, I32))


def _dsa_prompt_body(q_ref, kb_ref, vbt_ref, qi_ref, kid_ref, kiw_ref, o_ref, keys_ref, jcut_ref, acc_ref, qs_ref,
                     s_ref, p_ref, *, topk, seq):
    i = pl.program_id(1)
    nt = (i * DSA_QB + DSA_QB + DSA_TK - 1) // DSA_TK
    qpos = i * DSA_QB + lax.broadcasted_iota(I32, (1, DSA_QB), 1)
    row0 = lax.broadcasted_iota(I32, (DSA_TK, 1), 0)
    lane = lax.broadcasted_iota(I32, (1, LANES), 1)
    w_t = (kiw_ref[...] * (IDX_DIM ** -0.5)).T

    def score_tile(t, carry):
        kd = kid_ref[pl.ds(t * DSA_TK, DSA_TK), :]
        s = jnp.zeros((DSA_TK, DSA_QB), F32)
        for h in range(IDX_HEADS):
            qt = qi_ref[:, (h // 2) * LANES:(h // 2 + 1) * LANES]
            qh = jnp.where((lane // IDX_DIM) == (h % 2), qt, jnp.zeros_like(qt))
            s = s + w_t[IDX_DIM + h:IDX_DIM + h + 1, :] * jnp.maximum(_dot_nt(kd, qh), 0.0)
        valid = (t * DSA_TK + row0) <= qpos
        keys_ref[pl.ds(t * DSA_TK, DSA_TK), :] = jnp.where(valid, _float_key(s), INT_MIN)
        return carry

    lax.fori_loop(0, nt, score_tile, 0)

    def count(pred):
        def body(t, acc):
            kt = keys_ref[pl.ds(t * DSA_TK, DSA_TK), :]
            hit = pred(kt, t * DSA_TK + row0).astype(I32)
            return acc + _fold_rows(hit, jnp.add)
        acc = lax.fori_loop(0, nt, body, jnp.zeros((SUBLANES, DSA_QB), I32))
        return jnp.sum(acc, axis=0, keepdims=True)

    thr = _kth_threshold(lambda c: count(lambda kt, col: kt >= c), topk)
    n_gt = count(lambda kt, col: kt > thr)
    n_eq = count(lambda kt, col: (kt == thr) & (col <= qpos))
    need = topk - n_gt
    jcut_ref[...] = jnp.full(jcut_ref.shape, seq, I32)
    excess = (n_eq > need) & (thr > INT_MIN)

    @pl.when(jnp.max(excess.astype(I32)) > 0)
    def _():
        def step(b, jm):
            cand = jm + jnp.left_shift(jnp.int32(1), 30 - b)
            c = count(lambda kt, col: (kt == thr) & (col <= qpos) & (col < cand))
            return jnp.where(c < need, cand, jm)
        jm = lax.fori_loop(0, 31, step, jnp.zeros((1, DSA_QB), I32))
        jcut_ref[...] = jnp.broadcast_to(jnp.where(excess, jm, seq), jcut_ref.shape)

    jcut = jcut_ref[0:1, :]

    acc_ref[...] = jnp.zeros_like(acc_ref)
    for h in range(N_HEADS):
        qs_ref[h // GROUP, (h % GROUP) * DSA_QB:(h % GROUP + 1) * DSA_QB, :] = q_ref[:, h * HEAD_DIM:(h + 1) * HEAD_DIM]
    nta = (i * DSA_QB + DSA_QB + DSA_ATK - 1) // DSA_ATK
    rowa0 = lax.broadcasted_iota(I32, (DSA_ATK, 1), 0)

    def att_tile(t, carry):
        ms, ls = carry
        k0 = pl.multiple_of(t * DSA_ATK, DSA_ATK)
        kt = keys_ref[pl.ds(k0, DSA_ATK), :]
        kpos = k0 + rowa0
        sel = ((kt > thr) | ((kt == thr) & (kpos <= jcut))) & (kpos <= qpos)
        sel4 = jnp.concatenate([sel] * GROUP, axis=1)
        for g in range(N_KV_HEADS):
            k_t = kb_ref[pl.ds(k0, DSA_ATK), g * HEAD_DIM:(g + 1) * HEAD_DIM]
            s_ref[g] = _dot_nt(k_t, qs_ref[g])
        new_m, new_l, corrs = [], [], []
        for g in range(N_KV_HEADS):
            s = jnp.where(sel4, s_ref[g], -1e30)
            m_new = jnp.maximum(ms[g], jnp.max(_fold_rows(s, jnp.maximum), axis=0, keepdims=True))
            p = jnp.exp(s - m_new)
            corr = jnp.exp(ms[g] - m_new)
            new_l.append(ls[g] * corr + jnp.sum(_fold_rows(p, jnp.add), axis=0, keepdims=True))
            p_ref[g] = p.astype(BF16)
            new_m.append(m_new)
            corrs.append(corr)
        for g in range(N_KV_HEADS):
            vt_t = vbt_ref[g * HEAD_DIM:(g + 1) * HEAD_DIM, pl.ds(k0, DSA_ATK)]
            acc_ref[g] = acc_ref[g] * corrs[g] + _dot(vt_t, p_ref[g])
        return tuple(new_m), tuple(new_l)

    wq = GROUP * DSA_QB
    init = (tuple(jnp.full((1, wq), -1e29, F32) for _ in range(N_KV_HEADS)),
            tuple(jnp.zeros((1, wq), F32) for _ in range(N_KV_HEADS)))
    ms, ls = lax.fori_loop(0, nta, att_tile, init)
    for h in range(N_HEADS):
        g, c = h // GROUP, (h % GROUP) * DSA_QB
        o_t = acc_ref[g, :, c:c + DSA_QB] / ls[g][:, c:c + DSA_QB]
        o_ref[:, h * HEAD_DIM:(h + 1) * HEAD_DIM] = o_t.T.astype(o_ref.dtype)


def _dsa_prompt(qb, kb, vbt, qib, kid, kiw, batch, seq, topk):
    nb = seq // DSA_QB
    body = functools.partial(_dsa_prompt_body, topk=topk, seq=seq)
    return pl.pallas_call(
        body,
        grid=(batch, nb),
        in_specs=[pl.BlockSpec((DSA_QB, ATT_WIDTH), lambda b, i: (b * nb + i, 0)),
                  pl.BlockSpec((seq, KV_WIDTH), lambda b, i: (b, 0)),
                  pl.BlockSpec((KV_WIDTH, seq), lambda b, i: (0, b)),
                  pl.BlockSpec((DSA_QB, IDX_HEADS * IDX_DIM), lambda b, i: (b * nb + i, 0)),
                  pl.BlockSpec((seq, LANES), lambda b, i: (b, 0)),
                  pl.BlockSpec((DSA_QB, LANES), lambda b, i: (b * nb + i, 0))],
        out_specs=pl.BlockSpec((DSA_QB, ATT_WIDTH), lambda b, i: (b * nb + i, 0)),
        out_shape=jax.ShapeDtypeStruct((batch * seq, ATT_WIDTH), BF16),
        scratch_shapes=[pltpu.VMEM((seq, DSA_QB), I32), pltpu.VMEM((SUBLANES, DSA_QB), I32),
                        pltpu.VMEM((N_KV_HEADS, HEAD_DIM, GROUP * DSA_QB), F32),
                        pltpu.VMEM((N_KV_HEADS, GROUP * DSA_QB, HEAD_DIM), BF16),
                        pltpu.VMEM((N_KV_HEADS, DSA_ATK, GROUP * DSA_QB), F32),
                        pltpu.VMEM((N_KV_HEADS, DSA_ATK, GROUP * DSA_QB), BF16)],
        compiler_params=_cparams(("parallel", "arbitrary")),
        name="dsa_prompt",
    )(qb, kb, vbt, qib, kid, kiw)


def _sel_sample_body(pt_ref, qi_ref, wi_ref, ks_ref, ck_hbm, pos_ref, ms_ref, kbuf, sc_ref, jm_ref, rk_ref, sem, *,
                     topk, npg):
    s = pl.program_id(0)
    slot = s % 2
    U = SEL_SPS

    def page_copy(step, u, p, sl):
        return pltpu.make_async_copy(ck_hbm.at[pt_ref[step * U + u, p]], kbuf.at[sl, u, p], sem.at[sl])

    def request(step, sl):
        for u in range(U):
            for p in range(npg):
                page_copy(step, u, p, sl).start()

    @pl.when(s == 0)
    def _():
        request(0, 0)

    @pl.when(s + 1 < pl.num_programs(0))
    def _():
        request(s + 1, 1 - slot)

    for u in range(U):
        for p in range(npg):
            page_copy(s, u, p, slot).wait()

    lane = lax.broadcasted_iota(I32, (1, PAGE_SIZE), 1)
    pos = lax.broadcasted_iota(I32, (npg, PAGE_SIZE), 0) * PAGE_SIZE + lane

    def total(x):
        return jnp.sum(jnp.sum(x.astype(I32), axis=1, keepdims=True), axis=0, keepdims=True)

    keys, k_self = [], []
    for u in range(U):
        qi, wi = qi_ref[u], wi_ref[u] * (IDX_DIM ** -0.5)
        for c in range(npg // SEL_CP):
            kt = jnp.concatenate([kbuf[slot, u, c * SEL_CP + r] for r in range(SEL_CP)], axis=1).astype(BF16)
            sc = jnp.sum(wi * jnp.maximum(_dot(qi, kt), 0.0), axis=0, keepdims=True)
            for r in range(SEL_CP):
                sc_ref[u, c * SEL_CP + r:c * SEL_CP + r + 1, :] = sc[:, r * PAGE_SIZE:(r + 1) * PAGE_SIZE]
        keys.append(_float_key(sc_ref[u]))
        d = jnp.sum(qi.astype(F32) * ks_ref[u].astype(F32), axis=-1, keepdims=True)
        k_self.append(_float_key(jnp.sum(wi * jnp.maximum(d, 0.0), axis=0, keepdims=True)))

    def step(b, thrs):
        out = []
        for u in range(U):
            cand = thrs[u] + jnp.left_shift(jnp.int32(1), 31 - b)
            c = total(keys[u] >= cand) + (k_self[u] >= cand).astype(I32)
            out.append(jnp.where(c >= topk, cand, thrs[u]))
        return tuple(out)

    thrs = lax.fori_loop(0, 32, step, tuple(jnp.full((1, 1), INT_MIN, I32) for _ in range(U)))

    ri = lax.broadcasted_iota(I32, (PAGE_SIZE, PAGE_SIZE), 0)
    ci = lax.broadcasted_iota(I32, (PAGE_SIZE, PAGE_SIZE), 1)
    pr_ = lax.broadcasted_iota(I32, (npg, npg), 0)
    pc_ = lax.broadcasted_iota(I32, (npg, npg), 1)
    jcol = lax.broadcasted_iota(I32, (topk, PAGE_SIZE), 0)
    lane_f = lax.broadcasted_iota(I32, (topk, PAGE_SIZE), 1).astype(F32)
    ones8 = jnp.ones((SUBLANES, PAGE_SIZE), BF16)
    for u in range(U):
        thr = thrs[u]
        need = topk - total(keys[u] > thr) - (k_self[u] > thr).astype(I32)
        eq = keys[u] == thr
        jm_ref[u] = jnp.full((SUBLANES, LANES), npg * PAGE_SIZE, I32)

        @pl.when(jnp.max((total(eq) > need).astype(I32)) > 0)
        def _():
            def jstep(b, jm):
                cand = jm + jnp.left_shift(jnp.int32(1), 30 - b)
                return jnp.where(total(eq & (pos < cand)) < need, cand, jm)
            jm_ref[u] = jnp.broadcast_to(lax.fori_loop(0, 31, jstep, jnp.zeros((1, 1), I32)), (SUBLANES, LANES))

        jm = jm_ref[u, 0:1, 0:1]
        sel = (keys[u] > thr) | (eq & (pos <= jm))
        self_sel = (k_self[u] > thr) | ((k_self[u] == thr) & (total(eq & (pos <= jm)) < need))
        ms_ref[u] = jnp.broadcast_to(self_sel.astype(F32), (1, LANES))

        sel_b = sel.astype(BF16)
        within = _dot(sel_b, (ri <= ci).astype(BF16))
        tot = _dot(sel_b, jnp.ones((PAGE_SIZE, PAGE_SIZE), BF16))
        before = _dot((pc_ < pr_).astype(BF16), tot.astype(BF16))
        rk_ref[u] = jnp.where(sel, (before + within).astype(I32) - 1, -1)

        def gather_pos(p, carry):
            hi, lo = carry
            hit = jnp.broadcast_to(rk_ref[u, pl.ds(p, 1), :], (topk, PAGE_SIZE)) == jcol
            return hi + jnp.where(hit, jnp.asarray(p, F32), 0.0), lo + jnp.where(hit, lane_f, 0.0)

        zero = jnp.zeros((topk, PAGE_SIZE), F32)
        hi, lo = lax.fori_loop(0, npg, gather_pos, (zero, zero))
        pos_row = _dot_nt(ones8, hi.astype(BF16)) * PAGE_SIZE + _dot_nt(ones8, lo.astype(BF16))
        pos_ref[u] = pos_row[0:1].astype(I32)


SEL_CP = 8
SEL_SPS = 4


def _sel_sample(page_table, qi, wi, kself, cache_kt, topk):
    n, npg = page_table.shape
    U = SEL_SPS
    assert npg % SEL_CP == 0 and npg <= PAGE_SIZE and n % U == 0
    grid_spec = pltpu.PrefetchScalarGridSpec(
        num_scalar_prefetch=1,
        grid=(n // U,),
        in_specs=[pl.BlockSpec((U, IDX_HEADS, IDX_DIM), lambda s, pt: (s, 0, 0)),
                  pl.BlockSpec((U, IDX_HEADS, 1), lambda s, pt: (s, 0, 0)),
                  pl.BlockSpec((U, 1, IDX_DIM), lambda s, pt: (s, 0, 0)),
                  pl.BlockSpec(memory_space=pl.ANY)],
        out_specs=[pl.BlockSpec((U, 1, topk), lambda s, pt: (s, 0, 0)),
                   pl.BlockSpec((U, 1, LANES), lambda s, pt: (s, 0, 0))],
        scratch_shapes=[pltpu.VMEM((2, U, npg, IDX_DIM, PAGE_SIZE), F32), pltpu.VMEM((U, npg, PAGE_SIZE), F32),
                        pltpu.VMEM((U, SUBLANES, LANES), I32), pltpu.VMEM((U, npg, PAGE_SIZE), I32),
                        pltpu.SemaphoreType.DMA((2,))],
    )
    return pl.pallas_call(
        functools.partial(_sel_sample_body, topk=topk, npg=npg),
        grid_spec=grid_spec,
        out_shape=[jax.ShapeDtypeStruct((n, 1, topk), I32), jax.ShapeDtypeStruct((n, 1, LANES), F32)],
        compiler_params=_cparams(("arbitrary",)),
        name="sel_sample",
    )(page_table, qi, wi, kself, cache_kt)


def _att_sel_body(pt_ref, pos_ref, q_ref, ms_ref, ks_ref, vs_ref, ck_hbm, cv_hbm, o_ref, kbuf, vbuf, sem, *, topk):
    s = pl.program_id(0)
    slot = s % 2

    def request(seq, sl):
        def body(j, c):
            pos = pos_ref[seq, j]
            pg = pt_ref[seq, pos // PAGE_SIZE]
            r = pos % PAGE_SIZE
            pltpu.make_async_copy(ck_hbm.at[pg, r], kbuf.at[sl, j], sem.at[0, sl]).start(priority=0)
            pltpu.make_async_copy(cv_hbm.at[pg, r], vbuf.at[sl, j], sem.at[1, sl]).start(priority=1)
            return c
        lax.fori_loop(0, topk, body, 0, unroll=8)

    @pl.when(s == 0)
    def _():
        request(0, 0)

    @pl.when(s + 1 < pl.num_programs(0))
    def _():
        request(s + 1, 1 - slot)

    for h in range(topk // PAGE_SIZE):
        rows = pl.ds(h * PAGE_SIZE, PAGE_SIZE)
        pltpu.make_async_copy(ck_hbm.at[0], kbuf.at[slot, rows], sem.at[0, slot]).wait()
        pltpu.make_async_copy(cv_hbm.at[0], vbuf.at[slot, rows], sem.at[1, slot]).wait()

    q = q_ref[0]
    row_g = lax.broadcasted_iota(I32, (N_HEADS, 1), 0) // GROUP
    lane_g = lax.broadcasted_iota(I32, (1, KV_WIDTH), 1) // HEAD_DIM
    q_bd = jnp.where(row_g == lane_g, jnp.tile(q, (1, N_KV_HEADS)), jnp.zeros((N_HEADS, KV_WIDTH), BF16))
    k2 = jnp.concatenate([kbuf[slot, :, g, :] for g in range(N_KV_HEADS)], axis=1).astype(BF16)
    v2 = jnp.concatenate([vbuf[slot, :, g, :] for g in range(N_KV_HEADS)], axis=1).astype(BF16)
    self_row = ms_ref[0]
    self_f = self_row[:, 0:1]
    n_past = topk - jnp.tile(self_row, (1, topk // LANES))
    valid = lax.broadcasted_iota(I32, (1, topk), 1).astype(F32) < n_past
    sc = jnp.where(valid, _dot_nt(q_bd, k2), -1e30)
    s1 = jnp.sum(q.astype(F32) * ks_ref[0].astype(F32), axis=-1, keepdims=True)
    s1 = jnp.where(self_f > 0.5, s1, -1e30)
    m = jnp.maximum(jnp.max(sc, axis=-1, keepdims=True), jnp.maximum(s1, -1e29))
    pr = jnp.exp(sc - m)
    p1 = jnp.exp(s1 - m)
    l = jnp.sum(pr, axis=-1, keepdims=True) + p1
    pv = _dot(pr.astype(BF16), v2)
    own = jnp.zeros((N_HEADS, HEAD_DIM), F32)
    for g in range(N_KV_HEADS):
        own = jnp.where(row_g == g, pv[:, g * HEAD_DIM:(g + 1) * HEAD_DIM], own)
    o_ref[0] = ((own + p1.astype(BF16).astype(F32) * vs_ref[0].astype(F32)) / l).astype(o_ref.dtype)


def _att_sel(page_table, pos_list, q, mself, kself, vself, cache_k, cache_v):
    n, topk = pos_list.shape
    assert topk % PAGE_SIZE == 0
    seqspec = lambda r, c: pl.BlockSpec((1, r, c), lambda s, pt, pos: (s, 0, 0))
    anyspec = pl.BlockSpec(memory_space=pl.ANY)
    rows = (2, topk, N_KV_HEADS, HEAD_DIM)
    grid_spec = pltpu.PrefetchScalarGridSpec(
        num_scalar_prefetch=2,
        grid=(n,),
        in_specs=[seqspec(N_HEADS, HEAD_DIM), seqspec(1, LANES), seqspec(N_HEADS, HEAD_DIM),
                  seqspec(N_HEADS, HEAD_DIM), anyspec, anyspec],
        out_specs=seqspec(N_HEADS, HEAD_DIM),
        scratch_shapes=[pltpu.VMEM(rows, F32), pltpu.VMEM(rows, F32), pltpu.SemaphoreType.DMA((2, 2))],
    )
    return pl.pallas_call(
        functools.partial(_att_sel_body, topk=topk),
        grid_spec=grid_spec,
        out_shape=jax.ShapeDtypeStruct((n, N_HEADS, HEAD_DIM), BF16),
        compiler_params=_cparams(("arbitrary",)),
        name="att_sel",
    )(page_table, pos_list, q, mself, kself, vself, cache_k, cache_v)


def _mem_att_prompt_body(q_ref, k_ref, v_ref, o_ref):
    scale = MEM_HEAD_DIM ** -0.5
    for h in range(MEM_HEADS):
        sl = slice(h * MEM_HEAD_DIM, (h + 1) * MEM_HEAD_DIM)
        s = _dot_nt(q_ref[:, sl], k_ref[:, sl]) * scale
        m = jnp.max(s, axis=-1, keepdims=True)
        e = jnp.exp(s - m)
        pr = e / jnp.sum(e, axis=-1, keepdims=True)
        o_ref[:, sl] = _dot(pr.astype(BF16), v_ref[:, sl]).astype(o_ref.dtype)


def _mem_att_prompt(mq, mk, mv, batch, seq, tq):
    m = mk.shape[0] // batch
    nb = seq // tq
    return pl.pallas_call(
        _mem_att_prompt_body,
        grid=(batch * nb,),
        in_specs=[pl.BlockSpec((tq, MEM_WIDTH), lambda i: (i, 0)),
                  pl.BlockSpec((m, MEM_WIDTH), lambda i: (i // nb, 0)),
                  pl.BlockSpec((m, MEM_WIDTH), lambda i: (i // nb, 0))],
        out_specs=pl.BlockSpec((tq, MEM_WIDTH), lambda i: (i, 0)),
        out_shape=jax.ShapeDtypeStruct((batch * seq, MEM_WIDTH), BF16),
        compiler_params=_cparams(("parallel",)),
        name="mem_att_prompt",
    )(mq, mk, mv)


def _mem_att_sample_body(q_ref, k_ref, v_ref, o_ref):
    scale = MEM_HEAD_DIM ** -0.5
    q = q_ref[0].astype(F32)
    for h in range(MEM_HEADS):
        sl = slice(h * MEM_HEAD_DIM, (h + 1) * MEM_HEAD_DIM)
        s = jnp.sum(k_ref[0, :, h, :] * q[:, sl], axis=-1, keepdims=True) * scale
        m = jnp.max(s, axis=0, keepdims=True)
        e = jnp.exp(s - m)
        pr = e / jnp.sum(e, axis=0, keepdims=True)
        o_ref[0, :, sl] = jnp.sum(pr * v_ref[0, :, h, :], axis=0, keepdims=True).astype(o_ref.dtype)


def _mem_att_sample(mq, mk, mv):
    n, m, nh, hd = mk.shape
    w = nh * hd
    return pl.pallas_call(
        _mem_att_sample_body,
        grid=(n,),
        in_specs=[pl.BlockSpec((1, 1, w), lambda s: (s, 0, 0)),
                  pl.BlockSpec((1, m, nh, hd), lambda s: (s, 0, 0, 0)),
                  pl.BlockSpec((1, m, nh, hd), lambda s: (s, 0, 0, 0))],
        out_specs=pl.BlockSpec((1, 1, w), lambda s: (s, 0, 0)),
        out_shape=jax.ShapeDtypeStruct((n, 1, w), BF16),
        compiler_params=_cparams(("parallel",)),
        name="mem_att_sample",
    )(mq, mk, mv)


def _router_body(x_ref, w_ref, b_ref, ei_ref, ew_ref, acc_ref):
    k = pl.program_id(1)

    @pl.when(k == 0)
    def _():
        acc_ref[...] = jnp.zeros_like(acc_ref)

    acc_ref[...] += _dot(x_ref[...], w_ref[...], HIGHEST)

    @pl.when(k == pl.num_programs(1) - 1)
    def _():
        lg = acc_ref[...] + b_ref[...]
        lane = lax.broadcasted_iota(I32, lg.shape, 1)
        neg = jnp.float32(-jnp.inf)
        is_g = lane < N_GROUPS
        glm = jnp.where(is_g, lg, neg)
        gmax = jnp.max(glm, axis=-1, keepdims=True)
        g_sel = jnp.min(jnp.where(glm == gmax, lane, LANES), axis=-1, keepdims=True)
        g_prob = 1.0 / jnp.sum(jnp.where(is_g, jnp.exp(lg - gmax), 0.0), axis=-1, keepdims=True)
        e_id = lane - N_GROUPS
        in_grp = (e_id >= 0) & (e_id < N_EXPERTS) & ((e_id // EXPERTS_PER_GROUP) == g_sel)
        el = jnp.where(in_grp, lg, neg)
        m1 = jnp.max(el, axis=-1, keepdims=True)
        i1 = jnp.min(jnp.where(in_grp & (el == m1), lane, LANES), axis=-1, keepdims=True)
        rest = in_grp & (lane != i1)
        el2 = jnp.where(rest, lg, neg)
        m2 = jnp.max(el2, axis=-1, keepdims=True)
        i2 = jnp.min(jnp.where(rest & (el2 == m2), lane, LANES), axis=-1, keepdims=True)
        t = jnp.exp(m2 - m1)
        w1 = g_prob / (1.0 + t)
        w2 = g_prob * t / (1.0 + t)
        ei_ref[...] = jnp.where(lane == 0, i1 - N_GROUPS, jnp.where(lane == 1, i2 - N_GROUPS, 0))
        ew_ref[...] = jnp.where(lane == 0, w1, jnp.where(lane == 1, w2, 0.0))


def _router(x, w, b, tm, tk):
    m, kd = x.shape
    return pl.pallas_call(
        _router_body,
        grid=(m // tm, kd // tk),
        in_specs=[pl.BlockSpec((tm, tk), lambda i, k: (i, k)),
                  pl.BlockSpec((tk, LANES), lambda i, k: (k, 0)),
                  pl.BlockSpec((1, LANES), lambda i, k: (0, 0))],
        out_specs=[pl.BlockSpec((tm, LANES), lambda i, k: (i, 0)), pl.BlockSpec((tm, LANES), lambda i, k: (i, 0))],
        out_shape=[jax.ShapeDtypeStruct((m, LANES), I32), jax.ShapeDtypeStruct((m, LANES), F32)],
        scratch_shapes=[pltpu.VMEM((tm, LANES), F32)],
        compiler_params=_cparams(("parallel", "arbitrary")),
        name="router",
    )(x, w, b)


MOE_BR = 128


def _expert_up_body(be_ref, nblk_ref, tok_ref, x_hbm, wg_ref, wu_ref, h_ref, wgb_ref, wub_ref, xbuf, sem):
    i = pl.program_id(0)
    n_used = nblk_ref[0]
    slot = i % 2
    changed = jnp.logical_or(i == 0, be_ref[i] != be_ref[jnp.maximum(i - 1, 0)])

    def request(blk, sl):
        def issue(r2, c):
            for pr in range(2):
                r = 2 * r2 + pr
                pltpu.make_async_copy(x_hbm.at[pl.ds(tok_ref[blk * MOE_BR + r], 1)], xbuf.at[sl, pl.ds(r, 1)],
                                      sem.at[sl]).start(priority=pr)
            return c
        lax.fori_loop(0, MOE_BR // 2, issue, 0, unroll=4)

    @pl.when(jnp.logical_and(i == 0, n_used > 0))
    def _():
        request(0, 0)

    @pl.when(i + 1 < n_used)
    def _():
        request(i + 1, 1 - slot)

    @pl.when(jnp.logical_and(i < n_used, changed))
    def _():
        wgb_ref[...] = wg_ref[0].astype(BF16)
        wub_ref[...] = wu_ref[0].astype(BF16)

    @pl.when(i < n_used)
    def _():
        pltpu.make_async_copy(x_hbm.at[pl.ds(0, MOE_BR)], xbuf.at[slot], sem.at[slot]).wait()
        x = xbuf[slot].astype(BF16)
        a = _dot(x, wgb_ref[...])
        u = _dot(x, wub_ref[...])
        h_ref[...] = (a * _sigmoid(a) * u).astype(h_ref.dtype)

    @pl.when(i >= n_used)
    def _():
        h_ref[...] = jnp.zeros_like(h_ref)


def _expert_up(block_e, nblk, row_token, x, w_gate, w_up):
    nr = row_token.shape[0]
    d = x.shape[1]
    nb = nr // MOE_BR
    blk = lambda i, nbk: jnp.minimum(i, nbk[0] - 1)
    grid_spec = pltpu.PrefetchScalarGridSpec(
        num_scalar_prefetch=3,
        grid=(nb,),
        in_specs=[pl.BlockSpec(memory_space=pl.ANY),
                  pl.BlockSpec((1, d, D_EXPERT), lambda i, be, nbk, tok: (be[blk(i, nbk)], 0, 0)),
                  pl.BlockSpec((1, d, D_EXPERT), lambda i, be, nbk, tok: (be[blk(i, nbk)], 0, 0))],
        out_specs=pl.BlockSpec((MOE_BR, D_EXPERT), lambda i, be, nbk, tok: (i, 0)),
        scratch_shapes=[pltpu.VMEM((d, D_EXPERT), BF16), pltpu.VMEM((d, D_EXPERT), BF16),
                        pltpu.VMEM((2, MOE_BR, d), F32), pltpu.SemaphoreType.DMA((2,))],
    )
    return pl.pallas_call(
        _expert_up_body,
        grid_spec=grid_spec,
        out_shape=jax.ShapeDtypeStruct((nr, D_EXPERT), BF16),
        compiler_params=_cparams(("arbitrary",)),
        name="expert_up",
    )(block_e, nblk, row_token, x, w_gate, w_up)


def _expert_down_body(be_ref, nblk_ref, h_ref, wd_ref, y_ref, wdb_ref):
    i = pl.program_id(0)
    changed = jnp.logical_or(i == 0, be_ref[i] != be_ref[jnp.maximum(i - 1, 0)])

    @pl.when(jnp.logical_and(i < nblk_ref[0], changed))
    def _():
        wdb_ref[...] = wd_ref[0].astype(BF16)

    @pl.when(i < nblk_ref[0])
    def _():
        y_ref[...] = _dot(h_ref[...], wdb_ref[...])

    @pl.when(i >= nblk_ref[0])
    def _():
        y_ref[...] = jnp.zeros_like(y_ref)


def _expert_down(block_e, nblk, h, w_down):
    nr = h.shape[0]
    d = w_down.shape[2]
    blk = lambda i, nbk: jnp.minimum(i, nbk[0] - 1)
    grid_spec = pltpu.PrefetchScalarGridSpec(
        num_scalar_prefetch=2,
        grid=(nr // MOE_BR,),
        in_specs=[pl.BlockSpec((MOE_BR, D_EXPERT), lambda i, be, nbk: (blk(i, nbk), 0)),
                  pl.BlockSpec((1, D_EXPERT, d), lambda i, be, nbk: (be[blk(i, nbk)], 0, 0))],
        out_specs=pl.BlockSpec((MOE_BR, d), lambda i, be, nbk: (i, 0)),
        scratch_shapes=[pltpu.VMEM((D_EXPERT, d), BF16)],
    )
    return pl.pallas_call(
        _expert_down_body,
        grid_spec=grid_spec,
        out_shape=jax.ShapeDtypeStruct((nr, d), F32),
        compiler_params=_cparams(("arbitrary",)),
        name="expert_down",
    )(block_e, nblk, h, w_down)


def _combine_ln_body(slot_ref, x_ref, y_hbm, ew_ref, g_ref, b_ref, op_ref, os_ref, ybuf, sem, *, npb, tm, mp):
    i = pl.program_id(0)
    sl = i % 2

    def request(blk, s_):
        def issue(r, c):
            t = blk * tm + r
            pltpu.make_async_copy(y_hbm.at[pl.ds(slot_ref[t], 1)], ybuf.at[s_, pl.ds(r, 1)],
                                  sem.at[s_]).start(priority=0)
            pltpu.make_async_copy(y_hbm.at[pl.ds(slot_ref[mp + t], 1)], ybuf.at[s_, pl.ds(tm + r, 1)],
                                  sem.at[s_]).start(priority=1)
            return c
        lax.fori_loop(0, tm, issue, 0, unroll=8)

    @pl.when(i == 0)
    def _():
        request(0, 0)

    @pl.when(i + 1 < pl.num_programs(0))
    def _():
        request(i + 1, 1 - sl)

    pltpu.make_async_copy(y_hbm.at[pl.ds(0, 2 * tm)], ybuf.at[sl], sem.at[sl]).wait()
    ew = ew_ref[...]
    ff = ybuf[sl, 0:tm] * ew[:, 0:1] + ybuf[sl, tm:2 * tm] * ew[:, 1:2]
    y = _layer_norm_rows(DEEPNORM_ALPHA * x_ref[...] + ff, g_ref[...], b_ref[...])

    @pl.when(i < npb)
    def _():
        op_ref[...] = y

    @pl.when(i >= npb)
    def _():
        os_ref[...] = y


def _combine_ln(x, y_rows, slot2, ew, g, b, tm, n_prompt):
    m, d = x.shape
    nb = m // tm
    npb = n_prompt // tm
    grid_spec = pltpu.PrefetchScalarGridSpec(
        num_scalar_prefetch=1,
        grid=(nb,),
        in_specs=[pl.BlockSpec((tm, d), lambda i, sl: (i, 0)),
                  pl.BlockSpec(memory_space=pl.ANY),
                  pl.BlockSpec((tm, LANES), lambda i, sl: (i, 0)),
                  pl.BlockSpec((1, d), lambda i, sl: (0, 0)),
                  pl.BlockSpec((1, d), lambda i, sl: (0, 0))],
        out_specs=[pl.BlockSpec((tm, d), lambda i, sl: (jnp.minimum(i, npb - 1), 0)),
                   pl.BlockSpec((tm, d), lambda i, sl: (jnp.maximum(i - npb, 0), 0))],
        scratch_shapes=[pltpu.VMEM((2, 2 * tm, d), F32), pltpu.SemaphoreType.DMA((2,))],
    )
    return pl.pallas_call(
        functools.partial(_combine_ln_body, npb=npb, tm=tm, mp=m),
        grid_spec=grid_spec,
        out_shape=[jax.ShapeDtypeStruct((n_prompt, d), F32), jax.ShapeDtypeStruct((m - n_prompt, d), F32)],
        compiler_params=_cparams(("arbitrary",)),
        name="combine_ln",
    )(slot2, x, y_rows, ew, g, b)


def _pad_cols(x, n):
    return jnp.pad(x, ((0, 0), (0, n - x.shape[1])))


def _split_w_in(w):
    o = [int(v) for v in np.cumsum([0, RW_PROJ, ATT_WIDTH + 2 * KV_WIDTH + IDX_HEADS * IDX_DIM, IDX_DIM + IDX_HEADS,
                                    2 * D_MODEL])]
    w_rkv = w[:, 0:3 * RW_WIDTH].astype(BF16)
    w_att = w[:, o[1]:o[2]].astype(BF16)
    w_gate = w[:, o[3]:o[4]].astype(BF16)
    w_small = jnp.concatenate([_lora_cols(w[:, 0:RW_PROJ]), _pad_cols(w[:, o[2]:o[3]], LANES)], axis=1).astype(BF16)
    return w_rkv, w_att, w_gate, w_small


def _lora_cols(x):
    return jnp.concatenate([_pad_cols(x[:, 6144:6240], 128), _pad_cols(x[:, 6240:6336], 128), x[:, 6336:6592]], axis=1)


def _pack_rwkv(rw_mu, rw_w0, rw_w2, rw_a0, rw_a2, rw_g2, rw_k_k, rw_k_a, rw_r_k, rw_ln_w, rw_ln_b):
    flat = lambda t: t.reshape(1, RW_WIDTH)
    mu = rw_mu.reshape(1, RW_PROJ)
    rows = [mu[:, 0:2048], mu[:, 2048:4096], mu[:, 4096:6144], flat(rw_w0), flat(rw_a0), flat(rw_k_k),
            flat(rw_k_a), flat(rw_r_k), flat(rw_ln_w), flat(rw_ln_b)]
    prm = jnp.pad(jnp.concatenate(rows, axis=0), ((0, 6), (0, 0)))
    mu_l = jnp.pad(_lora_cols(mu), ((0, 7), (0, 0)))
    w2 = jnp.pad(rw_w2, ((0, 128 - W_LORA), (0, 0))).astype(BF16)
    a2 = jnp.pad(rw_a2, ((0, 128 - A_LORA), (0, 0))).astype(BF16)
    g2 = rw_g2.astype(BF16)
    return prm, mu_l, w2, a2, g2


def _head_indicators(width):
    lane = np.arange(width)[:, None] // RW_HEAD_DIM
    ind = (lane == np.arange(128)[None, :]).astype(np.float32)
    return jnp.asarray(ind), jnp.asarray(ind.T)


def _head_selectors():
    sel = np.zeros((WKV_HQ, WKV_W, RW_HEAD_DIM), np.float32)
    for j in range(WKV_HQ):
        sel[j, j * RW_HEAD_DIM + np.arange(RW_HEAD_DIM), np.arange(RW_HEAD_DIM)] = 1.0
    return jnp.asarray(sel)


def kernel(x_prompt, x_sample, mem_prompt, cache_k, cache_v, cache_idx_k, page_table, state_wkv, state_shift, cache_mem_k, cache_mem_v, w_in, rw_mu, rw_w0, rw_w2, rw_a0, rw_a2, rw_g2, rw_k_k, rw_k_a, rw_r_k, rw_ln_w, rw_ln_b, idx_ln_w, idx_ln_b, w_branch_a, w_branch_b, w_out, ln1_w, ln1_b, w_mem_q, w_mem_k, w_mem_v, w_mem_o, ln2_w, ln2_b, w_router_grp, b_router_grp, w_router_exp, b_router_exp, w_exp_gate, w_exp_up, w_exp_down, ln3_w, ln3_b):
    B, S, D = x_prompt.shape
    DB, DS, _ = x_sample.shape
    assert DS == 1 and cache_k.shape[0] == 1
    TP = B * S
    T = TP + DB
    MP = _round_up(T, DENSE_TM)
    past = page_table.shape[1] * PAGE_SIZE
    n_mem = mem_prompt.shape[1]
    row1 = lambda a: a.reshape(1, -1)

    def pad_rows(a):
        return jnp.concatenate([a, jnp.zeros((MP - a.shape[0],) + a.shape[1:], a.dtype)], axis=0)

    x_all = pad_rows(jnp.concatenate([x_prompt.reshape(TP, D), x_sample.reshape(DB, D)], axis=0))
    xb = x_all.astype(BF16)
    w_rkv, w_att, w_gate, w_small = _split_w_in(w_in[0])
    z_rkv = _mm(xb, w_rkv, DENSE_TM, DENSE_TN, D, name="in_proj_rkv")
    z_att = _mm(xb, w_att, DENSE_TM, DENSE_TN, D, name="in_proj_att")
    z_gate = _mm(xb, w_gate, DENSE_TM, DENSE_TN, D, name="in_proj_gate")
    z_small = _mm(xb, w_small, DENSE_TM, S_TOTAL, D, name="in_proj_small")

    prm, mu_l, w2, a2, g2 = _pack_rwkv(rw_mu[0], rw_w0[0], rw_w2[0], rw_a0[0], rw_a2[0], rw_g2[0], rw_k_k[0],
                                       rw_k_a[0], rw_r_k[0], rw_ln_w[0], rw_ln_b[0])
    rw_p, wkv_p = _wkv_prompt(z_rkv, z_small, prm, mu_l, w2, a2, g2, _head_selectors(), B, S)
    ss = state_shift[0]
    ind_f, indt_f = _head_indicators(RW_WIDTH)
    tok = _wkv_tokens_sample(z_rkv, z_small, ss[:, 0:2048], ss[:, 2048:4096], ss[:, 4096:6144], _lora_cols(ss), prm,
                             mu_l, w2, a2, g2, ind_f, indt_f, TP, DB)
    t_r, t_w, t_al, t_be, t_km, t_vt, t_gt, t_bot = tok
    rowv = lambda a: a.reshape(DB, RW_HEADS, 1, RW_HEAD_DIM)
    y_col, wkv_s = _wkv_step(state_wkv[0], rowv(t_w), rowv(t_al), rowv(t_be), rowv(t_km), rowv(t_r), t_vt, t_gt,
                             t_bot, rw_ln_w[0].reshape(1, RW_HEADS, RW_HEAD_DIM, 1),
                             rw_ln_b[0].reshape(1, RW_HEADS, RW_HEAD_DIM, 1))
    rw_all = pad_rows(jnp.concatenate([rw_p, y_col.reshape(DB, RW_WIDTH).astype(BF16)], axis=0))

    pos = jnp.concatenate([jnp.tile(jnp.arange(S, dtype=I32), B), jnp.full((MP - TP,), past, I32)])
    tab_a = _rope_tables(pos, ROT_DIM, HEAD_DIM)
    tab_i = _rope_tables(pos, IDX_ROT_DIM, IDX_DIM)
    qb, k_rot, kb, vb, qib, kiw, kid, vbt = _prep(z_att, z_small, tab_a, tab_i, _pad_cols(row1(idx_ln_w[0]), LANES),
                                                  _pad_cols(row1(idx_ln_b[0]), LANES), ROW_TM)
    att_p = _dsa_prompt(qb, kb, vbt, qib, kid, kiw, B, S, min(TOPK_MAX, S // 4))
    qi_s = qib[TP:T].reshape(DB, IDX_HEADS, IDX_DIM)
    wi_s = kiw[TP:T, IDX_DIM:IDX_DIM + IDX_HEADS].reshape(DB, IDX_HEADS, 1)
    topk_s = min(TOPK_MAX, (past + DS) // 4)
    pos_sel, mself = _sel_sample(page_table, qi_s, wi_s, kid[TP:T, 0:IDX_DIM].reshape(DB, 1, IDX_DIM),
                                 jnp.swapaxes(cache_idx_k[0], 1, 2), topk_s)
    expand = lambda a: jnp.repeat(a[TP:T].reshape(DB, N_KV_HEADS, HEAD_DIM), GROUP, axis=1)
    att_s = _att_sel(page_table, pos_sel.reshape(DB, topk_s), qb[TP:T].reshape(DB, N_HEADS, HEAD_DIM), mself,
                     expand(kb), expand(vb), cache_k[0], cache_v[0])
    att_all = pad_rows(jnp.concatenate([att_p, att_s.reshape(DB, ATT_WIDTH)], axis=0))

    merged = _branch_merge(rw_all, att_all, w_branch_a[0].astype(BF16), w_branch_b[0].astype(BF16), z_gate, DENSE_TM,
                           DENSE_TN)
    x1, x1b = _mm_ln(merged, w_out[0].astype(BF16), x_all, row1(ln1_w[0]), row1(ln1_b[0]), LN_TM, LN_TN,
                     name="out_ln1")

    mq = _mm(x1b, w_mem_q[0].astype(BF16), DENSE_TM, MEM_WIDTH, D, out_dtype=BF16, name="mem_q")
    mem2d = mem_prompt.reshape(B * n_mem, D).astype(BF16)
    mem_k = _mm(mem2d, w_mem_k[0].astype(BF16), B * n_mem, MEM_WIDTH, D, name="mem_k")
    mem_v = _mm(mem2d, w_mem_v[0].astype(BF16), B * n_mem, MEM_WIDTH, D, name="mem_v")
    ma_p = _mem_att_prompt(mq, mem_k.astype(BF16), mem_v.astype(BF16), B, S, 512)
    ma_s = _mem_att_sample(mq[TP:T].reshape(DB, 1, MEM_WIDTH), cache_mem_k[0], cache_mem_v[0])
    ma_all = pad_rows(jnp.concatenate([ma_p, ma_s.reshape(DB, MEM_WIDTH)], axis=0))
    x2, _ = _mm_ln(ma_all, w_mem_o[0].astype(BF16), x1, row1(ln2_w[0]), row1(ln2_b[0]), LN_TM, LN_TN,
                   name="mem_o_ln2")

    w_r = _pad_cols(jnp.concatenate([w_router_grp[0], w_router_exp[0]], axis=1), LANES)
    b_r = _pad_cols(row1(jnp.concatenate([b_router_grp[0], b_router_exp[0]])), LANES)
    e_idx, e_w = _router(x2, w_r, b_r, DENSE_TM, DENSE_TN)
    n_assign = 2 * T
    flat_e = e_idx[:T, 0:2].reshape(n_assign)
    order = jnp.argsort(flat_e).astype(I32)
    rank = jnp.argsort(order).astype(I32)
    experts = jnp.arange(N_EXPERTS, dtype=I32)
    onehot = flat_e[:, None] == experts[None, :]
    counts = jnp.sum(onehot, axis=0, dtype=I32)
    padded = (counts + MOE_BR - 1) // MOE_BR * MOE_BR
    pad_end = jnp.cumsum(padded)
    pad_start = pad_end - padded
    start = jnp.cumsum(counts) - counts
    slot = (rank + jnp.sum(jnp.where(onehot, (pad_start - start)[None, :], 0), axis=1)).reshape(T, 2)
    n_blocks = -(-n_assign // MOE_BR) + N_EXPERTS
    blk_row0 = jnp.arange(n_blocks, dtype=I32) * MOE_BR
    block_e = jnp.minimum(jnp.sum(pad_end[None, :] <= blk_row0[:, None], axis=1, dtype=I32), N_EXPERTS - 1)
    blk_hot = block_e[:, None] == experts[None, :]
    pick = lambda tab: jnp.sum(jnp.where(blk_hot, tab[None, :], 0), axis=1)
    j_in_e = (blk_row0 - pick(pad_start))[:, None] + jnp.arange(MOE_BR, dtype=I32)[None, :]
    src = jnp.clip(pick(start)[:, None] + j_in_e, 0, n_assign - 1)
    row_token = jnp.where(j_in_e < pick(counts)[:, None], order[src] // 2, 0).reshape(n_blocks * MOE_BR)
    n_used = (pad_end[-1] // MOE_BR).astype(I32).reshape(1)
    hid = _expert_up(block_e, n_used, row_token, x2, w_exp_gate[0], w_exp_up[0])
    y_rows = _expert_down(block_e, n_used, hid, w_exp_down[0])
    slot_pad = jnp.concatenate([jnp.pad(slot[:, 0], (0, MP - T)), jnp.pad(slot[:, 1], (0, MP - T))])
    y_p, y_s = _combine_ln(x2, y_rows, slot_pad, e_w, row1(ln3_w[0]), row1(ln3_b[0]), ROW_TM, TP)

    kv5 = lambda a, n, s: a.reshape(1, n, s, N_KV_HEADS, HEAD_DIM)
    va = z_att[:, A_VA:A_VA + KV_WIDTH]
    ki = kiw[:, 0:IDX_DIM]
    last = lambda a: jnp.concatenate([a[(b + 1) * S - 1:(b + 1) * S] for b in range(B)] + [a[TP:T]], axis=0)
    zl, zsl = last(z_rkv), last(z_small)
    shift_cols = jnp.concatenate([zl, zsl[:, S_LORA:S_LORA + W_LORA], zsl[:, S_LORA + 128:S_LORA + 128 + A_LORA],
                                  zsl[:, S_LORA + 256:S_LORA + 512]], axis=1)
    mem5 = lambda a: a.reshape(1, B, n_mem, MEM_HEADS, MEM_HEAD_DIM)
    return (y_p.reshape(B, S, D), y_s[:DB].reshape(DB, DS, D),
            kv5(k_rot[:TP], B, S), kv5(va[:TP], B, S), ki[:TP].reshape(1, B, S, IDX_DIM),
            wkv_p[None], shift_cols[:B][None], mem5(mem_k), mem5(mem_v),
            kv5(k_rot[TP:T], DB, DS), kv5(va[TP:T], DB, DS), ki[TP:T].reshape(1, DB, DS, IDX_DIM),
            wkv_s[None], shift_cols[B:][None])
```
